```python
import math
import jax, jax.numpy as jnp
from jax import lax
import numpy as np

D_MODEL = 1024
BATCH = 2
SEQ = 8192
DEPTH = 1

HEAD_DIM = 64
FOX_HEADS = 8
NSA_HEADS = 8
NSA_KV_HEADS = 2
CMP_BLOCK = 32
CMP_STRIDE = 16
CMP_HIDDEN = 256
SLC_BLOCK = 64
SLC_TOPK = 16
WINDOW = 512
Q_BLOCK = 128
N_GROUPS = 4
EXPERTS_PER_GROUP = 4
N_EXPERTS = N_GROUPS * EXPERTS_PER_GROUP
TOP_K_IN_GROUP = 2
D_EXPERT = 512
RMS_EPS = 1e-6
NEG_INF = -1e30
FORCE_SCORE = 1e4

FOX_W = FOX_HEADS * HEAD_DIM
NSA_W = NSA_HEADS * HEAD_DIM
NSA_KV_W = NSA_KV_HEADS * HEAD_DIM
IN_COLS = (FOX_W, FOX_W, FOX_W, FOX_HEADS,
           NSA_W, NSA_KV_W, NSA_KV_W, NSA_KV_W, NSA_KV_W, NSA_KV_W, NSA_KV_W, 3 * NSA_HEADS,
           D_MODEL, D_MODEL)
D_IN = sum(IN_COLS)

kernel_name = "hybrid_fox_nsa_hiermoe_block"


def rms_norm(x, g):
    xf = x.astype(jnp.float32)
    y = xf * lax.rsqrt(jnp.mean(xf * xf, axis=-1, keepdims=True) + RMS_EPS)
    return (y * g.astype(jnp.float32)).astype(x.dtype)


def split_cols(z):
    outs, off = [], 0
    for c in IN_COLS:
        outs.append(z[..., off:off + c])
        off += c
    return outs


def to_heads(t, n):
    B, S, _ = t.shape
    return t.reshape(B, S, n, HEAD_DIM).transpose(0, 2, 1, 3)


def from_heads(t):
    B, n, S, dh = t.shape
    return t.transpose(0, 2, 1, 3).reshape(B, S, n * dh)


def alibi_slopes(n):
    return jnp.asarray([2.0 ** (-8.0 * (h + 1) / n) for h in range(n)], jnp.float32)


def fox_attention(q, k, v, log_f):
    B, H, S, dh = q.shape
    cum = jnp.cumsum(log_f, axis=-1)
    scale = dh ** -0.5
    kpos = jnp.arange(S)

    def block(i):
        q0 = i * Q_BLOCK
        qpos = q0 + jnp.arange(Q_BLOCK)
        qb = lax.dynamic_slice_in_dim(q, q0, Q_BLOCK, axis=2)
        cb = lax.dynamic_slice_in_dim(cum, q0, Q_BLOCK, axis=2)
        s = jnp.einsum('bhqd,bhkd->bhqk', qb, k, preferred_element_type=jnp.float32) * scale
        s = s + cb[..., :, None] - cum[..., None, :]
        s = jnp.where(kpos[None, :] <= qpos[:, None], s, NEG_INF)
        p = jax.nn.softmax(s, axis=-1)
        return jnp.einsum('bhqk,bhkd->bhqd', p.astype(v.dtype), v)

    o = lax.map(block, jnp.arange(S // Q_BLOCK))
    return jnp.moveaxis(o, 0, 2).reshape(B, H, S, dh)


def compress_blocks(t, w1, w2, pos):
    B, S, G, dh = t.shape
    n_cmp = (S - CMP_BLOCK) // CMP_STRIDE + 1
    idx = jnp.arange(n_cmp)[:, None] * CMP_STRIDE + jnp.arange(CMP_BLOCK)[None, :]
    blk = t[:, idx] + pos[None, None, :, None, :]
    blk = jnp.moveaxis(blk, 3, 1).reshape(B, G, n_cmp, CMP_BLOCK * dh)
    return jax.nn.gelu(blk @ w1) @ w2


def nsa_attention(q, k_cmp, v_cmp, k_slc, v_slc, k_win, v_win, gates):
    B, H, S, dh = q.shape
    G = k_slc.shape[1]
    R = H // G
    n_cmp = k_cmp.shape[2]
    n_slc = S // SLC_BLOCK
    top_n = min(SLC_TOPK, n_slc)
    scale = dh ** -0.5
    slopes = alibi_slopes(H).reshape(G, R)[None, :, :, None, None]

    cmp_start = jnp.arange(n_cmp) * CMP_STRIDE
    cmp_end = cmp_start + CMP_BLOCK - 1
    slc_start = jnp.arange(n_slc) * SLC_BLOCK
    ov = jnp.minimum(cmp_start[:, None] + CMP_BLOCK, slc_start[None, :] + SLC_BLOCK) - \
        jnp.maximum(cmp_start[:, None], slc_start[None, :])
    overlap = jnp.clip(ov, 0, None).astype(jnp.float32) / CMP_BLOCK
    slc_ids = jnp.arange(n_slc)

    k_sb = k_slc.reshape(B, G, n_slc, SLC_BLOCK, dh)
    v_sb = v_slc.reshape(B, G, n_slc, SLC_BLOCK, dh)
    pad = ((0, 0), (0, 0), (WINDOW, 0), (0, 0))
    k_wp = jnp.pad(k_win, pad)
    v_wp = jnp.pad(v_win, pad)
    gather = jax.vmap(jax.vmap(lambda blocks, idx: blocks[idx]))

    def block(i):
        q0 = i * Q_BLOCK
        qpos = q0 + jnp.arange(Q_BLOCK)
        qb = lax.dynamic_slice_in_dim(q, q0, Q_BLOCK, axis=2).reshape(B, G, R, Q_BLOCK, dh)

        s = jnp.einsum('bgrqd,bgnd->bgrqn', qb, k_cmp, preferred_element_type=jnp.float32) * scale
        dist = (qpos[:, None] - cmp_end[None, :]).astype(jnp.float32)
        valid = dist >= 0
        s = jnp.where(valid, s - slopes * dist, NEG_INF)
        p_cmp = jax.nn.softmax(s, axis=-1) * jnp.any(valid, axis=-1)[:, None].astype(jnp.float32)
        o_cmp = jnp.einsum('bgrqn,bgnd->bgrqd', p_cmp.astype(v_cmp.dtype), v_cmp)

        imp = jnp.einsum('bgrqn,nj->bgqj', p_cmp, overlap)
        qblk = qpos // SLC_BLOCK
        forced = (slc_ids[None, :] == 0) | (slc_ids[None, :] == qblk[:, None]) | \
            (slc_ids[None, :] == qblk[:, None] - 1)
        future = slc_ids[None, :] * SLC_BLOCK > qpos[:, None]
        score = jnp.where(future, -1.0, jnp.where(forced, FORCE_SCORE, imp))
        _, idx = lax.top_k(score, top_n)
        n_tok = top_n * SLC_BLOCK
        ks_g = gather(k_sb, idx).reshape(B, G, Q_BLOCK, n_tok, dh)
        vs_g = gather(v_sb, idx).reshape(B, G, Q_BLOCK, n_tok, dh)
        spos = (idx[..., None] * SLC_BLOCK + jnp.arange(SLC_BLOCK)).reshape(B, G, Q_BLOCK, n_tok)
        dist = (qpos[:, None] - spos).astype(jnp.float32)[:, :, None]
        s = jnp.einsum('bgrqd,bgqkd->bgrqk', qb, ks_g, preferred_element_type=jnp.float32) * scale
        s = jnp.where(dist >= 0, s - slopes * dist, NEG_INF)
        o_slc = jnp.einsum('bgrqk,bgqkd->bgrqd', jax.nn.softmax(s, axis=-1).astype(vs_g.dtype), vs_g)

        kw_b = lax.dynamic_slice_in_dim(k_wp, q0, WINDOW + Q_BLOCK, axis=2)
        vw_b = lax.dynamic_slice_in_dim(v_wp, q0, WINDOW + Q_BLOCK, axis=2)
        wpos = q0 - WINDOW + jnp.arange(WINDOW + Q_BLOCK)
        dist = (qpos[:, None] - wpos[None, :]).astype(jnp.float32)
        valid = (dist >= 0) & (dist < WINDOW) & (wpos[None, :] >= 0)
        s = jnp.einsum('bgrqd,bgkd->bgrqk', qb, kw_b, preferred_element_type=jnp.float32) * scale
        s = jnp.where(valid, s - slopes * dist, NEG_INF)
        o_win = jnp.einsum('bgrqk,bgkd->bgrqd', jax.nn.softmax(s, axis=-1).astype(vw_b.dtype), vw_b)

        gb = lax.dynamic_slice_in_dim(gates, q0, Q_BLOCK, axis=2).reshape(B, G, R, Q_BLOCK, 3)
        o = gb[..., 0:1] * o_cmp + gb[..., 1:2] * o_slc + gb[..., 2:3] * o_win
        return o.reshape(B, H, Q_BLOCK, dh)

    o = lax.map(block, jnp.arange(S // Q_BLOCK))
    return jnp.moveaxis(o, 0, 2).reshape(B, H, S, dh)


def hierarchical_moe(x, w_group, b_group, w_router, b_router, w_gate, w_up, w_down):
    B, S, D = x.shape
    t = x.reshape(B * S, D)
    g_logits = (t @ w_group).astype(jnp.float32) + b_group.astype(jnp.float32)
    p_g, g_top = lax.top_k(jax.nn.softmax(g_logits, axis=-1), 1)
    e_logits = ((t @ w_router).astype(jnp.float32) + b_router.astype(jnp.float32)).reshape(-1, N_GROUPS, EXPERTS_PER_GROUP)
    e_in = jnp.take_along_axis(e_logits, g_top[:, :, None], axis=1)[:, 0]
    top_p, top_i = lax.top_k(jax.nn.softmax(e_in, axis=-1), TOP_K_IN_GROUP)
    w = top_p / jnp.sum(top_p, axis=-1, keepdims=True) * p_g
    expert_id = g_top * EXPERTS_PER_GROUP + top_i
    combine = jnp.sum(jax.nn.one_hot(expert_id, N_EXPERTS, dtype=jnp.float32) * w[..., None], axis=1)
    hid = jax.nn.silu(jnp.einsum('td,edf->etf', t, w_gate)) * jnp.einsum('td,edf->etf', t, w_up)
    hid = hid * combine.T.astype(hid.dtype)[:, :, None]
    y = jnp.einsum('etf,efd->td', hid, w_down)
    return y.reshape(B, S, D)


def setup_inputs(seed: int = 0) -> dict:
    key = jax.random.key(seed)
    ks = jax.random.split(key, 26)
    L = DEPTH

    def nrm(k, shape, scale):
        return jax.random.normal(k, shape, jnp.float32) * scale

    def gain(k, shape):
        return 1.0 + 0.02 * jax.random.normal(k, shape, jnp.float32)

    cmp_in = CMP_BLOCK * HEAD_DIM
    return {
        "x": nrm(ks[0], (BATCH, SEQ, D_MODEL), 1.0),
        "norm_mix_g": gain(ks[1], (L, D_MODEL)),
        "w_in": nrm(ks[2], (L, D_MODEL, D_IN), D_MODEL ** -0.5),
        "b_forget": jnp.linspace(1.0, 6.0, FOX_HEADS, dtype=jnp.float32)[None, :] + nrm(ks[3], (L, FOX_HEADS), 0.1),
        "fox_q_g": gain(ks[4], (L, HEAD_DIM)),
        "fox_k_g": gain(ks[5], (L, HEAD_DIM)),
        "nsa_q_g": gain(ks[6], (L, HEAD_DIM)),
        "nsa_k_g": gain(ks[7], (L, HEAD_DIM)),
        "cmp_k_w1": nrm(ks[8], (L, cmp_in, CMP_HIDDEN), cmp_in ** -0.5),
        "cmp_k_w2": nrm(ks[9], (L, CMP_HIDDEN, HEAD_DIM), CMP_HIDDEN ** -0.5),
        "cmp_k_pos": nrm(ks[10], (L, CMP_BLOCK, HEAD_DIM), 0.1),
        "cmp_v_w1": nrm(ks[11], (L, cmp_in, CMP_HIDDEN), cmp_in ** -0.5),
        "cmp_v_w2": nrm(ks[12], (L, CMP_HIDDEN, HEAD_DIM), CMP_HIDDEN ** -0.5),
        "cmp_v_pos": nrm(ks[13], (L, CMP_BLOCK, HEAD_DIM), 0.1),
        "w_fox_up": nrm(ks[14], (L, FOX_W, D_MODEL), FOX_W ** -0.5),
        "w_nsa_up": nrm(ks[15], (L, NSA_W, D_MODEL), NSA_W ** -0.5),
        "w_out": nrm(ks[16], (L, D_MODEL, D_MODEL), D_MODEL ** -0.5),
        "norm_ffn_g": gain(ks[17], (L, D_MODEL)),
        "w_group": nrm(ks[18], (L, D_MODEL, N_GROUPS), D_MODEL ** -0.5),
        "b_group": nrm(ks[19], (L, N_GROUPS), 0.01),
        "w_router": nrm(ks[20], (L, D_MODEL, N_EXPERTS), D_MODEL ** -0.5),
        "b_router": nrm(ks[21], (L, N_EXPERTS), 0.01),
        "w_gate": nrm(ks[22], (L, N_EXPERTS, D_MODEL, D_EXPERT), D_MODEL ** -0.5),
        "w_up": nrm(ks[23], (L, N_EXPERTS, D_MODEL, D_EXPERT), D_MODEL ** -0.5),
        "w_down": nrm(ks[24], (L, N_EXPERTS, D_EXPERT, D_MODEL), D_EXPERT ** -0.5),
    }


def reference(x, norm_mix_g, w_in, b_forget, fox_q_g, fox_k_g, nsa_q_g, nsa_k_g,
              cmp_k_w1, cmp_k_w2, cmp_k_pos, cmp_v_w1, cmp_v_w2, cmp_v_pos,
              w_fox_up, w_nsa_up, w_out, norm_ffn_g, w_group, b_group, w_router, b_router,
              w_gate, w_up, w_down):
    B, S, D = x.shape
    for l in range(DEPTH):
        h = rms_norm(x, norm_mix_g[l])
        (f_q, f_k, f_v, f_f, n_q, k_c, v_c, k_s, v_s, k_w, v_w, n_g, g_a, g_b) = split_cols(h @ w_in[l])

        qa = rms_norm(to_heads(f_q, FOX_HEADS), fox_q_g[l])
        ka = rms_norm(to_heads(f_k, FOX_HEADS), fox_k_g[l])
        va = to_heads(f_v, FOX_HEADS)
        log_f = jax.nn.log_sigmoid(f_f.astype(jnp.float32) + b_forget[l].astype(jnp.float32)).transpose(0, 2, 1)
        out_a = from_heads(fox_attention(qa, ka, va, log_f)) @ w_fox_up[l]

        qb = rms_norm(to_heads(n_q, NSA_HEADS), nsa_q_g[l])
        kc = compress_blocks(k_c.reshape(B, S, NSA_KV_HEADS, HEAD_DIM), cmp_k_w1[l], cmp_k_w2[l], cmp_k_pos[l])
        kc = rms_norm(kc, nsa_k_g[l])
        vc = compress_blocks(v_c.reshape(B, S, NSA_KV_HEADS, HEAD_DIM), cmp_v_w1[l], cmp_v_w2[l], cmp_v_pos[l])
        ks_ = rms_norm(to_heads(k_s, NSA_KV_HEADS), nsa_k_g[l])
        vs_ = to_heads(v_s, NSA_KV_HEADS)
        kw_ = rms_norm(to_heads(k_w, NSA_KV_HEADS), nsa_k_g[l])
        vw_ = to_heads(v_w, NSA_KV_HEADS)
        gates = jax.nn.sigmoid(n_g.reshape(B, S, NSA_HEADS, 3)).transpose(0, 2, 1, 3)
        out_b = from_heads(nsa_attention(qb, kc, vc, ks_, vs_, kw_, vw_, gates)) @ w_nsa_up[l]

        mix = jax.nn.sigmoid(g_a) * out_a + jax.nn.sigmoid(g_b) * out_b
        x = x + mix @ w_out[l]

        x = x + hierarchical_moe(rms_norm(x, norm_ffn_g[l]), w_group[l], b_group[l], w_router[l], b_router[l],
                                 w_gate[l], w_up[l], w_down[l])
    return x
```

```python
import functools

import jax
import jax.numpy as jnp
from jax import lax
from jax.experimental import pallas as pl
from jax.experimental.pallas import tpu as pltpu

F32 = jnp.float32
BF16 = jnp.bfloat16

D_MODEL = 1024
HEAD_DIM = 64
FOX_HEADS = 8
NSA_HEADS = 8
NSA_KV_HEADS = 2
NSA_REP = NSA_HEADS // NSA_KV_HEADS
CMP_BLOCK = 32
CMP_STRIDE = 16
CMP_HIDDEN = 256
SLC_BLOCK = 64
SLC_TOPK = 16
WINDOW = 512
N_GROUPS = 4
EXPERTS_PER_GROUP = 4
N_EXPERTS = N_GROUPS * EXPERTS_PER_GROUP
D_EXPERT = 512
RMS_EPS = 1e-6
NEG_INF = -1e30
FORCE_SCORE = 1e4

LANES = 128
HALF = LANES // 2
VMEM_LIMIT = 56 * 1024 * 1024

FOX_W = FOX_HEADS * HEAD_DIM
NSA_W = NSA_HEADS * HEAD_DIM
NSA_KV_W = NSA_KV_HEADS * HEAD_DIM

OFF_FQ = 0
OFF_FK = OFF_FQ + FOX_W
OFF_FV = OFF_FK + FOX_W
OFF_NQ = OFF_FV + FOX_W
OFF_KC = OFF_NQ + NSA_W
OFF_VC = OFF_KC + NSA_KV_W
OFF_KS = OFF_VC + NSA_KV_W
OFF_VS = OFF_KS + NSA_KV_W
OFF_KW = OFF_VS + NSA_KV_W
OFF_VW = OFF_KW + NSA_KV_W
OFF_FF = OFF_VW + NSA_KV_W
OFF_NG = OFF_FF + LANES
OFF_GA = OFF_NG + LANES
N_PROJ = OFF_GA + 2 * D_MODEL

TM_PROJ = 256
TQ_FOX = 256
TQ_NSA = 128
TK_SLC = 256
TM_MERGE = 512
TM_MOE = 1024


def _dot(a, b):
    return jnp.dot(a, b, preferred_element_type=F32)


def _dot_nt(a, b):
    return lax.dot_general(a, b, (((1,), (1,)), ((), ())), preferred_element_type=F32)


def _split3(x):
    hi = x.astype(BF16)
    r = x - hi.astype(F32)
    mid = r.astype(BF16)
    lo = (r - mid.astype(F32)).astype(BF16)
    return hi, mid, lo


def _cparams(sem):
    return pltpu.CompilerParams(dimension_semantics=sem, vmem_limit_bytes=VMEM_LIMIT)


def _const_spec(shape):
    nd = len(shape)
    return pl.BlockSpec(shape, lambda *_: (0,) * nd)


def _inproj_kernel(x_ref, g_ref, w_ref, bf_ref, gqa_ref, gka_ref, gqn_ref, gkn_ref,
                   bd_ref, bd2_ref, tri_ref, selq_ref, selk_ref, cq_ref, ck_ref, cv_ref,
                   nqc_ref, kpc_ref, vone_ref,
                   qa_ref, ka_ref, va_ref, qn_ref, kcr_ref, vcr_ref,
                   ksa_ref, vsa_ref, kwa_ref, vwa_ref, gt_ref, gab_ref,
                   carry_ref):
    tm = x_ref.shape[1]

    @pl.when(pl.program_id(1) == 0)
    def _():
        carry_ref[...] = jnp.zeros_like(carry_ref)

    x = x_ref[0]
    y = x * lax.rsqrt(jnp.mean(x * x, axis=-1, keepdims=True) + RMS_EPS)
    h = (y * g_ref[...]).astype(BF16)

    def proj(off, n):
        return _dot(h, w_ref[:, off:off + n])

    lo_half = lax.broadcasted_iota(jnp.int32, (tm, LANES), 1) < HALF

    def headnorm(z, bd, grow):
        msq = _dot((z * z).astype(BF16), bd[...])
        return z * lax.rsqrt(msq + RMS_EPS) * grow[...]

    def spread_pairs(out_ref, src, aug):
        for m in range(4):
            s = src[:, LANES * m:LANES * (m + 1)]
            out_ref[0, :, LANES * 2 * m:LANES * (2 * m + 1)] = jnp.where(lo_half, s, aug(2 * m)).astype(BF16)
            out_ref[0, :, LANES * (2 * m + 1):LANES * (2 * m + 2)] = jnp.where(lo_half, aug(2 * m + 1), s).astype(BF16)

    zf = proj(OFF_FF, LANES) + bf_ref[...]
    logf = jnp.minimum(zf, 0.0) - jnp.log(1.0 + jnp.exp(-jnp.abs(zf)))
    l_hi, l_mid, l_lo = _split3(logf)
    tri = tri_ref[...]
    cum = carry_ref[...] + (_dot(tri, l_hi) + _dot(tri, l_mid) + _dot(tri, l_lo))
    carry_ref[...] = cum[tm - 1:tm, :]
    pq = jnp.concatenate(_split3(cum), axis=1)
    augq = _dot(pq, selq_ref[...]) + cq_ref[...]
    augk = _dot(pq, selk_ref[...]) + ck_ref[...]

    zq = headnorm(proj(OFF_FQ, FOX_W), bd_ref, gqa_ref)
    spread_pairs(qa_ref, zq, lambda k: augq[:, LANES * k:LANES * (k + 1)])
    zk = headnorm(proj(OFF_FK, FOX_W), bd_ref, gka_ref)
    spread_pairs(ka_ref, zk, lambda k: augk[:, LANES * k:LANES * (k + 1)])
    zv = proj(OFF_FV, FOX_W)
    spread_pairs(va_ref, zv, lambda k: cv_ref[:, LANES * k:LANES * (k + 1)])

    zn = headnorm(proj(OFF_NQ, NSA_W), bd_ref, gqn_ref)
    for m in range(NSA_REP):
        s = zn[:, LANES * m:LANES * (m + 1)]
        c0 = nqc_ref[:, LANES * m:LANES * (m + 1)].astype(F32)
        c1 = nqc_ref[:, LANES * (NSA_REP + m):LANES * (NSA_REP + m + 1)].astype(F32)
        qn_ref[0, :, LANES * m:LANES * (m + 1)] = jnp.where(lo_half, s, c0).astype(BF16)
        qn_ref[0, :, LANES * (NSA_REP + m):LANES * (NSA_REP + m + 1)] = jnp.where(lo_half, c1, s).astype(BF16)

    kcr_ref[0] = proj(OFF_KC, NSA_KV_W).astype(BF16)
    vcr_ref[0] = proj(OFF_VC, NSA_KV_W).astype(BF16)

    kp0 = kpc_ref[:, 0:LANES].astype(F32)
    kp1 = kpc_ref[:, LANES:2 * LANES].astype(F32)

    def kv_pair(k_out, v_out, off_k, off_v):
        zk2 = headnorm(proj(off_k, NSA_KV_W), bd2_ref, gkn_ref)
        k_out[0, :, 0:LANES] = jnp.where(lo_half, zk2, kp0).astype(BF16)
        k_out[0, :, LANES:2 * LANES] = jnp.where(lo_half, kp1, zk2).astype(BF16)
        zv2 = proj(off_v, NSA_KV_W)
        v_out[0, :, 0:LANES] = jnp.where(lo_half, zv2, vone_ref[:, 0:LANES]).astype(BF16)
        v_out[0, :, LANES:2 * LANES] = jnp.where(lo_half, vone_ref[:, LANES:2 * LANES], zv2).astype(BF16)

    kv_pair(ksa_ref, vsa_ref, OFF_KS, OFF_VS)
    kv_pair(kwa_ref, vwa_ref, OFF_KW, OFF_VW)

    gt_ref[0] = jax.nn.sigmoid(proj(OFF_NG, LANES))
    gab_ref[0, :, 0:D_MODEL] = jax.nn.sigmoid(proj(OFF_GA, D_MODEL)).astype(BF16)
    gab_ref[0, :, D_MODEL:2 * D_MODEL] = jax.nn.sigmoid(proj(OFF_GA + D_MODEL, D_MODEL)).astype(BF16)


def _pos_pieces(pos):
    return ((pos // 256) * 256).astype(F32), (pos % 256).astype(F32)


def _inproj(x, norm_g, w_in, b_forget, fox_q_g, fox_k_g, nsa_q_g, nsa_k_g):
    B, S, D = x.shape
    tm = min(TM_PROJ, S)
    scale = HEAD_DIM ** -0.5

    c = [0]
    for n in (FOX_W, FOX_W, FOX_W, FOX_HEADS, NSA_W) + (NSA_KV_W,) * 6 + (3 * NSA_HEADS, D_MODEL, D_MODEL):
        c.append(c[-1] + n)
    fq, fk, fv, ff, nq, kc, vc, ks, vs, kw, vw, ng, ga, gb = [w_in[:, c[i]:c[i + 1]] for i in range(14)]
    perm = jnp.asarray([0, 4, 1, 5, 2, 6, 3, 7])
    nq = nq.reshape(D, NSA_HEADS, HEAD_DIM)[:, perm, :].reshape(D, NSA_W)
    padl = lambda a: jnp.pad(a, ((0, 0), (0, LANES - a.shape[1])))
    w = jnp.concatenate([fq, fk, fv, nq, kc, vc, ks, vs, kw, vw, padl(ff), padl(ng), ga, gb], axis=1).astype(BF16)
    assert w.shape[1] == N_PROJ

    bf = jnp.pad(b_forget.astype(F32), (0, LANES - FOX_HEADS))[None, :]
    gqa = jnp.tile(fox_q_g.astype(F32) * scale, FOX_HEADS)[None, :]
    gka = jnp.tile(fox_k_g.astype(F32), FOX_HEADS)[None, :]
    gqn = jnp.tile(nsa_q_g.astype(F32) * scale, NSA_HEADS)[None, :]
    gkn = jnp.tile(nsa_k_g.astype(F32), NSA_KV_HEADS)[None, :]

    r512 = jnp.arange(FOX_W)
    bd = jnp.where((r512[:, None] // HEAD_DIM) == (r512[None, :] // HEAD_DIM), 1.0 / HEAD_DIM, 0.0).astype(BF16)
    bd2 = bd[:LANES, :LANES]
    rt = jnp.arange(tm)
    tri = (rt[None, :] <= rt[:, None]).astype(BF16)

    heads = jnp.arange(FOX_HEADS)
    base = heads * LANES + jnp.where(heads % 2 == 0, HALF, 0)
    rows = jnp.arange(3 * LANES)
    piece, hlane = rows // LANES, rows % LANES
    cols = jnp.arange(FOX_HEADS * LANES)
    tgt_q = jnp.where(hlane < FOX_HEADS, base[jnp.minimum(hlane, FOX_HEADS - 1)] + 3 + piece, -1)
    tgt_k = jnp.where(hlane < FOX_HEADS, base[jnp.minimum(hlane, FOX_HEADS - 1)] + piece, -1)
    selq = (cols[None, :] == tgt_q[:, None]).astype(BF16)
    selk = -(cols[None, :] == tgt_k[:, None]).astype(BF16)
    off_in_blk = cols - base[cols // LANES]
    cq = ((off_in_blk >= 0) & (off_in_blk < 3)).astype(F32)[None, :]
    ck = ((off_in_blk >= 3) & (off_in_blk < 6)).astype(F32)[None, :]
    cv = (off_in_blk == 0).astype(F32)[None, :]

    pos = jnp.arange(S)
    pa, pb = _pos_pieces(pos)
    blk = cols // LANES
    gq = blk // NSA_REP
    slope = 2.0 ** (-(blk + 1).astype(F32))
    o = cols % LANES - jnp.where(gq == 0, HALF, 0)
    nqc = jnp.where(o[None, :] == 0, slope[None, :], 0.0)
    nqc = jnp.where(o[None, :] == 1, slope[None, :], nqc)
    nqc = jnp.where(o[None, :] == 2, -slope[None, :] * pa[:, None], nqc)
    nqc = jnp.where(o[None, :] == 3, -slope[None, :] * pb[:, None], nqc).astype(BF16)

    kpc, vone = _kv_consts(pa, pb)

    grid = (B, S // tm)
    row_spec = lambda n: pl.BlockSpec((1, tm, n), lambda b, s: (b, s, 0))
    pos_spec = lambda n: pl.BlockSpec((tm, n), lambda b, s: (s, 0))
    consts = [norm_g.astype(F32)[None, :], w, bf, gqa, gka, gqn, gkn, bd, bd2, tri, selq, selk, cq, ck, cv]
    out_widths = [(8 * LANES, BF16)] * 4 + [(LANES, BF16)] * 2 + [(2 * LANES, BF16)] * 4 + \
                 [(LANES, F32), (2 * D_MODEL, BF16)]
    outs = pl.pallas_call(
        _inproj_kernel,
        grid=grid,
        in_specs=[row_spec(D)] + [_const_spec(a.shape) for a in consts] +
                 [pos_spec(8 * LANES), pos_spec(2 * LANES), _const_spec(vone.shape)],
        out_specs=[row_spec(n) for n, _ in out_widths],
        out_shape=[jax.ShapeDtypeStruct((B, S, n), dt) for n, dt in out_widths],
        scratch_shapes=[pltpu.VMEM((1, LANES), F32)],
        compiler_params=_cparams(("arbitrary", "arbitrary")),
        name="inproj",
    )(x, *consts, nqc, kpc, vone)
    return outs


def _kv_consts(pa, pb):
    cols = jnp.arange(2 * LANES)
    o = cols % LANES - jnp.where(cols // LANES == 0, HALF, 0)
    kpc = jnp.where(o[None, :] == 0, pa[:, None], 0.0)
    kpc = jnp.where(o[None, :] == 1, pb[:, None], kpc)
    kpc = jnp.where((o[None, :] == 2) | (o[None, :] == 3), 1.0, kpc).astype(BF16)
    vone = (o == 0).astype(F32)[None, :]
    return kpc, vone


def _compress_kernel(hk_ref, hv_ref, w1k_ref, w1v_ref, posk_ref, posv_ref, w2k_ref, w2v_ref,
                     gk_ref, kcc_ref, vone_ref, kc_ref, vc_ref):
    nc = hk_ref.shape[2]
    half_in = w1k_ref.shape[0] // 2

    def mlp(h_ref, w1_ref, pos_ref, w2_ref):
        hm = h_ref[0, 0]
        top = _dot(hm, w1_ref[0:half_in, :])
        bot = _dot(hm, w1_ref[half_in:2 * half_in, :])
        posw = _dot(pos_ref[...], w1_ref[...])[0:1, :]
        pre = top + pltpu.roll(bot, nc - 1, axis=0) + posw
        act = pre * (0.5 * (1.0 + jnp.tanh(0.7978845608028654 * (pre + 0.044715 * (pre * pre * pre)))))
        return _dot(act.astype(BF16), w2_ref[0])

    kc = mlp(hk_ref, w1k_ref, posk_ref, w2k_ref)
    msq = jnp.sum(kc * kc, axis=-1, keepdims=True) * (1.0 / HEAD_DIM)
    kc_ref[0, 0] = (kc * lax.rsqrt(msq + RMS_EPS) * gk_ref[0] + kcc_ref[0]).astype(BF16)
    vc = mlp(hv_ref, w1v_ref, posv_ref, w2v_ref)
    vc_ref[0, 0] = (vc + vone_ref[0]).astype(BF16)


def _compress(kcr, vcr, cmp_k_w1, cmp_k_w2, cmp_k_pos, cmp_v_w1, cmp_v_w2, cmp_v_pos, nsa_k_g):
    B, S, _ = kcr.shape
    G = NSA_KV_HEADS
    nc = S // CMP_STRIDE
    half_in = CMP_STRIDE * HEAD_DIM

    def halves(t):
        return t.reshape(B, nc, CMP_STRIDE, G, HEAD_DIM).transpose(0, 3, 1, 2, 4).reshape(B, G, nc, half_in)

    def w2_spread(w2):
        z = jnp.zeros_like(w2)
        return jnp.stack([jnp.concatenate([w2, z], 1), jnp.concatenate([z, w2], 1)]).astype(BF16)

    def pos8(p):
        return jnp.tile(p.reshape(1, CMP_BLOCK * HEAD_DIM), (8, 1)).astype(BF16)

    gk = nsa_k_g.astype(F32)
    z = jnp.zeros_like(gk)
    gk2 = jnp.stack([jnp.concatenate([gk, z]), jnp.concatenate([z, gk])])[:, None, :]
    cend = jnp.arange(nc) * CMP_STRIDE + CMP_BLOCK - 1
    kcc, vone = _kv_consts(*_pos_pieces(cend))
    kcc = kcc.astype(F32).reshape(nc, G, LANES).transpose(1, 0, 2)
    vone = vone.reshape(G, 1, LANES)

    blk = pl.BlockSpec((1, 1, nc, half_in), lambda b, g: (b, g, 0, 0))
    per_g = lambda a: pl.BlockSpec((1,) + a.shape[1:], lambda b, g: (g,) + (0,) * (a.ndim - 1))
    w1k, w1v = cmp_k_w1.astype(BF16), cmp_v_w1.astype(BF16)
    pk, pv = pos8(cmp_k_pos), pos8(cmp_v_pos)
    w2k, w2v = w2_spread(cmp_k_w2), w2_spread(cmp_v_w2)
    out_spec = pl.BlockSpec((1, 1, nc, LANES), lambda b, g: (b, g, 0, 0))
    return pl.pallas_call(
        _compress_kernel,
        grid=(B, G),
        in_specs=[blk, blk, _const_spec(w1k.shape), _const_spec(w1v.shape), _const_spec(pk.shape),
                  _const_spec(pv.shape), per_g(w2k), per_g(w2v), per_g(gk2), per_g(kcc), per_g(vone)],
        out_specs=[out_spec, out_spec],
        out_shape=[jax.ShapeDtypeStruct((B, G, nc, LANES), BF16)] * 2,
        compiler_params=_cparams(("arbitrary", "arbitrary")),
        name="compress",
    )(halves(kcr), halves(vcr), w1k, w1v, pk, pv, w2k, w2v, gk2, kcc, vone)


def _attn_first(s, v):
    m = jnp.max(s, axis=-1, keepdims=True)
    p = jnp.exp(s - m)
    return m, _dot(p.astype(BF16), v)


def _attn_step(carry, s, v):
    m, acc = carry
    m_new = jnp.maximum(m, jnp.max(s, axis=-1, keepdims=True))
    p = jnp.exp(s - m_new)
    return m_new, jnp.exp(m - m_new) * acc + _dot(p.astype(BF16), v)


def _nsa_kernel(q_ref, kc_ref, vc_ref, ks_ref, e_ref, vs_ref, kw_ref, vw_ref, gt_ref, ov_ref, o_ref):
    tq = q_ref.shape[1]
    nc = kc_ref.shape[2]
    tk = TK_SLC
    rq = NSA_REP * tq
    g = pl.program_id(1)
    i = pl.program_id(2)
    q0 = i * tq

    q4 = jnp.concatenate([q_ref[0, :, LANES * r:LANES * (r + 1)] for r in range(NSA_REP)], axis=0)

    def qpos_of(shape):
        return q0 + (lax.broadcasted_iota(jnp.int32, shape, 0) & (tq - 1))

    s = _dot_nt(q4, kc_ref[0, 0])
    qpos = qpos_of((rq, nc))
    cend = lax.broadcasted_iota(jnp.int32, (rq, nc), 1) * CMP_STRIDE + (CMP_BLOCK - 1)
    s = jnp.where(qpos >= cend, s, NEG_INF)
    m = jnp.max(s, axis=-1, keepdims=True)
    e = jnp.exp(s - m)
    anyv = (qpos_of((rq, 1)) >= CMP_BLOCK - 1).astype(F32)
    p = e * (anyv / jnp.sum(e, axis=-1, keepdims=True))
    o_cmp = _dot(p.astype(BF16), vc_ref[0, 0])

    psum = p[0:tq] + p[tq:2 * tq] + p[2 * tq:3 * tq] + p[3 * tq:4 * tq]
    p_hi, p_mid, p_lo = _split3(psum)
    ov = ov_ref[...]
    imp = _dot(p_hi, ov) + _dot(p_mid, ov) + _dot(p_lo, ov)
    lane_i = lax.broadcasted_iota(jnp.int32, (tq, LANES), 1)
    lane_f = lane_i.astype(F32)
    qblk = (q0 + lax.broadcasted_iota(jnp.int32, (tq, LANES), 0)) // SLC_BLOCK
    forced = (lane_i == 0) | (lane_i == qblk) | (lane_i == qblk - 1)
    score = jnp.where(lane_i > qblk, -1.0, jnp.where(forced, FORCE_SCORE, imp))
    selb = jnp.full((tq, LANES), NEG_INF, F32)
    for _ in range(SLC_TOPK):
        mx = jnp.max(score, axis=-1, keepdims=True)
        first = jnp.min(jnp.where(score == mx, lane_f, float(LANES)), axis=-1, keepdims=True)
        hit = lane_f == first
        selb = jnp.where(hit, 0.0, selb)
        score = jnp.where(hit, -3e38, score)
    sb = selb.astype(BF16)
    q2 = jnp.concatenate([q4, jnp.concatenate([sb] * NSA_REP, axis=0)], axis=1)

    def slc_scores(j):
        start = pl.multiple_of(j * tk, tk)
        kk = jnp.concatenate([ks_ref[0, pl.ds(start, tk), :], e_ref[pl.ds(start, tk), :]], axis=1)
        return _dot_nt(q2, kk), vs_ref[0, pl.ds(start, tk), :]

    jd = q0 // tk
    s, v = slc_scores(jd)
    kpos = jd * tk + lax.broadcasted_iota(jnp.int32, (rq, tk), 1)
    s = jnp.where(kpos <= qpos_of((rq, tk)), s, NEG_INF)
    carry = _attn_first(s, v)

    def slc_body(j, carry):
        s, v = slc_scores(j)
        return _attn_step(carry, s, v)

    _, acc_slc = lax.fori_loop(0, jd, slc_body, carry)

    def win_scores(t):
        start = pl.multiple_of(t * tq, tq)
        return _dot_nt(q4, kw_ref[0, pl.ds(start, tq), :]), vw_ref[0, pl.ds(start, tq), :]

    rel = lax.broadcasted_iota(jnp.int32, (rq, tq), 1) - (lax.broadcasted_iota(jnp.int32, (rq, tq), 0) & (tq - 1))
    s, v = win_scores(i)
    carry = _attn_first(jnp.where(rel <= 0, s, NEG_INF), v)

    def win_body(t, carry):
        s, v = win_scores(t)
        return _attn_step(carry, s, v)

    nwin = WINDOW // tq
    carry = lax.fori_loop(jnp.maximum(i - nwin + 1, 0), i, win_body, carry)
    s, v = win_scores(jnp.maximum(i - nwin, 0))
    s = jnp.where((rel > 0) & (i >= nwin), s, NEG_INF)
    _, acc_win = _attn_step(carry, s, v)

    lane_r = lax.broadcasted_iota(jnp.int32, (rq, LANES), 1)
    data0 = HALF * g
    ones_lane = HALF - data0

    def normalise(acc):
        l = jnp.sum(jnp.where(lane_r == ones_lane, acc, 0.0), axis=-1, keepdims=True)
        return acc * (1.0 / l)

    o_slc = normalise(acc_slc)
    o_win = normalise(acc_win)
    gt = gt_ref[0]
    lane_g = lax.broadcasted_iota(jnp.int32, (tq, LANES), 1)
    is_data_q = (lane_g >= data0) & (lane_g < data0 + HALF)
    for r in range(NSA_REP):
        col = 3 * (NSA_REP * g + r)
        gate = [jnp.sum(jnp.where(lane_g == col + b, gt, 0.0), axis=-1, keepdims=True) for b in range(3)]
        rows = slice(r * tq, (r + 1) * tq)
        o = gate[0] * o_cmp[rows] + gate[1] * o_slc[rows] + gate[2] * o_win[rows]
        o_ref[0, :, LANES * r:LANES * (r + 1)] = jnp.where(is_data_q, o, 0.0).astype(BF16)


def _nsa(qn, kca, vca, ksa, vsa, kwa, vwa, gt):
    B, S, _ = qn.shape
    G = NSA_KV_HEADS
    tq = min(TQ_NSA, S)
    nc = S // CMP_STRIDE
    n_slc = S // SLC_BLOCK
    assert n_slc <= LANES and S % TK_SLC == 0

    cs = jnp.arange(nc)[:, None] * CMP_STRIDE
    ss = jnp.arange(LANES)[None, :] * SLC_BLOCK
    ovl = jnp.clip(jnp.minimum(cs + CMP_BLOCK, ss + SLC_BLOCK) - jnp.maximum(cs, ss), 0, None)
    valid = (jnp.arange(nc)[:, None] < (S - CMP_BLOCK) // CMP_STRIDE + 1) & (jnp.arange(LANES)[None, :] < n_slc)
    ov = jnp.where(valid, ovl.astype(F32) / CMP_BLOCK, 0.0).astype(BF16)
    e1h = (jnp.arange(S)[:, None] // SLC_BLOCK == jnp.arange(LANES)[None, :]).astype(BF16)

    q_spec = pl.BlockSpec((1, tq, NSA_REP * LANES), lambda b, g, i: (b, i, g))
    c_spec = pl.BlockSpec((1, 1, nc, LANES), lambda b, g, i: (b, g, 0, 0))
    kv_spec = pl.BlockSpec((1, S, LANES), lambda b, g, i: (b, 0, g))
    return pl.pallas_call(
        _nsa_kernel,
        grid=(B, G, S // tq),
        in_specs=[q_spec, c_spec, c_spec, kv_spec, _const_spec(e1h.shape), kv_spec, kv_spec, kv_spec,
                  pl.BlockSpec((1, tq, LANES), lambda b, g, i: (b, i, 0)), _const_spec(ov.shape)],
        out_specs=q_spec,
        out_shape=jax.ShapeDtypeStruct((B, S, NSA_HEADS * LANES), BF16),
        compiler_params=_cparams(("arbitrary", "arbitrary", "arbitrary")),
        name="nsa",
    )(qn, kca, vca, ksa, e1h, vsa, kwa, vwa, gt, ov)


def _fox_kernel(q_ref, k_ref, v_ref, o_ref):
    tq = q_ref.shape[1]
    i = pl.program_id(2)
    lane = lax.broadcasted_iota(jnp.int32, (tq, LANES), 1)
    causal = lax.broadcasted_iota(jnp.int32, (tq, tq), 1) <= lax.broadcasted_iota(jnp.int32, (tq, tq), 0)
    outs = []
    for hh in range(2):
        cols = slice(LANES * hh, LANES * (hh + 1))
        q = q_ref[0, :, cols]

        def scores(j, q=q, cols=cols):
            start = pl.multiple_of(j * tq, tq)
            return _dot_nt(q, k_ref[0, pl.ds(start, tq), cols]), v_ref[0, pl.ds(start, tq), cols]

        s, v = scores(i)
        carry = _attn_first(jnp.where(causal, s, NEG_INF), v)

        def body(j, carry, scores=scores):
            s, v = scores(j)
            return _attn_step(carry, s, v)

        _, acc = lax.fori_loop(0, i, body, carry)
        ones_lane = HALF if hh == 0 else 0
        l = jnp.sum(jnp.where(lane == ones_lane, acc, 0.0), axis=-1, keepdims=True)
        outs.append(acc * (1.0 / l))
    o_ref[0] = jnp.where(lane < HALF, outs[0], outs[1]).astype(BF16)


def _fox(qa, ka, va):
    B, S, _ = qa.shape
    tq = min(TQ_FOX, S)
    q_spec = pl.BlockSpec((1, tq, 2 * LANES), lambda b, h, i: (b, i, h))
    kv_spec = pl.BlockSpec((1, S, 2 * LANES), lambda b, h, i: (b, 0, h))
    return pl.pallas_call(
        _fox_kernel,
        grid=(B, FOX_HEADS // 2, S // tq),
        in_specs=[q_spec, kv_spec, kv_spec],
        out_specs=pl.BlockSpec((1, tq, LANES), lambda b, h, i: (b, i, h)),
        out_shape=jax.ShapeDtypeStruct((B, S, FOX_W), BF16),
        compiler_params=_cparams(("arbitrary", "arbitrary", "arbitrary")),
        name="fox",
    )(qa, ka, va)


def _merge_kernel(x_ref, oa_ref, ob_ref, gab_ref, wa_ref, wb_ref, wo_ref, g2_ref, wr_hi_ref, wr_lo_ref, br_ref,
                  x1_ref, h2_ref, cmb_ref):
    tm = x_ref.shape[0]
    out_a = _dot(oa_ref[...], wa_ref[...])
    out_b = _dot(ob_ref[...], wb_ref[...])
    mix = gab_ref[:, 0:D_MODEL].astype(F32) * out_a + gab_ref[:, D_MODEL:2 * D_MODEL].astype(F32) * out_b
    x1 = x_ref[...] + _dot(mix.astype(BF16), wo_ref[...])
    x1_ref[...] = x1
    h2 = x1 * lax.rsqrt(jnp.mean(x1 * x1, axis=-1, keepdims=True) + RMS_EPS) * g2_ref[...]
    h2_ref[...] = h2.astype(BF16)

    h_hi = h2.astype(BF16)
    h_lo = (h2 - h_hi.astype(F32)).astype(BF16)
    logits = _dot(h_hi, wr_hi_ref[...]) + (_dot(h_hi, wr_lo_ref[...]) + _dot(h_lo, wr_hi_ref[...])) + br_ref[...]
    lane = lax.broadcasted_iota(jnp.int32, (tm, LANES), 1)
    lane_f = lane.astype(F32)

    def first_argmax(vals):
        mx = jnp.max(vals, axis=-1, keepdims=True)
        idx = jnp.min(jnp.where(vals == mx, lane_f, float(LANES)), axis=-1, keepdims=True)
        return mx, idx

    is_grp = (lane >= N_EXPERTS) & (lane < N_EXPERTS + N_GROUPS)
    gl = jnp.where(is_grp, logits, NEG_INF)
    gmax, gidx = first_argmax(gl)
    p_g = 1.0 / jnp.sum(jnp.where(is_grp, jnp.exp(gl - gmax), 0.0), axis=-1, keepdims=True)
    g_top = gidx - float(N_EXPERTS)
    e_lo = g_top * float(EXPERTS_PER_GROUP)
    in_grp = (lane_f >= e_lo) & (lane_f < e_lo + float(EXPERTS_PER_GROUP))
    el = jnp.where(in_grp, logits, NEG_INF)
    m1, i1 = first_argmax(el)
    el2 = jnp.where(lane_f == i1, NEG_INF, el)
    m2, i2 = first_argmax(el2)
    e2 = jnp.exp(m2 - m1)
    w1 = p_g / (1.0 + e2)
    w2 = p_g * e2 / (1.0 + e2)
    cmb_ref[...] = jnp.where(lane_f == i1, w1, jnp.where(lane_f == i2, w2, 0.0))


def _merge(x2d, oa, ob, gab, w_fox_up, w_nsa_up, w_out, norm_ffn_g, w_group, b_group, w_router, b_router):
    T, D = x2d.shape
    tm = min(TM_MERGE, T)
    wa = w_fox_up.astype(BF16)
    wn = w_nsa_up.reshape(NSA_KV_HEADS, NSA_REP, HEAD_DIM, D)
    z = jnp.zeros_like(wn[0])
    wb = jnp.stack([jnp.concatenate([wn[0], z], axis=1), jnp.concatenate([z, wn[1]], axis=1)])
    wb = wb.reshape(NSA_HEADS * LANES, D).astype(BF16)
    wo = w_out.astype(BF16)
    wr = jnp.pad(jnp.concatenate([w_router, w_group], axis=1).astype(F32),
                 ((0, 0), (0, LANES - N_EXPERTS - N_GROUPS)))
    wr_hi = wr.astype(BF16)
    wr_lo = (wr - wr_hi.astype(F32)).astype(BF16)
    br = jnp.pad(jnp.concatenate([b_router, b_group]).astype(F32), (0, LANES - N_EXPERTS - N_GROUPS))[None, :]
    g2 = norm_ffn_g.astype(F32)[None, :]

    row = lambda n: pl.BlockSpec((tm, n), lambda t: (t, 0))
    consts = [wa, wb, wo, g2, wr_hi, wr_lo, br]
    return pl.pallas_call(
        _merge_kernel,
        grid=(T // tm,),
        in_specs=[row(D), row(FOX_W), row(NSA_HEADS * LANES), row(2 * D)] + [_const_spec(a.shape) for a in consts],
        out_specs=[row(D), row(D), row(LANES)],
        out_shape=[jax.ShapeDtypeStruct((T, D), F32), jax.ShapeDtypeStruct((T, D), BF16),
                   jax.ShapeDtypeStruct((T, LANES), F32)],
        compiler_params=_cparams(("arbitrary",)),
        name="merge",
    )(x2d, oa, ob, gab, *consts)


def _moe_kernel(h_ref, x1_ref, cmb_ref, wg_ref, wu_ref, wd_ref, o_ref):
    e = pl.program_id(1)

    @pl.when(e == 0)
    def _():
        o_ref[...] = x1_ref[...]

    h = h_ref[...]
    lane = lax.broadcasted_iota(jnp.int32, cmb_ref.shape, 1)
    c = jnp.sum(jnp.where(lane == e, cmb_ref[...], 0.0), axis=-1, keepdims=True)
    a = _dot(h, wg_ref[0])
    hid = (a * jax.nn.sigmoid(a)) * _dot(h, wu_ref[0]) * c
    o_ref[...] += _dot(hid.astype(BF16), wd_ref[0])


def _moe(h2, x1, cmb, w_gate, w_up, w_down):
    T, D = x1.shape
    tm = min(TM_MOE, T)
    row = lambda n: pl.BlockSpec((tm, n), lambda t, e: (t, 0))
    return pl.pallas_call(
        _moe_kernel,
        grid=(T // tm, N_EXPERTS),
        in_specs=[row(D), row(D), row(LANES),
                  pl.BlockSpec((1, D, D_EXPERT), lambda t, e: (e, 0, 0)),
                  pl.BlockSpec((1, D, D_EXPERT), lambda t, e: (e, 0, 0)),
                  pl.BlockSpec((1, D_EXPERT, D), lambda t, e: (e, 0, 0))],
        out_specs=row(D),
        out_shape=jax.ShapeDtypeStruct((T, D), F32),
        compiler_params=_cparams(("arbitrary", "arbitrary")),
        name="moe",
    )(h2, x1, cmb, w_gate.astype(BF16), w_up.astype(BF16), w_down.astype(BF16))


def kernel(x, norm_mix_g, w_in, b_forget, fox_q_g, fox_k_g, nsa_q_g, nsa_k_g, cmp_k_w1, cmp_k_w2, cmp_k_pos,
           cmp_v_w1, cmp_v_w2, cmp_v_pos, w_fox_up, w_nsa_up, w_out, norm_ffn_g, w_group, b_group, w_router,
           b_router, w_gate, w_up, w_down):
    B, S, D = x.shape
    for l in range(w_in.shape[0]):
        qa, ka, va, qn, kcr, vcr, ksa, vsa, kwa, vwa, gt, gab = _inproj(
            x, norm_mix_g[l], w_in[l], b_forget[l], fox_q_g[l], fox_k_g[l], nsa_q_g[l], nsa_k_g[l])
        kca, vca = _compress(kcr, vcr, cmp_k_w1[l], cmp_k_w2[l], cmp_k_pos[l],
                             cmp_v_w1[l], cmp_v_w2[l], cmp_v_pos[l], nsa_k_g[l])
        ob = _nsa(qn, kca, vca, ksa, vsa, kwa, vwa, gt)
        oa = _fox(qa, ka, va)
        x1, h2, cmb = _merge(x.reshape(B * S, D), oa.reshape(B * S, FOX_W), ob.reshape(B * S, NSA_HEADS * LANES),
                             gab.reshape(B * S, 2 * D), w_fox_up[l], w_nsa_up[l], w_out[l], norm_ffn_g[l],
                             w_group[l], b_group[l], w_router[l], b_router[l])
        x = _moe(h2, x1, cmb, w_gate[l], w_up[l], w_down[l]).reshape(B, S, D)
    return x
```

```python
import functools

import jax
import jax.numpy as jnp
from jax import lax
from jax.experimental import pallas as pl
from jax.experimental.pallas import tpu as pltpu

F32 = jnp.float32
BF16 = jnp.bfloat16

D_MODEL = 1024
HEAD_DIM = 64
FOX_HEADS = 8
NSA_HEADS = 8
NSA_KV_HEADS = 2
NSA_REP = NSA_HEADS // NSA_KV_HEADS
CMP_BLOCK = 32
CMP_STRIDE = 16
CMP_HIDDEN = 256
SLC_BLOCK = 64
SLC_TOPK = 16
WINDOW = 512
N_GROUPS = 4
EXPERTS_PER_GROUP = 4
N_EXPERTS = N_GROUPS * EXPERTS_PER_GROUP
D_EXPERT = 512
RMS_EPS = 1e-6
NEG_INF = -1e30
FORCE_SCORE = 1e4

LANES = 128
HALF = LANES // 2
VMEM_LIMIT = 56 * 1024 * 1024

FOX_W = FOX_HEADS * HEAD_DIM
NSA_W = NSA_HEADS * HEAD_DIM
NSA_KV_W = NSA_KV_HEADS * HEAD_DIM

OFF_FQ = 0
OFF_FK = OFF_FQ + FOX_W
OFF_FV = OFF_FK + FOX_W
OFF_NQ = OFF_FV + FOX_W
OFF_KC = OFF_NQ + NSA_W
OFF_VC = OFF_KC + NSA_KV_W
OFF_KS = OFF_VC + NSA_KV_W
OFF_VS = OFF_KS + NSA_KV_W
OFF_KW = OFF_VS + NSA_KV_W
OFF_VW = OFF_KW + NSA_KV_W
OFF_FF = OFF_VW + NSA_KV_W
OFF_NG = OFF_FF + LANES
OFF_GA = OFF_NG + LANES
N_PROJ = OFF_GA + 2 * D_MODEL

TM_PROJ = 256
TQ_FOX = 512
TQ_NSA = 128
TK_SLC = 512
TM_MERGE = 512
TM_MOE = 1024


def _dot(a, b):
    return jnp.dot(a, b, preferred_element_type=F32)


def _dot_nt(a, b):
    return lax.dot_general(a, b, (((1,), (1,)), ((), ())), preferred_element_type=F32)


def _split3(x):
    hi = x.astype(BF16)
    r = x - hi.astype(F32)
    mid = r.astype(BF16)
    lo = (r - mid.astype(F32)).astype(BF16)
    return hi, mid, lo


def _cparams(sem):
    return pltpu.CompilerParams(dimension_semantics=sem, vmem_limit_bytes=VMEM_LIMIT)


def _const_spec(shape):
    nd = len(shape)
    return pl.BlockSpec(shape, lambda *_: (0,) * nd)


def _inproj_kernel(x_ref, g_ref, w_ref, bf_ref, gqa_ref, gka_ref, gqn_ref, gkn_ref,
                   bd_ref, bd2_ref, tri_ref, selq_ref, selk_ref, cq_ref, ck_ref, cv_ref,
                   nqc_ref, kpc_ref, vone_ref,
                   qa_ref, ka_ref, va_ref, qn_ref, kcr_ref, vcr_ref,
                   ksa_ref, vsa_ref, kwa_ref, vwa_ref, gt_ref, gab_ref,
                   carry_ref):
    tm = x_ref.shape[1]

    @pl.when(pl.program_id(1) == 0)
    def _():
        carry_ref[...] = jnp.zeros_like(carry_ref)

    x = x_ref[0]
    y = x * lax.rsqrt(jnp.mean(x * x, axis=-1, keepdims=True) + RMS_EPS)
    h = (y * g_ref[...]).astype(BF16)

    def proj(off, n):
        return _dot(h, w_ref[:, off:off + n])

    lo_half = lax.broadcasted_iota(jnp.int32, (tm, LANES), 1) < HALF

    def headnorm(z, bd, grow):
        msq = _dot((z * z).astype(BF16), bd[...])
        return z * lax.rsqrt(msq + RMS_EPS) * grow[...]

    def spread_pairs(out_ref, src, aug):
        for m in range(4):
            s = src[:, LANES * m:LANES * (m + 1)]
            out_ref[0, :, LANES * 2 * m:LANES * (2 * m + 1)] = jnp.where(lo_half, s, aug(2 * m)).astype(BF16)
            out_ref[0, :, LANES * (2 * m + 1):LANES * (2 * m + 2)] = jnp.where(lo_half, aug(2 * m + 1), s).astype(BF16)

    zf = proj(OFF_FF, LANES) + bf_ref[...]
    logf = jnp.minimum(zf, 0.0) - jnp.log(1.0 + jnp.exp(-jnp.abs(zf)))
    l_hi, l_mid, l_lo = _split3(logf)
    tri = tri_ref[...]
    cum = carry_ref[...] + (_dot(tri, l_hi) + _dot(tri, l_mid) + _dot(tri, l_lo))
    carry_ref[...] = cum[tm - 1:tm, :]
    pq = jnp.concatenate(_split3(cum), axis=1)
    augq = _dot(pq, selq_ref[...]) + cq_ref[...]
    augk = _dot(pq, selk_ref[...]) + ck_ref[...]

    zq = headnorm(proj(OFF_FQ, FOX_W), bd_ref, gqa_ref)
    spread_pairs(qa_ref, zq, lambda k: augq[:, LANES * k:LANES * (k + 1)])
    zk = headnorm(proj(OFF_FK, FOX_W), bd_ref, gka_ref)
    spread_pairs(ka_ref, zk, lambda k: augk[:, LANES * k:LANES * (k + 1)])
    zv = proj(OFF_FV, FOX_W)
    spread_pairs(va_ref, zv, lambda k: cv_ref[:, LANES * k:LANES * (k + 1)])

    zn = headnorm(proj(OFF_NQ, NSA_W), bd_ref, gqn_ref)
    for m in range(NSA_REP):
        s = zn[:, LANES * m:LANES * (m + 1)]
        c0 = nqc_ref[:, LANES * m:LANES * (m + 1)].astype(F32)
        c1 = nqc_ref[:, LANES * (NSA_REP + m):LANES * (NSA_REP + m + 1)].astype(F32)
        qn_ref[0, :, LANES * m:LANES * (m + 1)] = jnp.where(lo_half, s, c0).astype(BF16)
        qn_ref[0, :, LANES * (NSA_REP + m):LANES * (NSA_REP + m + 1)] = jnp.where(lo_half, c1, s).astype(BF16)

    kcr_ref[0] = proj(OFF_KC, NSA_KV_W).astype(BF16)
    vcr_ref[0] = proj(OFF_VC, NSA_KV_W).astype(BF16)

    kp0 = kpc_ref[:, 0:LANES].astype(F32)
    kp1 = kpc_ref[:, LANES:2 * LANES].astype(F32)

    def kv_pair(k_out, v_out, off_k, off_v):
        zk2 = headnorm(proj(off_k, NSA_KV_W), bd2_ref, gkn_ref)
        k_out[0, :, 0:LANES] = jnp.where(lo_half, zk2, kp0).astype(BF16)
        k_out[0, :, LANES:2 * LANES] = jnp.where(lo_half, kp1, zk2).astype(BF16)
        zv2 = proj(off_v, NSA_KV_W)
        v_out[0, :, 0:LANES] = jnp.where(lo_half, zv2, vone_ref[:, 0:LANES]).astype(BF16)
        v_out[0, :, LANES:2 * LANES] = jnp.where(lo_half, vone_ref[:, LANES:2 * LANES], zv2).astype(BF16)

    kv_pair(ksa_ref, vsa_ref, OFF_KS, OFF_VS)
    kv_pair(kwa_ref, vwa_ref, OFF_KW, OFF_VW)

    gt_ref[0] = jax.nn.sigmoid(proj(OFF_NG, LANES))
    gab_ref[0, :, 0:D_MODEL] = jax.nn.sigmoid(proj(OFF_GA, D_MODEL)).astype(BF16)
    gab_ref[0, :, D_MODEL:2 * D_MODEL] = jax.nn.sigmoid(proj(OFF_GA + D_MODEL, D_MODEL)).astype(BF16)


def _pos_pieces(pos):
    return ((pos // 256) * 256).astype(F32), (pos % 256).astype(F32)


def _inproj(x, norm_g, w_in, b_forget, fox_q_g, fox_k_g, nsa_q_g, nsa_k_g):
    B, S, D = x.shape
    tm = min(TM_PROJ, S)
    scale = HEAD_DIM ** -0.5

    c = [0]
    for n in (FOX_W, FOX_W, FOX_W, FOX_HEADS, NSA_W) + (NSA_KV_W,) * 6 + (3 * NSA_HEADS, D_MODEL, D_MODEL):
        c.append(c[-1] + n)
    fq, fk, fv, ff, nq, kc, vc, ks, vs, kw, vw, ng, ga, gb = [w_in[:, c[i]:c[i + 1]] for i in range(14)]
    perm = jnp.asarray([0, 4, 1, 5, 2, 6, 3, 7])
    nq = nq.reshape(D, NSA_HEADS, HEAD_DIM)[:, perm, :].reshape(D, NSA_W)
    padl = lambda a: jnp.pad(a, ((0, 0), (0, LANES - a.shape[1])))
    w = jnp.concatenate([fq, fk, fv, nq, kc, vc, ks, vs, kw, vw, padl(ff), padl(ng), ga, gb], axis=1).astype(BF16)
    assert w.shape[1] == N_PROJ

    bf = jnp.pad(b_forget.astype(F32), (0, LANES - FOX_HEADS))[None, :]
    gqa = jnp.tile(fox_q_g.astype(F32) * scale, FOX_HEADS)[None, :]
    gka = jnp.tile(fox_k_g.astype(F32), FOX_HEADS)[None, :]
    gqn = jnp.tile(nsa_q_g.astype(F32) * scale, NSA_HEADS)[None, :]
    gkn = jnp.tile(nsa_k_g.astype(F32), NSA_KV_HEADS)[None, :]

    r512 = jnp.arange(FOX_W)
    bd = jnp.where((r512[:, None] // HEAD_DIM) == (r512[None, :] // HEAD_DIM), 1.0 / HEAD_DIM, 0.0).astype(BF16)
    bd2 = bd[:LANES, :LANES]
    rt = jnp.arange(tm)
    tri = (rt[None, :] <= rt[:, None]).astype(BF16)

    heads = jnp.arange(FOX_HEADS)
    base = heads * LANES + jnp.where(heads % 2 == 0, HALF, 0)
    rows = jnp.arange(3 * LANES)
    piece, hlane = rows // LANES, rows % LANES
    cols = jnp.arange(FOX_HEADS * LANES)
    tgt_q = jnp.where(hlane < FOX_HEADS, base[jnp.minimum(hlane, FOX_HEADS - 1)] + 3 + piece, -1)
    tgt_k = jnp.where(hlane < FOX_HEADS, base[jnp.minimum(hlane, FOX_HEADS - 1)] + piece, -1)
    selq = (cols[None, :] == tgt_q[:, None]).astype(BF16)
    selk = -(cols[None, :] == tgt_k[:, None]).astype(BF16)
    off_in_blk = cols - base[cols // LANES]
    cq = ((off_in_blk >= 0) & (off_in_blk < 3)).astype(F32)[None, :]
    ck = ((off_in_blk >= 3) & (off_in_blk < 6)).astype(F32)[None, :]
    cv = (off_in_blk == 0).astype(F32)[None, :]

    pos = jnp.arange(S)
    pa, pb = _pos_pieces(pos)
    blk = cols // LANES
    gq = blk // NSA_REP
    slope = 2.0 ** (-(blk + 1).astype(F32))
    o = cols % LANES - jnp.where(gq == 0, HALF, 0)
    nqc = jnp.where(o[None, :] == 0, slope[None, :], 0.0)
    nqc = jnp.where(o[None, :] == 1, slope[None, :], nqc)
    nqc = jnp.where(o[None, :] == 2, -slope[None, :] * pa[:, None], nqc)
    nqc = jnp.where(o[None, :] == 3, -slope[None, :] * pb[:, None], nqc).astype(BF16)

    kpc, vone = _kv_consts(pa, pb)

    grid = (B, S // tm)
    row_spec = lambda n: pl.BlockSpec((1, tm, n), lambda b, s: (b, s, 0))
    pos_spec = lambda n: pl.BlockSpec((tm, n), lambda b, s: (s, 0))
    consts = [norm_g.astype(F32)[None, :], w, bf, gqa, gka, gqn, gkn, bd, bd2, tri, selq, selk, cq, ck, cv]
    out_widths = [(8 * LANES, BF16)] * 4 + [(LANES, BF16)] * 2 + [(2 * LANES, BF16)] * 4 + \
                 [(LANES, F32), (2 * D_MODEL, BF16)]
    outs = pl.pallas_call(
        _inproj_kernel,
        grid=grid,
        in_specs=[row_spec(D)] + [_const_spec(a.shape) for a in consts] +
                 [pos_spec(8 * LANES), pos_spec(2 * LANES), _const_spec(vone.shape)],
        out_specs=[row_spec(n) for n, _ in out_widths],
        out_shape=[jax.ShapeDtypeStruct((B, S, n), dt) for n, dt in out_widths],
        scratch_shapes=[pltpu.VMEM((1, LANES), F32)],
        compiler_params=_cparams(("arbitrary", "arbitrary")),
        name="inproj",
    )(x, *consts, nqc, kpc, vone)
    return outs


def _kv_consts(pa, pb):
    cols = jnp.arange(2 * LANES)
    o = cols % LANES - jnp.where(cols // LANES == 0, HALF, 0)
    kpc = jnp.where(o[None, :] == 0, pa[:, None], 0.0)
    kpc = jnp.where(o[None, :] == 1, pb[:, None], kpc)
    kpc = jnp.where((o[None, :] == 2) | (o[None, :] == 3), 1.0, kpc).astype(BF16)
    vone = (o == 0).astype(F32)[None, :]
    return kpc, vone


def _compress_kernel(hk_ref, hv_ref, w1k_ref, w1v_ref, posk_ref, posv_ref, w2k_ref, w2v_ref,
                     gk_ref, kcc_ref, vone_ref, kc_ref, vc_ref):
    nc = hk_ref.shape[2]
    half_in = w1k_ref.shape[0] // 2

    def mlp(h_ref, w1_ref, pos_ref, w2_ref):
        hm = h_ref[0, 0]
        top = _dot(hm, w1_ref[0:half_in, :])
        bot = _dot(hm, w1_ref[half_in:2 * half_in, :])
        posw = _dot(pos_ref[...], w1_ref[...])[0:1, :]
        pre = top + pltpu.roll(bot, nc - 1, axis=0) + posw
        act = pre * (0.5 * (1.0 + jnp.tanh(0.7978845608028654 * (pre + 0.044715 * (pre * pre * pre)))))
        return _dot(act.astype(BF16), w2_ref[0])

    kc = mlp(hk_ref, w1k_ref, posk_ref, w2k_ref)
    msq = jnp.sum(kc * kc, axis=-1, keepdims=True) * (1.0 / HEAD_DIM)
    kc_ref[0, 0] = (kc * lax.rsqrt(msq + RMS_EPS) * gk_ref[0] + kcc_ref[0]).astype(BF16)
    vc = mlp(hv_ref, w1v_ref, posv_ref, w2v_ref)
    vc_ref[0, 0] = (vc + vone_ref[0]).astype(BF16)


def _compress(kcr, vcr, cmp_k_w1, cmp_k_w2, cmp_k_pos, cmp_v_w1, cmp_v_w2, cmp_v_pos, nsa_k_g):
    B, S, _ = kcr.shape
    G = NSA_KV_HEADS
    nc = S // CMP_STRIDE
    half_in = CMP_STRIDE * HEAD_DIM

    def halves(t):
        return t.reshape(B, nc, CMP_STRIDE, G, HEAD_DIM).transpose(0, 3, 1, 2, 4).reshape(B, G, nc, half_in)

    def w2_spread(w2):
        z = jnp.zeros_like(w2)
        return jnp.stack([jnp.concatenate([w2, z], 1), jnp.concatenate([z, w2], 1)]).astype(BF16)

    def pos8(p):
        return jnp.tile(p.reshape(1, CMP_BLOCK * HEAD_DIM), (8, 1)).astype(BF16)

    gk = nsa_k_g.astype(F32)
    z = jnp.zeros_like(gk)
    gk2 = jnp.stack([jnp.concatenate([gk, z]), jnp.concatenate([z, gk])])[:, None, :]
    cend = jnp.arange(nc) * CMP_STRIDE + CMP_BLOCK - 1
    kcc, vone = _kv_consts(*_pos_pieces(cend))
    kcc = kcc.astype(F32).reshape(nc, G, LANES).transpose(1, 0, 2)
    vone = vone.reshape(G, 1, LANES)

    blk = pl.BlockSpec((1, 1, nc, half_in), lambda b, g: (b, g, 0, 0))
    per_g = lambda a: pl.BlockSpec((1,) + a.shape[1:], lambda b, g: (g,) + (0,) * (a.ndim - 1))
    w1k, w1v = cmp_k_w1.astype(BF16), cmp_v_w1.astype(BF16)
    pk, pv = pos8(cmp_k_pos), pos8(cmp_v_pos)
    w2k, w2v = w2_spread(cmp_k_w2), w2_spread(cmp_v_w2)
    out_spec = pl.BlockSpec((1, 1, nc, LANES), lambda b, g: (b, g, 0, 0))
    return pl.pallas_call(
        _compress_kernel,
        grid=(B, G),
        in_specs=[blk, blk, _const_spec(w1k.shape), _const_spec(w1v.shape), _const_spec(pk.shape),
                  _const_spec(pv.shape), per_g(w2k), per_g(w2v), per_g(gk2), per_g(kcc), per_g(vone)],
        out_specs=[out_spec, out_spec],
        out_shape=[jax.ShapeDtypeStruct((B, G, nc, LANES), BF16)] * 2,
        compiler_params=_cparams(("arbitrary", "arbitrary")),
        name="compress",
    )(halves(kcr), halves(vcr), w1k, w1v, pk, pv, w2k, w2v, gk2, kcc, vone)


def _attn_first(s, v):
    m = jnp.max(s, axis=-1, keepdims=True)
    p = jnp.exp(s - m)
    return m, _dot(p.astype(BF16), v)


def _attn_step(carry, s, v):
    m, acc = carry
    m_new = jnp.maximum(m, jnp.max(s, axis=-1, keepdims=True))
    p = jnp.exp(s - m_new)
    return m_new, jnp.exp(m - m_new) * acc + _dot(p.astype(BF16), v)


def _nsa_kernel(q_ref, kc_ref, vc_ref, ks_ref, e_ref, vs_ref, kw_ref, vw_ref, gt_ref, ov_ref, o_ref):
    tq = q_ref.shape[1]
    nc = kc_ref.shape[2]
    tk = TK_SLC
    rq = NSA_REP * tq
    g = pl.program_id(1)
    i = pl.program_id(2)
    q0 = i * tq

    q4 = jnp.concatenate([q_ref[0, :, LANES * r:LANES * (r + 1)] for r in range(NSA_REP)], axis=0)

    def qpos_of(shape):
        return q0 + (lax.broadcasted_iota(jnp.int32, shape, 0) & (tq - 1))

    s = _dot_nt(q4, kc_ref[0, 0])
    qpos = qpos_of((rq, nc))
    cend = lax.broadcasted_iota(jnp.int32, (rq, nc), 1) * CMP_STRIDE + (CMP_BLOCK - 1)
    s = jnp.where(qpos >= cend, s, NEG_INF)
    m = jnp.max(s, axis=-1, keepdims=True)
    e = jnp.exp(s - m)
    anyv = (qpos_of((rq, 1)) >= CMP_BLOCK - 1).astype(F32)
    p = e * (anyv / jnp.sum(e, axis=-1, keepdims=True))
    o_cmp = _dot(p.astype(BF16), vc_ref[0, 0])

    psum = p[0:tq] + p[tq:2 * tq] + p[2 * tq:3 * tq] + p[3 * tq:4 * tq]
    p_hi, p_mid, p_lo = _split3(psum)
    ov = ov_ref[...]
    imp = _dot(p_hi, ov) + _dot(p_mid, ov) + _dot(p_lo, ov)
    blk_i = lax.broadcasted_iota(jnp.int32, (LANES, tq), 0)
    blk_f = blk_i.astype(F32)
    qblk = (q0 + lax.broadcasted_iota(jnp.int32, (LANES, tq), 1)) // SLC_BLOCK
    forced = (blk_i == 0) | (blk_i == qblk) | (blk_i == qblk - 1)
    score = jnp.where(blk_i > qblk, -1.0, jnp.where(forced, FORCE_SCORE, imp.T))
    selb = jnp.full((LANES, tq), NEG_INF, F32)
    for _ in range(SLC_TOPK):
        mx = jnp.max(score, axis=0, keepdims=True)
        first = jnp.min(jnp.where(score == mx, blk_f, float(LANES)), axis=0, keepdims=True)
        hit = blk_f == first
        selb = jnp.where(hit, 0.0, selb)
        score = jnp.where(hit, -3e38, score)
    sb = selb.T.astype(BF16)
    q2 = jnp.concatenate([q4, jnp.concatenate([sb] * NSA_REP, axis=0)], axis=1)

    def slc_scores(j):
        start = pl.multiple_of(j * tk, tk)
        kk = jnp.concatenate([ks_ref[0, pl.ds(start, tk), :], e_ref[pl.ds(start, tk), :]], axis=1)
        return _dot_nt(q2, kk), vs_ref[0, pl.ds(start, tk), :]

    jd = q0 // tk
    s, v = slc_scores(jd)
    kpos = jd * tk + lax.broadcasted_iota(jnp.int32, (rq, tk), 1)
    s = jnp.where(kpos <= qpos_of((rq, tk)), s, NEG_INF)
    carry = _attn_first(s, v)

    def slc_body(j, carry):
        s, v = slc_scores(j)
        return _attn_step(carry, s, v)

    _, acc_slc = lax.fori_loop(0, jd, slc_body, carry)

    nw = WINDOW + tq
    wstart = pl.multiple_of(jnp.maximum(q0 - WINDOW, 0), tq)
    s = _dot_nt(q4, kw_ref[0, pl.ds(wstart, nw), :])
    dist = qpos_of((rq, nw)) - (wstart + lax.broadcasted_iota(jnp.int32, (rq, nw), 1))
    s = jnp.where(lax.bitcast_convert_type(dist, jnp.uint32) < WINDOW, s, NEG_INF)
    _, acc_win = _attn_first(s, vw_ref[0, pl.ds(wstart, nw), :])

    lane_r = lax.broadcasted_iota(jnp.int32, (rq, LANES), 1)
    data0 = HALF * g
    ones_lane = HALF - data0

    def normalise(acc):
        l = jnp.sum(jnp.where(lane_r == ones_lane, acc, 0.0), axis=-1, keepdims=True)
        return acc * (1.0 / l)

    o_slc = normalise(acc_slc)
    o_win = normalise(acc_win)
    gt = gt_ref[0]
    lane_g = lax.broadcasted_iota(jnp.int32, (tq, LANES), 1)
    is_data_q = (lane_g >= data0) & (lane_g < data0 + HALF)
    for r in range(NSA_REP):
        col = 3 * (NSA_REP * g + r)
        gate = [jnp.sum(jnp.where(lane_g == col + b, gt, 0.0), axis=-1, keepdims=True) for b in range(3)]
        rows = slice(r * tq, (r + 1) * tq)
        o = gate[0] * o_cmp[rows] + gate[1] * o_slc[rows] + gate[2] * o_win[rows]
        o_ref[0, :, LANES * r:LANES * (r + 1)] = jnp.where(is_data_q, o, 0.0).astype(BF16)


def _nsa(qn, kca, vca, ksa, vsa, kwa, vwa, gt):
    B, S, _ = qn.shape
    G = NSA_KV_HEADS
    tq = min(TQ_NSA, S)
    nc = S // CMP_STRIDE
    n_slc = S // SLC_BLOCK
    assert n_slc <= LANES and S % TK_SLC == 0

    cs = jnp.arange(nc)[:, None] * CMP_STRIDE
    ss = jnp.arange(LANES)[None, :] * SLC_BLOCK
    ovl = jnp.clip(jnp.minimum(cs + CMP_BLOCK, ss + SLC_BLOCK) - jnp.maximum(cs, ss), 0, None)
    valid = (jnp.arange(nc)[:, None] < (S - CMP_BLOCK) // CMP_STRIDE + 1) & (jnp.arange(LANES)[None, :] < n_slc)
    ov = jnp.where(valid, ovl.astype(F32) / CMP_BLOCK, 0.0).astype(BF16)
    e1h = (jnp.arange(S)[:, None] // SLC_BLOCK == jnp.arange(LANES)[None, :]).astype(BF16)

    q_spec = pl.BlockSpec((1, tq, NSA_REP * LANES), lambda b, g, i: (b, i, g))
    c_spec = pl.BlockSpec((1, 1, nc, LANES), lambda b, g, i: (b, g, 0, 0))
    kv_spec = pl.BlockSpec((1, S, LANES), lambda b, g, i: (b, 0, g))
    return pl.pallas_call(
        _nsa_kernel,
        grid=(B, G, S // tq),
        in_specs=[q_spec, c_spec, c_spec, kv_spec, _const_spec(e1h.shape), kv_spec, kv_spec, kv_spec,
                  pl.BlockSpec((1, tq, LANES), lambda b, g, i: (b, i, 0)), _const_spec(ov.shape)],
        out_specs=q_spec,
        out_shape=jax.ShapeDtypeStruct((B, S, NSA_HEADS * LANES), BF16),
        compiler_params=_cparams(("arbitrary", "arbitrary", "arbitrary")),
        name="nsa",
    )(qn, kca, vca, ksa, e1h, vsa, kwa, vwa, gt, ov)


def _fox_kernel(q_ref, k_ref, v_ref, o_ref):
    tq = q_ref.shape[1]
    i = pl.program_id(2)
    lane = lax.broadcasted_iota(jnp.int32, (tq, LANES), 1)
    causal = lax.broadcasted_iota(jnp.int32, (tq, tq), 1) <= lax.broadcasted_iota(jnp.int32, (tq, tq), 0)

    def scores(hh, j):
        cols = slice(LANES * hh, LANES * (hh + 1))
        start = pl.multiple_of(j * tq, tq)
        return _dot_nt(q_ref[0, :, cols], k_ref[0, pl.ds(start, tq), cols]), v_ref[0, pl.ds(start, tq), cols]

    def diag(hh):
        s, v = scores(hh, i)
        return _attn_first(jnp.where(causal, s, NEG_INF), v)

    def body(j, carry):
        return tuple(_attn_step(carry[hh], *scores(hh, j)) for hh in range(2))

    carry = lax.fori_loop(0, i, body, (diag(0), diag(1)))
    outs = []
    for hh in range(2):
        acc = carry[hh][1]
        ones_lane = HALF if hh == 0 else 0
        l = jnp.sum(jnp.where(lane == ones_lane, acc, 0.0), axis=-1, keepdims=True)
        outs.append(acc * (1.0 / l))
    o_ref[0] = jnp.where(lane < HALF, outs[0], outs[1]).astype(BF16)


def _fox(qa, ka, va):
    B, S, _ = qa.shape
    tq = min(TQ_FOX, S)
    q_spec = pl.BlockSpec((1, tq, 2 * LANES), lambda b, h, i: (b, i, h))
    kv_spec = pl.BlockSpec((1, S, 2 * LANES), lambda b, h, i: (b, 0, h))
    return pl.pallas_call(
        _fox_kernel,
        grid=(B, FOX_HEADS // 2, S // tq),
        in_specs=[q_spec, kv_spec, kv_spec],
        out_specs=pl.BlockSpec((1, tq, LANES), lambda b, h, i: (b, i, h)),
        out_shape=jax.ShapeDtypeStruct((B, S, FOX_W), BF16),
        compiler_params=_cparams(("arbitrary", "arbitrary", "arbitrary")),
        name="fox",
    )(qa, ka, va)


def _merge_kernel(x_ref, oa_ref, ob_ref, gab_ref, wa_ref, wb_ref, wo_ref, g2_ref, wr_hi_ref, wr_lo_ref, br_ref,
                  x1_ref, h2_ref, cmb_ref):
    tm = x_ref.shape[0]
    out_a = _dot(oa_ref[...], wa_ref[...])
    out_b = _dot(ob_ref[...], wb_ref[...])
    mix = gab_ref[:, 0:D_MODEL].astype(F32) * out_a + gab_ref[:, D_MODEL:2 * D_MODEL].astype(F32) * out_b
    x1 = x_ref[...] + _dot(mix.astype(BF16), wo_ref[...])
    x1_ref[...] = x1
    h2 = x1 * lax.rsqrt(jnp.mean(x1 * x1, axis=-1, keepdims=True) + RMS_EPS) * g2_ref[...]
    h2_ref[...] = h2.astype(BF16)

    h_hi = h2.astype(BF16)
    h_lo = (h2 - h_hi.astype(F32)).astype(BF16)
    logits = _dot(h_hi, wr_hi_ref[...]) + (_dot(h_hi, wr_lo_ref[...]) + _dot(h_lo, wr_hi_ref[...])) + br_ref[...]
    lane = lax.broadcasted_iota(jnp.int32, (tm, LANES), 1)
    lane_f = lane.astype(F32)

    def first_argmax(vals):
        mx = jnp.max(vals, axis=-1, keepdims=True)
        idx = jnp.min(jnp.where(vals == mx, lane_f, float(LANES)), axis=-1, keepdims=True)
        return mx, idx

    is_grp = (lane >= N_EXPERTS) & (lane < N_EXPERTS + N_GROUPS)
    gl = jnp.where(is_grp, logits, NEG_INF)
    gmax, gidx = first_argmax(gl)
    p_g = 1.0 / jnp.sum(jnp.where(is_grp, jnp.exp(gl - gmax), 0.0), axis=-1, keepdims=True)
    g_top = gidx - float(N_EXPERTS)
    e_lo = g_top * float(EXPERTS_PER_GROUP)
    in_grp = (lane_f >= e_lo) & (lane_f < e_lo + float(EXPERTS_PER_GROUP))
    el = jnp.where(in_grp, logits, NEG_INF)
    m1, i1 = first_argmax(el)
    el2 = jnp.where(lane_f == i1, NEG_INF, el)
    m2, i2 = first_argmax(el2)
    e2 = jnp.exp(m2 - m1)
    w1 = p_g / (1.0 + e2)
    w2 = p_g * e2 / (1.0 + e2)
    cmb_ref[...] = jnp.where(lane_f == i1, w1, jnp.where(lane_f == i2, w2, 0.0))


def _merge(x2d, oa, ob, gab, w_fox_up, w_nsa_up, w_out, norm_ffn_g, w_group, b_group, w_router, b_router):
    T, D = x2d.shape
    tm = min(TM_MERGE, T)
    wa = w_fox_up.astype(BF16)
    wn = w_nsa_up.reshape(NSA_KV_HEADS, NSA_REP, HEAD_DIM, D)
    z = jnp.zeros_like(wn[0])
    wb = jnp.stack([jnp.concatenate([wn[0], z], axis=1), jnp.concatenate([z, wn[1]], axis=1)])
    wb = wb.reshape(NSA_HEADS * LANES, D).astype(BF16)
    wo = w_out.astype(BF16)
    wr = jnp.pad(jnp.concatenate([w_router, w_group], axis=1).astype(F32),
                 ((0, 0), (0, LANES - N_EXPERTS - N_GROUPS)))
    wr_hi = wr.astype(BF16)
    wr_lo = (wr - wr_hi.astype(F32)).astype(BF16)
    br = jnp.pad(jnp.concatenate([b_router, b_group]).astype(F32), (0, LANES - N_EXPERTS - N_GROUPS))[None, :]
    g2 = norm_ffn_g.astype(F32)[None, :]

    row = lambda n: pl.BlockSpec((tm, n), lambda t: (t, 0))
    consts = [wa, wb, wo, g2, wr_hi, wr_lo, br]
    return pl.pallas_call(
        _merge_kernel,
        grid=(T // tm,),
        in_specs=[row(D), row(FOX_W), row(NSA_HEADS * LANES), row(2 * D)] + [_const_spec(a.shape) for a in consts],
        out_specs=[row(D), row(D), row(LANES)],
        out_shape=[jax.ShapeDtypeStruct((T, D), F32), jax.ShapeDtypeStruct((T, D), BF16),
                   jax.ShapeDtypeStruct((T, LANES), F32)],
        compiler_params=_cparams(("arbitrary",)),
        name="merge",
    )(x2d, oa, ob, gab, *consts)


def _moe_kernel(h_ref, x1_ref, cmb_ref, wg_ref, wu_ref, wd_ref, o_ref):
    e = pl.program_id(1)

    @pl.when(e == 0)
    def _():
        o_ref[...] = x1_ref[...]

    h = h_ref[...]
    lane = lax.broadcasted_iota(jnp.int32, cmb_ref.shape, 1)
    c = jnp.sum(jnp.where(lane == e, cmb_ref[...], 0.0), axis=-1, keepdims=True)
    a = _dot(h, wg_ref[0])
    hid = (a * jax.nn.sigmoid(a)) * _dot(h, wu_ref[0]) * c
    o_ref[...] += _dot(hid.astype(BF16), wd_ref[0])


def _moe(h2, x1, cmb, w_gate, w_up, w_down):
    T, D = x1.shape
    tm = min(TM_MOE, T)
    row = lambda n: pl.BlockSpec((tm, n), lambda t, e: (t, 0))
    return pl.pallas_call(
        _moe_kernel,
        grid=(T // tm, N_EXPERTS),
        in_specs=[row(D), row(D), row(LANES),
                  pl.BlockSpec((1, D, D_EXPERT), lambda t, e: (e, 0, 0)),
                  pl.BlockSpec((1, D, D_EXPERT), lambda t, e: (e, 0, 0)),
                  pl.BlockSpec((1, D_EXPERT, D), lambda t, e: (e, 0, 0))],
        out_specs=row(D),
        out_shape=jax.ShapeDtypeStruct((T, D), F32),
        compiler_params=_cparams(("arbitrary", "arbitrary")),
        name="moe",
    )(h2, x1, cmb, w_gate.astype(BF16), w_up.astype(BF16), w_down.astype(BF16))


def kernel(x, norm_mix_g, w_in, b_forget, fox_q_g, fox_k_g, nsa_q_g, nsa_k_g, cmp_k_w1, cmp_k_w2, cmp_k_pos,
           cmp_v_w1, cmp_v_w2, cmp_v_pos, w_fox_up, w_nsa_up, w_out, norm_ffn_g, w_group, b_group, w_router,
           b_router, w_gate, w_up, w_down):
    B, S, D = x.shape
    for l in range(w_in.shape[0]):
        qa, ka, va, qn, kcr, vcr, ksa, vsa, kwa, vwa, gt, gab = _inproj(
            x, norm_mix_g[l], w_in[l], b_forget[l], fox_q_g[l], fox_k_g[l], nsa_q_g[l], nsa_k_g[l])
        kca, vca = _compress(kcr, vcr, cmp_k_w1[l], cmp_k_w2[l], cmp_k_pos[l],
                             cmp_v_w1[l], cmp_v_w2[l], cmp_v_pos[l], nsa_k_g[l])
        ob = _nsa(qn, kca, vca, ksa, vsa, kwa, vwa, gt)
        oa = _fox(qa, ka, va)
        x1, h2, cmb = _merge(x.reshape(B * S, D), oa.reshape(B * S, FOX_W), ob.reshape(B * S, NSA_HEADS * LANES),
                             gab.reshape(B * S, 2 * D), w_fox_up[l], w_nsa_up[l], w_out[l], norm_ffn_g[l],
                             w_group[l], b_group[l], w_router[l], b_router[l])
        x = _moe(h2, x1, cmb, w_gate[l], w_up[l], w_down[l]).reshape(B, S, D)
    return x
```

```python
import functools

import jax
import jax.numpy as jnp
from jax import lax
from jax.experimental import pallas as pl
from jax.experimental.pallas import tpu as pltpu

F32 = jnp.float32
BF16 = jnp.bfloat16

D_MODEL = 1024
HEAD_DIM = 64
FOX_HEADS = 8
NSA_HEADS = 8
NSA_KV_HEADS = 2
NSA_REP = NSA_HEADS // NSA_KV_HEADS
CMP_BLOCK = 32
CMP_STRIDE = 16
CMP_HIDDEN = 256
SLC_BLOCK = 64
SLC_TOPK = 16
WINDOW = 512
N_GROUPS = 4
EXPERTS_PER_GROUP = 4
N_EXPERTS = N_GROUPS * EXPERTS_PER_GROUP
D_EXPERT = 512
RMS_EPS = 1e-6
NEG_INF = -1e30
FORCE_SCORE = 1e4

LANES = 128
HALF = LANES // 2
VMEM_LIMIT = 56 * 1024 * 1024

FOX_W = FOX_HEADS * HEAD_DIM
NSA_W = NSA_HEADS * HEAD_DIM
NSA_KV_W = NSA_KV_HEADS * HEAD_DIM

OFF_FQ = 0
OFF_FK = OFF_FQ + FOX_W
OFF_FV = OFF_FK + FOX_W
OFF_NQ = OFF_FV + FOX_W
OFF_KC = OFF_NQ + NSA_W
OFF_VC = OFF_KC + NSA_KV_W
OFF_KS = OFF_VC + NSA_KV_W
OFF_VS = OFF_KS + NSA_KV_W
OFF_KW = OFF_VS + NSA_KV_W
OFF_VW = OFF_KW + NSA_KV_W
OFF_FF = OFF_VW + NSA_KV_W
OFF_NG = OFF_FF + LANES
OFF_GA = OFF_NG + LANES
N_PROJ = OFF_GA + 2 * D_MODEL

TM_PROJ = 256
TQ_FOX = 512
TQ_NSA = 128
TK_SLC = 512
TM_MERGE = 512
TM_MOE = 1024


def _dot(a, b):
    return jnp.dot(a, b, preferred_element_type=F32)


def _dot_nt(a, b):
    return lax.dot_general(a, b, (((1,), (1,)), ((), ())), preferred_element_type=F32)


def _split3(x):
    hi = x.astype(BF16)
    r = x - hi.astype(F32)
    mid = r.astype(BF16)
    lo = (r - mid.astype(F32)).astype(BF16)
    return hi, mid, lo


def _cparams(sem):
    return pltpu.CompilerParams(dimension_semantics=sem, vmem_limit_bytes=VMEM_LIMIT)


def _const_spec(shape):
    nd = len(shape)
    return pl.BlockSpec(shape, lambda *_: (0,) * nd)


def _inproj_kernel(x_ref, g_ref, w_ref, bf_ref, gqa_ref, gka_ref, gqn_ref, gkn_ref,
                   bd_ref, bd2_ref, tri_ref, selq_ref, selk_ref, cq_ref, ck_ref, cv_ref,
                   nqc_ref, kpc_ref, vone_ref,
                   qa_ref, ka_ref, va_ref, qn_ref, kcr_ref, vcr_ref,
                   ksa_ref, vsa_ref, kwa_ref, vwa_ref, gt_ref, gab_ref,
                   carry_ref):
    tm = x_ref.shape[1]

    @pl.when(pl.program_id(1) == 0)
    def _():
        carry_ref[...] = jnp.zeros_like(carry_ref)

    x = x_ref[0]
    y = x * lax.rsqrt(jnp.mean(x * x, axis=-1, keepdims=True) + RMS_EPS)
    h = (y * g_ref[...]).astype(BF16)

    def proj(off, n):
        return _dot(h, w_ref[:, off:off + n])

    lo_half = lax.broadcasted_iota(jnp.int32, (tm, LANES), 1) < HALF

    def headnorm(z, bd, grow):
        msq = _dot((z * z).astype(BF16), bd[...])
        return z * lax.rsqrt(msq + RMS_EPS) * grow[...]

    def spread_pairs(out_ref, src, aug):
        for m in range(4):
            s = src[:, LANES * m:LANES * (m + 1)]
            out_ref[0, :, LANES * 2 * m:LANES * (2 * m + 1)] = jnp.where(lo_half, s, aug(2 * m)).astype(BF16)
            out_ref[0, :, LANES * (2 * m + 1):LANES * (2 * m + 2)] = jnp.where(lo_half, aug(2 * m + 1), s).astype(BF16)

    zf = proj(OFF_FF, LANES) + bf_ref[...]
    logf = jnp.minimum(zf, 0.0) - jnp.log(1.0 + jnp.exp(-jnp.abs(zf)))
    l_hi, l_mid, l_lo = _split3(logf)
    tri = tri_ref[...]
    cum = carry_ref[...] + (_dot(tri, l_hi) + _dot(tri, l_mid) + _dot(tri, l_lo))
    carry_ref[...] = cum[tm - 1:tm, :]
    pq = jnp.concatenate(_split3(cum), axis=1)
    augq = _dot(pq, selq_ref[...]) + cq_ref[...]
    augk = _dot(pq, selk_ref[...]) + ck_ref[...]

    zq = headnorm(proj(OFF_FQ, FOX_W), bd_ref, gqa_ref)
    spread_pairs(qa_ref, zq, lambda k: augq[:, LANES * k:LANES * (k + 1)])
    zk = headnorm(proj(OFF_FK, FOX_W), bd_ref, gka_ref)
    spread_pairs(ka_ref, zk, lambda k: augk[:, LANES * k:LANES * (k + 1)])
    zv = proj(OFF_FV, FOX_W)
    spread_pairs(va_ref, zv, lambda k: cv_ref[:, LANES * k:LANES * (k + 1)])

    zn = headnorm(proj(OFF_NQ, NSA_W), bd_ref, gqn_ref)
    for m in range(NSA_REP):
        s = zn[:, LANES * m:LANES * (m + 1)]
        c0 = nqc_ref[:, LANES * m:LANES * (m + 1)].astype(F32)
        c1 = nqc_ref[:, LANES * (NSA_REP + m):LANES * (NSA_REP + m + 1)].astype(F32)
        qn_ref[0, :, LANES * m:LANES * (m + 1)] = jnp.where(lo_half, s, c0).astype(BF16)
        qn_ref[0, :, LANES * (NSA_REP + m):LANES * (NSA_REP + m + 1)] = jnp.where(lo_half, c1, s).astype(BF16)

    kcr_ref[0] = proj(OFF_KC, NSA_KV_W).astype(BF16)
    vcr_ref[0] = proj(OFF_VC, NSA_KV_W).astype(BF16)

    kp0 = kpc_ref[:, 0:LANES].astype(F32)
    kp1 = kpc_ref[:, LANES:2 * LANES].astype(F32)

    def kv_pair(k_out, v_out, off_k, off_v):
        zk2 = headnorm(proj(off_k, NSA_KV_W), bd2_ref, gkn_ref)
        k_out[0, :, 0:LANES] = jnp.where(lo_half, zk2, kp0).astype(BF16)
        k_out[0, :, LANES:2 * LANES] = jnp.where(lo_half, kp1, zk2).astype(BF16)
        zv2 = proj(off_v, NSA_KV_W)
        v_out[0, :, 0:LANES] = jnp.where(lo_half, zv2, vone_ref[:, 0:LANES]).astype(BF16)
        v_out[0, :, LANES:2 * LANES] = jnp.where(lo_half, vone_ref[:, LANES:2 * LANES], zv2).astype(BF16)

    kv_pair(ksa_ref, vsa_ref, OFF_KS, OFF_VS)
    kv_pair(kwa_ref, vwa_ref, OFF_KW, OFF_VW)

    gt_ref[0] = jax.nn.sigmoid(proj(OFF_NG, LANES))
    gab_ref[0, :, 0:D_MODEL] = jax.nn.sigmoid(proj(OFF_GA, D_MODEL)).astype(BF16)
    gab_ref[0, :, D_MODEL:2 * D_MODEL] = jax.nn.sigmoid(proj(OFF_GA + D_MODEL, D_MODEL)).astype(BF16)


def _pos_pieces(pos):
    return ((pos // 256) * 256).astype(F32), (pos % 256).astype(F32)


def _inproj(x, norm_g, w_in, b_forget, fox_q_g, fox_k_g, nsa_q_g, nsa_k_g):
    B, S, D = x.shape
    tm = min(TM_PROJ, S)
    scale = HEAD_DIM ** -0.5

    c = [0]
    for n in (FOX_W, FOX_W, FOX_W, FOX_HEADS, NSA_W) + (NSA_KV_W,) * 6 + (3 * NSA_HEADS, D_MODEL, D_MODEL):
        c.append(c[-1] + n)
    fq, fk, fv, ff, nq, kc, vc, ks, vs, kw, vw, ng, ga, gb = [w_in[:, c[i]:c[i + 1]] for i in range(14)]
    perm = jnp.asarray([0, 4, 1, 5, 2, 6, 3, 7])
    nq = nq.reshape(D, NSA_HEADS, HEAD_DIM)[:, perm, :].reshape(D, NSA_W)
    padl = lambda a: jnp.pad(a, ((0, 0), (0, LANES - a.shape[1])))
    w = jnp.concatenate([fq, fk, fv, nq, kc, vc, ks, vs, kw, vw, padl(ff), padl(ng), ga, gb], axis=1).astype(BF16)
    assert w.shape[1] == N_PROJ

    bf = jnp.pad(b_forget.astype(F32), (0, LANES - FOX_HEADS))[None, :]
    gqa = jnp.tile(fox_q_g.astype(F32) * scale, FOX_HEADS)[None, :]
    gka = jnp.tile(fox_k_g.astype(F32), FOX_HEADS)[None, :]
    gqn = jnp.tile(nsa_q_g.astype(F32) * scale, NSA_HEADS)[None, :]
    gkn = jnp.tile(nsa_k_g.astype(F32), NSA_KV_HEADS)[None, :]

    r512 = jnp.arange(FOX_W)
    bd = jnp.where((r512[:, None] // HEAD_DIM) == (r512[None, :] // HEAD_DIM), 1.0 / HEAD_DIM, 0.0).astype(BF16)
    bd2 = bd[:LANES, :LANES]
    rt = jnp.arange(tm)
    tri = (rt[None, :] <= rt[:, None]).astype(BF16)

    heads = jnp.arange(FOX_HEADS)
    base = heads * LANES + jnp.where(heads % 2 == 0, HALF, 0)
    rows = jnp.arange(3 * LANES)
    piece, hlane = rows // LANES, rows % LANES
    cols = jnp.arange(FOX_HEADS * LANES)
    tgt_q = jnp.where(hlane < FOX_HEADS, base[jnp.minimum(hlane, FOX_HEADS - 1)] + 3 + piece, -1)
    tgt_k = jnp.where(hlane < FOX_HEADS, base[jnp.minimum(hlane, FOX_HEADS - 1)] + piece, -1)
    selq = (cols[None, :] == tgt_q[:, None]).astype(BF16)
    selk = -(cols[None, :] == tgt_k[:, None]).astype(BF16)
    off_in_blk = cols - base[cols // LANES]
    cq = ((off_in_blk >= 0) & (off_in_blk < 3)).astype(F32)[None, :]
    ck = ((off_in_blk >= 3) & (off_in_blk < 6)).astype(F32)[None, :]
    cv = (off_in_blk == 0).astype(F32)[None, :]

    pos = jnp.arange(S)
    pa, pb = _pos_pieces(pos)
    blk = cols // LANES
    gq = blk // NSA_REP
    slope = 2.0 ** (-(blk + 1).astype(F32))
    o = cols % LANES - jnp.where(gq == 0, HALF, 0)
    nqc = jnp.where(o[None, :] == 0, slope[None, :], 0.0)
    nqc = jnp.where(o[None, :] == 1, slope[None, :], nqc)
    nqc = jnp.where(o[None, :] == 2, -slope[None, :] * pa[:, None], nqc)
    nqc = jnp.where(o[None, :] == 3, -slope[None, :] * pb[:, None], nqc).astype(BF16)

    kpc, vone = _kv_consts(pa, pb)

    grid = (B, S // tm)
    row_spec = lambda n: pl.BlockSpec((1, tm, n), lambda b, s: (b, s, 0))
    pos_spec = lambda n: pl.BlockSpec((tm, n), lambda b, s: (s, 0))
    consts = [norm_g.astype(F32)[None, :], w, bf, gqa, gka, gqn, gkn, bd, bd2, tri, selq, selk, cq, ck, cv]
    out_widths = [(8 * LANES, BF16)] * 4 + [(LANES, BF16)] * 2 + [(2 * LANES, BF16)] * 4 + \
                 [(LANES, F32), (2 * D_MODEL, BF16)]
    outs = pl.pallas_call(
        _inproj_kernel,
        grid=grid,
        in_specs=[row_spec(D)] + [_const_spec(a.shape) for a in consts] +
                 [pos_spec(8 * LANES), pos_spec(2 * LANES), _const_spec(vone.shape)],
        out_specs=[row_spec(n) for n, _ in out_widths],
        out_shape=[jax.ShapeDtypeStruct((B, S, n), dt) for n, dt in out_widths],
        scratch_shapes=[pltpu.VMEM((1, LANES), F32)],
        compiler_params=_cparams(("arbitrary", "arbitrary")),
        name="inproj",
    )(x, *consts, nqc, kpc, vone)
    return outs


def _kv_consts(pa, pb):
    cols = jnp.arange(2 * LANES)
    o = cols % LANES - jnp.where(cols // LANES == 0, HALF, 0)
    kpc = jnp.where(o[None, :] == 0, pa[:, None], 0.0)
    kpc = jnp.where(o[None, :] == 1, pb[:, None], kpc)
    kpc = jnp.where((o[None, :] == 2) | (o[None, :] == 3), 1.0, kpc).astype(BF16)
    vone = (o == 0).astype(F32)[None, :]
    return kpc, vone


def _compress_kernel(hk_ref, hv_ref, w1k_ref, w1v_ref, posk_ref, posv_ref, w2k_ref, w2v_ref,
                     gk_ref, kcc_ref, vone_ref, kc_ref, vc_ref):
    nc = hk_ref.shape[2]
    half_in = w1k_ref.shape[0] // 2

    def mlp(h_ref, w1_ref, pos_ref, w2_ref):
        hm = h_ref[0, 0]
        top = _dot(hm, w1_ref[0:half_in, :])
        bot = _dot(hm, w1_ref[half_in:2 * half_in, :])
        posw = _dot(pos_ref[...], w1_ref[...])[0:1, :]
        pre = top + pltpu.roll(bot, nc - 1, axis=0) + posw
        act = pre * (0.5 * (1.0 + jnp.tanh(0.7978845608028654 * (pre + 0.044715 * (pre * pre * pre)))))
        return _dot(act.astype(BF16), w2_ref[0])

    kc = mlp(hk_ref, w1k_ref, posk_ref, w2k_ref)
    msq = jnp.sum(kc * kc, axis=-1, keepdims=True) * (1.0 / HEAD_DIM)
    kc_ref[0, 0] = (kc * lax.rsqrt(msq + RMS_EPS) * gk_ref[0] + kcc_ref[0]).astype(BF16)
    vc = mlp(hv_ref, w1v_ref, posv_ref, w2v_ref)
    vc_ref[0, 0] = (vc + vone_ref[0]).astype(BF16)


def _compress(kcr, vcr, cmp_k_w1, cmp_k_w2, cmp_k_pos, cmp_v_w1, cmp_v_w2, cmp_v_pos, nsa_k_g):
    B, S, _ = kcr.shape
    G = NSA_KV_HEADS
    nc = S // CMP_STRIDE
    half_in = CMP_STRIDE * HEAD_DIM

    def halves(t):
        return t.reshape(B, nc, CMP_STRIDE, G, HEAD_DIM).transpose(0, 3, 1, 2, 4).reshape(B, G, nc, half_in)

    def w2_spread(w2):
        z = jnp.zeros_like(w2)
        return jnp.stack([jnp.concatenate([w2, z], 1), jnp.concatenate([z, w2], 1)]).astype(BF16)

    def pos8(p):
        return jnp.tile(p.reshape(1, CMP_BLOCK * HEAD_DIM), (8, 1)).astype(BF16)

    gk = nsa_k_g.astype(F32)
    z = jnp.zeros_like(gk)
    gk2 = jnp.stack([jnp.concatenate([gk, z]), jnp.concatenate([z, gk])])[:, None, :]
    cend = jnp.arange(nc) * CMP_STRIDE + CMP_BLOCK - 1
    kcc, vone = _kv_consts(*_pos_pieces(cend))
    kcc = kcc.astype(F32).reshape(nc, G, LANES).transpose(1, 0, 2)
    vone = vone.reshape(G, 1, LANES)

    blk = pl.BlockSpec((1, 1, nc, half_in), lambda b, g: (b, g, 0, 0))
    per_g = lambda a: pl.BlockSpec((1,) + a.shape[1:], lambda b, g: (g,) + (0,) * (a.ndim - 1))
    w1k, w1v = cmp_k_w1.astype(BF16), cmp_v_w1.astype(BF16)
    pk, pv = pos8(cmp_k_pos), pos8(cmp_v_pos)
    w2k, w2v = w2_spread(cmp_k_w2), w2_spread(cmp_v_w2)
    out_spec = pl.BlockSpec((1, 1, nc, LANES), lambda b, g: (b, g, 0, 0))
    return pl.pallas_call(
        _compress_kernel,
        grid=(B, G),
        in_specs=[blk, blk, _const_spec(w1k.shape), _const_spec(w1v.shape), _const_spec(pk.shape),
                  _const_spec(pv.shape), per_g(w2k), per_g(w2v), per_g(gk2), per_g(kcc), per_g(vone)],
        out_specs=[out_spec, out_spec],
        out_shape=[jax.ShapeDtypeStruct((B, G, nc, LANES), BF16)] * 2,
        compiler_params=_cparams(("arbitrary", "arbitrary")),
        name="compress",
    )(halves(kcr), halves(vcr), w1k, w1v, pk, pv, w2k, w2v, gk2, kcc, vone)


def _attn_first(s, v):
    m = jnp.max(s, axis=-1, keepdims=True)
    p = jnp.exp((s - m).astype(BF16))
    return m, _dot(p, v)


def _attn_step(carry, s, v):
    m, acc = carry
    m_new = jnp.maximum(m, jnp.max(s, axis=-1, keepdims=True))
    p = jnp.exp((s - m_new).astype(BF16))
    return m_new, jnp.exp(m - m_new) * acc + _dot(p, v)


def _nsa_kernel(q_ref, kc_ref, vc_ref, ks_ref, e_ref, vs_ref, kw_ref, vw_ref, gt_ref, ov_ref, o_ref, tiles_ref):
    tq = q_ref.shape[1]
    nc = kc_ref.shape[2]
    tk = TK_SLC
    rq = NSA_REP * tq
    g = pl.program_id(1)
    i = pl.program_id(2)
    q0 = i * tq

    q4 = jnp.concatenate([q_ref[0, :, LANES * r:LANES * (r + 1)] for r in range(NSA_REP)], axis=0)

    def qpos_of(shape):
        return q0 + (lax.broadcasted_iota(jnp.int32, shape, 0) & (tq - 1))

    s = _dot_nt(q4, kc_ref[0, 0])
    qpos = qpos_of((rq, nc))
    cend = lax.broadcasted_iota(jnp.int32, (rq, nc), 1) * CMP_STRIDE + (CMP_BLOCK - 1)
    s = jnp.where(qpos >= cend, s, NEG_INF)
    m = jnp.max(s, axis=-1, keepdims=True)
    e = jnp.exp(s - m)
    anyv = (qpos_of((rq, 1)) >= CMP_BLOCK - 1).astype(F32)
    p = e * (anyv / jnp.sum(e, axis=-1, keepdims=True))
    o_cmp = _dot(p.astype(BF16), vc_ref[0, 0])

    psum = p[0:tq] + p[tq:2 * tq] + p[2 * tq:3 * tq] + p[3 * tq:4 * tq]
    p_hi, p_mid, p_lo = _split3(psum)
    ov = ov_ref[...]
    imp = _dot(p_hi, ov) + _dot(p_mid, ov) + _dot(p_lo, ov)
    blk_i = lax.broadcasted_iota(jnp.int32, (LANES, tq), 0)
    blk_f = blk_i.astype(F32)
    qblk = (q0 + lax.broadcasted_iota(jnp.int32, (LANES, tq), 1)) // SLC_BLOCK
    forced = (blk_i == 0) | (blk_i == qblk) | (blk_i == qblk - 1)
    score = jnp.where(blk_i > qblk, -1.0, jnp.where(forced, FORCE_SCORE, imp.T))
    selb = jnp.full((LANES, tq), NEG_INF, F32)
    for _ in range(SLC_TOPK):
        mx = jnp.max(score, axis=0, keepdims=True)
        first = jnp.min(jnp.where(score == mx, blk_f, float(LANES)), axis=0, keepdims=True)
        hit = blk_f == first
        selb = jnp.where(hit, 0.0, selb)
        score = jnp.where(hit, -3e38, score)
    sb = selb.T.astype(BF16)
    q2 = jnp.concatenate([q4, jnp.concatenate([sb] * NSA_REP, axis=0)], axis=1)

    def slc_scores(j):
        start = pl.multiple_of(j * tk, tk)
        kk = jnp.concatenate([ks_ref[0, pl.ds(start, tk), :], e_ref[pl.ds(start, tk), :]], axis=1)
        return _dot_nt(q2, kk), vs_ref[0, pl.ds(start, tk), :]

    jd = q0 // tk
    s, v = slc_scores(jd)
    kpos = jd * tk + lax.broadcasted_iota(jnp.int32, (rq, tk), 1)
    s = jnp.where(kpos <= qpos_of((rq, tk)), s, NEG_INF)
    carry = _attn_first(s, v)

    blocks_per_tile = tk // SLC_BLOCK
    anysel = jnp.max(selb, axis=1, keepdims=True)
    cnt = jnp.int32(0)
    for j in range(ks_ref.shape[1] // tk):
        tile_sel = jnp.max(anysel[blocks_per_tile * j:blocks_per_tile * (j + 1), :]) > -1.0
        tiles_ref[cnt] = j
        cnt = cnt + (tile_sel & (j < jd)).astype(jnp.int32)

    def slc_body(t, carry):
        s, v = slc_scores(tiles_ref[t])
        return _attn_step(carry, s, v)

    _, acc_slc = lax.fori_loop(0, cnt, slc_body, carry)

    nw = WINDOW + tq
    wstart = pl.multiple_of(jnp.maximum(q0 - WINDOW, 0), tq)
    s = _dot_nt(q4, kw_ref[0, pl.ds(wstart, nw), :])
    dist = qpos_of((rq, nw)) - (wstart + lax.broadcasted_iota(jnp.int32, (rq, nw), 1))
    s = jnp.where(lax.bitcast_convert_type(dist, jnp.uint32) < WINDOW, s, NEG_INF)
    _, acc_win = _attn_first(s, vw_ref[0, pl.ds(wstart, nw), :])

    lane_r = lax.broadcasted_iota(jnp.int32, (rq, LANES), 1)
    data0 = HALF * g
    ones_lane = HALF - data0

    def normalise(acc):
        l = jnp.sum(jnp.where(lane_r == ones_lane, acc, 0.0), axis=-1, keepdims=True)
        return acc * (1.0 / l)

    o_slc = normalise(acc_slc)
    o_win = normalise(acc_win)
    gt = gt_ref[0]
    lane_g = lax.broadcasted_iota(jnp.int32, (tq, LANES), 1)
    is_data_q = (lane_g >= data0) & (lane_g < data0 + HALF)
    for r in range(NSA_REP):
        col = 3 * (NSA_REP * g + r)
        gate = [jnp.sum(jnp.where(lane_g == col + b, gt, 0.0), axis=-1, keepdims=True) for b in range(3)]
        rows = slice(r * tq, (r + 1) * tq)
        o = gate[0] * o_cmp[rows] + gate[1] * o_slc[rows] + gate[2] * o_win[rows]
        o_ref[0, :, LANES * r:LANES * (r + 1)] = jnp.where(is_data_q, o, 0.0).astype(BF16)


def _nsa(qn, kca, vca, ksa, vsa, kwa, vwa, gt):
    B, S, _ = qn.shape
    G = NSA_KV_HEADS
    tq = min(TQ_NSA, S)
    nc = S // CMP_STRIDE
    n_slc = S // SLC_BLOCK
    assert n_slc <= LANES and S % TK_SLC == 0

    cs = jnp.arange(nc)[:, None] * CMP_STRIDE
    ss = jnp.arange(LANES)[None, :] * SLC_BLOCK
    ovl = jnp.clip(jnp.minimum(cs + CMP_BLOCK, ss + SLC_BLOCK) - jnp.maximum(cs, ss), 0, None)
    valid = (jnp.arange(nc)[:, None] < (S - CMP_BLOCK) // CMP_STRIDE + 1) & (jnp.arange(LANES)[None, :] < n_slc)
    ov = jnp.where(valid, ovl.astype(F32) / CMP_BLOCK, 0.0).astype(BF16)
    e1h = (jnp.arange(S)[:, None] // SLC_BLOCK == jnp.arange(LANES)[None, :]).astype(BF16)

    q_spec = pl.BlockSpec((1, tq, NSA_REP * LANES), lambda b, g, i: (b, i, g))
    c_spec = pl.BlockSpec((1, 1, nc, LANES), lambda b, g, i: (b, g, 0, 0))
    kv_spec = pl.BlockSpec((1, S, LANES), lambda b, g, i: (b, 0, g))
    return pl.pallas_call(
        _nsa_kernel,
        grid=(B, G, S // tq),
        in_specs=[q_spec, c_spec, c_spec, kv_spec, _const_spec(e1h.shape), kv_spec, kv_spec, kv_spec,
                  pl.BlockSpec((1, tq, LANES), lambda b, g, i: (b, i, 0)), _const_spec(ov.shape)],
        out_specs=q_spec,
        out_shape=jax.ShapeDtypeStruct((B, S, NSA_HEADS * LANES), BF16),
        scratch_shapes=[pltpu.SMEM((S // TK_SLC + 1,), jnp.int32)],
        compiler_params=_cparams(("arbitrary", "arbitrary", "arbitrary")),
        name="nsa",
    )(qn, kca, vca, ksa, e1h, vsa, kwa, vwa, gt, ov)


def _fox_kernel(q_ref, k_ref, v_ref, o_ref, sa_ref, sb_ref):
    tq = q_ref.shape[1]
    i = pl.program_id(2)
    lane = lax.broadcasted_iota(jnp.int32, (tq, LANES), 1)
    causal = lax.broadcasted_iota(jnp.int32, (tq, tq), 1) <= lax.broadcasted_iota(jnp.int32, (tq, tq), 0)

    def cols(hh):
        return slice(LANES * hh, LANES * (hh + 1))

    def qk(hh, j):
        start = pl.multiple_of(j * tq, tq)
        return _dot_nt(q_ref[0, :, cols(hh)], k_ref[0, pl.ds(start, tq), cols(hh)])

    def vtile(hh, j):
        return v_ref[0, pl.ds(pl.multiple_of(j * tq, tq), tq), cols(hh)]

    n_tiles = i + 1

    def tile_of(n):
        return jnp.where(n == 0, i, n - 1)

    def step_all(carry, s_buf, n):
        t = tile_of(n)
        return tuple(_attn_step(carry[hh], s_buf[hh], vtile(hh, t)) for hh in range(2))

    for hh in range(2):
        sa_ref[hh] = jnp.where(causal, qk(hh, i), NEG_INF)
    init = tuple((jnp.full((tq, 1), NEG_INF, F32), jnp.zeros((tq, LANES), F32)) for hh in range(2))

    def body(p, carry):
        for hh in range(2):
            sb_ref[hh] = qk(hh, tile_of(2 * p + 1))
        carry = step_all(carry, sa_ref, 2 * p)
        nxt = jnp.minimum(2 * p + 2, n_tiles - 1)
        for hh in range(2):
            sa_ref[hh] = qk(hh, tile_of(nxt))
        return step_all(carry, sb_ref, 2 * p + 1)

    carry = lax.fori_loop(0, n_tiles // 2, body, init)
    carry = lax.cond(n_tiles % 2 == 1, lambda c: step_all(c, sa_ref, n_tiles - 1), lambda c: c, carry)
    outs = []
    for hh in range(2):
        acc = carry[hh][1]
        ones_lane = HALF if hh == 0 else 0
        l = jnp.sum(jnp.where(lane == ones_lane, acc, 0.0), axis=-1, keepdims=True)
        outs.append(acc * (1.0 / l))
    o_ref[0] = jnp.where(lane < HALF, outs[0], outs[1]).astype(BF16)


def _fox(qa, ka, va):
    B, S, _ = qa.shape
    tq = min(TQ_FOX, S)
    q_spec = pl.BlockSpec((1, tq, 2 * LANES), lambda b, h, i: (b, i, h))
    kv_spec = pl.BlockSpec((1, S, 2 * LANES), lambda b, h, i: (b, 0, h))
    return pl.pallas_call(
        _fox_kernel,
        grid=(B, FOX_HEADS // 2, S // tq),
        in_specs=[q_spec, kv_spec, kv_spec],
        out_specs=pl.BlockSpec((1, tq, LANES), lambda b, h, i: (b, i, h)),
        out_shape=jax.ShapeDtypeStruct((B, S, FOX_W), BF16),
        scratch_shapes=[pltpu.VMEM((2, tq, tq), F32), pltpu.VMEM((2, tq, tq), F32)],
        compiler_params=_cparams(("arbitrary", "arbitrary", "arbitrary")),
        name="fox",
    )(qa, ka, va)


def _merge_kernel(x_ref, oa_ref, ob_ref, gab_ref, wa_ref, wb_ref, wo_ref, g2_ref, wr_hi_ref, wr_lo_ref, br_ref,
                  x1_ref, h2_ref, cmb_ref):
    tm = x_ref.shape[0]
    out_a = _dot(oa_ref[...], wa_ref[...])
    out_b = _dot(ob_ref[...], wb_ref[...])
    mix = gab_ref[:, 0:D_MODEL].astype(F32) * out_a + gab_ref[:, D_MODEL:2 * D_MODEL].astype(F32) * out_b
    x1 = x_ref[...] + _dot(mix.astype(BF16), wo_ref[...])
    x1_ref[...] = x1
    h2 = x1 * lax.rsqrt(jnp.mean(x1 * x1, axis=-1, keepdims=True) + RMS_EPS) * g2_ref[...]
    h2_ref[...] = h2.astype(BF16)

    h_hi = h2.astype(BF16)
    h_lo = (h2 - h_hi.astype(F32)).astype(BF16)
    logits = _dot(h_hi, wr_hi_ref[...]) + (_dot(h_hi, wr_lo_ref[...]) + _dot(h_lo, wr_hi_ref[...])) + br_ref[...]
    lane = lax.broadcasted_iota(jnp.int32, (tm, LANES), 1)
    lane_f = lane.astype(F32)

    def first_argmax(vals):
        mx = jnp.max(vals, axis=-1, keepdims=True)
        idx = jnp.min(jnp.where(vals == mx, lane_f, float(LANES)), axis=-1, keepdims=True)
        return mx, idx

    is_grp = (lane >= N_EXPERTS) & (lane < N_EXPERTS + N_GROUPS)
    gl = jnp.where(is_grp, logits, NEG_INF)
    gmax, gidx = first_argmax(gl)
    p_g = 1.0 / jnp.sum(jnp.where(is_grp, jnp.exp(gl - gmax), 0.0), axis=-1, keepdims=True)
    g_top = gidx - float(N_EXPERTS)
    e_lo = g_top * float(EXPERTS_PER_GROUP)
    in_grp = (lane_f >= e_lo) & (lane_f < e_lo + float(EXPERTS_PER_GROUP))
    el = jnp.where(in_grp, logits, NEG_INF)
    m1, i1 = first_argmax(el)
    el2 = jnp.where(lane_f == i1, NEG_INF, el)
    m2, i2 = first_argmax(el2)
    e2 = jnp.exp(m2 - m1)
    w1 = p_g / (1.0 + e2)
    w2 = p_g * e2 / (1.0 + e2)
    cmb_ref[...] = jnp.where(lane_f == i1, w1, jnp.where(lane_f == i2, w2, 0.0))


def _merge(x2d, oa, ob, gab, w_fox_up, w_nsa_up, w_out, norm_ffn_g, w_group, b_group, w_router, b_router):
    T, D = x2d.shape
    tm = min(TM_MERGE, T)
    wa = w_fox_up.astype(BF16)
    wn = w_nsa_up.reshape(NSA_KV_HEADS, NSA_REP, HEAD_DIM, D)
    z = jnp.zeros_like(wn[0])
    wb = jnp.stack([jnp.concatenate([wn[0], z], axis=1), jnp.concatenate([z, wn[1]], axis=1)])
    wb = wb.reshape(NSA_HEADS * LANES, D).astype(BF16)
    wo = w_out.astype(BF16)
    wr = jnp.pad(jnp.concatenate([w_router, w_group], axis=1).astype(F32),
                 ((0, 0), (0, LANES - N_EXPERTS - N_GROUPS)))
    wr_hi = wr.astype(BF16)
    wr_lo = (wr - wr_hi.astype(F32)).astype(BF16)
    br = jnp.pad(jnp.concatenate([b_router, b_group]).astype(F32), (0, LANES - N_EXPERTS - N_GROUPS))[None, :]
    g2 = norm_ffn_g.astype(F32)[None, :]

    row = lambda n: pl.BlockSpec((tm, n), lambda t: (t, 0))
    consts = [wa, wb, wo, g2, wr_hi, wr_lo, br]
    return pl.pallas_call(
        _merge_kernel,
        grid=(T // tm,),
        in_specs=[row(D), row(FOX_W), row(NSA_HEADS * LANES), row(2 * D)] + [_const_spec(a.shape) for a in consts],
        out_specs=[row(D), row(D), row(LANES)],
        out_shape=[jax.ShapeDtypeStruct((T, D), F32), jax.ShapeDtypeStruct((T, D), BF16),
                   jax.ShapeDtypeStruct((T, LANES), F32)],
        compiler_params=_cparams(("arbitrary",)),
        name="merge",
    )(x2d, oa, ob, gab, *consts)


def _moe_kernel(h_ref, x1_ref, cmb_ref, wg_ref, wu_ref, wd_ref, o_ref):
    e = pl.program_id(1)

    @pl.when(e == 0)
    def _():
        o_ref[...] = x1_ref[...]

    h = h_ref[...]
    lane = lax.broadcasted_iota(jnp.int32, cmb_ref.shape, 1)
    c = jnp.sum(jnp.where(lane == e, cmb_ref[...], 0.0), axis=-1, keepdims=True)
    a = _dot(h, wg_ref[0])
    hid = (a * jax.nn.sigmoid(a)) * _dot(h, wu_ref[0]) * c
    o_ref[...] += _dot(hid.astype(BF16), wd_ref[0])


def _moe(h2, x1, cmb, w_gate, w_up, w_down):
    T, D = x1.shape
    tm = min(TM_MOE, T)
    row = lambda n: pl.BlockSpec((tm, n), lambda t, e: (t, 0))
    return pl.pallas_call(
        _moe_kernel,
        grid=(T // tm, N_EXPERTS),
        in_specs=[row(D), row(D), row(LANES),
                  pl.BlockSpec((1, D, D_EXPERT), lambda t, e: (e, 0, 0)),
                  pl.BlockSpec((1, D, D_EXPERT), lambda t, e: (e, 0, 0)),
                  pl.BlockSpec((1, D_EXPERT, D), lambda t, e: (e, 0, 0))],
        out_specs=row(D),
        out_shape=jax.ShapeDtypeStruct((T, D), F32),
        compiler_params=_cparams(("arbitrary", "arbitrary")),
        name="moe",
    )(h2, x1, cmb, w_gate.astype(BF16), w_up.astype(BF16), w_down.astype(BF16))


def kernel(x, norm_mix_g, w_in, b_forget, fox_q_g, fox_k_g, nsa_q_g, nsa_k_g, cmp_k_w1, cmp_k_w2, cmp_k_pos,
           cmp_v_w1, cmp_v_w2, cmp_v_pos, w_fox_up, w_nsa_up, w_out, norm_ffn_g, w_group, b_group, w_router,
           b_router, w_gate, w_up, w_down):
    B, S, D = x.shape
    for l in range(w_in.shape[0]):
        qa, ka, va, qn, kcr, vcr, ksa, vsa, kwa, vwa, gt, gab = _inproj(
            x, norm_mix_g[l], w_in[l], b_forget[l], fox_q_g[l], fox_k_g[l], nsa_q_g[l], nsa_k_g[l])
        kca, vca = _compress(kcr, vcr, cmp_k_w1[l], cmp_k_w2[l], cmp_k_pos[l],
                             cmp_v_w1[l], cmp_v_w2[l], cmp_v_pos[l], nsa_k_g[l])
        ob = _nsa(qn, kca, vca, ksa, vsa, kwa, vwa, gt)
        oa = _fox(qa, ka, va)
        x1, h2, cmb = _merge(x.reshape(B * S, D), oa.reshape(B * S, FOX_W), ob.reshape(B * S, NSA_HEADS * LANES),
                             gab.reshape(B * S, 2 * D), w_fox_up[l], w_nsa_up[l], w_out[l], norm_ffn_g[l],
                             w_group[l], b_group[l], w_router[l], b_router[l])
        x = _moe(h2, x1, cmb, w_gate[l], w_up[l], w_down[l]).reshape(B, S, D)
    return x
```

```python
import functools

import jax
import jax.numpy as jnp
from jax import lax
from jax.experimental import pallas as pl
from jax.experimental.pallas import tpu as pltpu

F32 = jnp.float32
BF16 = jnp.bfloat16

D_MODEL = 1024
HEAD_DIM = 64
FOX_HEADS = 8
NSA_HEADS = 8
NSA_KV_HEADS = 2
NSA_REP = NSA_HEADS // NSA_KV_HEADS
CMP_BLOCK = 32
CMP_STRIDE = 16
CMP_HIDDEN = 256
SLC_BLOCK = 64
SLC_TOPK = 16
WINDOW = 512
N_GROUPS = 4
EXPERTS_PER_GROUP = 4
N_EXPERTS = N_GROUPS * EXPERTS_PER_GROUP
D_EXPERT = 512
RMS_EPS = 1e-6
NEG_INF = -1e30
FORCE_SCORE = 1e4

LANES = 128
HALF = LANES // 2
VMEM_LIMIT = 56 * 1024 * 1024

FOX_W = FOX_HEADS * HEAD_DIM
NSA_W = NSA_HEADS * HEAD_DIM
NSA_KV_W = NSA_KV_HEADS * HEAD_DIM

OFF_FQ = 0
OFF_FK = OFF_FQ + FOX_W
OFF_FV = OFF_FK + FOX_W
OFF_NQ = OFF_FV + FOX_W
OFF_KC = OFF_NQ + NSA_W
OFF_VC = OFF_KC + NSA_KV_W
OFF_KS = OFF_VC + NSA_KV_W
OFF_VS = OFF_KS + NSA_KV_W
OFF_KW = OFF_VS + NSA_KV_W
OFF_VW = OFF_KW + NSA_KV_W
OFF_FF = OFF_VW + NSA_KV_W
OFF_NG = OFF_FF + LANES
OFF_GA = OFF_NG + LANES
N_PROJ = OFF_GA + 2 * D_MODEL

TM_PROJ = 256
TQ_FOX = 512
TQ_NSA = 256
TK_SLC = 512
TM_MERGE = 512
TM_EXPERT = 256
TM_DISPATCH = 512


def _dot(a, b):
    return jnp.dot(a, b, preferred_element_type=F32)


def _dot_nt(a, b):
    return lax.dot_general(a, b, (((1,), (1,)), ((), ())), preferred_element_type=F32)


def _split3(x):
    hi = x.astype(BF16)
    r = x - hi.astype(F32)
    mid = r.astype(BF16)
    lo = (r - mid.astype(F32)).astype(BF16)
    return hi, mid, lo


def _cparams(sem):
    return pltpu.CompilerParams(dimension_semantics=sem, vmem_limit_bytes=VMEM_LIMIT)


def _const_spec(shape):
    nd = len(shape)
    return pl.BlockSpec(shape, lambda *_: (0,) * nd)


def _inproj_kernel(x_ref, g_ref, w_ref, bf_ref, gqa_ref, gka_ref, gqn_ref, gkn_ref,
                   bd_ref, bd2_ref, tri_ref, selq_ref, selk_ref, cq_ref, ck_ref, cv_ref,
                   nqc_ref, kpc_ref, vone_ref,
                   qa_ref, ka_ref, va_ref, qn_ref, kcr_ref, vcr_ref,
                   ksa_ref, vsa_ref, kwa_ref, vwa_ref, gt_ref, gab_ref,
                   carry_ref):
    tm = x_ref.shape[1]

    @pl.when(pl.program_id(1) == 0)
    def _():
        carry_ref[...] = jnp.zeros_like(carry_ref)

    x = x_ref[0]
    y = x * lax.rsqrt(jnp.mean(x * x, axis=-1, keepdims=True) + RMS_EPS)
    h = (y * g_ref[...]).astype(BF16)

    def proj(off, n):
        return _dot(h, w_ref[:, off:off + n])

    lo_half = lax.broadcasted_iota(jnp.int32, (tm, LANES), 1) < HALF

    def headnorm(z, bd, grow):
        msq = _dot((z * z).astype(BF16), bd[...])
        return z * lax.rsqrt(msq + RMS_EPS) * grow[...]

    def spread_pairs(out_ref, src, aug):
        for m in range(4):
            s = src[:, LANES * m:LANES * (m + 1)]
            out_ref[0, :, LANES * 2 * m:LANES * (2 * m + 1)] = jnp.where(lo_half, s, aug(2 * m)).astype(BF16)
            out_ref[0, :, LANES * (2 * m + 1):LANES * (2 * m + 2)] = jnp.where(lo_half, aug(2 * m + 1), s).astype(BF16)

    zf = proj(OFF_FF, LANES) + bf_ref[...]
    logf = jnp.minimum(zf, 0.0) - jnp.log(1.0 + jnp.exp(-jnp.abs(zf)))
    l_hi, l_mid, l_lo = _split3(logf)
    tri = tri_ref[...]
    cum = carry_ref[...] + (_dot(tri, l_hi) + _dot(tri, l_mid) + _dot(tri, l_lo))
    carry_ref[...] = cum[tm - 1:tm, :]
    pq = jnp.concatenate(_split3(cum), axis=1)
    augq = _dot(pq, selq_ref[...]) + cq_ref[...]
    augk = _dot(pq, selk_ref[...]) + ck_ref[...]

    zq = headnorm(proj(OFF_FQ, FOX_W), bd_ref, gqa_ref)
    spread_pairs(qa_ref, zq, lambda k: augq[:, LANES * k:LANES * (k + 1)])
    zk = headnorm(proj(OFF_FK, FOX_W), bd_ref, gka_ref)
    spread_pairs(ka_ref, zk, lambda k: augk[:, LANES * k:LANES * (k + 1)])
    zv = proj(OFF_FV, FOX_W)
    spread_pairs(va_ref, zv, lambda k: cv_ref[:, LANES * k:LANES * (k + 1)])

    zn = headnorm(proj(OFF_NQ, NSA_W), bd_ref, gqn_ref)
    for m in range(NSA_REP):
        s = zn[:, LANES * m:LANES * (m + 1)]
        c0 = nqc_ref[:, LANES * m:LANES * (m + 1)].astype(F32)
        c1 = nqc_ref[:, LANES * (NSA_REP + m):LANES * (NSA_REP + m + 1)].astype(F32)
        qn_ref[0, :, LANES * m:LANES * (m + 1)] = jnp.where(lo_half, s, c0).astype(BF16)
        qn_ref[0, :, LANES * (NSA_REP + m):LANES * (NSA_REP + m + 1)] = jnp.where(lo_half, c1, s).astype(BF16)

    kcr_ref[0] = proj(OFF_KC, NSA_KV_W).astype(BF16)
    vcr_ref[0] = proj(OFF_VC, NSA_KV_W).astype(BF16)

    kp0 = kpc_ref[:, 0:LANES].astype(F32)
    kp1 = kpc_ref[:, LANES:2 * LANES].astype(F32)

    def kv_pair(k_out, v_out, off_k, off_v):
        zk2 = headnorm(proj(off_k, NSA_KV_W), bd2_ref, gkn_ref)
        k_out[0, :, 0:LANES] = jnp.where(lo_half, zk2, kp0).astype(BF16)
        k_out[0, :, LANES:2 * LANES] = jnp.where(lo_half, kp1, zk2).astype(BF16)
        zv2 = proj(off_v, NSA_KV_W)
        v_out[0, :, 0:LANES] = jnp.where(lo_half, zv2, vone_ref[:, 0:LANES]).astype(BF16)
        v_out[0, :, LANES:2 * LANES] = jnp.where(lo_half, vone_ref[:, LANES:2 * LANES], zv2).astype(BF16)

    kv_pair(ksa_ref, vsa_ref, OFF_KS, OFF_VS)
    kv_pair(kwa_ref, vwa_ref, OFF_KW, OFF_VW)

    gt_ref[0] = jax.nn.sigmoid(proj(OFF_NG, LANES))
    gab_ref[0, :, 0:D_MODEL] = jax.nn.sigmoid(proj(OFF_GA, D_MODEL)).astype(BF16)
    gab_ref[0, :, D_MODEL:2 * D_MODEL] = jax.nn.sigmoid(proj(OFF_GA + D_MODEL, D_MODEL)).astype(BF16)


def _pos_pieces(pos):
    return ((pos // 256) * 256).astype(F32), (pos % 256).astype(F32)


def _inproj(x, norm_g, w_in, b_forget, fox_q_g, fox_k_g, nsa_q_g, nsa_k_g):
    B, S, D = x.shape
    tm = min(TM_PROJ, S)
    scale = HEAD_DIM ** -0.5

    c = [0]
    for n in (FOX_W, FOX_W, FOX_W, FOX_HEADS, NSA_W) + (NSA_KV_W,) * 6 + (3 * NSA_HEADS, D_MODEL, D_MODEL):
        c.append(c[-1] + n)
    fq, fk, fv, ff, nq, kc, vc, ks, vs, kw, vw, ng, ga, gb = [w_in[:, c[i]:c[i + 1]] for i in range(14)]
    perm = jnp.asarray([0, 4, 1, 5, 2, 6, 3, 7])
    nq = nq.reshape(D, NSA_HEADS, HEAD_DIM)[:, perm, :].reshape(D, NSA_W)
    padl = lambda a: jnp.pad(a, ((0, 0), (0, LANES - a.shape[1])))
    w = jnp.concatenate([fq, fk, fv, nq, kc, vc, ks, vs, kw, vw, padl(ff), padl(ng), ga, gb], axis=1).astype(BF16)
    assert w.shape[1] == N_PROJ

    bf = jnp.pad(b_forget.astype(F32), (0, LANES - FOX_HEADS))[None, :]
    gqa = jnp.tile(fox_q_g.astype(F32) * scale, FOX_HEADS)[None, :]
    gka = jnp.tile(fox_k_g.astype(F32), FOX_HEADS)[None, :]
    gqn = jnp.tile(nsa_q_g.astype(F32) * scale, NSA_HEADS)[None, :]
    gkn = jnp.tile(nsa_k_g.astype(F32), NSA_KV_HEADS)[None, :]

    r512 = jnp.arange(FOX_W)
    bd = jnp.where((r512[:, None] // HEAD_DIM) == (r512[None, :] // HEAD_DIM), 1.0 / HEAD_DIM, 0.0).astype(BF16)
    bd2 = bd[:LANES, :LANES]
    rt = jnp.arange(tm)
    tri = (rt[None, :] <= rt[:, None]).astype(BF16)

    heads = jnp.arange(FOX_HEADS)
    base = heads * LANES + jnp.where(heads % 2 == 0, HALF, 0)
    rows = jnp.arange(3 * LANES)
    piece, hlane = rows // LANES, rows % LANES
    cols = jnp.arange(FOX_HEADS * LANES)
    tgt_q = jnp.where(hlane < FOX_HEADS, base[jnp.minimum(hlane, FOX_HEADS - 1)] + 3 + piece, -1)
    tgt_k = jnp.where(hlane < FOX_HEADS, base[jnp.minimum(hlane, FOX_HEADS - 1)] + piece, -1)
    selq = (cols[None, :] == tgt_q[:, None]).astype(BF16)
    selk = -(cols[None, :] == tgt_k[:, None]).astype(BF16)
    off_in_blk = cols - base[cols // LANES]
    cq = ((off_in_blk >= 0) & (off_in_blk < 3)).astype(F32)[None, :]
    ck = ((off_in_blk >= 3) & (off_in_blk < 6)).astype(F32)[None, :]
    cv = (off_in_blk == 0).astype(F32)[None, :]

    pos = jnp.arange(S)
    pa, pb = _pos_pieces(pos)
    blk = cols // LANES
    gq = blk // NSA_REP
    slope = 2.0 ** (-(blk + 1).astype(F32))
    o = cols % LANES - jnp.where(gq == 0, HALF, 0)
    nqc = jnp.where(o[None, :] == 0, slope[None, :], 0.0)
    nqc = jnp.where(o[None, :] == 1, slope[None, :], nqc)
    nqc = jnp.where(o[None, :] == 2, -slope[None, :] * pa[:, None], nqc)
    nqc = jnp.where(o[None, :] == 3, -slope[None, :] * pb[:, None], nqc).astype(BF16)

    kpc, vone = _kv_consts(pa, pb)

    grid = (B, S // tm)
    row_spec = lambda n: pl.BlockSpec((1, tm, n), lambda b, s: (b, s, 0))
    pos_spec = lambda n: pl.BlockSpec((tm, n), lambda b, s: (s, 0))
    consts = [norm_g.astype(F32)[None, :], w, bf, gqa, gka, gqn, gkn, bd, bd2, tri, selq, selk, cq, ck, cv]
    out_widths = [(8 * LANES, BF16)] * 4 + [(LANES, BF16)] * 2 + [(2 * LANES, BF16)] * 4 + \
                 [(LANES, F32), (2 * D_MODEL, BF16)]
    outs = pl.pallas_call(
        _inproj_kernel,
        grid=grid,
        in_specs=[row_spec(D)] + [_const_spec(a.shape) for a in consts] +
                 [pos_spec(8 * LANES), pos_spec(2 * LANES), _const_spec(vone.shape)],
        out_specs=[row_spec(n) for n, _ in out_widths],
        out_shape=[jax.ShapeDtypeStruct((B, S, n), dt) for n, dt in out_widths],
        scratch_shapes=[pltpu.VMEM((1, LANES), F32)],
        compiler_params=_cparams(("arbitrary", "arbitrary")),
        name="inproj",
    )(x, *consts, nqc, kpc, vone)
    return outs


def _kv_consts(pa, pb):
    cols = jnp.arange(2 * LANES)
    o = cols % LANES - jnp.where(cols // LANES == 0, HALF, 0)
    kpc = jnp.where(o[None, :] == 0, pa[:, None], 0.0)
    kpc = jnp.where(o[None, :] == 1, pb[:, None], kpc)
    kpc = jnp.where((o[None, :] == 2) | (o[None, :] == 3), 1.0, kpc).astype(BF16)
    vone = (o == 0).astype(F32)[None, :]
    return kpc, vone


def _compress_kernel(hk_ref, hv_ref, w1k_ref, w1v_ref, posk_ref, posv_ref, w2k_ref, w2v_ref,
                     gk_ref, kcc_ref, vone_ref, kc_ref, vc_ref):
    nc = hk_ref.shape[2]
    half_in = w1k_ref.shape[0] // 2

    def mlp(h_ref, w1_ref, pos_ref, w2_ref):
        hm = h_ref[0, 0]
        top = _dot(hm, w1_ref[0:half_in, :])
        bot = _dot(hm, w1_ref[half_in:2 * half_in, :])
        posw = _dot(pos_ref[...], w1_ref[...])[0:1, :]
        pre = top + pltpu.roll(bot, nc - 1, axis=0) + posw
        act = pre * (0.5 * (1.0 + jnp.tanh(0.7978845608028654 * (pre + 0.044715 * (pre * pre * pre)))))
        return _dot(act.astype(BF16), w2_ref[0])

    kc = mlp(hk_ref, w1k_ref, posk_ref, w2k_ref)
    msq = jnp.sum(kc * kc, axis=-1, keepdims=True) * (1.0 / HEAD_DIM)
    kc_ref[0, 0] = (kc * lax.rsqrt(msq + RMS_EPS) * gk_ref[0] + kcc_ref[0]).astype(BF16)
    vc = mlp(hv_ref, w1v_ref, posv_ref, w2v_ref)
    vc_ref[0, 0] = (vc + vone_ref[0]).astype(BF16)


def _compress(kcr, vcr, cmp_k_w1, cmp_k_w2, cmp_k_pos, cmp_v_w1, cmp_v_w2, cmp_v_pos, nsa_k_g):
    B, S, _ = kcr.shape
    G = NSA_KV_HEADS
    nc = S // CMP_STRIDE
    half_in = CMP_STRIDE * HEAD_DIM

    def halves(t):
        return t.reshape(B, nc, CMP_STRIDE, G, HEAD_DIM).transpose(0, 3, 1, 2, 4).reshape(B, G, nc, half_in)

    def w2_spread(w2):
        z = jnp.zeros_like(w2)
        return jnp.stack([jnp.concatenate([w2, z], 1), jnp.concatenate([z, w2], 1)]).astype(BF16)

    def pos8(p):
        return jnp.tile(p.reshape(1, CMP_BLOCK * HEAD_DIM), (8, 1)).astype(BF16)

    gk = nsa_k_g.astype(F32)
    z = jnp.zeros_like(gk)
    gk2 = jnp.stack([jnp.concatenate([gk, z]), jnp.concatenate([z, gk])])[:, None, :]
    cend = jnp.arange(nc) * CMP_STRIDE + CMP_BLOCK - 1
    kcc, vone = _kv_consts(*_pos_pieces(cend))
    kcc = kcc.astype(F32).reshape(nc, G, LANES).transpose(1, 0, 2)
    vone = vone.reshape(G, 1, LANES)

    blk = pl.BlockSpec((1, 1, nc, half_in), lambda b, g: (b, g, 0, 0))
    per_g = lambda a: pl.BlockSpec((1,) + a.shape[1:], lambda b, g: (g,) + (0,) * (a.ndim - 1))
    w1k, w1v = cmp_k_w1.astype(BF16), cmp_v_w1.astype(BF16)
    pk, pv = pos8(cmp_k_pos), pos8(cmp_v_pos)
    w2k, w2v = w2_spread(cmp_k_w2), w2_spread(cmp_v_w2)
    out_spec = pl.BlockSpec((1, 1, nc, LANES), lambda b, g: (b, g, 0, 0))
    return pl.pallas_call(
        _compress_kernel,
        grid=(B, G),
        in_specs=[blk, blk, _const_spec(w1k.shape), _const_spec(w1v.shape), _const_spec(pk.shape),
                  _const_spec(pv.shape), per_g(w2k), per_g(w2v), per_g(gk2), per_g(kcc), per_g(vone)],
        out_specs=[out_spec, out_spec],
        out_shape=[jax.ShapeDtypeStruct((B, G, nc, LANES), BF16)] * 2,
        compiler_params=_cparams(("arbitrary", "arbitrary")),
        name="compress",
    )(halves(kcr), halves(vcr), w1k, w1v, pk, pv, w2k, w2v, gk2, kcc, vone)


def _attn_first(s, v):
    m = jnp.max(s, axis=-1, keepdims=True)
    p = jnp.exp((s - m).astype(BF16))
    return m, _dot(p, v)


def _attn_step(carry, s, v):
    m, acc = carry
    m_new = jnp.maximum(m, jnp.max(s, axis=-1, keepdims=True))
    p = jnp.exp((s - m_new).astype(BF16))
    return m_new, jnp.exp(m - m_new) * acc + _dot(p, v)


def _nsa_kernel(q_ref, kc_ref, vc_ref, ks_ref, e_ref, vs_ref, kw_ref, vw_ref, gt_ref, ov_ref, o_ref, tiles_ref):
    tq = q_ref.shape[1]
    nc = kc_ref.shape[2]
    tk = TK_SLC
    rq = NSA_REP * tq
    g = pl.program_id(1)
    i = pl.program_id(2)
    q0 = i * tq

    q4 = jnp.concatenate([q_ref[0, :, LANES * r:LANES * (r + 1)] for r in range(NSA_REP)], axis=0)

    def qpos_of(shape):
        return q0 + (lax.broadcasted_iota(jnp.int32, shape, 0) & (tq - 1))

    def add_mask(s, valid):
        bias = jnp.where(valid, 0.0, NEG_INF)
        return (s.reshape(NSA_REP, tq, s.shape[1]) + bias[None]).reshape(s.shape)

    def qrow(n):
        return q0 + lax.broadcasted_iota(jnp.int32, (tq, n), 0)

    def kcol(n):
        return lax.broadcasted_iota(jnp.int32, (tq, n), 1)

    s = add_mask(_dot_nt(q4, kc_ref[0, 0]), qrow(nc) >= kcol(nc) * CMP_STRIDE + (CMP_BLOCK - 1))
    m = jnp.max(s, axis=-1, keepdims=True)
    e = jnp.exp(s - m)
    anyv = (qpos_of((rq, 1)) >= CMP_BLOCK - 1).astype(F32)
    p = e * (anyv / jnp.sum(e, axis=-1, keepdims=True))
    o_cmp = _dot(p.astype(BF16), vc_ref[0, 0])

    psum = p[0:tq] + p[tq:2 * tq] + p[2 * tq:3 * tq] + p[3 * tq:4 * tq]
    p_hi, p_mid, p_lo = _split3(psum)
    ov = ov_ref[...]
    imp = _dot(p_hi, ov) + _dot(p_mid, ov) + _dot(p_lo, ov)
    blk_i = lax.broadcasted_iota(jnp.int32, (LANES, tq), 0)
    blk_f = blk_i.astype(F32)
    qblk = (q0 + lax.broadcasted_iota(jnp.int32, (LANES, tq), 1)) // SLC_BLOCK
    forced = (blk_i == 0) | (blk_i == qblk) | (blk_i == qblk - 1)
    score = jnp.where(blk_i > qblk, -1.0, jnp.where(forced, FORCE_SCORE, imp.T))
    selb = jnp.full((LANES, tq), NEG_INF, F32)
    for _ in range(SLC_TOPK):
        mx = jnp.max(score, axis=0, keepdims=True)
        first = jnp.min(jnp.where(score == mx, blk_f, float(LANES)), axis=0, keepdims=True)
        hit = blk_f == first
        selb = jnp.where(hit, 0.0, selb)
        score = jnp.where(hit, -3e38, score)
    sb = selb.T.astype(BF16)
    q2 = jnp.concatenate([q4, jnp.concatenate([sb] * NSA_REP, axis=0)], axis=1)

    def slc_scores(j):
        start = pl.multiple_of(j * tk, tk)
        kk = jnp.concatenate([ks_ref[0, pl.ds(start, tk), :], e_ref[pl.ds(start, tk), :]], axis=1)
        return _dot_nt(q2, kk), vs_ref[0, pl.ds(start, tk), :]

    jd = q0 // tk
    s, v = slc_scores(jd)
    carry = _attn_first(add_mask(s, jd * tk + kcol(tk) <= qrow(tk)), v)

    blocks_per_tile = tk // SLC_BLOCK
    anysel = jnp.max(selb, axis=1, keepdims=True)
    cnt = jnp.int32(0)
    for j in range(ks_ref.shape[1] // tk):
        tile_sel = jnp.max(anysel[blocks_per_tile * j:blocks_per_tile * (j + 1), :]) > -1.0
        tiles_ref[cnt] = j
        cnt = cnt + (tile_sel & (j < jd)).astype(jnp.int32)

    def slc_body(t, carry):
        s, v = slc_scores(tiles_ref[t])
        return _attn_step(carry, s, v)

    _, acc_slc = lax.fori_loop(0, cnt, slc_body, carry)

    nw = WINDOW + tq
    wstart = pl.multiple_of(jnp.maximum(q0 - WINDOW, 0), tq)
    dist = qrow(nw) - (wstart + kcol(nw))
    s = add_mask(_dot_nt(q4, kw_ref[0, pl.ds(wstart, nw), :]),
                 lax.bitcast_convert_type(dist, jnp.uint32) < WINDOW)
    _, acc_win = _attn_first(s, vw_ref[0, pl.ds(wstart, nw), :])

    lane_r = lax.broadcasted_iota(jnp.int32, (rq, LANES), 1)
    data0 = HALF * g
    ones_lane = HALF - data0

    def normalise(acc):
        l = jnp.sum(jnp.where(lane_r == ones_lane, acc, 0.0), axis=-1, keepdims=True)
        return acc * (1.0 / l)

    o_slc = normalise(acc_slc)
    o_win = normalise(acc_win)
    gt = gt_ref[0]
    lane_g = lax.broadcasted_iota(jnp.int32, (tq, LANES), 1)
    is_data_q = (lane_g >= data0) & (lane_g < data0 + HALF)
    for r in range(NSA_REP):
        col = 3 * (NSA_REP * g + r)
        gate = [jnp.sum(jnp.where(lane_g == col + b, gt, 0.0), axis=-1, keepdims=True) for b in range(3)]
        rows = slice(r * tq, (r + 1) * tq)
        o = gate[0] * o_cmp[rows] + gate[1] * o_slc[rows] + gate[2] * o_win[rows]
        o_ref[0, :, LANES * r:LANES * (r + 1)] = jnp.where(is_data_q, o, 0.0).astype(BF16)


def _nsa(qn, kca, vca, ksa, vsa, kwa, vwa, gt):
    B, S, _ = qn.shape
    G = NSA_KV_HEADS
    tq = min(TQ_NSA, S)
    nc = S // CMP_STRIDE
    n_slc = S // SLC_BLOCK
    assert n_slc <= LANES and S % TK_SLC == 0

    cs = jnp.arange(nc)[:, None] * CMP_STRIDE
    ss = jnp.arange(LANES)[None, :] * SLC_BLOCK
    ovl = jnp.clip(jnp.minimum(cs + CMP_BLOCK, ss + SLC_BLOCK) - jnp.maximum(cs, ss), 0, None)
    valid = (jnp.arange(nc)[:, None] < (S - CMP_BLOCK) // CMP_STRIDE + 1) & (jnp.arange(LANES)[None, :] < n_slc)
    ov = jnp.where(valid, ovl.astype(F32) / CMP_BLOCK, 0.0).astype(BF16)
    e1h = (jnp.arange(S)[:, None] // SLC_BLOCK == jnp.arange(LANES)[None, :]).astype(BF16)

    q_spec = pl.BlockSpec((1, tq, NSA_REP * LANES), lambda b, g, i: (b, i, g))
    c_spec = pl.BlockSpec((1, 1, nc, LANES), lambda b, g, i: (b, g, 0, 0))
    kv_spec = pl.BlockSpec((1, S, LANES), lambda b, g, i: (b, 0, g))
    return pl.pallas_call(
        _nsa_kernel,
        grid=(B, G, S // tq),
        in_specs=[q_spec, c_spec, c_spec, kv_spec, _const_spec(e1h.shape), kv_spec, kv_spec, kv_spec,
                  pl.BlockSpec((1, tq, LANES), lambda b, g, i: (b, i, 0)), _const_spec(ov.shape)],
        out_specs=q_spec,
        out_shape=jax.ShapeDtypeStruct((B, S, NSA_HEADS * LANES), BF16),
        scratch_shapes=[pltpu.SMEM((S // TK_SLC + 1,), jnp.int32)],
        compiler_params=_cparams(("arbitrary", "arbitrary", "arbitrary")),
        name="nsa",
    )(qn, kca, vca, ksa, e1h, vsa, kwa, vwa, gt, ov)


def _fox_kernel(q_ref, k_ref, v_ref, o_ref, sa_ref, sb_ref):
    tq = q_ref.shape[1]
    i = pl.program_id(2)
    lane = lax.broadcasted_iota(jnp.int32, (tq, LANES), 1)
    causal = lax.broadcasted_iota(jnp.int32, (tq, tq), 1) <= lax.broadcasted_iota(jnp.int32, (tq, tq), 0)

    def cols(hh):
        return slice(LANES * hh, LANES * (hh + 1))

    def qk(hh, j):
        start = pl.multiple_of(j * tq, tq)
        return _dot_nt(q_ref[0, :, cols(hh)], k_ref[0, pl.ds(start, tq), cols(hh)])

    def vtile(hh, j):
        return v_ref[0, pl.ds(pl.multiple_of(j * tq, tq), tq), cols(hh)]

    n_tiles = i + 1

    def tile_of(n):
        return jnp.where(n == 0, i, n - 1)

    def step_all(carry, s_buf, n):
        t = tile_of(n)
        return tuple(_attn_step(carry[hh], s_buf[hh], vtile(hh, t)) for hh in range(2))

    for hh in range(2):
        sa_ref[hh] = jnp.where(causal, qk(hh, i), NEG_INF)
    init = tuple((jnp.full((tq, 1), NEG_INF, F32), jnp.zeros((tq, LANES), F32)) for hh in range(2))

    def body(p, carry):
        for hh in range(2):
            sb_ref[hh] = qk(hh, tile_of(2 * p + 1))
        carry = step_all(carry, sa_ref, 2 * p)
        nxt = jnp.minimum(2 * p + 2, n_tiles - 1)
        for hh in range(2):
            sa_ref[hh] = qk(hh, tile_of(nxt))
        return step_all(carry, sb_ref, 2 * p + 1)

    carry = lax.fori_loop(0, n_tiles // 2, body, init)
    carry = lax.cond(n_tiles % 2 == 1, lambda c: step_all(c, sa_ref, n_tiles - 1), lambda c: c, carry)
    outs = []
    for hh in range(2):
        acc = carry[hh][1]
        ones_lane = HALF if hh == 0 else 0
        l = jnp.sum(jnp.where(lane == ones_lane, acc, 0.0), axis=-1, keepdims=True)
        outs.append(acc * (1.0 / l))
    o_ref[0] = jnp.where(lane < HALF, outs[0], outs[1]).astype(BF16)


def _fox(qa, ka, va):
    B, S, _ = qa.shape
    tq = min(TQ_FOX, S)
    q_spec = pl.BlockSpec((1, tq, 2 * LANES), lambda b, h, i: (b, i, h))
    kv_spec = pl.BlockSpec((1, S, 2 * LANES), lambda b, h, i: (b, 0, h))
    return pl.pallas_call(
        _fox_kernel,
        grid=(B, FOX_HEADS // 2, S // tq),
        in_specs=[q_spec, kv_spec, kv_spec],
        out_specs=pl.BlockSpec((1, tq, LANES), lambda b, h, i: (b, i, h)),
        out_shape=jax.ShapeDtypeStruct((B, S, FOX_W), BF16),
        scratch_shapes=[pltpu.VMEM((2, tq, tq), F32), pltpu.VMEM((2, tq, tq), F32)],
        compiler_params=_cparams(("arbitrary", "arbitrary", "arbitrary")),
        name="fox",
    )(qa, ka, va)


R_E1, R_E2, R_W1, R_W2, R_RANK1, R_RANK2 = range(6)


def _merge_kernel(x_ref, oa_ref, ob_ref, gab_ref, wa_ref, wb_ref, wo_ref, g2_ref, wr_hi_ref, wr_lo_ref, br_ref,
                  ltri_ref, x1_ref, route_ref, cnt_ref, carry_ref):
    tm = x_ref.shape[1]

    @pl.when((pl.program_id(0) == 0) & (pl.program_id(1) == 0))
    def _():
        carry_ref[...] = jnp.zeros_like(carry_ref)

    out_a = _dot(oa_ref[0], wa_ref[...])
    out_b = _dot(ob_ref[0], wb_ref[...])
    mix = gab_ref[0, :, 0:D_MODEL].astype(F32) * out_a + gab_ref[0, :, D_MODEL:2 * D_MODEL].astype(F32) * out_b
    x1 = x_ref[0] + _dot(mix.astype(BF16), wo_ref[...])
    x1_ref[...] = x1
    h2 = x1 * lax.rsqrt(jnp.mean(x1 * x1, axis=-1, keepdims=True) + RMS_EPS) * g2_ref[...]

    h_hi = h2.astype(BF16)
    h_lo = (h2 - h_hi.astype(F32)).astype(BF16)
    logits = _dot(h_hi, wr_hi_ref[...]) + (_dot(h_hi, wr_lo_ref[...]) + _dot(h_lo, wr_hi_ref[...])) + br_ref[...]
    lane = lax.broadcasted_iota(jnp.int32, (tm, LANES), 1)
    lane_f = lane.astype(F32)

    def first_argmax(vals):
        mx = jnp.max(vals, axis=-1, keepdims=True)
        idx = jnp.min(jnp.where(vals == mx, lane_f, float(LANES)), axis=-1, keepdims=True)
        return mx, idx

    is_grp = (lane >= N_EXPERTS) & (lane < N_EXPERTS + N_GROUPS)
    gl = jnp.where(is_grp, logits, NEG_INF)
    gmax, gidx = first_argmax(gl)
    p_g = 1.0 / jnp.sum(jnp.where(is_grp, jnp.exp(gl - gmax), 0.0), axis=-1, keepdims=True)
    e_lo = (gidx - float(N_EXPERTS)) * float(EXPERTS_PER_GROUP)
    in_grp = (lane_f >= e_lo) & (lane_f < e_lo + float(EXPERTS_PER_GROUP))
    el = jnp.where(in_grp, logits, NEG_INF)
    m1, i1 = first_argmax(el)
    m2, i2 = first_argmax(jnp.where(lane_f == i1, NEG_INF, el))
    e2 = jnp.exp(m2 - m1)
    w1 = p_g / (1.0 + e2)
    w2 = p_g * e2 / (1.0 + e2)

    hit1 = lane_f == i1
    hit2 = lane_f == i2
    onehot = jnp.where(hit1 | hit2, 1.0, 0.0)
    before = carry_ref[...] + _dot(ltri_ref[...], onehot.astype(BF16))
    rank1 = jnp.sum(jnp.where(hit1, before, 0.0), axis=-1, keepdims=True)
    rank2 = jnp.sum(jnp.where(hit2, before, 0.0), axis=-1, keepdims=True)
    total = carry_ref[...] + jnp.sum(onehot, axis=0, keepdims=True)
    carry_ref[...] = total
    cnt_ref[...] = total

    rec = jnp.zeros((tm, LANES), F32)
    for k, val in ((R_E1, i1), (R_E2, i2), (R_W1, w1), (R_W2, w2), (R_RANK1, rank1), (R_RANK2, rank2)):
        rec = jnp.where(lane == k, val, rec)
    route_ref[...] = rec


def _merge(x, oa, ob, gab, w_fox_up, w_nsa_up, w_out, norm_ffn_g, w_group, b_group, w_router, b_router):
    B, S, D = x.shape
    tm = min(TM_MERGE, S)
    wa = w_fox_up.astype(BF16)
    wn = w_nsa_up.reshape(NSA_KV_HEADS, NSA_REP, HEAD_DIM, D)
    z = jnp.zeros_like(wn[0])
    wb = jnp.stack([jnp.concatenate([wn[0], z], axis=1), jnp.concatenate([z, wn[1]], axis=1)])
    wb = wb.reshape(NSA_HEADS * LANES, D).astype(BF16)
    wo = w_out.astype(BF16)
    wr = jnp.pad(jnp.concatenate([w_router, w_group], axis=1).astype(F32),
                 ((0, 0), (0, LANES - N_EXPERTS - N_GROUPS)))
    wr_hi = wr.astype(BF16)
    wr_lo = (wr - wr_hi.astype(F32)).astype(BF16)
    br = jnp.pad(jnp.concatenate([b_router, b_group]).astype(F32), (0, LANES - N_EXPERTS - N_GROUPS))[None, :]
    g2 = norm_ffn_g.astype(F32)[None, :]
    rt = jnp.arange(tm)
    ltri = (rt[None, :] < rt[:, None]).astype(BF16)

    row = lambda n: pl.BlockSpec((1, tm, n), lambda b, s: (b, s, 0))
    flat = lambda n: pl.BlockSpec((tm, n), lambda b, s: (b * (S // tm) + s, 0))
    consts = [wa, wb, wo, g2, wr_hi, wr_lo, br, ltri]
    return pl.pallas_call(
        _merge_kernel,
        grid=(B, S // tm),
        in_specs=[row(D), row(FOX_W), row(NSA_HEADS * LANES), row(2 * D)] + [_const_spec(a.shape) for a in consts],
        out_specs=[flat(D), flat(LANES), _const_spec((1, LANES))],
        out_shape=[jax.ShapeDtypeStruct((B * S, D), F32), jax.ShapeDtypeStruct((B * S, LANES), F32),
                   jax.ShapeDtypeStruct((1, LANES), F32)],
        scratch_shapes=[pltpu.VMEM((1, LANES), F32)],
        compiler_params=_cparams(("arbitrary", "arbitrary")),
        name="merge",
    )(x, oa, ob, gab, *consts)


def _row_copy(src_ref, src_row, dst_ref, dst_row, sem):
    return pltpu.make_async_copy(src_ref.at[pl.ds(src_row, 1), :], dst_ref.at[pl.ds(dst_row, 1), :], sem)


def _dispatch_kernel(row1_ref, row2_ref, x1_ref, xs_in_ref, xs_ref, sem):
    del xs_in_ref
    tm = x1_ref.shape[0]
    base = pl.program_id(0) * tm

    def start(t, _):
        _row_copy(x1_ref, t, xs_ref, row1_ref[base + t], sem).start()
        _row_copy(x1_ref, t, xs_ref, row2_ref[base + t], sem).start()
        return _

    def wait(t, _):
        _row_copy(x1_ref, t, xs_ref, row1_ref[base + t], sem).wait()
        _row_copy(x1_ref, t, xs_ref, row2_ref[base + t], sem).wait()
        return _

    lax.fori_loop(0, tm, start, 0, unroll=8)
    lax.fori_loop(0, tm, wait, 0, unroll=8)


def _experts_kernel(tile_e_ref, n_used_ref, xs_ref, g2_ref, wg_ref, wu_ref, wd_ref, ys_ref):
    k = pl.program_id(0)

    @pl.when(k < n_used_ref[0])
    def _():
        x = xs_ref[...]
        h = (x * lax.rsqrt(jnp.mean(x * x, axis=-1, keepdims=True) + RMS_EPS) * g2_ref[...]).astype(BF16)
        a = _dot(h, wg_ref[0])
        hid = (a * jax.nn.sigmoid(a)) * _dot(h, wu_ref[0])
        ys_ref[...] = _dot(hid.astype(BF16), wd_ref[0])

    @pl.when(k >= n_used_ref[0])
    def _():
        ys_ref[...] = jnp.zeros_like(ys_ref)


def _combine_kernel(row1_ref, row2_ref, x1_ref, route_ref, ys_ref, o_ref, y1_ref, y2_ref, sem):
    tm = x1_ref.shape[0]
    base = pl.program_id(0) * tm

    def start(t, _):
        _row_copy(ys_ref, row1_ref[base + t], y1_ref, t, sem).start()
        _row_copy(ys_ref, row2_ref[base + t], y2_ref, t, sem).start()
        return _

    def wait(t, _):
        _row_copy(ys_ref, row1_ref[base + t], y1_ref, t, sem).wait()
        _row_copy(ys_ref, row2_ref[base + t], y2_ref, t, sem).wait()
        return _

    lax.fori_loop(0, tm, start, 0, unroll=8)
    lax.fori_loop(0, tm, wait, 0, unroll=8)
    rec = route_ref[...]
    lane = lax.broadcasted_iota(jnp.int32, rec.shape, 1)
    w1 = jnp.sum(jnp.where(lane == R_W1, rec, 0.0), axis=-1, keepdims=True)
    w2 = jnp.sum(jnp.where(lane == R_W2, rec, 0.0), axis=-1, keepdims=True)
    o_ref[0] = x1_ref[...] + (w1 * y1_ref[...] + w2 * y2_ref[...])


def _moe(x1, route, cnt, norm_ffn_g, w_gate, w_up, w_down, B, S):
    T, D = x1.shape
    tme = min(TM_EXPERT, T)
    tmd = min(TM_DISPATCH, S)

    counts = cnt[0, :N_EXPERTS].astype(jnp.int32)
    tiles_per_e = (counts + tme - 1) // tme
    tile_end = jnp.cumsum(tiles_per_e)
    offs = (tile_end - tiles_per_e) * tme
    e1 = route[:, R_E1].astype(jnp.int32)
    e2 = route[:, R_E2].astype(jnp.int32)
    row1 = offs[e1] + route[:, R_RANK1].astype(jnp.int32)
    row2 = offs[e2] + route[:, R_RANK2].astype(jnp.int32)
    max_tiles = (2 * T) // tme + N_EXPERTS
    tile_e = jnp.minimum(jnp.searchsorted(tile_end, jnp.arange(max_tiles), side="right"), N_EXPERTS - 1)
    tile_e = tile_e.astype(jnp.int32)
    n_used = tile_end[-1:].astype(jnp.int32)
    n_rows = max_tiles * tme

    xs = pl.pallas_call(
        _dispatch_kernel,
        grid_spec=pltpu.PrefetchScalarGridSpec(
            num_scalar_prefetch=2,
            grid=(T // tmd,),
            in_specs=[pl.BlockSpec((tmd, D), lambda t, r1, r2: (t, 0)), pl.BlockSpec(memory_space=pl.ANY)],
            out_specs=pl.BlockSpec(memory_space=pl.ANY),
            scratch_shapes=[pltpu.SemaphoreType.DMA(())],
        ),
        out_shape=jax.ShapeDtypeStruct((n_rows, D), F32),
        input_output_aliases={3: 0},
        compiler_params=_cparams(("arbitrary",)),
        name="dispatch",
    )(row1, row2, x1, jnp.zeros((n_rows, D), F32))

    g2 = norm_ffn_g.astype(F32)[None, :]
    w_spec = lambda shape: pl.BlockSpec((1,) + shape, lambda k, te, nu: (te[k], 0, 0))
    ys = pl.pallas_call(
        _experts_kernel,
        grid_spec=pltpu.PrefetchScalarGridSpec(
            num_scalar_prefetch=2,
            grid=(max_tiles,),
            in_specs=[pl.BlockSpec((tme, D), lambda k, te, nu: (k, 0)),
                      pl.BlockSpec((1, D), lambda k, te, nu: (0, 0)),
                      w_spec((D, D_EXPERT)), w_spec((D, D_EXPERT)), w_spec((D_EXPERT, D))],
            out_specs=pl.BlockSpec((tme, D), lambda k, te, nu: (k, 0)),
        ),
        out_shape=jax.ShapeDtypeStruct((n_rows, D), F32),
        compiler_params=_cparams(("arbitrary",)),
        name="experts",
    )(tile_e, n_used, xs, g2, w_gate.astype(BF16), w_up.astype(BF16), w_down.astype(BF16))

    return pl.pallas_call(
        _combine_kernel,
        grid_spec=pltpu.PrefetchScalarGridSpec(
            num_scalar_prefetch=2,
            grid=(T // tmd,),
            in_specs=[pl.BlockSpec((tmd, D), lambda t, r1, r2: (t, 0)),
                      pl.BlockSpec((tmd, LANES), lambda t, r1, r2: (t, 0)),
                      pl.BlockSpec(memory_space=pl.ANY)],
            out_specs=pl.BlockSpec((1, tmd, D), lambda t, r1, r2: (t // (S // tmd), t % (S // tmd), 0)),
            scratch_shapes=[pltpu.VMEM((tmd, D), F32), pltpu.VMEM((tmd, D), F32), pltpu.SemaphoreType.DMA(())],
        ),
        out_shape=jax.ShapeDtypeStruct((B, S, D), F32),
        compiler_params=_cparams(("arbitrary",)),
        name="combine",
    )(row1, row2, x1, route, ys)


def kernel(x, norm_mix_g, w_in, b_forget, fox_q_g, fox_k_g, nsa_q_g, nsa_k_g, cmp_k_w1, cmp_k_w2, cmp_k_pos,
           cmp_v_w1, cmp_v_w2, cmp_v_pos, w_fox_up, w_nsa_up, w_out, norm_ffn_g, w_group, b_group, w_router,
           b_router, w_gate, w_up, w_down):
    B, S, D = x.shape
    for l in range(w_in.shape[0]):
        qa, ka, va, qn, kcr, vcr, ksa, vsa, kwa, vwa, gt, gab = _inproj(
            x, norm_mix_g[l], w_in[l], b_forget[l], fox_q_g[l], fox_k_g[l], nsa_q_g[l], nsa_k_g[l])
        kca, vca = _compress(kcr, vcr, cmp_k_w1[l], cmp_k_w2[l], cmp_k_pos[l],
                             cmp_v_w1[l], cmp_v_w2[l], cmp_v_pos[l], nsa_k_g[l])
        ob = _nsa(qn, kca, vca, ksa, vsa, kwa, vwa, gt)
        oa = _fox(qa, ka, va)
        x1, route, cnt = _merge(x, oa, ob, gab, w_fox_up[l], w_nsa_up[l], w_out[l], norm_ffn_g[l],
                                w_group[l], b_group[l], w_router[l], b_router[l])
        x = _moe(x1, route, cnt, norm_ffn_g[l], w_gate[l], w_up[l], w_down[l], B, S)
    return x
```

```python
import functools

import jax
import jax.numpy as jnp
from jax import lax
from jax.experimental import pallas as pl
from jax.experimental.pallas import tpu as pltpu

F32 = jnp.float32
BF16 = jnp.bfloat16

D_MODEL = 1024
HEAD_DIM = 64
FOX_HEADS = 8
NSA_HEADS = 8
NSA_KV_HEADS = 2
NSA_REP = NSA_HEADS // NSA_KV_HEADS
CMP_BLOCK = 32
CMP_STRIDE = 16
CMP_HIDDEN = 256
SLC_BLOCK = 64
SLC_TOPK = 16
WINDOW = 512
N_GROUPS = 4
EXPERTS_PER_GROUP = 4
N_EXPERTS = N_GROUPS * EXPERTS_PER_GROUP
D_EXPERT = 512
RMS_EPS = 1e-6
NEG_INF = -1e30
FORCE_SCORE = 1e4

LANES = 128
HALF = LANES // 2
VMEM_LIMIT = 56 * 1024 * 1024

FOX_W = FOX_HEADS * HEAD_DIM
NSA_W = NSA_HEADS * HEAD_DIM
NSA_KV_W = NSA_KV_HEADS * HEAD_DIM

OFF_FQ = 0
OFF_FK = OFF_FQ + FOX_W
OFF_FV = OFF_FK + FOX_W
OFF_NQ = OFF_FV + FOX_W
OFF_KC = OFF_NQ + NSA_W
OFF_VC = OFF_KC + NSA_KV_W
OFF_KS = OFF_VC + NSA_KV_W
OFF_VS = OFF_KS + NSA_KV_W
OFF_KW = OFF_VS + NSA_KV_W
OFF_VW = OFF_KW + NSA_KV_W
OFF_FF = OFF_VW + NSA_KV_W
OFF_NG = OFF_FF + LANES
OFF_GA = OFF_NG + LANES
N_PROJ = OFF_GA + 2 * D_MODEL

TM_PROJ = 256
TQ_FOX = 512
TQ_NSA = 256
TK_SLC = 256
TM_MERGE = 512
TM_EXPERT = 256
TM_DISPATCH = 512


def _dot(a, b):
    return jnp.dot(a, b, preferred_element_type=F32)


def _dot_nt(a, b):
    return lax.dot_general(a, b, (((1,), (1,)), ((), ())), preferred_element_type=F32)


def _split3(x):
    hi = x.astype(BF16)
    r = x - hi.astype(F32)
    mid = r.astype(BF16)
    lo = (r - mid.astype(F32)).astype(BF16)
    return hi, mid, lo


def _cparams(sem):
    return pltpu.CompilerParams(dimension_semantics=sem, vmem_limit_bytes=VMEM_LIMIT)


def _const_spec(shape):
    nd = len(shape)
    return pl.BlockSpec(shape, lambda *_: (0,) * nd)


def _inproj_kernel(x_ref, g_ref, w_ref, bf_ref, gqa_ref, gka_ref, gqn_ref, gkn_ref,
                   bd_ref, bd2_ref, tri_ref, selq_ref, selk_ref, cq_ref, ck_ref, cv_ref,
                   qc0_ref, qca_ref, qcb_ref, kc0_ref, kca_ref, kcb_ref, vone_ref,
                   qa_ref, ka_ref, va_ref, qn_ref, kcr_ref, vcr_ref,
                   ksa_ref, vsa_ref, kwa_ref, vwa_ref, gt_ref, gab_ref,
                   carry_ref):
    tm = x_ref.shape[1]

    @pl.when(pl.program_id(1) == 0)
    def _():
        carry_ref[...] = jnp.zeros_like(carry_ref)

    x = x_ref[0]
    y = x * lax.rsqrt(jnp.mean(x * x, axis=-1, keepdims=True) + RMS_EPS)
    h = (y * g_ref[...]).astype(BF16)

    def proj(off, n):
        return _dot(h, w_ref[:, off:off + n])

    lo_half = lax.broadcasted_iota(jnp.int32, (tm, LANES), 1) < HALF

    pos = pl.program_id(1) * tm + lax.broadcasted_iota(jnp.int32, (tm, 1), 0)
    pos_a = ((pos >> 8) << 8).astype(F32)
    pos_b = (pos & 255).astype(F32)

    def pos_channels(c0_ref, ca_ref, cb_ref, k):
        blk = slice(LANES * k, LANES * (k + 1))
        return c0_ref[:, blk] + ca_ref[:, blk] * pos_a + cb_ref[:, blk] * pos_b

    def headnorm(z, bd, grow):
        msq = _dot((z * z).astype(BF16), bd[...])
        return z * lax.rsqrt(msq + RMS_EPS) * grow[...]

    def spread_pairs(out_ref, src, aug):
        for m in range(4):
            s = src[:, LANES * m:LANES * (m + 1)]
            out_ref[0, :, LANES * 2 * m:LANES * (2 * m + 1)] = jnp.where(lo_half, s, aug(2 * m)).astype(BF16)
            out_ref[0, :, LANES * (2 * m + 1):LANES * (2 * m + 2)] = jnp.where(lo_half, aug(2 * m + 1), s).astype(BF16)

    zf = proj(OFF_FF, LANES) + bf_ref[...]
    logf = jnp.minimum(zf, 0.0) - jnp.log(1.0 + jnp.exp(-jnp.abs(zf)))
    l_hi, l_mid, l_lo = _split3(logf)
    tri = tri_ref[...]
    cum = carry_ref[...] + (_dot(tri, l_hi) + _dot(tri, l_mid) + _dot(tri, l_lo))
    carry_ref[...] = cum[tm - 1:tm, :]
    pq = jnp.concatenate(_split3(cum), axis=1)
    augq = _dot(pq, selq_ref[...]) + cq_ref[...]
    augk = _dot(pq, selk_ref[...]) + ck_ref[...]

    zq = headnorm(proj(OFF_FQ, FOX_W), bd_ref, gqa_ref)
    spread_pairs(qa_ref, zq, lambda k: augq[:, LANES * k:LANES * (k + 1)])
    zk = headnorm(proj(OFF_FK, FOX_W), bd_ref, gka_ref)
    spread_pairs(ka_ref, zk, lambda k: augk[:, LANES * k:LANES * (k + 1)])
    zv = proj(OFF_FV, FOX_W)
    spread_pairs(va_ref, zv, lambda k: cv_ref[:, LANES * k:LANES * (k + 1)])

    zn = headnorm(proj(OFF_NQ, NSA_W), bd_ref, gqn_ref)
    for m in range(NSA_REP):
        s = zn[:, LANES * m:LANES * (m + 1)]
        c0 = pos_channels(qc0_ref, qca_ref, qcb_ref, m)
        c1 = pos_channels(qc0_ref, qca_ref, qcb_ref, NSA_REP + m)
        qn_ref[0, :, LANES * m:LANES * (m + 1)] = jnp.where(lo_half, s, c0).astype(BF16)
        qn_ref[0, :, LANES * (NSA_REP + m):LANES * (NSA_REP + m + 1)] = jnp.where(lo_half, c1, s).astype(BF16)

    kcr_ref[0] = proj(OFF_KC, NSA_KV_W).astype(BF16)
    vcr_ref[0] = proj(OFF_VC, NSA_KV_W).astype(BF16)

    kp0 = pos_channels(kc0_ref, kca_ref, kcb_ref, 0)
    kp1 = pos_channels(kc0_ref, kca_ref, kcb_ref, 1)

    def kv_pair(k_out, v_out, off_k, off_v):
        zk2 = headnorm(proj(off_k, NSA_KV_W), bd2_ref, gkn_ref)
        k_out[0, :, 0:LANES] = jnp.where(lo_half, zk2, kp0).astype(BF16)
        k_out[0, :, LANES:2 * LANES] = jnp.where(lo_half, kp1, zk2).astype(BF16)
        zv2 = proj(off_v, NSA_KV_W)
        v_out[0, :, 0:LANES] = jnp.where(lo_half, zv2, vone_ref[:, 0:LANES]).astype(BF16)
        v_out[0, :, LANES:2 * LANES] = jnp.where(lo_half, vone_ref[:, LANES:2 * LANES], zv2).astype(BF16)

    kv_pair(ksa_ref, vsa_ref, OFF_KS, OFF_VS)
    kv_pair(kwa_ref, vwa_ref, OFF_KW, OFF_VW)

    gt_ref[0] = jax.nn.sigmoid(proj(OFF_NG, LANES))
    gab_ref[0, :, 0:D_MODEL] = jax.nn.sigmoid(proj(OFF_GA, D_MODEL)).astype(BF16)
    gab_ref[0, :, D_MODEL:2 * D_MODEL] = jax.nn.sigmoid(proj(OFF_GA + D_MODEL, D_MODEL)).astype(BF16)


def _pos_pieces(pos):
    return ((pos // 256) * 256).astype(F32), (pos % 256).astype(F32)


def _inproj(x, norm_g, w_in, b_forget, fox_q_g, fox_k_g, nsa_q_g, nsa_k_g):
    B, S, D = x.shape
    tm = min(TM_PROJ, S)
    scale = HEAD_DIM ** -0.5

    c = [0]
    for n in (FOX_W, FOX_W, FOX_W, FOX_HEADS, NSA_W) + (NSA_KV_W,) * 6 + (3 * NSA_HEADS, D_MODEL, D_MODEL):
        c.append(c[-1] + n)
    fq, fk, fv, ff, nq, kc, vc, ks, vs, kw, vw, ng, ga, gb = [w_in[:, c[i]:c[i + 1]] for i in range(14)]
    perm = jnp.asarray([0, 4, 1, 5, 2, 6, 3, 7])
    nq = nq.reshape(D, NSA_HEADS, HEAD_DIM)[:, perm, :].reshape(D, NSA_W)
    padl = lambda a: jnp.pad(a, ((0, 0), (0, LANES - a.shape[1])))
    w = jnp.concatenate([fq, fk, fv, nq, kc, vc, ks, vs, kw, vw, padl(ff), padl(ng), ga, gb], axis=1).astype(BF16)
    assert w.shape[1] == N_PROJ

    bf = jnp.pad(b_forget.astype(F32), (0, LANES - FOX_HEADS))[None, :]
    gqa = jnp.tile(fox_q_g.astype(F32) * scale, FOX_HEADS)[None, :]
    gka = jnp.tile(fox_k_g.astype(F32), FOX_HEADS)[None, :]
    gqn = jnp.tile(nsa_q_g.astype(F32) * scale, NSA_HEADS)[None, :]
    gkn = jnp.tile(nsa_k_g.astype(F32), NSA_KV_HEADS)[None, :]

    r512 = jnp.arange(FOX_W)
    bd = jnp.where((r512[:, None] // HEAD_DIM) == (r512[None, :] // HEAD_DIM), 1.0 / HEAD_DIM, 0.0).astype(BF16)
    bd2 = bd[:LANES, :LANES]
    rt = jnp.arange(tm)
    tri = (rt[None, :] <= rt[:, None]).astype(BF16)

    heads = jnp.arange(FOX_HEADS)
    base = heads * LANES + jnp.where(heads % 2 == 0, HALF, 0)
    rows = jnp.arange(3 * LANES)
    piece, hlane = rows // LANES, rows % LANES
    cols = jnp.arange(FOX_HEADS * LANES)
    tgt_q = jnp.where(hlane < FOX_HEADS, base[jnp.minimum(hlane, FOX_HEADS - 1)] + 3 + piece, -1)
    tgt_k = jnp.where(hlane < FOX_HEADS, base[jnp.minimum(hlane, FOX_HEADS - 1)] + piece, -1)
    selq = (cols[None, :] == tgt_q[:, None]).astype(BF16)
    selk = -(cols[None, :] == tgt_k[:, None]).astype(BF16)
    off_in_blk = cols - base[cols // LANES]
    cq = ((off_in_blk >= 0) & (off_in_blk < 3)).astype(F32)[None, :]
    ck = ((off_in_blk >= 3) & (off_in_blk < 6)).astype(F32)[None, :]
    cv = (off_in_blk == 0).astype(F32)[None, :]

    blk = cols // LANES
    slope = 2.0 ** (-(blk + 1).astype(F32))
    o = cols % LANES - jnp.where(blk // NSA_REP == 0, HALF, 0)
    qc0 = jnp.where((o == 0) | (o == 1), slope, 0.0)[None, :]
    qca = jnp.where(o == 2, -slope, 0.0)[None, :]
    qcb = jnp.where(o == 3, -slope, 0.0)[None, :]
    kc0, kca, kcb, vone = _kv_rows()

    grid = (B, S // tm)
    row_spec = lambda n: pl.BlockSpec((1, tm, n), lambda b, s: (b, s, 0))
    consts = [norm_g.astype(F32)[None, :], w, bf, gqa, gka, gqn, gkn, bd, bd2, tri, selq, selk, cq, ck, cv,
              qc0, qca, qcb, kc0, kca, kcb, vone]
    out_widths = [(8 * LANES, BF16)] * 4 + [(LANES, BF16)] * 2 + [(2 * LANES, BF16)] * 4 + \
                 [(LANES, F32), (2 * D_MODEL, BF16)]
    outs = pl.pallas_call(
        _inproj_kernel,
        grid=grid,
        in_specs=[row_spec(D)] + [_const_spec(a.shape) for a in consts],
        out_specs=[row_spec(n) for n, _ in out_widths],
        out_shape=[jax.ShapeDtypeStruct((B, S, n), dt) for n, dt in out_widths],
        scratch_shapes=[pltpu.VMEM((1, LANES), F32)],
        compiler_params=_cparams(("arbitrary", "arbitrary")),
        name="inproj",
    )(x, *consts)
    return outs


def _kv_rows():
    cols = jnp.arange(2 * LANES)
    o = cols % LANES - jnp.where(cols // LANES == 0, HALF, 0)
    row = lambda m: m.astype(F32)[None, :]
    return row((o == 2) | (o == 3)), row(o == 0), row(o == 1), row(o == 0)


def _kv_consts(pa, pb):
    c0, ca, cb, vone = _kv_rows()
    return (c0 + ca * pa[:, None] + cb * pb[:, None]).astype(BF16), vone


def _compress_kernel(hk_ref, hv_ref, w1k_ref, w1v_ref, posk_ref, posv_ref, w2k_ref, w2v_ref,
                     gk_ref, kcc_ref, vone_ref, kc_ref, vc_ref):
    nc = hk_ref.shape[2]
    half_in = w1k_ref.shape[0] // 2

    def mlp(h_ref, w1_ref, pos_ref, w2_ref):
        hm = h_ref[0, 0]
        top = _dot(hm, w1_ref[0:half_in, :])
        bot = _dot(hm, w1_ref[half_in:2 * half_in, :])
        posw = _dot(pos_ref[...], w1_ref[...])[0:1, :]
        pre = top + pltpu.roll(bot, nc - 1, axis=0) + posw
        act = pre * (0.5 * (1.0 + jnp.tanh(0.7978845608028654 * (pre + 0.044715 * (pre * pre * pre)))))
        return _dot(act.astype(BF16), w2_ref[0])

    kc = mlp(hk_ref, w1k_ref, posk_ref, w2k_ref)
    msq = jnp.sum(kc * kc, axis=-1, keepdims=True) * (1.0 / HEAD_DIM)
    kc_ref[0, 0] = (kc * lax.rsqrt(msq + RMS_EPS) * gk_ref[0] + kcc_ref[0]).astype(BF16)
    vc = mlp(hv_ref, w1v_ref, posv_ref, w2v_ref)
    vc_ref[0, 0] = (vc + vone_ref[0]).astype(BF16)


def _compress(kcr, vcr, cmp_k_w1, cmp_k_w2, cmp_k_pos, cmp_v_w1, cmp_v_w2, cmp_v_pos, nsa_k_g):
    B, S, _ = kcr.shape
    G = NSA_KV_HEADS
    nc = S // CMP_STRIDE
    half_in = CMP_STRIDE * HEAD_DIM

    def halves(t):
        return t.reshape(B, nc, CMP_STRIDE, G, HEAD_DIM).transpose(0, 3, 1, 2, 4).reshape(B, G, nc, half_in)

    def w2_spread(w2):
        z = jnp.zeros_like(w2)
        return jnp.stack([jnp.concatenate([w2, z], 1), jnp.concatenate([z, w2], 1)]).astype(BF16)

    def pos8(p):
        return jnp.tile(p.reshape(1, CMP_BLOCK * HEAD_DIM), (8, 1)).astype(BF16)

    gk = nsa_k_g.astype(F32)
    z = jnp.zeros_like(gk)
    gk2 = jnp.stack([jnp.concatenate([gk, z]), jnp.concatenate([z, gk])])[:, None, :]
    cend = jnp.arange(nc) * CMP_STRIDE + CMP_BLOCK - 1
    kcc, vone = _kv_consts(*_pos_pieces(cend))
    kcc = kcc.astype(F32).reshape(nc, G, LANES).transpose(1, 0, 2)
    vone = vone.reshape(G, 1, LANES)

    blk = pl.BlockSpec((1, 1, nc, half_in), lambda b, g: (b, g, 0, 0))
    per_g = lambda a: pl.BlockSpec((1,) + a.shape[1:], lambda b, g: (g,) + (0,) * (a.ndim - 1))
    w1k, w1v = cmp_k_w1.astype(BF16), cmp_v_w1.astype(BF16)
    pk, pv = pos8(cmp_k_pos), pos8(cmp_v_pos)
    w2k, w2v = w2_spread(cmp_k_w2), w2_spread(cmp_v_w2)
    out_spec = pl.BlockSpec((1, 1, nc, LANES), lambda b, g: (b, g, 0, 0))
    return pl.pallas_call(
        _compress_kernel,
        grid=(B, G),
        in_specs=[blk, blk, _const_spec(w1k.shape), _const_spec(w1v.shape), _const_spec(pk.shape),
                  _const_spec(pv.shape), per_g(w2k), per_g(w2v), per_g(gk2), per_g(kcc), per_g(vone)],
        out_specs=[out_spec, out_spec],
        out_shape=[jax.ShapeDtypeStruct((B, G, nc, LANES), BF16)] * 2,
        compiler_params=_cparams(("arbitrary", "arbitrary")),
        name="compress",
    )(halves(kcr), halves(vcr), w1k, w1v, pk, pv, w2k, w2v, gk2, kcc, vone)


def _attn_first(s, v):
    m = jnp.max(s, axis=-1, keepdims=True)
    p = jnp.exp((s - m).astype(BF16))
    return m, _dot(p, v)


def _attn_step(carry, s, v):
    m, acc = carry
    m_new = jnp.maximum(m, jnp.max(s, axis=-1, keepdims=True))
    p = jnp.exp((s - m_new).astype(BF16))
    return m_new, jnp.exp(m - m_new) * acc + _dot(p, v)


def _nsa_kernel(q_ref, kc_ref, vc_ref, ks_ref, e_ref, vs_ref, kw_ref, vw_ref, gt_ref, ov_ref, o_ref,
                tiles_ref, sa_ref, sb_ref):
    tq = q_ref.shape[1]
    nc = kc_ref.shape[2]
    tk = TK_SLC
    rq = NSA_REP * tq
    g = pl.program_id(1)
    i = pl.program_id(2)
    q0 = i * tq

    q4 = jnp.concatenate([q_ref[0, :, LANES * r:LANES * (r + 1)] for r in range(NSA_REP)], axis=0)

    def qpos_of(shape):
        return q0 + (lax.broadcasted_iota(jnp.int32, shape, 0) & (tq - 1))

    def add_mask(s, valid):
        bias = jnp.where(valid, 0.0, NEG_INF)
        return (s.reshape(NSA_REP, tq, s.shape[1]) + bias[None]).reshape(s.shape)

    def qrow(n):
        return q0 + lax.broadcasted_iota(jnp.int32, (tq, n), 0)

    def kcol(n):
        return lax.broadcasted_iota(jnp.int32, (tq, n), 1)

    s = add_mask(_dot_nt(q4, kc_ref[0, 0]), qrow(nc) >= kcol(nc) * CMP_STRIDE + (CMP_BLOCK - 1))
    m = jnp.max(s, axis=-1, keepdims=True)
    e = jnp.exp(s - m)
    anyv = (qpos_of((rq, 1)) >= CMP_BLOCK - 1).astype(F32)
    p = e * (anyv / jnp.sum(e, axis=-1, keepdims=True))
    o_cmp = _dot(p.astype(BF16), vc_ref[0, 0])

    psum = p[0:tq] + p[tq:2 * tq] + p[2 * tq:3 * tq] + p[3 * tq:4 * tq]
    p_hi, p_mid, p_lo = _split3(psum)
    ov = ov_ref[...]
    imp = _dot(p_hi, ov) + _dot(p_mid, ov) + _dot(p_lo, ov)
    blk_i = lax.broadcasted_iota(jnp.int32, (LANES, tq), 0)
    blk_f = blk_i.astype(F32)
    qblk = (q0 + lax.broadcasted_iota(jnp.int32, (LANES, tq), 1)) // SLC_BLOCK
    forced = (blk_i == 0) | (blk_i == qblk) | (blk_i == qblk - 1)
    score = jnp.where(blk_i > qblk, -1.0, jnp.where(forced, FORCE_SCORE, imp.T))
    selb = jnp.full((LANES, tq), NEG_INF, F32)
    for _ in range(SLC_TOPK):
        mx = jnp.max(score, axis=0, keepdims=True)
        first = jnp.min(jnp.where(score == mx, blk_f, float(LANES)), axis=0, keepdims=True)
        hit = blk_f == first
        selb = jnp.where(hit, 0.0, selb)
        score = jnp.where(hit, -3e38, score)
    sb = selb.T.astype(BF16)
    q2 = jnp.concatenate([q4, jnp.concatenate([sb] * NSA_REP, axis=0)], axis=1)

    def slc_qk(j):
        start = pl.multiple_of(j * tk, tk)
        kk = jnp.concatenate([ks_ref[0, pl.ds(start, tk), :], e_ref[pl.ds(start, tk), :]], axis=1)
        return _dot_nt(q2, kk)

    def slc_v(j):
        return vs_ref[0, pl.ds(pl.multiple_of(j * tk, tk), tk), :]

    jd = q0 // tk
    blocks_per_tile = tk // SLC_BLOCK
    anysel = jnp.max(selb, axis=1, keepdims=True)
    cnt = jnp.int32(0)
    for j in range(ks_ref.shape[1] // tk):
        tile_sel = jnp.max(anysel[blocks_per_tile * j:blocks_per_tile * (j + 1), :]) > -1.0
        tiles_ref[cnt] = j
        cnt = cnt + (tile_sel & (j < jd)).astype(jnp.int32)

    n_seq = cnt + 1

    def tile_of(n):
        return jnp.where(n == 0, jd, tiles_ref[jnp.maximum(n - 1, 0)])

    def slc_step(carry, s_buf, n):
        return _attn_step(carry, s_buf[...], slc_v(tile_of(n)))

    sa_ref[...] = add_mask(slc_qk(jd), jd * tk + kcol(tk) <= qrow(tk))

    def slc_body(p, carry):
        sb_ref[...] = slc_qk(tile_of(2 * p + 1))
        carry = slc_step(carry, sa_ref, 2 * p)
        sa_ref[...] = slc_qk(tile_of(jnp.minimum(2 * p + 2, n_seq - 1)))
        return slc_step(carry, sb_ref, 2 * p + 1)

    carry = (jnp.full((rq, 1), NEG_INF, F32), jnp.zeros((rq, LANES), F32))
    carry = lax.fori_loop(0, n_seq // 2, slc_body, carry)
    _, acc_slc = lax.cond(n_seq % 2 == 1, lambda c: slc_step(c, sa_ref, n_seq - 1), lambda c: c, carry)

    nw = WINDOW + tq
    wstart = pl.multiple_of(jnp.maximum(q0 - WINDOW, 0), tq)
    dist = qrow(nw) - (wstart + kcol(nw))
    s = add_mask(_dot_nt(q4, kw_ref[0, pl.ds(wstart, nw), :]),
                 lax.bitcast_convert_type(dist, jnp.uint32) < WINDOW)
    _, acc_win = _attn_first(s, vw_ref[0, pl.ds(wstart, nw), :])

    lane_r = lax.broadcasted_iota(jnp.int32, (rq, LANES), 1)
    data0 = HALF * g
    ones_lane = HALF - data0

    def normalise(acc):
        l = jnp.sum(jnp.where(lane_r == ones_lane, acc, 0.0), axis=-1, keepdims=True)
        return acc * (1.0 / l)

    o_slc = normalise(acc_slc)
    o_win = normalise(acc_win)
    gt = gt_ref[0]
    lane_g = lax.broadcasted_iota(jnp.int32, (tq, LANES), 1)
    is_data_q = (lane_g >= data0) & (lane_g < data0 + HALF)
    for r in range(NSA_REP):
        col = 3 * (NSA_REP * g + r)
        gate = [jnp.sum(jnp.where(lane_g == col + b, gt, 0.0), axis=-1, keepdims=True) for b in range(3)]
        rows = slice(r * tq, (r + 1) * tq)
        o = gate[0] * o_cmp[rows] + gate[1] * o_slc[rows] + gate[2] * o_win[rows]
        o_ref[0, :, LANES * r:LANES * (r + 1)] = jnp.where(is_data_q, o, 0.0).astype(BF16)


def _nsa(qn, kca, vca, ksa, vsa, kwa, vwa, gt):
    B, S, _ = qn.shape
    G = NSA_KV_HEADS
    tq = min(TQ_NSA, S)
    nc = S // CMP_STRIDE
    n_slc = S // SLC_BLOCK
    assert n_slc <= LANES and S % TK_SLC == 0

    cs = jnp.arange(nc)[:, None] * CMP_STRIDE
    ss = jnp.arange(LANES)[None, :] * SLC_BLOCK
    ovl = jnp.clip(jnp.minimum(cs + CMP_BLOCK, ss + SLC_BLOCK) - jnp.maximum(cs, ss), 0, None)
    valid = (jnp.arange(nc)[:, None] < (S - CMP_BLOCK) // CMP_STRIDE + 1) & (jnp.arange(LANES)[None, :] < n_slc)
    ov = jnp.where(valid, ovl.astype(F32) / CMP_BLOCK, 0.0).astype(BF16)
    e1h = (jnp.arange(S)[:, None] // SLC_BLOCK == jnp.arange(LANES)[None, :]).astype(BF16)

    q_spec = pl.BlockSpec((1, tq, NSA_REP * LANES), lambda b, g, i: (b, i, g))
    c_spec = pl.BlockSpec((1, 1, nc, LANES), lambda b, g, i: (b, g, 0, 0))
    kv_spec = pl.BlockSpec((1, S, LANES), lambda b, g, i: (b, 0, g))
    return pl.pallas_call(
        _nsa_kernel,
        grid=(B, G, S // tq),
        in_specs=[q_spec, c_spec, c_spec, kv_spec, _const_spec(e1h.shape), kv_spec, kv_spec, kv_spec,
                  pl.BlockSpec((1, tq, LANES), lambda b, g, i: (b, i, 0)), _const_spec(ov.shape)],
        out_specs=q_spec,
        out_shape=jax.ShapeDtypeStruct((B, S, NSA_HEADS * LANES), BF16),
        scratch_shapes=[pltpu.SMEM((S // TK_SLC + 1,), jnp.int32),
                        pltpu.VMEM((NSA_REP * tq, TK_SLC), F32), pltpu.VMEM((NSA_REP * tq, TK_SLC), F32)],
        compiler_params=_cparams(("arbitrary", "arbitrary", "arbitrary")),
        name="nsa",
    )(qn, kca, vca, ksa, e1h, vsa, kwa, vwa, gt, ov)


def _fox_kernel(q_ref, k_ref, v_ref, o_ref, sa_ref, sb_ref):
    tq = q_ref.shape[1]
    i = pl.program_id(2)
    lane = lax.broadcasted_iota(jnp.int32, (tq, LANES), 1)
    causal = lax.broadcasted_iota(jnp.int32, (tq, tq), 1) <= lax.broadcasted_iota(jnp.int32, (tq, tq), 0)

    def cols(hh):
        return slice(LANES * hh, LANES * (hh + 1))

    def qk(hh, j):
        start = pl.multiple_of(j * tq, tq)
        return _dot_nt(q_ref[0, :, cols(hh)], k_ref[0, pl.ds(start, tq), cols(hh)])

    def vtile(hh, j):
        return v_ref[0, pl.ds(pl.multiple_of(j * tq, tq), tq), cols(hh)]

    n_tiles = i + 1

    def tile_of(n):
        return jnp.where(n == 0, i, n - 1)

    def step_all(carry, s_buf, n):
        t = tile_of(n)
        return tuple(_attn_step(carry[hh], s_buf[hh], vtile(hh, t)) for hh in range(2))

    for hh in range(2):
        sa_ref[hh] = jnp.where(causal, qk(hh, i), NEG_INF)
    init = tuple((jnp.full((tq, 1), NEG_INF, F32), jnp.zeros((tq, LANES), F32)) for hh in range(2))

    def body(p, carry):
        for hh in range(2):
            sb_ref[hh] = qk(hh, tile_of(2 * p + 1))
        carry = step_all(carry, sa_ref, 2 * p)
        nxt = jnp.minimum(2 * p + 2, n_tiles - 1)
        for hh in range(2):
            sa_ref[hh] = qk(hh, tile_of(nxt))
        return step_all(carry, sb_ref, 2 * p + 1)

    carry = lax.fori_loop(0, n_tiles // 2, body, init)
    carry = lax.cond(n_tiles % 2 == 1, lambda c: step_all(c, sa_ref, n_tiles - 1), lambda c: c, carry)
    outs = []
    for hh in range(2):
        acc = carry[hh][1]
        ones_lane = HALF if hh == 0 else 0
        l = jnp.sum(jnp.where(lane == ones_lane, acc, 0.0), axis=-1, keepdims=True)
        outs.append(acc * (1.0 / l))
    o_ref[0] = jnp.where(lane < HALF, outs[0], outs[1]).astype(BF16)


def _fox(qa, ka, va):
    B, S, _ = qa.shape
    tq = min(TQ_FOX, S)
    q_spec = pl.BlockSpec((1, tq, 2 * LANES), lambda b, h, i: (b, i, h))
    kv_spec = pl.BlockSpec((1, S, 2 * LANES), lambda b, h, i: (b, 0, h))
    return pl.pallas_call(
        _fox_kernel,
        grid=(B, FOX_HEADS // 2, S // tq),
        in_specs=[q_spec, kv_spec, kv_spec],
        out_specs=pl.BlockSpec((1, tq, LANES), lambda b, h, i: (b, i, h)),
        out_shape=jax.ShapeDtypeStruct((B, S, FOX_W), BF16),
        scratch_shapes=[pltpu.VMEM((2, tq, tq), F32), pltpu.VMEM((2, tq, tq), F32)],
        compiler_params=_cparams(("arbitrary", "arbitrary", "arbitrary")),
        name="fox",
    )(qa, ka, va)


R_E1, R_E2, R_W1, R_W2, R_RANK1, R_RANK2 = range(6)


def _merge_kernel(x_ref, oa_ref, ob_ref, gab_ref, wa_ref, wb_ref, wo_ref, g2_ref, wr_hi_ref, wr_lo_ref, br_ref,
                  ltri_ref, x1_ref, route_ref, cnt_ref, carry_ref):
    tm = x_ref.shape[1]

    @pl.when((pl.program_id(0) == 0) & (pl.program_id(1) == 0))
    def _():
        carry_ref[...] = jnp.zeros_like(carry_ref)

    out_a = _dot(oa_ref[0], wa_ref[...])
    out_b = _dot(ob_ref[0], wb_ref[...])
    mix = gab_ref[0, :, 0:D_MODEL].astype(F32) * out_a + gab_ref[0, :, D_MODEL:2 * D_MODEL].astype(F32) * out_b
    x1 = x_ref[0] + _dot(mix.astype(BF16), wo_ref[...])
    x1_ref[...] = x1
    h2 = x1 * lax.rsqrt(jnp.mean(x1 * x1, axis=-1, keepdims=True) + RMS_EPS) * g2_ref[...]

    h_hi = h2.astype(BF16)
    h_lo = (h2 - h_hi.astype(F32)).astype(BF16)
    logits = _dot(h_hi, wr_hi_ref[...]) + (_dot(h_hi, wr_lo_ref[...]) + _dot(h_lo, wr_hi_ref[...])) + br_ref[...]
    lane = lax.broadcasted_iota(jnp.int32, (tm, LANES), 1)
    lane_f = lane.astype(F32)

    def first_argmax(vals):
        mx = jnp.max(vals, axis=-1, keepdims=True)
        idx = jnp.min(jnp.where(vals == mx, lane_f, float(LANES)), axis=-1, keepdims=True)
        return mx, idx

    is_grp = (lane >= N_EXPERTS) & (lane < N_EXPERTS + N_GROUPS)
    gl = jnp.where(is_grp, logits, NEG_INF)
    gmax, gidx = first_argmax(gl)
    p_g = 1.0 / jnp.sum(jnp.where(is_grp, jnp.exp(gl - gmax), 0.0), axis=-1, keepdims=True)
    e_lo = (gidx - float(N_EXPERTS)) * float(EXPERTS_PER_GROUP)
    in_grp = (lane_f >= e_lo) & (lane_f < e_lo + float(EXPERTS_PER_GROUP))
    el = jnp.where(in_grp, logits, NEG_INF)
    m1, i1 = first_argmax(el)
    m2, i2 = first_argmax(jnp.where(lane_f == i1, NEG_INF, el))
    e2 = jnp.exp(m2 - m1)
    w1 = p_g / (1.0 + e2)
    w2 = p_g * e2 / (1.0 + e2)

    hit1 = lane_f == i1
    hit2 = lane_f == i2
    onehot = jnp.where(hit1 | hit2, 1.0, 0.0)
    before = carry_ref[...] + _dot(ltri_ref[...], onehot.astype(BF16))
    rank1 = jnp.sum(jnp.where(hit1, before, 0.0), axis=-1, keepdims=True)
    rank2 = jnp.sum(jnp.where(hit2, before, 0.0), axis=-1, keepdims=True)
    total = carry_ref[...] + jnp.sum(onehot, axis=0, keepdims=True)
    carry_ref[...] = total
    cnt_ref[...] = total

    rec = jnp.zeros((tm, LANES), F32)
    for k, val in ((R_E1, i1), (R_E2, i2), (R_W1, w1), (R_W2, w2), (R_RANK1, rank1), (R_RANK2, rank2)):
        rec = jnp.where(lane == k, val, rec)
    route_ref[...] = rec


def _merge(x, oa, ob, gab, w_fox_up, w_nsa_up, w_out, norm_ffn_g, w_group, b_group, w_router, b_router):
    B, S, D = x.shape
    tm = min(TM_MERGE, S)
    wa = w_fox_up.astype(BF16)
    wn = w_nsa_up.reshape(NSA_KV_HEADS, NSA_REP, HEAD_DIM, D)
    z = jnp.zeros_like(wn[0])
    wb = jnp.stack([jnp.concatenate([wn[0], z], axis=1), jnp.concatenate([z, wn[1]], axis=1)])
    wb = wb.reshape(NSA_HEADS * LANES, D).astype(BF16)
    wo = w_out.astype(BF16)
    wr = jnp.pad(jnp.concatenate([w_router, w_group], axis=1).astype(F32),
                 ((0, 0), (0, LANES - N_EXPERTS - N_GROUPS)))
    wr_hi = wr.astype(BF16)
    wr_lo = (wr - wr_hi.astype(F32)).astype(BF16)
    br = jnp.pad(jnp.concatenate([b_router, b_group]).astype(F32), (0, LANES - N_EXPERTS - N_GROUPS))[None, :]
    g2 = norm_ffn_g.astype(F32)[None, :]
    rt = jnp.arange(tm)
    ltri = (rt[None, :] < rt[:, None]).astype(BF16)

    row = lambda n: pl.BlockSpec((1, tm, n), lambda b, s: (b, s, 0))
    flat = lambda n: pl.BlockSpec((tm, n), lambda b, s: (b * (S // tm) + s, 0))
    consts = [wa, wb, wo, g2, wr_hi, wr_lo, br, ltri]
    return pl.pallas_call(
        _merge_kernel,
        grid=(B, S // tm),
        in_specs=[row(D), row(FOX_W), row(NSA_HEADS * LANES), row(2 * D)] + [_const_spec(a.shape) for a in consts],
        out_specs=[flat(D), flat(LANES), _const_spec((1, LANES))],
        out_shape=[jax.ShapeDtypeStruct((B * S, D), F32), jax.ShapeDtypeStruct((B * S, LANES), F32),
                   jax.ShapeDtypeStruct((1, LANES), F32)],
        scratch_shapes=[pltpu.VMEM((1, LANES), F32)],
        compiler_params=_cparams(("arbitrary", "arbitrary")),
        name="merge",
    )(x, oa, ob, gab, *consts)


def _row_copy(src_ref, src_row, dst_ref, dst_row, sem):
    return pltpu.make_async_copy(src_ref.at[pl.ds(src_row, 1), :], dst_ref.at[pl.ds(dst_row, 1), :], sem)


def _dispatch_kernel(row1_ref, row2_ref, clear_ref, x1_ref, xs_ref, zero_ref, sem, zsem):
    tm = x1_ref.shape[0]
    base = pl.program_id(0) * tm

    @pl.when(pl.program_id(0) == 0)
    def _():
        zero_ref[...] = jnp.zeros_like(zero_ref)

        def clear(c):
            start = pl.multiple_of(jnp.maximum(clear_ref[c], 0), zero_ref.shape[0])
            return pltpu.make_async_copy(zero_ref, xs_ref.at[pl.ds(start, zero_ref.shape[0]), :], zsem)

        for c in range(clear_ref.shape[0]):
            @pl.when(clear_ref[c] >= 0)
            def _(c=c):
                clear(c).start()
        for c in range(clear_ref.shape[0]):
            @pl.when(clear_ref[c] >= 0)
            def _(c=c):
                clear(c).wait()

    def start(t, _):
        _row_copy(x1_ref, t, xs_ref, row1_ref[base + t], sem).start()
        _row_copy(x1_ref, t, xs_ref, row2_ref[base + t], sem).start()
        return _

    def wait(t, _):
        _row_copy(x1_ref, t, xs_ref, row1_ref[base + t], sem).wait()
        _row_copy(x1_ref, t, xs_ref, row2_ref[base + t], sem).wait()
        return _

    lax.fori_loop(0, tm, start, 0, unroll=8)
    lax.fori_loop(0, tm, wait, 0, unroll=8)


def _experts_kernel(tile_e_ref, n_used_ref, xs_ref, g2_ref, wg_ref, wu_ref, wd_ref, ys_ref,
                    wg_bf, wu_bf, wd_bf):
    k = pl.program_id(0)
    used = k < n_used_ref[0]

    @pl.when(used & ((k == 0) | (tile_e_ref[k] != tile_e_ref[jnp.maximum(k - 1, 0)])))
    def _():
        wg_bf[...] = wg_ref[0, 0].astype(BF16)
        wu_bf[...] = wu_ref[0, 0].astype(BF16)
        wd_bf[...] = wd_ref[0, 0].astype(BF16)

    @pl.when(used)
    def _():
        x = xs_ref[...]
        h = (x * lax.rsqrt(jnp.mean(x * x, axis=-1, keepdims=True) + RMS_EPS) * g2_ref[...]).astype(BF16)
        a = _dot(h, wg_bf[...])
        hid = (a * jax.nn.sigmoid(a)) * _dot(h, wu_bf[...])
        ys_ref[...] = _dot(hid.astype(BF16), wd_bf[...])

    @pl.when(jnp.logical_not(used))
    def _():
        ys_ref[...] = jnp.zeros_like(ys_ref)


def _combine_kernel(row1_ref, row2_ref, x1_ref, route_ref, ys_ref, o_ref, y1_ref, y2_ref, sem):
    tm = x1_ref.shape[0]
    base = pl.program_id(0) * tm

    def start(t, _):
        _row_copy(ys_ref, row1_ref[base + t], y1_ref, t, sem).start()
        _row_copy(ys_ref, row2_ref[base + t], y2_ref, t, sem).start()
        return _

    def wait(t, _):
        _row_copy(ys_ref, row1_ref[base + t], y1_ref, t, sem).wait()
        _row_copy(ys_ref, row2_ref[base + t], y2_ref, t, sem).wait()
        return _

    lax.fori_loop(0, tm, start, 0, unroll=8)
    lax.fori_loop(0, tm, wait, 0, unroll=8)
    rec = route_ref[...]
    lane = lax.broadcasted_iota(jnp.int32, rec.shape, 1)
    w1 = jnp.sum(jnp.where(lane == R_W1, rec, 0.0), axis=-1, keepdims=True)
    w2 = jnp.sum(jnp.where(lane == R_W2, rec, 0.0), axis=-1, keepdims=True)
    o_ref[0] = x1_ref[...] + (w1 * y1_ref[...] + w2 * y2_ref[...])


def _moe(x1, route, cnt, norm_ffn_g, w_gate, w_up, w_down, layer, B, S):
    T, D = x1.shape
    tme = min(TM_EXPERT, T)
    tmd = min(TM_DISPATCH, S)

    counts = cnt[0, :N_EXPERTS].astype(jnp.int32)
    tiles_per_e = (counts + tme - 1) // tme
    tile_end = jnp.cumsum(tiles_per_e)
    offs = (tile_end - tiles_per_e) * tme
    last_tile_row = jnp.where(tiles_per_e > 0, (tile_end - 1) * tme, -1).astype(jnp.int32)
    rec = route[:, :8].T.astype(jnp.int32)
    experts = jnp.arange(N_EXPERTS, dtype=jnp.int32)[:, None]
    row_of = lambda e, rank: rank + jnp.sum(jnp.where(e[None, :] == experts, offs[:, None], 0), axis=0)
    row1 = row_of(rec[R_E1], rec[R_RANK1])
    row2 = row_of(rec[R_E2], rec[R_RANK2])
    max_tiles = (2 * T) // tme + N_EXPERTS
    tile_e = jnp.sum(jnp.arange(max_tiles, dtype=jnp.int32)[:, None] >= tile_end[None, :], axis=1)
    tile_e = jnp.minimum(tile_e, N_EXPERTS - 1).astype(jnp.int32)
    n_used = tile_end[-1:].astype(jnp.int32)
    n_rows = max_tiles * tme
    spare = n_used + jnp.arange(N_EXPERTS, dtype=jnp.int32)
    clear_rows = jnp.concatenate([last_tile_row, jnp.where(spare < max_tiles, spare * tme, -1)]).astype(jnp.int32)

    xs = pl.pallas_call(
        _dispatch_kernel,
        grid_spec=pltpu.PrefetchScalarGridSpec(
            num_scalar_prefetch=3,
            grid=(T // tmd,),
            in_specs=[pl.BlockSpec((tmd, D), lambda t, r1, r2, lt: (t, 0))],
            out_specs=pl.BlockSpec(memory_space=pl.ANY),
            scratch_shapes=[pltpu.VMEM((tme, D), F32), pltpu.SemaphoreType.DMA(()), pltpu.SemaphoreType.DMA(())],
        ),
        out_shape=jax.ShapeDtypeStruct((n_rows, D), F32),
        compiler_params=_cparams(("arbitrary",)),
        name="dispatch",
    )(row1, row2, clear_rows, x1)

    g2 = norm_ffn_g.astype(F32)[None, :]
    w_spec = lambda shape: pl.BlockSpec((1, 1) + shape, lambda k, te, nu: (layer, te[k], 0, 0))
    ys = pl.pallas_call(
        _experts_kernel,
        grid_spec=pltpu.PrefetchScalarGridSpec(
            num_scalar_prefetch=2,
            grid=(max_tiles,),
            in_specs=[pl.BlockSpec((tme, D), lambda k, te, nu: (jnp.minimum(k, nu[0] - 1), 0)),
                      pl.BlockSpec((1, D), lambda k, te, nu: (0, 0)),
                      w_spec((D, D_EXPERT)), w_spec((D, D_EXPERT)), w_spec((D_EXPERT, D))],
            out_specs=pl.BlockSpec((tme, D), lambda k, te, nu: (k, 0)),
            scratch_shapes=[pltpu.VMEM((D, D_EXPERT), BF16), pltpu.VMEM((D, D_EXPERT), BF16),
                            pltpu.VMEM((D_EXPERT, D), BF16)],
        ),
        out_shape=jax.ShapeDtypeStruct((n_rows, D), F32),
        compiler_params=_cparams(("arbitrary",)),
        name="experts",
    )(tile_e, n_used, xs, g2, w_gate, w_up, w_down)

    return pl.pallas_call(
        _combine_kernel,
        grid_spec=pltpu.PrefetchScalarGridSpec(
            num_scalar_prefetch=2,
            grid=(T // tmd,),
            in_specs=[pl.BlockSpec((tmd, D), lambda t, r1, r2: (t, 0)),
                      pl.BlockSpec((tmd, LANES), lambda t, r1, r2: (t, 0)),
                      pl.BlockSpec(memory_space=pl.ANY)],
            out_specs=pl.BlockSpec((1, tmd, D), lambda t, r1, r2: (t // (S // tmd), t % (S // tmd), 0)),
            scratch_shapes=[pltpu.VMEM((tmd, D), F32), pltpu.VMEM((tmd, D), F32), pltpu.SemaphoreType.DMA(())],
        ),
        out_shape=jax.ShapeDtypeStruct((B, S, D), F32),
        compiler_params=_cparams(("arbitrary",)),
        name="combine",
    )(row1, row2, x1, route, ys)


def kernel(x, norm_mix_g, w_in, b_forget, fox_q_g, fox_k_g, nsa_q_g, nsa_k_g, cmp_k_w1, cmp_k_w2, cmp_k_pos,
           cmp_v_w1, cmp_v_w2, cmp_v_pos, w_fox_up, w_nsa_up, w_out, norm_ffn_g, w_group, b_group, w_router,
           b_router, w_gate, w_up, w_down):
    B, S, D = x.shape
    for l in range(w_in.shape[0]):
        qa, ka, va, qn, kcr, vcr, ksa, vsa, kwa, vwa, gt, gab = _inproj(
            x, norm_mix_g[l], w_in[l], b_forget[l], fox_q_g[l], fox_k_g[l], nsa_q_g[l], nsa_k_g[l])
        kca, vca = _compress(kcr, vcr, cmp_k_w1[l], cmp_k_w2[l], cmp_k_pos[l],
                             cmp_v_w1[l], cmp_v_w2[l], cmp_v_pos[l], nsa_k_g[l])
        ob = _nsa(qn, kca, vca, ksa, vsa, kwa, vwa, gt)
        oa = _fox(qa, ka, va)
        x1, route, cnt = _merge(x, oa, ob, gab, w_fox_up[l], w_nsa_up[l], w_out[l], norm_ffn_g[l],
                                w_group[l], b_group[l], w_router[l], b_router[l])
        x = _moe(x1, route, cnt, norm_ffn_g[l], w_gate, w_up, w_down, l, B, S)
    return x
```

```python
import functools

import jax
import jax.numpy as jnp
from jax import lax
from jax.experimental import pallas as pl
from jax.experimental.pallas import tpu as pltpu

F32 = jnp.float32
BF16 = jnp.bfloat16

D_MODEL = 1024
HEAD_DIM = 64
FOX_HEADS = 8
NSA_HEADS = 8
NSA_KV_HEADS = 2
NSA_REP = NSA_HEADS // NSA_KV_HEADS
CMP_BLOCK = 32
CMP_STRIDE = 16
CMP_HIDDEN = 256
SLC_BLOCK = 64
SLC_TOPK = 16
WINDOW = 512
N_GROUPS = 4
EXPERTS_PER_GROUP = 4
N_EXPERTS = N_GROUPS * EXPERTS_PER_GROUP
D_EXPERT = 512
RMS_EPS = 1e-6
NEG_INF = -1e30
FORCE_SCORE = 1e4

LANES = 128
HALF = LANES // 2
VMEM_LIMIT = 56 * 1024 * 1024

FOX_W = FOX_HEADS * HEAD_DIM
NSA_W = NSA_HEADS * HEAD_DIM
NSA_KV_W = NSA_KV_HEADS * HEAD_DIM

OFF_FQ = 0
OFF_FK = OFF_FQ + FOX_W
OFF_FV = OFF_FK + FOX_W
OFF_NQ = OFF_FV + FOX_W
OFF_KC = OFF_NQ + NSA_W
OFF_VC = OFF_KC + NSA_KV_W
OFF_KS = OFF_VC + NSA_KV_W
OFF_VS = OFF_KS + NSA_KV_W
OFF_KW = OFF_VS + NSA_KV_W
OFF_VW = OFF_KW + NSA_KV_W
OFF_FF = OFF_VW + NSA_KV_W
OFF_NG = OFF_FF + LANES
OFF_GA = OFF_NG + LANES
N_PROJ = OFF_GA + 2 * D_MODEL

TM_PROJ = 512
TQ_FOX = 512
TQ_NSA = 256
TK_SLC = 256
TM_MERGE = 512
TM_EXPERT = 256
TM_DISPATCH = 512


def _dot(a, b):
    return jnp.dot(a, b, preferred_element_type=F32)


def _dot_nt(a, b):
    return lax.dot_general(a, b, (((1,), (1,)), ((), ())), preferred_element_type=F32)


def _split3(x):
    hi = x.astype(BF16)
    r = x - hi.astype(F32)
    mid = r.astype(BF16)
    lo = (r - mid.astype(F32)).astype(BF16)
    return hi, mid, lo


def _cparams(sem):
    return pltpu.CompilerParams(dimension_semantics=sem, vmem_limit_bytes=VMEM_LIMIT)


def _const_spec(shape):
    nd = len(shape)
    return pl.BlockSpec(shape, lambda *_: (0,) * nd)


def _inproj_kernel(x_ref, g_ref, w_ref, bf_ref, gqa_ref, gka_ref, gqn_ref, gkn_ref,
                   bd_ref, bd2_ref, tri_ref, selq_ref, selk_ref, cq_ref, ck_ref, cv_ref,
                   qc0_ref, qca_ref, qcb_ref, kc0_ref, kca_ref, kcb_ref, vone_ref,
                   qa_ref, ka_ref, va_ref, qn_ref, kcr_ref, vcr_ref,
                   ksa_ref, vsa_ref, kwa_ref, vwa_ref, gt_ref, gab_ref,
                   carry_ref):
    tm = x_ref.shape[1]

    @pl.when(pl.program_id(1) == 0)
    def _():
        carry_ref[...] = jnp.zeros_like(carry_ref)

    x = x_ref[0]
    y = x * lax.rsqrt(jnp.mean(x * x, axis=-1, keepdims=True) + RMS_EPS)
    h = (y * g_ref[...]).astype(BF16)

    def proj(off, n):
        return _dot(h, w_ref[:, off:off + n])

    lo_half = lax.broadcasted_iota(jnp.int32, (tm, LANES), 1) < HALF

    pos = pl.program_id(1) * tm + lax.broadcasted_iota(jnp.int32, (tm, 1), 0)
    pos_a = ((pos >> 8) << 8).astype(F32)
    pos_b = (pos & 255).astype(F32)

    def pos_channels(c0_ref, ca_ref, cb_ref, k):
        blk = slice(LANES * k, LANES * (k + 1))
        return c0_ref[:, blk] + ca_ref[:, blk] * pos_a + cb_ref[:, blk] * pos_b

    def headnorm(z, bd, grow):
        msq = _dot((z * z).astype(BF16), bd[...])
        return z * lax.rsqrt(msq + RMS_EPS) * grow[...]

    def spread_pairs(out_ref, src, aug):
        for m in range(4):
            s = src[:, LANES * m:LANES * (m + 1)]
            out_ref[0, :, LANES * 2 * m:LANES * (2 * m + 1)] = jnp.where(lo_half, s, aug(2 * m)).astype(BF16)
            out_ref[0, :, LANES * (2 * m + 1):LANES * (2 * m + 2)] = jnp.where(lo_half, aug(2 * m + 1), s).astype(BF16)

    zf = proj(OFF_FF, LANES) + bf_ref[...]
    logf = jnp.minimum(zf, 0.0) - jnp.log(1.0 + jnp.exp(-jnp.abs(zf)))
    l_hi, l_mid, l_lo = _split3(logf)
    tri = tri_ref[...]
    cum = carry_ref[...] + (_dot(tri, l_hi) + _dot(tri, l_mid) + _dot(tri, l_lo))
    carry_ref[...] = cum[tm - 1:tm, :]
    pq = jnp.concatenate(_split3(cum), axis=1)
    augq = _dot(pq, selq_ref[...]) + cq_ref[...]
    augk = _dot(pq, selk_ref[...]) + ck_ref[...]

    zq = headnorm(proj(OFF_FQ, FOX_W), bd_ref, gqa_ref)
    spread_pairs(qa_ref, zq, lambda k: augq[:, LANES * k:LANES * (k + 1)])
    zk = headnorm(proj(OFF_FK, FOX_W), bd_ref, gka_ref)
    spread_pairs(ka_ref, zk, lambda k: augk[:, LANES * k:LANES * (k + 1)])
    zv = proj(OFF_FV, FOX_W)
    spread_pairs(va_ref, zv, lambda k: cv_ref[:, LANES * k:LANES * (k + 1)])

    zn = headnorm(proj(OFF_NQ, NSA_W), bd_ref, gqn_ref)
    for m in range(NSA_REP):
        s = zn[:, LANES * m:LANES * (m + 1)]
        c0 = pos_channels(qc0_ref, qca_ref, qcb_ref, m)
        c1 = pos_channels(qc0_ref, qca_ref, qcb_ref, NSA_REP + m)
        qn_ref[0, :, LANES * m:LANES * (m + 1)] = jnp.where(lo_half, s, c0).astype(BF16)
        qn_ref[0, :, LANES * (NSA_REP + m):LANES * (NSA_REP + m + 1)] = jnp.where(lo_half, c1, s).astype(BF16)

    kcr_ref[0] = proj(OFF_KC, NSA_KV_W)
    vcr_ref[0] = proj(OFF_VC, NSA_KV_W)

    kp0 = pos_channels(kc0_ref, kca_ref, kcb_ref, 0)
    kp1 = pos_channels(kc0_ref, kca_ref, kcb_ref, 1)

    def kv_pair(k_out, v_out, off_k, off_v):
        zk2 = headnorm(proj(off_k, NSA_KV_W), bd2_ref, gkn_ref)
        k_out[0, :, 0:LANES] = jnp.where(lo_half, zk2, kp0).astype(BF16)
        k_out[0, :, LANES:2 * LANES] = jnp.where(lo_half, kp1, zk2).astype(BF16)
        zv2 = proj(off_v, NSA_KV_W)
        v_out[0, :, 0:LANES] = jnp.where(lo_half, zv2, vone_ref[:, 0:LANES]).astype(BF16)
        v_out[0, :, LANES:2 * LANES] = jnp.where(lo_half, vone_ref[:, LANES:2 * LANES], zv2).astype(BF16)

    kv_pair(ksa_ref, vsa_ref, OFF_KS, OFF_VS)
    kv_pair(kwa_ref, vwa_ref, OFF_KW, OFF_VW)

    gt_ref[0] = jax.nn.sigmoid(proj(OFF_NG, LANES))
    gab_ref[0, :, 0:D_MODEL] = jax.nn.sigmoid(proj(OFF_GA, D_MODEL)).astype(BF16)
    gab_ref[0, :, D_MODEL:2 * D_MODEL] = jax.nn.sigmoid(proj(OFF_GA + D_MODEL, D_MODEL)).astype(BF16)


def _pos_pieces(pos):
    return ((pos // 256) * 256).astype(F32), (pos % 256).astype(F32)


def _inproj(x, norm_g, w_in, b_forget, fox_q_g, fox_k_g, nsa_q_g, nsa_k_g):
    B, S, D = x.shape
    tm = min(TM_PROJ, S)
    scale = HEAD_DIM ** -0.5

    c = [0]
    for n in (FOX_W, FOX_W, FOX_W, FOX_HEADS, NSA_W) + (NSA_KV_W,) * 6 + (3 * NSA_HEADS, D_MODEL, D_MODEL):
        c.append(c[-1] + n)
    fq, fk, fv, ff, nq, kc, vc, ks, vs, kw, vw, ng, ga, gb = [w_in[:, c[i]:c[i + 1]] for i in range(14)]
    perm = jnp.asarray([0, 4, 1, 5, 2, 6, 3, 7])
    nq = nq.reshape(D, NSA_HEADS, HEAD_DIM)[:, perm, :].reshape(D, NSA_W)
    padl = lambda a: jnp.pad(a, ((0, 0), (0, LANES - a.shape[1])))
    w = jnp.concatenate([fq, fk, fv, nq, kc, vc, ks, vs, kw, vw, padl(ff), padl(ng), ga, gb], axis=1).astype(BF16)
    assert w.shape[1] == N_PROJ

    bf = jnp.pad(b_forget.astype(F32), (0, LANES - FOX_HEADS))[None, :]
    gqa = jnp.tile(fox_q_g.astype(F32) * scale, FOX_HEADS)[None, :]
    gka = jnp.tile(fox_k_g.astype(F32), FOX_HEADS)[None, :]
    gqn = jnp.tile(nsa_q_g.astype(F32) * scale, NSA_HEADS)[None, :]
    gkn = jnp.tile(nsa_k_g.astype(F32), NSA_KV_HEADS)[None, :]

    r512 = jnp.arange(FOX_W)
    bd = jnp.where((r512[:, None] // HEAD_DIM) == (r512[None, :] // HEAD_DIM), 1.0 / HEAD_DIM, 0.0).astype(BF16)
    bd2 = bd[:LANES, :LANES]
    rt = jnp.arange(tm)
    tri = (rt[None, :] <= rt[:, None]).astype(BF16)

    heads = jnp.arange(FOX_HEADS)
    base = heads * LANES + jnp.where(heads % 2 == 0, HALF, 0)
    rows = jnp.arange(3 * LANES)
    piece, hlane = rows // LANES, rows % LANES
    cols = jnp.arange(FOX_HEADS * LANES)
    tgt_q = jnp.where(hlane < FOX_HEADS, base[jnp.minimum(hlane, FOX_HEADS - 1)] + 3 + piece, -1)
    tgt_k = jnp.where(hlane < FOX_HEADS, base[jnp.minimum(hlane, FOX_HEADS - 1)] + piece, -1)
    selq = (cols[None, :] == tgt_q[:, None]).astype(BF16)
    selk = -(cols[None, :] == tgt_k[:, None]).astype(BF16)
    off_in_blk = cols - base[cols // LANES]
    cq = ((off_in_blk >= 0) & (off_in_blk < 3)).astype(F32)[None, :]
    ck = ((off_in_blk >= 3) & (off_in_blk < 9)).astype(F32)[None, :]
    cv = (off_in_blk == 0).astype(F32)[None, :]

    blk = cols // LANES
    slope = 2.0 ** (-(blk + 1).astype(F32))
    o = cols % LANES - jnp.where(blk // NSA_REP == 0, HALF, 0)
    qc0 = jnp.where((o == 0) | (o == 1), slope, 0.0)[None, :]
    qca = jnp.where(o == 2, -slope, 0.0)[None, :]
    qcb = jnp.where(o == 3, -slope, 0.0)[None, :]
    kc0, kca, kcb, vone = _kv_rows()

    grid = (B, S // tm)
    row_spec = lambda n: pl.BlockSpec((1, tm, n), lambda b, s: (b, s, 0))
    consts = [norm_g.astype(F32)[None, :], w, bf, gqa, gka, gqn, gkn, bd, bd2, tri, selq, selk, cq, ck, cv,
              qc0, qca, qcb, kc0, kca, kcb, vone]
    out_widths = [(8 * LANES, BF16)] * 4 + [(LANES, F32)] * 2 + [(2 * LANES, BF16)] * 4 + \
                 [(LANES, F32), (2 * D_MODEL, BF16)]
    outs = pl.pallas_call(
        _inproj_kernel,
        grid=grid,
        in_specs=[row_spec(D)] + [_const_spec(a.shape) for a in consts],
        out_specs=[row_spec(n) for n, _ in out_widths],
        out_shape=[jax.ShapeDtypeStruct((B, S, n), dt) for n, dt in out_widths],
        scratch_shapes=[pltpu.VMEM((1, LANES), F32)],
        compiler_params=_cparams(("arbitrary", "arbitrary")),
        name="inproj",
    )(x, *consts)
    return outs


def _kv_rows():
    cols = jnp.arange(2 * LANES)
    o = cols % LANES - jnp.where(cols // LANES == 0, HALF, 0)
    row = lambda m: m.astype(F32)[None, :]
    return row((o == 2) | (o == 3)), row(o == 0), row(o == 1), row(o == 0)


def _kv_consts(pa, pb):
    c0, ca, cb, vone = _kv_rows()
    return (c0 + ca * pa[:, None] + cb * pb[:, None]).astype(BF16), vone


def _compress_kernel(kt_ref, vt_ref, w1k_ref, w1v_ref, posk_ref, posv_ref, pw1k_ref, pw1v_ref, w2k_ref, w2v_ref,
                     gk_ref, kcc_ref, vone_ref, kc_ref, vc_ref):
    nc = kc_ref.shape[2]

    def mlp(t_ref, w1_ref, pos_ref, pw1_ref, w2_ref):
        both = jnp.zeros((nc, 2 * CMP_HIDDEN), F32)
        for l in range(CMP_STRIDE):
            rows = t_ref[0, pl.ds(l, nc, stride=CMP_STRIDE), :].astype(BF16)
            both = both + _dot(rows, w1_ref[0, l])
        posw = _dot(pos_ref[...], pw1_ref[...])[0:1, :]
        pre = both[:, 0:CMP_HIDDEN] + pltpu.roll(both[:, CMP_HIDDEN:2 * CMP_HIDDEN], nc - 1, axis=0) + posw
        act = pre * (0.5 * (1.0 + jnp.tanh(0.7978845608028654 * (pre + 0.044715 * (pre * pre * pre)))))
        return _dot(act.astype(BF16), w2_ref[0])

    kc = mlp(kt_ref, w1k_ref, posk_ref, pw1k_ref, w2k_ref)
    msq = jnp.sum(kc * kc, axis=-1, keepdims=True) * (1.0 / HEAD_DIM)
    kc_ref[0, 0] = (kc * lax.rsqrt(msq + RMS_EPS) * gk_ref[0] + kcc_ref[0]).astype(BF16)
    vc = mlp(vt_ref, w1v_ref, posv_ref, pw1v_ref, w2v_ref)
    vc_ref[0, 0] = (vc + vone_ref[0]).astype(BF16)


def _compress(kcr, vcr, cmp_k_w1, cmp_k_w2, cmp_k_pos, cmp_v_w1, cmp_v_w2, cmp_v_pos, nsa_k_g):
    B, S, _ = kcr.shape
    G = NSA_KV_HEADS
    nc = S // CMP_STRIDE

    def w1_strided(w1):
        w = w1.reshape(2, CMP_STRIDE, HEAD_DIM, CMP_HIDDEN)
        w = jnp.concatenate([w[0], w[1]], axis=-1)
        z = jnp.zeros_like(w)
        return jnp.stack([jnp.concatenate([w, z], axis=1), jnp.concatenate([z, w], axis=1)]).astype(BF16)

    def w2_spread(w2):
        z = jnp.zeros_like(w2)
        return jnp.stack([jnp.concatenate([w2, z], 1), jnp.concatenate([z, w2], 1)]).astype(BF16)

    def pos8(p):
        return jnp.tile(p.reshape(1, CMP_BLOCK * HEAD_DIM), (8, 1)).astype(BF16)

    gk = nsa_k_g.astype(F32)
    z = jnp.zeros_like(gk)
    gk2 = jnp.stack([jnp.concatenate([gk, z]), jnp.concatenate([z, gk])])[:, None, :]
    cend = jnp.arange(nc) * CMP_STRIDE + CMP_BLOCK - 1
    kcc, vone = _kv_consts(*_pos_pieces(cend))
    kcc = kcc.astype(F32).reshape(nc, G, LANES).transpose(1, 0, 2)
    vone = vone.reshape(G, 1, LANES)

    tok = pl.BlockSpec((1, S, G * HEAD_DIM), lambda b, g: (b, 0, 0))
    per_g = lambda a: pl.BlockSpec((1,) + a.shape[1:], lambda b, g: (g,) + (0,) * (a.ndim - 1))
    w1k, w1v = w1_strided(cmp_k_w1), w1_strided(cmp_v_w1)
    pk, pv = pos8(cmp_k_pos), pos8(cmp_v_pos)
    pw1k, pw1v = cmp_k_w1.astype(BF16), cmp_v_w1.astype(BF16)
    w2k, w2v = w2_spread(cmp_k_w2), w2_spread(cmp_v_w2)
    out_spec = pl.BlockSpec((1, 1, nc, LANES), lambda b, g: (b, g, 0, 0))
    return pl.pallas_call(
        _compress_kernel,
        grid=(B, G),
        in_specs=[tok, tok, per_g(w1k), per_g(w1v), _const_spec(pk.shape), _const_spec(pv.shape),
                  _const_spec(pw1k.shape), _const_spec(pw1v.shape), per_g(w2k), per_g(w2v),
                  per_g(gk2), per_g(kcc), per_g(vone)],
        out_specs=[out_spec, out_spec],
        out_shape=[jax.ShapeDtypeStruct((B, G, nc, LANES), BF16)] * 2,
        compiler_params=_cparams(("arbitrary", "arbitrary")),
        name="compress",
    )(kcr, vcr, w1k, w1v, pk, pv, pw1k, pw1v, w2k, w2v, gk2, kcc, vone)


def _attn_first(s, v):
    m = jnp.max(s, axis=-1, keepdims=True)
    p = jnp.exp((s - m).astype(BF16))
    return m, _dot(p, v)


def _attn_step(carry, s, v):
    m, acc = carry
    m_new = jnp.maximum(m, jnp.max(s, axis=-1, keepdims=True))
    p = jnp.exp((s - m_new).astype(BF16))
    return m_new, jnp.exp(m - m_new) * acc + _dot(p, v)


def _nsa_kernel(q_ref, kc_ref, vc_ref, ks_ref, e_ref, vs_ref, kw_ref, vw_ref, gt_ref, ov_ref, o_ref,
                tiles_ref, sa_ref, sb_ref):
    tq = q_ref.shape[1]
    nc = kc_ref.shape[2]
    tk = TK_SLC
    rq = NSA_REP * tq
    g = pl.program_id(1)
    i = pl.program_id(2)
    q0 = i * tq

    q4 = jnp.concatenate([q_ref[0, :, LANES * r:LANES * (r + 1)] for r in range(NSA_REP)], axis=0)

    def qpos_of(shape):
        return q0 + (lax.broadcasted_iota(jnp.int32, shape, 0) & (tq - 1))

    def add_mask(s, valid):
        bias = jnp.where(valid, 0.0, NEG_INF)
        return (s.reshape(NSA_REP, tq, s.shape[1]) + bias[None]).reshape(s.shape)

    def qrow(n):
        return q0 + lax.broadcasted_iota(jnp.int32, (tq, n), 0)

    def kcol(n):
        return lax.broadcasted_iota(jnp.int32, (tq, n), 1)

    s = add_mask(_dot_nt(q4, kc_ref[0, 0]), qrow(nc) >= kcol(nc) * CMP_STRIDE + (CMP_BLOCK - 1))
    m = jnp.max(s, axis=-1, keepdims=True)
    e = jnp.exp(s - m)
    anyv = (qpos_of((rq, 1)) >= CMP_BLOCK - 1).astype(F32)
    p = e * (anyv / jnp.sum(e, axis=-1, keepdims=True))
    o_cmp = _dot(p.astype(BF16), vc_ref[0, 0])

    psum = p[0:tq] + p[tq:2 * tq] + p[2 * tq:3 * tq] + p[3 * tq:4 * tq]
    p_hi, p_mid, p_lo = _split3(psum)
    ov = ov_ref[...]
    imp = _dot(p_hi, ov) + _dot(p_mid, ov) + _dot(p_lo, ov)
    blk_i = lax.broadcasted_iota(jnp.int32, (LANES, tq), 0)
    blk_f = blk_i.astype(F32)
    qblk = (q0 + lax.broadcasted_iota(jnp.int32, (LANES, tq), 1)) // SLC_BLOCK
    forced = (blk_i == 0) | (blk_i == qblk) | (blk_i == qblk - 1)
    score = jnp.where(blk_i > qblk, -1.0, jnp.where(forced, FORCE_SCORE, imp.T))
    selb = jnp.full((LANES, tq), NEG_INF, F32)
    for _ in range(SLC_TOPK):
        mx = jnp.max(score, axis=0, keepdims=True)
        first = jnp.min(jnp.where(score == mx, blk_f, float(LANES)), axis=0, keepdims=True)
        hit = blk_f == first
        selb = jnp.where(hit, 0.0, selb)
        score = jnp.where(hit, -3e38, score)
    sb = selb.T.astype(BF16)
    q2 = jnp.concatenate([q4, jnp.concatenate([sb] * NSA_REP, axis=0)], axis=1)

    def slc_qk(j):
        start = pl.multiple_of(j * tk, tk)
        kk = jnp.concatenate([ks_ref[0, pl.ds(start, tk), :], e_ref[pl.ds(start, tk), :]], axis=1)
        return _dot_nt(q2, kk)

    def slc_v(j):
        return vs_ref[0, pl.ds(pl.multiple_of(j * tk, tk), tk), :]

    jd = q0 // tk
    blocks_per_tile = tk // SLC_BLOCK
    anysel = jnp.max(selb, axis=1, keepdims=True)
    cnt = jnp.int32(0)
    for j in range(ks_ref.shape[1] // tk):
        tile_sel = jnp.max(anysel[blocks_per_tile * j:blocks_per_tile * (j + 1), :]) > -1.0
        tiles_ref[cnt] = j
        cnt = cnt + (tile_sel & (j < jd)).astype(jnp.int32)

    n_seq = cnt + 1

    def tile_of(n):
        return jnp.where(n == 0, jd, tiles_ref[jnp.maximum(n - 1, 0)])

    def slc_step(carry, s_buf, n):
        return _attn_step(carry, s_buf[...], slc_v(tile_of(n)))

    sa_ref[...] = add_mask(slc_qk(jd), jd * tk + kcol(tk) <= qrow(tk))

    def slc_body(p, carry):
        sb_ref[...] = slc_qk(tile_of(2 * p + 1))
        carry = slc_step(carry, sa_ref, 2 * p)
        sa_ref[...] = slc_qk(tile_of(jnp.minimum(2 * p + 2, n_seq - 1)))
        return slc_step(carry, sb_ref, 2 * p + 1)

    carry = (jnp.full((rq, 1), NEG_INF, F32), jnp.zeros((rq, LANES), F32))
    carry = lax.fori_loop(0, n_seq // 2, slc_body, carry)
    _, acc_slc = lax.cond(n_seq % 2 == 1, lambda c: slc_step(c, sa_ref, n_seq - 1), lambda c: c, carry)

    nw = WINDOW + tq
    wstart = pl.multiple_of(jnp.maximum(q0 - WINDOW, 0), tq)
    dist = qrow(nw) - (wstart + kcol(nw))
    s = add_mask(_dot_nt(q4, kw_ref[0, pl.ds(wstart, nw), :]),
                 lax.bitcast_convert_type(dist, jnp.uint32) < WINDOW)
    _, acc_win = _attn_first(s, vw_ref[0, pl.ds(wstart, nw), :])

    lane_r = lax.broadcasted_iota(jnp.int32, (rq, LANES), 1)
    data0 = HALF * g
    ones_lane = HALF - data0

    def normalise(acc):
        l = jnp.sum(jnp.where(lane_r == ones_lane, acc, 0.0), axis=-1, keepdims=True)
        return acc * (1.0 / l)

    o_slc = normalise(acc_slc)
    o_win = normalise(acc_win)
    gt = gt_ref[0]
    lane_g = lax.broadcasted_iota(jnp.int32, (tq, LANES), 1)
    is_data_q = (lane_g >= data0) & (lane_g < data0 + HALF)
    for r in range(NSA_REP):
        col = 3 * (NSA_REP * g + r)
        gate = [jnp.sum(jnp.where(lane_g == col + b, gt, 0.0), axis=-1, keepdims=True) for b in range(3)]
        rows = slice(r * tq, (r + 1) * tq)
        o = gate[0] * o_cmp[rows] + gate[1] * o_slc[rows] + gate[2] * o_win[rows]
        o_ref[0, :, LANES * r:LANES * (r + 1)] = jnp.where(is_data_q, o, 0.0).astype(BF16)


def _nsa(qn, kca, vca, ksa, vsa, kwa, vwa, gt):
    B, S, _ = qn.shape
    G = NSA_KV_HEADS
    tq = min(TQ_NSA, S)
    nc = S // CMP_STRIDE
    n_slc = S // SLC_BLOCK
    assert n_slc <= LANES and S % TK_SLC == 0

    cs = jnp.arange(nc)[:, None] * CMP_STRIDE
    ss = jnp.arange(LANES)[None, :] * SLC_BLOCK
    ovl = jnp.clip(jnp.minimum(cs + CMP_BLOCK, ss + SLC_BLOCK) - jnp.maximum(cs, ss), 0, None)
    valid = (jnp.arange(nc)[:, None] < (S - CMP_BLOCK) // CMP_STRIDE + 1) & (jnp.arange(LANES)[None, :] < n_slc)
    ov = jnp.where(valid, ovl.astype(F32) / CMP_BLOCK, 0.0).astype(BF16)
    e1h = (jnp.arange(S)[:, None] // SLC_BLOCK == jnp.arange(LANES)[None, :]).astype(BF16)

    q_spec = pl.BlockSpec((1, tq, NSA_REP * LANES), lambda b, g, i: (b, i, g))
    c_spec = pl.BlockSpec((1, 1, nc, LANES), lambda b, g, i: (b, g, 0, 0))
    kv_spec = pl.BlockSpec((1, S, LANES), lambda b, g, i: (b, 0, g))
    return pl.pallas_call(
        _nsa_kernel,
        grid=(B, G, S // tq),
        in_specs=[q_spec, c_spec, c_spec, kv_spec, _const_spec(e1h.shape), kv_spec, kv_spec, kv_spec,
                  pl.BlockSpec((1, tq, LANES), lambda b, g, i: (b, i, 0)), _const_spec(ov.shape)],
        out_specs=q_spec,
        out_shape=jax.ShapeDtypeStruct((B, S, NSA_HEADS * LANES), BF16),
        scratch_shapes=[pltpu.SMEM((S // TK_SLC + 1,), jnp.int32),
                        pltpu.VMEM((NSA_REP * tq, TK_SLC), F32), pltpu.VMEM((NSA_REP * tq, TK_SLC), F32)],
        compiler_params=_cparams(("arbitrary", "arbitrary", "arbitrary")),
        name="nsa",
    )(qn, kca, vca, ksa, e1h, vsa, kwa, vwa, gt, ov)


def _fox_kernel(q_ref, k_ref, v_ref, o_ref, sa_ref, sb_ref):
    tq = q_ref.shape[1]
    i = pl.program_id(2)
    lane = lax.broadcasted_iota(jnp.int32, (tq, LANES), 1)
    causal = lax.broadcasted_iota(jnp.int32, (tq, tq), 1) <= lax.broadcasted_iota(jnp.int32, (tq, tq), 0)

    def cols(hh):
        return slice(LANES * hh, LANES * (hh + 1))

    def qk(hh, j):
        start = pl.multiple_of(j * tq, tq)
        return _dot_nt(q_ref[0, :, cols(hh)], k_ref[0, pl.ds(start, tq), cols(hh)])

    def vtile(hh, j):
        return v_ref[0, pl.ds(pl.multiple_of(j * tq, tq), tq), cols(hh)]

    n_tiles = i + 1

    def tile_of(n):
        return jnp.where(n == 0, i, n - 1)

    def step_all(carry, s_buf, n):
        t = tile_of(n)
        return tuple(_attn_step(carry[hh], s_buf[hh], vtile(hh, t)) for hh in range(2))

    for hh in range(2):
        sa_ref[hh] = jnp.where(causal, qk(hh, i), NEG_INF)
    init = tuple((jnp.full((tq, 1), NEG_INF, F32), jnp.zeros((tq, LANES), F32)) for hh in range(2))

    def body(p, carry):
        for hh in range(2):
            sb_ref[hh] = qk(hh, tile_of(2 * p + 1))
        carry = step_all(carry, sa_ref, 2 * p)
        nxt = jnp.minimum(2 * p + 2, n_tiles - 1)
        for hh in range(2):
            sa_ref[hh] = qk(hh, tile_of(nxt))
        return step_all(carry, sb_ref, 2 * p + 1)

    carry = lax.fori_loop(0, n_tiles // 2, body, init)
    carry = lax.cond(n_tiles % 2 == 1, lambda c: step_all(c, sa_ref, n_tiles - 1), lambda c: c, carry)
    outs = []
    for hh in range(2):
        acc = carry[hh][1]
        ones_lane = HALF if hh == 0 else 0
        l = jnp.sum(jnp.where(lane == ones_lane, acc, 0.0), axis=-1, keepdims=True)
        outs.append(acc * (1.0 / l))
    o_ref[0] = jnp.where(lane < HALF, outs[0], outs[1]).astype(BF16)


def _fox_bounded_kernel(q_ref, k_ref, v_ref, o_ref, qs_ref, acc_ref):
    tq = q_ref.shape[1]
    i = pl.program_id(2)
    lane = lax.broadcasted_iota(jnp.int32, (tq, LANES), 1)
    causal = lax.broadcasted_iota(jnp.int32, (tq, tq), 1) <= lax.broadcasted_iota(jnp.int32, (tq, tq), 0)

    def cols(hh):
        return slice(LANES * hh, LANES * (hh + 1))

    def ktile(hh, j):
        return k_ref[0, pl.ds(pl.multiple_of(j * tq, tq), tq), cols(hh)]

    def vtile(hh, j):
        return v_ref[0, pl.ds(pl.multiple_of(j * tq, tq), tq), cols(hh)]

    for hh in range(2):
        q = q_ref[0, :, cols(hh)]
        s = jnp.where(causal, _dot_nt(q, ktile(hh, i)), NEG_INF)
        m = jnp.max(s, axis=-1, keepdims=True)
        acc_ref[hh] = _dot(jnp.exp((s - m).astype(BF16)), vtile(hh, i))
        qf = q.astype(F32)
        free0 = (HALF if hh == 0 else 0) + 6
        for off, piece in enumerate(_split3(-m)):
            qf = jnp.where(lane == free0 + off, piece.astype(F32), qf)
        qs_ref[hh] = qf.astype(BF16)

    def body(j, _):
        for hh in range(2):
            p = jnp.exp(_dot_nt(qs_ref[hh], ktile(hh, j)).astype(BF16))
            acc_ref[hh] += _dot(p, vtile(hh, j))
        return _

    lax.fori_loop(0, i, body, 0)
    outs = []
    for hh in range(2):
        acc = acc_ref[hh]
        ones_lane = HALF if hh == 0 else 0
        l = jnp.sum(jnp.where(lane == ones_lane, acc, 0.0), axis=-1, keepdims=True)
        outs.append(acc * (1.0 / l))
    o_ref[0] = jnp.where(lane < HALF, outs[0], outs[1]).astype(BF16)


FOX_MAX_EXPONENT = 60.0


def _fox_scores_bounded(fox_q_g, fox_k_g):
    bound = HEAD_DIM ** 0.5 * jnp.max(jnp.abs(fox_q_g)) * jnp.max(jnp.abs(fox_k_g))
    return 2.04 * bound + 0.05 <= FOX_MAX_EXPONENT


def _fox(qa, ka, va, scores_bounded):
    B, S, _ = qa.shape
    tq = min(TQ_FOX, S)
    q_spec = pl.BlockSpec((1, tq, 2 * LANES), lambda b, h, i: (b, i, h))
    kv_spec = pl.BlockSpec((1, S, 2 * LANES), lambda b, h, i: (b, 0, h))

    def call(body, scratch, name):
        return pl.pallas_call(
            body,
            grid=(B, FOX_HEADS // 2, S // tq),
            in_specs=[q_spec, kv_spec, kv_spec],
            out_specs=pl.BlockSpec((1, tq, LANES), lambda b, h, i: (b, i, h)),
            out_shape=jax.ShapeDtypeStruct((B, S, FOX_W), BF16),
            scratch_shapes=scratch,
            compiler_params=_cparams(("arbitrary", "arbitrary", "arbitrary")),
            name=name,
        )

    general = call(_fox_kernel, [pltpu.VMEM((2, tq, tq), F32), pltpu.VMEM((2, tq, tq), F32)], "fox")
    bounded = call(_fox_bounded_kernel, [pltpu.VMEM((2, tq, LANES), BF16), pltpu.VMEM((2, tq, LANES), F32)],
                   "fox_bounded")
    return lax.cond(scores_bounded, bounded, general, qa, ka, va)


R_E1, R_E2, R_W1, R_W2, R_RANK1, R_RANK2 = range(6)


def _merge_kernel(x_ref, oa_ref, ob_ref, gab_ref, wa_ref, wb_ref, wo_ref, g2_ref, wr_hi_ref, wr_lo_ref, br_ref,
                  ltri_ref, x1_ref, route_ref, cnt_ref, carry_ref):
    tm = x_ref.shape[1]

    @pl.when((pl.program_id(0) == 0) & (pl.program_id(1) == 0))
    def _():
        carry_ref[...] = jnp.zeros_like(carry_ref)

    out_a = _dot(oa_ref[0], wa_ref[...])
    out_b = _dot(ob_ref[0], wb_ref[...])
    mix = gab_ref[0, :, 0:D_MODEL].astype(F32) * out_a + gab_ref[0, :, D_MODEL:2 * D_MODEL].astype(F32) * out_b
    x1 = x_ref[0] + _dot(mix.astype(BF16), wo_ref[...])
    x1_ref[...] = x1
    h2 = x1 * lax.rsqrt(jnp.mean(x1 * x1, axis=-1, keepdims=True) + RMS_EPS) * g2_ref[...]

    h_hi = h2.astype(BF16)
    h_lo = (h2 - h_hi.astype(F32)).astype(BF16)
    logits = _dot(h_hi, wr_hi_ref[...]) + (_dot(h_hi, wr_lo_ref[...]) + _dot(h_lo, wr_hi_ref[...])) + br_ref[...]
    lane = lax.broadcasted_iota(jnp.int32, (tm, LANES), 1)
    lane_f = lane.astype(F32)

    def first_argmax(vals):
        mx = jnp.max(vals, axis=-1, keepdims=True)
        idx = jnp.min(jnp.where(vals == mx, lane_f, float(LANES)), axis=-1, keepdims=True)
        return mx, idx

    is_grp = (lane >= N_EXPERTS) & (lane < N_EXPERTS + N_GROUPS)
    gl = jnp.where(is_grp, logits, NEG_INF)
    gmax, gidx = first_argmax(gl)
    p_g = 1.0 / jnp.sum(jnp.where(is_grp, jnp.exp(gl - gmax), 0.0), axis=-1, keepdims=True)
    e_lo = (gidx - float(N_EXPERTS)) * float(EXPERTS_PER_GROUP)
    in_grp = (lane_f >= e_lo) & (lane_f < e_lo + float(EXPERTS_PER_GROUP))
    el = jnp.where(in_grp, logits, NEG_INF)
    m1, i1 = first_argmax(el)
    m2, i2 = first_argmax(jnp.where(lane_f == i1, NEG_INF, el))
    e2 = jnp.exp(m2 - m1)
    w1 = p_g / (1.0 + e2)
    w2 = p_g * e2 / (1.0 + e2)

    hit1 = lane_f == i1
    hit2 = lane_f == i2
    onehot = jnp.where(hit1 | hit2, 1.0, 0.0)
    before = carry_ref[...] + _dot(ltri_ref[...], onehot.astype(BF16))
    rank1 = jnp.sum(jnp.where(hit1, before, 0.0), axis=-1, keepdims=True)
    rank2 = jnp.sum(jnp.where(hit2, before, 0.0), axis=-1, keepdims=True)
    total = carry_ref[...] + jnp.sum(onehot, axis=0, keepdims=True)
    carry_ref[...] = total
    cnt_ref[...] = total

    rec = jnp.zeros((tm, LANES), F32)
    for k, val in ((R_E1, i1), (R_E2, i2), (R_W1, w1), (R_W2, w2), (R_RANK1, rank1), (R_RANK2, rank2)):
        rec = jnp.where(lane == k, val, rec)
    route_ref[...] = rec


def _merge(x, oa, ob, gab, w_fox_up, w_nsa_up, w_out, norm_ffn_g, w_group, b_group, w_router, b_router):
    B, S, D = x.shape
    tm = min(TM_MERGE, S)
    wa = w_fox_up.astype(BF16)
    wn = w_nsa_up.reshape(NSA_KV_HEADS, NSA_REP, HEAD_DIM, D)
    z = jnp.zeros_like(wn[0])
    wb = jnp.stack([jnp.concatenate([wn[0], z], axis=1), jnp.concatenate([z, wn[1]], axis=1)])
    wb = wb.reshape(NSA_HEADS * LANES, D).astype(BF16)
    wo = w_out.astype(BF16)
    wr = jnp.pad(jnp.concatenate([w_router, w_group], axis=1).astype(F32),
                 ((0, 0), (0, LANES - N_EXPERTS - N_GROUPS)))
    wr_hi = wr.astype(BF16)
    wr_lo = (wr - wr_hi.astype(F32)).astype(BF16)
    br = jnp.pad(jnp.concatenate([b_router, b_group]).astype(F32), (0, LANES - N_EXPERTS - N_GROUPS))[None, :]
    g2 = norm_ffn_g.astype(F32)[None, :]
    rt = jnp.arange(tm)
    ltri = (rt[None, :] < rt[:, None]).astype(BF16)

    row = lambda n: pl.BlockSpec((1, tm, n), lambda b, s: (b, s, 0))
    flat = lambda n: pl.BlockSpec((tm, n), lambda b, s: (b * (S // tm) + s, 0))
    consts = [wa, wb, wo, g2, wr_hi, wr_lo, br, ltri]
    return pl.pallas_call(
        _merge_kernel,
        grid=(B, S // tm),
        in_specs=[row(D), row(FOX_W), row(NSA_HEADS * LANES), row(2 * D)] + [_const_spec(a.shape) for a in consts],
        out_specs=[flat(D), flat(LANES), _const_spec((1, LANES))],
        out_shape=[jax.ShapeDtypeStruct((B * S, D), F32), jax.ShapeDtypeStruct((B * S, LANES), F32),
                   jax.ShapeDtypeStruct((1, LANES), F32)],
        scratch_shapes=[pltpu.VMEM((1, LANES), F32)],
        compiler_params=_cparams(("arbitrary", "arbitrary")),
        name="merge",
    )(x, oa, ob, gab, *consts)


def _row_copy(src_ref, src_row, dst_ref, dst_row, sem):
    return pltpu.make_async_copy(src_ref.at[pl.ds(src_row, 1), :], dst_ref.at[pl.ds(dst_row, 1), :], sem)


def _dispatch_kernel(row1_ref, row2_ref, clear_ref, x1_ref, xs_ref, zero_ref, sem, zsem):
    tm = x1_ref.shape[0]
    base = pl.program_id(0) * tm

    @pl.when(pl.program_id(0) == 0)
    def _():
        zero_ref[...] = jnp.zeros_like(zero_ref)

        def clear(c):
            start = pl.multiple_of(jnp.maximum(clear_ref[c], 0), zero_ref.shape[0])
            return pltpu.make_async_copy(zero_ref, xs_ref.at[pl.ds(start, zero_ref.shape[0]), :], zsem)

        for c in range(clear_ref.shape[0]):
            @pl.when(clear_ref[c] >= 0)
            def _(c=c):
                clear(c).start()
        for c in range(clear_ref.shape[0]):
            @pl.when(clear_ref[c] >= 0)
            def _(c=c):
                clear(c).wait()

    def start(t, _):
        _row_copy(x1_ref, t, xs_ref, row1_ref[base + t], sem).start()
        _row_copy(x1_ref, t, xs_ref, row2_ref[base + t], sem).start()
        return _

    def wait(t, _):
        _row_copy(x1_ref, t, xs_ref, row1_ref[base + t], sem).wait()
        _row_copy(x1_ref, t, xs_ref, row2_ref[base + t], sem).wait()
        return _

    lax.fori_loop(0, tm, start, 0, unroll=8)
    lax.fori_loop(0, tm, wait, 0, unroll=8)


def _experts_kernel(tile_e_ref, n_used_ref, xs_ref, g2_ref, wg_ref, wu_ref, wd_ref, ys_ref,
                    wg_bf, wu_bf, wd_bf):
    k = pl.program_id(0)
    used = k < n_used_ref[0]

    @pl.when(used & ((k == 0) | (tile_e_ref[k] != tile_e_ref[jnp.maximum(k - 1, 0)])))
    def _():
        wg_bf[...] = wg_ref[0, 0].astype(BF16)
        wu_bf[...] = wu_ref[0, 0].astype(BF16)
        wd_bf[...] = wd_ref[0, 0].astype(BF16)

    @pl.when(used)
    def _():
        x = xs_ref[...]
        h = (x * lax.rsqrt(jnp.mean(x * x, axis=-1, keepdims=True) + RMS_EPS) * g2_ref[...]).astype(BF16)
        a = _dot(h, wg_bf[...])
        hid = (a * jax.nn.sigmoid(a)) * _dot(h, wu_bf[...])
        ys_ref[...] = _dot(hid.astype(BF16), wd_bf[...])

    @pl.when(jnp.logical_not(used))
    def _():
        ys_ref[...] = jnp.zeros_like(ys_ref)


def _combine_kernel(row1_ref, row2_ref, x1_ref, route_ref, ys_ref, o_ref, y1_ref, y2_ref, sem):
    tm = x1_ref.shape[0]
    base = pl.program_id(0) * tm

    def start(t, _):
        _row_copy(ys_ref, row1_ref[base + t], y1_ref, t, sem).start()
        _row_copy(ys_ref, row2_ref[base + t], y2_ref, t, sem).start()
        return _

    def wait(t, _):
        _row_copy(ys_ref, row1_ref[base + t], y1_ref, t, sem).wait()
        _row_copy(ys_ref, row2_ref[base + t], y2_ref, t, sem).wait()
        return _

    lax.fori_loop(0, tm, start, 0, unroll=8)
    lax.fori_loop(0, tm, wait, 0, unroll=8)
    rec = route_ref[...]
    lane = lax.broadcasted_iota(jnp.int32, rec.shape, 1)
    w1 = jnp.sum(jnp.where(lane == R_W1, rec, 0.0), axis=-1, keepdims=True)
    w2 = jnp.sum(jnp.where(lane == R_W2, rec, 0.0), axis=-1, keepdims=True)
    o_ref[0] = x1_ref[...] + (w1 * y1_ref[...] + w2 * y2_ref[...])


def _moe(x1, route, cnt, norm_ffn_g, w_gate, w_up, w_down, layer, B, S):
    T, D = x1.shape
    tme = min(TM_EXPERT, T)
    tmd = min(TM_DISPATCH, S)

    counts = cnt[0, :N_EXPERTS].astype(jnp.int32)
    tiles_per_e = (counts + tme - 1) // tme
    tile_end = jnp.cumsum(tiles_per_e)
    offs = (tile_end - tiles_per_e) * tme
    last_tile_row = jnp.where(tiles_per_e > 0, (tile_end - 1) * tme, -1).astype(jnp.int32)
    rec = route[:, :8].T.astype(jnp.int32)
    experts = jnp.arange(N_EXPERTS, dtype=jnp.int32)[:, None]
    row_of = lambda e, rank: rank + jnp.sum(jnp.where(e[None, :] == experts, offs[:, None], 0), axis=0)
    row1 = row_of(rec[R_E1], rec[R_RANK1])
    row2 = row_of(rec[R_E2], rec[R_RANK2])
    max_tiles = (2 * T) // tme + N_EXPERTS
    tile_e = jnp.sum(jnp.arange(max_tiles, dtype=jnp.int32)[:, None] >= tile_end[None, :], axis=1)
    tile_e = jnp.minimum(tile_e, N_EXPERTS - 1).astype(jnp.int32)
    n_used = tile_end[-1:].astype(jnp.int32)
    n_rows = max_tiles * tme
    spare = n_used + jnp.arange(N_EXPERTS, dtype=jnp.int32)
    clear_rows = jnp.concatenate([last_tile_row, jnp.where(spare < max_tiles, spare * tme, -1)]).astype(jnp.int32)

    xs = pl.pallas_call(
        _dispatch_kernel,
        grid_spec=pltpu.PrefetchScalarGridSpec(
            num_scalar_prefetch=3,
            grid=(T // tmd,),
            in_specs=[pl.BlockSpec((tmd, D), lambda t, r1, r2, lt: (t, 0))],
            out_specs=pl.BlockSpec(memory_space=pl.ANY),
            scratch_shapes=[pltpu.VMEM((tme, D), F32), pltpu.SemaphoreType.DMA(()), pltpu.SemaphoreType.DMA(())],
        ),
        out_shape=jax.ShapeDtypeStruct((n_rows, D), F32),
        compiler_params=_cparams(("arbitrary",)),
        name="dispatch",
    )(row1, row2, clear_rows, x1)

    g2 = norm_ffn_g.astype(F32)[None, :]
    w_spec = lambda shape: pl.BlockSpec((1, 1) + shape, lambda k, te, nu: (layer, te[k], 0, 0))
    ys = pl.pallas_call(
        _experts_kernel,
        grid_spec=pltpu.PrefetchScalarGridSpec(
            num_scalar_prefetch=2,
            grid=(max_tiles,),
            in_specs=[pl.BlockSpec((tme, D), lambda k, te, nu: (jnp.minimum(k, nu[0] - 1), 0)),
                      pl.BlockSpec((1, D), lambda k, te, nu: (0, 0)),
                      w_spec((D, D_EXPERT)), w_spec((D, D_EXPERT)), w_spec((D_EXPERT, D))],
            out_specs=pl.BlockSpec((tme, D), lambda k, te, nu: (k, 0)),
            scratch_shapes=[pltpu.VMEM((D, D_EXPERT), BF16), pltpu.VMEM((D, D_EXPERT), BF16),
                            pltpu.VMEM((D_EXPERT, D), BF16)],
        ),
        out_shape=jax.ShapeDtypeStruct((n_rows, D), F32),
        compiler_params=_cparams(("arbitrary",)),
        name="experts",
    )(tile_e, n_used, xs, g2, w_gate, w_up, w_down)

    return pl.pallas_call(
        _combine_kernel,
        grid_spec=pltpu.PrefetchScalarGridSpec(
            num_scalar_prefetch=2,
            grid=(T // tmd,),
            in_specs=[pl.BlockSpec((tmd, D), lambda t, r1, r2: (t, 0)),
                      pl.BlockSpec((tmd, LANES), lambda t, r1, r2: (t, 0)),
                      pl.BlockSpec(memory_space=pl.ANY)],
            out_specs=pl.BlockSpec((1, tmd, D), lambda t, r1, r2: (t // (S // tmd), t % (S // tmd), 0)),
            scratch_shapes=[pltpu.VMEM((tmd, D), F32), pltpu.VMEM((tmd, D), F32), pltpu.SemaphoreType.DMA(())],
        ),
        out_shape=jax.ShapeDtypeStruct((B, S, D), F32),
        compiler_params=_cparams(("arbitrary",)),
        name="combine",
    )(row1, row2, x1, route, ys)


def kernel(x, norm_mix_g, w_in, b_forget, fox_q_g, fox_k_g, nsa_q_g, nsa_k_g, cmp_k_w1, cmp_k_w2, cmp_k_pos,
           cmp_v_w1, cmp_v_w2, cmp_v_pos, w_fox_up, w_nsa_up, w_out, norm_ffn_g, w_group, b_group, w_router,
           b_router, w_gate, w_up, w_down):
    B, S, D = x.shape
    for l in range(w_in.shape[0]):
        qa, ka, va, qn, kcr, vcr, ksa, vsa, kwa, vwa, gt, gab = _inproj(
            x, norm_mix_g[l], w_in[l], b_forget[l], fox_q_g[l], fox_k_g[l], nsa_q_g[l], nsa_k_g[l])
        kca, vca = _compress(kcr, vcr, cmp_k_w1[l], cmp_k_w2[l], cmp_k_pos[l],
                             cmp_v_w1[l], cmp_v_w2[l], cmp_v_pos[l], nsa_k_g[l])
        ob = _nsa(qn, kca, vca, ksa, vsa, kwa, vwa, gt)
        oa = _fox(qa, ka, va, _fox_scores_bounded(fox_q_g[l], fox_k_g[l]))
        x1, route, cnt = _merge(x, oa, ob, gab, w_fox_up[l], w_nsa_up[l], w_out[l], norm_ffn_g[l],
                                w_group[l], b_group[l], w_router[l], b_router[l])
        x = _moe(x1, route, cnt, norm_ffn_g[l], w_gate, w_up, w_down, l, B, S)
    return x
```

```python
import functools

import jax
import jax.numpy as jnp
from jax import lax
from jax.experimental import pallas as pl
from jax.experimental.pallas import tpu as pltpu

F32 = jnp.float32
BF16 = jnp.bfloat16

D_MODEL = 1024
HEAD_DIM = 64
FOX_HEADS = 8
NSA_HEADS = 8
NSA_KV_HEADS = 2
NSA_REP = NSA_HEADS // NSA_KV_HEADS
CMP_BLOCK = 32
CMP_STRIDE = 16
CMP_HIDDEN = 256
SLC_BLOCK = 64
SLC_TOPK = 16
WINDOW = 512
N_GROUPS = 4
EXPERTS_PER_GROUP = 4
N_EXPERTS = N_GROUPS * EXPERTS_PER_GROUP
D_EXPERT = 512
RMS_EPS = 1e-6
NEG_INF = -1e30
FORCE_SCORE = 1e4

LANES = 128
HALF = LANES // 2
VMEM_LIMIT = 56 * 1024 * 1024

FOX_W = FOX_HEADS * HEAD_DIM
NSA_W = NSA_HEADS * HEAD_DIM
NSA_KV_W = NSA_KV_HEADS * HEAD_DIM

OFF_FQ = 0
OFF_FK = OFF_FQ + FOX_W
OFF_FV = OFF_FK + FOX_W
OFF_NQ = OFF_FV + FOX_W
OFF_KC = OFF_NQ + NSA_W
OFF_VC = OFF_KC + NSA_KV_W
OFF_KS = OFF_VC + NSA_KV_W
OFF_VS = OFF_KS + NSA_KV_W
OFF_KW = OFF_VS + NSA_KV_W
OFF_VW = OFF_KW + NSA_KV_W
OFF_FF = OFF_VW + NSA_KV_W
OFF_NG = OFF_FF + LANES
OFF_GA = OFF_NG + LANES
N_PROJ = OFF_GA + 2 * D_MODEL

TM_PROJ = 512
TQ_FOX = 512
TQ_NSA = 256
TK_SLC = 256
TM_MERGE = 512
TM_EXPERT = 256
TM_DISPATCH = 512


def _dot(a, b):
    return jnp.dot(a, b, preferred_element_type=F32)


def _dot_nt(a, b):
    return lax.dot_general(a, b, (((1,), (1,)), ((), ())), preferred_element_type=F32)


def _split3(x):
    hi = x.astype(BF16)
    r = x - hi.astype(F32)
    mid = r.astype(BF16)
    lo = (r - mid.astype(F32)).astype(BF16)
    return hi, mid, lo


def _cparams(sem):
    return pltpu.CompilerParams(dimension_semantics=sem, vmem_limit_bytes=VMEM_LIMIT)


def _const_spec(shape):
    nd = len(shape)
    return pl.BlockSpec(shape, lambda *_: (0,) * nd)


def _inproj_kernel(x_ref, g_ref, w_ref, bf_ref, gqa_ref, gka_ref, gqn_ref, gkn_ref,
                   bd_ref, bd2_ref, tri_ref, selq_ref, selk_ref, cq_ref, ck_ref, cv_ref,
                   qc0_ref, qca_ref, qcb_ref, kc0_ref, kca_ref, kcb_ref, vone_ref,
                   qa_ref, ka_ref, va_ref, qn_ref, kcr_ref, vcr_ref,
                   ksa_ref, vsa_ref, kwa_ref, vwa_ref, gt_ref, gab_ref,
                   carry_ref):
    tm = x_ref.shape[1]

    @pl.when(pl.program_id(1) == 0)
    def _():
        carry_ref[...] = jnp.zeros_like(carry_ref)

    x = x_ref[0]
    y = x * lax.rsqrt(jnp.mean(x * x, axis=-1, keepdims=True) + RMS_EPS)
    h = (y * g_ref[...]).astype(BF16)

    def proj(off, n):
        return _dot(h, w_ref[:, off:off + n])

    lo_half = lax.broadcasted_iota(jnp.int32, (tm, LANES), 1) < HALF

    pos = pl.program_id(1) * tm + lax.broadcasted_iota(jnp.int32, (tm, 1), 0)
    pos_a = ((pos >> 8) << 8).astype(F32)
    pos_b = (pos & 255).astype(F32)

    def pos_channels(c0_ref, ca_ref, cb_ref, k):
        blk = slice(LANES * k, LANES * (k + 1))
        return c0_ref[:, blk] + ca_ref[:, blk] * pos_a + cb_ref[:, blk] * pos_b

    def headnorm(z, bd, grow):
        msq = _dot((z * z).astype(BF16), bd[...])
        return z * lax.rsqrt(msq + RMS_EPS) * grow[...]

    def spread_pairs(out_ref, src, aug):
        for m in range(4):
            s = src[:, LANES * m:LANES * (m + 1)]
            out_ref[0, :, LANES * 2 * m:LANES * (2 * m + 1)] = jnp.where(lo_half, s, aug(2 * m)).astype(BF16)
            out_ref[0, :, LANES * (2 * m + 1):LANES * (2 * m + 2)] = jnp.where(lo_half, aug(2 * m + 1), s).astype(BF16)

    zf = proj(OFF_FF, LANES) + bf_ref[...]
    logf = jnp.minimum(zf, 0.0) - jnp.log(1.0 + jnp.exp(-jnp.abs(zf)))
    l_hi, l_mid, l_lo = _split3(logf)
    tri = tri_ref[...]
    cum = carry_ref[...] + (_dot(tri, l_hi) + _dot(tri, l_mid) + _dot(tri, l_lo))
    carry_ref[...] = cum[tm - 1:tm, :]
    pq = jnp.concatenate(_split3(cum), axis=1)
    augq = _dot(pq, selq_ref[...]) + cq_ref[...]
    augk = _dot(pq, selk_ref[...]) + ck_ref[...]

    zq = headnorm(proj(OFF_FQ, FOX_W), bd_ref, gqa_ref)
    spread_pairs(qa_ref, zq, lambda k: augq[:, LANES * k:LANES * (k + 1)])
    zk = headnorm(proj(OFF_FK, FOX_W), bd_ref, gka_ref)
    spread_pairs(ka_ref, zk, lambda k: augk[:, LANES * k:LANES * (k + 1)])
    zv = proj(OFF_FV, FOX_W)
    spread_pairs(va_ref, zv, lambda k: cv_ref[:, LANES * k:LANES * (k + 1)])

    zn = headnorm(proj(OFF_NQ, NSA_W), bd_ref, gqn_ref)
    for m in range(NSA_REP):
        s = zn[:, LANES * m:LANES * (m + 1)]
        c0 = pos_channels(qc0_ref, qca_ref, qcb_ref, m)
        c1 = pos_channels(qc0_ref, qca_ref, qcb_ref, NSA_REP + m)
        qn_ref[0, :, LANES * m:LANES * (m + 1)] = jnp.where(lo_half, s, c0).astype(BF16)
        qn_ref[0, :, LANES * (NSA_REP + m):LANES * (NSA_REP + m + 1)] = jnp.where(lo_half, c1, s).astype(BF16)

    kcr_ref[0] = proj(OFF_KC, NSA_KV_W)
    vcr_ref[0] = proj(OFF_VC, NSA_KV_W)

    kp0 = pos_channels(kc0_ref, kca_ref, kcb_ref, 0)
    kp1 = pos_channels(kc0_ref, kca_ref, kcb_ref, 1)

    def kv_pair(k_out, v_out, off_k, off_v):
        zk2 = headnorm(proj(off_k, NSA_KV_W), bd2_ref, gkn_ref)
        k_out[0, :, 0:LANES] = jnp.where(lo_half, zk2, kp0).astype(BF16)
        k_out[0, :, LANES:2 * LANES] = jnp.where(lo_half, kp1, zk2).astype(BF16)
        zv2 = proj(off_v, NSA_KV_W)
        v_out[0, :, 0:LANES] = jnp.where(lo_half, zv2, vone_ref[:, 0:LANES]).astype(BF16)
        v_out[0, :, LANES:2 * LANES] = jnp.where(lo_half, vone_ref[:, LANES:2 * LANES], zv2).astype(BF16)

    kv_pair(ksa_ref, vsa_ref, OFF_KS, OFF_VS)
    kv_pair(kwa_ref, vwa_ref, OFF_KW, OFF_VW)

    gt_ref[0] = jax.nn.sigmoid(proj(OFF_NG, LANES))
    gab_ref[0, :, 0:D_MODEL] = jax.nn.sigmoid(proj(OFF_GA, D_MODEL)).astype(BF16)
    gab_ref[0, :, D_MODEL:2 * D_MODEL] = jax.nn.sigmoid(proj(OFF_GA + D_MODEL, D_MODEL)).astype(BF16)


def _pos_pieces(pos):
    return ((pos // 256) * 256).astype(F32), (pos % 256).astype(F32)


def _inproj(x, norm_g, w_in, b_forget, fox_q_g, fox_k_g, nsa_q_g, nsa_k_g):
    B, S, D = x.shape
    tm = min(TM_PROJ, S)
    scale = HEAD_DIM ** -0.5

    c = [0]
    for n in (FOX_W, FOX_W, FOX_W, FOX_HEADS, NSA_W) + (NSA_KV_W,) * 6 + (3 * NSA_HEADS, D_MODEL, D_MODEL):
        c.append(c[-1] + n)
    fq, fk, fv, ff, nq, kc, vc, ks, vs, kw, vw, ng, ga, gb = [w_in[:, c[i]:c[i + 1]] for i in range(14)]
    perm = jnp.asarray([0, 4, 1, 5, 2, 6, 3, 7])
    nq = nq.reshape(D, NSA_HEADS, HEAD_DIM)[:, perm, :].reshape(D, NSA_W)
    padl = lambda a: jnp.pad(a, ((0, 0), (0, LANES - a.shape[1])))
    w = jnp.concatenate([fq, fk, fv, nq, kc, vc, ks, vs, kw, vw, padl(ff), padl(ng), ga, gb], axis=1).astype(BF16)
    assert w.shape[1] == N_PROJ

    bf = jnp.pad(b_forget.astype(F32), (0, LANES - FOX_HEADS))[None, :]
    gqa = jnp.tile(fox_q_g.astype(F32) * scale, FOX_HEADS)[None, :]
    gka = jnp.tile(fox_k_g.astype(F32), FOX_HEADS)[None, :]
    gqn = jnp.tile(nsa_q_g.astype(F32) * scale, NSA_HEADS)[None, :]
    gkn = jnp.tile(nsa_k_g.astype(F32), NSA_KV_HEADS)[None, :]

    r512 = jnp.arange(FOX_W)
    bd = jnp.where((r512[:, None] // HEAD_DIM) == (r512[None, :] // HEAD_DIM), 1.0 / HEAD_DIM, 0.0).astype(BF16)
    bd2 = bd[:LANES, :LANES]
    rt = jnp.arange(tm)
    tri = (rt[None, :] <= rt[:, None]).astype(BF16)

    heads = jnp.arange(FOX_HEADS)
    base = heads * LANES + jnp.where(heads % 2 == 0, HALF, 0)
    rows = jnp.arange(3 * LANES)
    piece, hlane = rows // LANES, rows % LANES
    cols = jnp.arange(FOX_HEADS * LANES)
    tgt_q = jnp.where(hlane < FOX_HEADS, base[jnp.minimum(hlane, FOX_HEADS - 1)] + 3 + piece, -1)
    tgt_k = jnp.where(hlane < FOX_HEADS, base[jnp.minimum(hlane, FOX_HEADS - 1)] + piece, -1)
    selq = (cols[None, :] == tgt_q[:, None]).astype(BF16)
    selk = -(cols[None, :] == tgt_k[:, None]).astype(BF16)
    off_in_blk = cols - base[cols // LANES]
    cq = ((off_in_blk >= 0) & (off_in_blk < 3)).astype(F32)[None, :]
    ck = ((off_in_blk >= 3) & (off_in_blk < 9)).astype(F32)[None, :]
    cv = (off_in_blk == 0).astype(F32)[None, :]

    blk = cols // LANES
    slope = 2.0 ** (-(blk + 1).astype(F32))
    o = cols % LANES - jnp.where(blk // NSA_REP == 0, HALF, 0)
    qc0 = jnp.where((o == 0) | (o == 1), slope, 0.0)[None, :]
    qca = jnp.where(o == 2, -slope, 0.0)[None, :]
    qcb = jnp.where(o == 3, -slope, 0.0)[None, :]
    kc0, kca, kcb, vone = _kv_rows()

    grid = (B, S // tm)
    row_spec = lambda n: pl.BlockSpec((1, tm, n), lambda b, s: (b, s, 0))
    consts = [norm_g.astype(F32)[None, :], w, bf, gqa, gka, gqn, gkn, bd, bd2, tri, selq, selk, cq, ck, cv,
              qc0, qca, qcb, kc0, kca, kcb, vone]
    out_widths = [(8 * LANES, BF16)] * 4 + [(LANES, F32)] * 2 + [(2 * LANES, BF16)] * 4 + \
                 [(LANES, F32), (2 * D_MODEL, BF16)]
    outs = pl.pallas_call(
        _inproj_kernel,
        grid=grid,
        in_specs=[row_spec(D)] + [_const_spec(a.shape) for a in consts],
        out_specs=[row_spec(n) for n, _ in out_widths],
        out_shape=[jax.ShapeDtypeStruct((B, S, n), dt) for n, dt in out_widths],
        scratch_shapes=[pltpu.VMEM((1, LANES), F32)],
        compiler_params=_cparams(("arbitrary", "arbitrary")),
        name="inproj",
    )(x, *consts)
    return outs


def _kv_rows():
    cols = jnp.arange(2 * LANES)
    o = cols % LANES - jnp.where(cols // LANES == 0, HALF, 0)
    row = lambda m: m.astype(F32)[None, :]
    return row((o >= 2) & (o <= 6)), row(o == 0), row(o == 1), row(o == 0)


def _kv_consts(pa, pb):
    c0, ca, cb, vone = _kv_rows()
    return (c0 + ca * pa[:, None] + cb * pb[:, None]).astype(BF16), vone


def _compress_kernel(kt_ref, vt_ref, w1k_ref, w1v_ref, posk_ref, posv_ref, pw1k_ref, pw1v_ref, w2k_ref, w2v_ref,
                     gk_ref, kcc_ref, vone_ref, kc_ref, vc_ref):
    nc = kc_ref.shape[2]

    def mlp(t_ref, w1_ref, pos_ref, pw1_ref, w2_ref):
        both = jnp.zeros((nc, 2 * CMP_HIDDEN), F32)
        for l in range(CMP_STRIDE):
            rows = t_ref[0, pl.ds(l, nc, stride=CMP_STRIDE), :].astype(BF16)
            both = both + _dot(rows, w1_ref[0, l])
        posw = _dot(pos_ref[...], pw1_ref[...])[0:1, :]
        pre = both[:, 0:CMP_HIDDEN] + pltpu.roll(both[:, CMP_HIDDEN:2 * CMP_HIDDEN], nc - 1, axis=0) + posw
        act = pre * (0.5 * (1.0 + jnp.tanh(0.7978845608028654 * (pre + 0.044715 * (pre * pre * pre)))))
        return _dot(act.astype(BF16), w2_ref[0])

    kc = mlp(kt_ref, w1k_ref, posk_ref, pw1k_ref, w2k_ref)
    msq = jnp.sum(kc * kc, axis=-1, keepdims=True) * (1.0 / HEAD_DIM)
    kc_ref[0, 0] = (kc * lax.rsqrt(msq + RMS_EPS) * gk_ref[0] + kcc_ref[0]).astype(BF16)
    vc = mlp(vt_ref, w1v_ref, posv_ref, pw1v_ref, w2v_ref)
    vc_ref[0, 0] = (vc + vone_ref[0]).astype(BF16)


def _compress(kcr, vcr, cmp_k_w1, cmp_k_w2, cmp_k_pos, cmp_v_w1, cmp_v_w2, cmp_v_pos, nsa_k_g):
    B, S, _ = kcr.shape
    G = NSA_KV_HEADS
    nc = S // CMP_STRIDE

    def w1_strided(w1):
        w = w1.reshape(2, CMP_STRIDE, HEAD_DIM, CMP_HIDDEN)
        w = jnp.concatenate([w[0], w[1]], axis=-1)
        z = jnp.zeros_like(w)
        return jnp.stack([jnp.concatenate([w, z], axis=1), jnp.concatenate([z, w], axis=1)]).astype(BF16)

    def w2_spread(w2):
        z = jnp.zeros_like(w2)
        return jnp.stack([jnp.concatenate([w2, z], 1), jnp.concatenate([z, w2], 1)]).astype(BF16)

    def pos8(p):
        return jnp.tile(p.reshape(1, CMP_BLOCK * HEAD_DIM), (8, 1)).astype(BF16)

    gk = nsa_k_g.astype(F32)
    z = jnp.zeros_like(gk)
    gk2 = jnp.stack([jnp.concatenate([gk, z]), jnp.concatenate([z, gk])])[:, None, :]
    cend = jnp.arange(nc) * CMP_STRIDE + CMP_BLOCK - 1
    kcc, vone = _kv_consts(*_pos_pieces(cend))
    kcc = kcc.astype(F32).reshape(nc, G, LANES).transpose(1, 0, 2)
    vone = vone.reshape(G, 1, LANES)

    tok = pl.BlockSpec((1, S, G * HEAD_DIM), lambda b, g: (b, 0, 0))
    per_g = lambda a: pl.BlockSpec((1,) + a.shape[1:], lambda b, g: (g,) + (0,) * (a.ndim - 1))
    w1k, w1v = w1_strided(cmp_k_w1), w1_strided(cmp_v_w1)
    pk, pv = pos8(cmp_k_pos), pos8(cmp_v_pos)
    pw1k, pw1v = cmp_k_w1.astype(BF16), cmp_v_w1.astype(BF16)
    w2k, w2v = w2_spread(cmp_k_w2), w2_spread(cmp_v_w2)
    out_spec = pl.BlockSpec((1, 1, nc, LANES), lambda b, g: (b, g, 0, 0))
    return pl.pallas_call(
        _compress_kernel,
        grid=(B, G),
        in_specs=[tok, tok, per_g(w1k), per_g(w1v), _const_spec(pk.shape), _const_spec(pv.shape),
                  _const_spec(pw1k.shape), _const_spec(pw1v.shape), per_g(w2k), per_g(w2v),
                  per_g(gk2), per_g(kcc), per_g(vone)],
        out_specs=[out_spec, out_spec],
        out_shape=[jax.ShapeDtypeStruct((B, G, nc, LANES), BF16)] * 2,
        compiler_params=_cparams(("arbitrary", "arbitrary")),
        name="compress",
    )(kcr, vcr, w1k, w1v, pk, pv, pw1k, pw1v, w2k, w2v, gk2, kcc, vone)


MAX_EXPONENT = 60.0


def _scores_bounded(q_g, k_g):
    bound = HEAD_DIM ** 0.5 * jnp.max(jnp.abs(q_g)) * jnp.max(jnp.abs(k_g))
    return 2.04 * bound + 0.05 <= MAX_EXPONENT


def _attn_first(s, v):
    m = jnp.max(s, axis=-1, keepdims=True)
    p = jnp.exp((s - m).astype(BF16))
    return m, _dot(p, v)


def _attn_step(carry, s, v):
    m, acc = carry
    m_new = jnp.maximum(m, jnp.max(s, axis=-1, keepdims=True))
    p = jnp.exp((s - m_new).astype(BF16))
    return m_new, jnp.exp(m - m_new) * acc + _dot(p, v)


def _nsa_kernel(bounded, q_ref, kc_ref, vc_ref, ks_ref, e_ref, vs_ref, kw_ref, vw_ref, gt_ref, ov_ref, o_ref,
                tiles_ref, sa_ref, sb_ref):
    tq = q_ref.shape[1]
    nc = kc_ref.shape[2]
    tk = TK_SLC
    rq = NSA_REP * tq
    g = pl.program_id(1)
    i = pl.program_id(2)
    q0 = i * tq

    q4 = jnp.concatenate([q_ref[0, :, LANES * r:LANES * (r + 1)] for r in range(NSA_REP)], axis=0)

    def qpos_of(shape):
        return q0 + (lax.broadcasted_iota(jnp.int32, shape, 0) & (tq - 1))

    def add_mask(s, valid):
        bias = jnp.where(valid, 0.0, NEG_INF)
        return (s.reshape(NSA_REP, tq, s.shape[1]) + bias[None]).reshape(s.shape)

    def qrow(n):
        return q0 + lax.broadcasted_iota(jnp.int32, (tq, n), 0)

    def kcol(n):
        return lax.broadcasted_iota(jnp.int32, (tq, n), 1)

    s = add_mask(_dot_nt(q4, kc_ref[0, 0]), qrow(nc) >= kcol(nc) * CMP_STRIDE + (CMP_BLOCK - 1))
    anyv = qpos_of((rq, 1)) >= CMP_BLOCK - 1
    if bounded:
        e = jnp.exp(s)
        p = e * jnp.where(anyv, 1.0 / jnp.sum(e, axis=-1, keepdims=True), 0.0)
    else:
        m = jnp.max(s, axis=-1, keepdims=True)
        e = jnp.exp(s - m)
        p = e * (anyv.astype(F32) / jnp.sum(e, axis=-1, keepdims=True))
    o_cmp = _dot(p.astype(BF16), vc_ref[0, 0])

    psum = p[0:tq] + p[tq:2 * tq] + p[2 * tq:3 * tq] + p[3 * tq:4 * tq]
    p_hi, p_mid, p_lo = _split3(psum)
    ov = ov_ref[...]
    imp = _dot(p_hi, ov) + _dot(p_mid, ov) + _dot(p_lo, ov)
    blk_i = lax.broadcasted_iota(jnp.int32, (LANES, tq), 0)
    blk_f = blk_i.astype(F32)
    qblk = (q0 + lax.broadcasted_iota(jnp.int32, (LANES, tq), 1)) // SLC_BLOCK
    forced = (blk_i == 0) | (blk_i == qblk) | (blk_i == qblk - 1)
    score = jnp.where(blk_i > qblk, -1.0, jnp.where(forced, FORCE_SCORE, imp.T))
    selb = jnp.full((LANES, tq), NEG_INF, F32)
    for _ in range(SLC_TOPK):
        mx = jnp.max(score, axis=0, keepdims=True)
        first = jnp.min(jnp.where(score == mx, blk_f, float(LANES)), axis=0, keepdims=True)
        hit = blk_f == first
        selb = jnp.where(hit, 0.0, selb)
        score = jnp.where(hit, -3e38, score)
    sb = selb.T.astype(BF16)
    sb4 = jnp.concatenate([sb] * NSA_REP, axis=0)

    def slc_qk(j):
        start = pl.multiple_of(j * tk, tk)
        kk = jnp.concatenate([ks_ref[0, pl.ds(start, tk), :], e_ref[pl.ds(start, tk), :]], axis=1)
        return _dot_nt(q2, kk)

    def slc_v(j):
        return vs_ref[0, pl.ds(pl.multiple_of(j * tk, tk), tk), :]

    jd = q0 // tk
    blocks_per_tile = tk // SLC_BLOCK
    anysel = jnp.max(selb, axis=1, keepdims=True)
    cnt = jnp.int32(0)
    for j in range(ks_ref.shape[1] // tk):
        tile_sel = jnp.max(anysel[blocks_per_tile * j:blocks_per_tile * (j + 1), :]) > -1.0
        tiles_ref[cnt] = j
        cnt = cnt + (tile_sel & (j < jd)).astype(jnp.int32)

    lane_r = lax.broadcasted_iota(jnp.int32, (rq, LANES), 1)
    data0 = HALF * g
    ones_lane = HALF - data0
    nw = WINDOW + tq
    wstart = pl.multiple_of(jnp.maximum(q0 - WINDOW, 0), tq)
    win_valid = lax.bitcast_convert_type(qrow(nw) - (wstart + kcol(nw)), jnp.uint32) < WINDOW
    diag_valid = jd * tk + kcol(tk) <= qrow(tk)

    if bounded:
        def shifted_by_own_score(kself):
            own = jnp.sum(q4.astype(F32).reshape(NSA_REP, tq, LANES) * kself.astype(F32)[None],
                          axis=-1, keepdims=True).reshape(rq, 1)
            qf = q4.astype(F32)
            for off, piece in enumerate(_split3(-own)):
                qf = jnp.where(lane_r == ones_lane + 4 + off, piece.astype(F32), qf)
            return qf.astype(BF16)

        q2 = jnp.concatenate([shifted_by_own_score(ks_ref[0, pl.ds(q0, tq), :]), sb4], axis=1)
        sa_ref[:, 0:LANES] = _dot(jnp.exp(add_mask(slc_qk(jd), diag_valid)).astype(BF16), slc_v(jd))

        def slc_pv(t):
            j = tiles_ref[t]
            return _dot(jnp.exp(slc_qk(j)).astype(BF16), slc_v(j))

        def slc_body(p, _):
            sa_ref[:, 0:LANES] += slc_pv(2 * p) + slc_pv(2 * p + 1)
            return _

        lax.fori_loop(0, cnt // 2, slc_body, 0)

        @pl.when(cnt % 2 == 1)
        def _():
            sa_ref[:, 0:LANES] += slc_pv(cnt - 1)

        acc_slc = sa_ref[:, 0:LANES]

        qw = shifted_by_own_score(kw_ref[0, pl.ds(q0, tq), :])
        s = add_mask(_dot_nt(qw, kw_ref[0, pl.ds(wstart, nw), :]), win_valid)
        acc_win = _dot(jnp.exp(s).astype(BF16), vw_ref[0, pl.ds(wstart, nw), :])
    else:
        q2 = jnp.concatenate([q4, sb4], axis=1)
        n_seq = cnt + 1

        def tile_of(n):
            return jnp.where(n == 0, jd, tiles_ref[jnp.maximum(n - 1, 0)])

        def slc_step(carry, s_buf, n):
            return _attn_step(carry, s_buf[...], slc_v(tile_of(n)))

        sa_ref[...] = add_mask(slc_qk(jd), diag_valid)

        def slc_body(p, carry):
            sb_ref[...] = slc_qk(tile_of(2 * p + 1))
            carry = slc_step(carry, sa_ref, 2 * p)
            sa_ref[...] = slc_qk(tile_of(jnp.minimum(2 * p + 2, n_seq - 1)))
            return slc_step(carry, sb_ref, 2 * p + 1)

        carry = (jnp.full((rq, 1), NEG_INF, F32), jnp.zeros((rq, LANES), F32))
        carry = lax.fori_loop(0, n_seq // 2, slc_body, carry)
        _, acc_slc = lax.cond(n_seq % 2 == 1, lambda c: slc_step(c, sa_ref, n_seq - 1), lambda c: c, carry)

        s = add_mask(_dot_nt(q4, kw_ref[0, pl.ds(wstart, nw), :]), win_valid)
        _, acc_win = _attn_first(s, vw_ref[0, pl.ds(wstart, nw), :])


    def normalise(acc):
        l = jnp.sum(jnp.where(lane_r == ones_lane, acc, 0.0), axis=-1, keepdims=True)
        return acc * (1.0 / l)

    o_slc = normalise(acc_slc)
    o_win = normalise(acc_win)
    gt = gt_ref[0]
    lane_g = lax.broadcasted_iota(jnp.int32, (tq, LANES), 1)
    is_data_q = (lane_g >= data0) & (lane_g < data0 + HALF)
    for r in range(NSA_REP):
        col = 3 * (NSA_REP * g + r)
        gate = [jnp.sum(jnp.where(lane_g == col + b, gt, 0.0), axis=-1, keepdims=True) for b in range(3)]
        rows = slice(r * tq, (r + 1) * tq)
        o = gate[0] * o_cmp[rows] + gate[1] * o_slc[rows] + gate[2] * o_win[rows]
        o_ref[0, :, LANES * r:LANES * (r + 1)] = jnp.where(is_data_q, o, 0.0).astype(BF16)


def _nsa(qn, kca, vca, ksa, vsa, kwa, vwa, gt, scores_bounded):
    B, S, _ = qn.shape
    G = NSA_KV_HEADS
    tq = min(TQ_NSA, S)
    nc = S // CMP_STRIDE
    n_slc = S // SLC_BLOCK
    assert n_slc <= LANES and S % TK_SLC == 0

    cs = jnp.arange(nc)[:, None] * CMP_STRIDE
    ss = jnp.arange(LANES)[None, :] * SLC_BLOCK
    ovl = jnp.clip(jnp.minimum(cs + CMP_BLOCK, ss + SLC_BLOCK) - jnp.maximum(cs, ss), 0, None)
    valid = (jnp.arange(nc)[:, None] < (S - CMP_BLOCK) // CMP_STRIDE + 1) & (jnp.arange(LANES)[None, :] < n_slc)
    ov = jnp.where(valid, ovl.astype(F32) / CMP_BLOCK, 0.0).astype(BF16)
    e1h = (jnp.arange(S)[:, None] // SLC_BLOCK == jnp.arange(LANES)[None, :]).astype(BF16)

    q_spec = pl.BlockSpec((1, tq, NSA_REP * LANES), lambda b, g, i: (b, i, g))
    c_spec = pl.BlockSpec((1, 1, nc, LANES), lambda b, g, i: (b, g, 0, 0))
    kv_spec = pl.BlockSpec((1, S, LANES), lambda b, g, i: (b, 0, g))

    def call(bounded, name):
        return pl.pallas_call(
            functools.partial(_nsa_kernel, bounded),
            grid=(B, G, S // tq),
            in_specs=[q_spec, c_spec, c_spec, kv_spec, _const_spec(e1h.shape), kv_spec, kv_spec, kv_spec,
                      pl.BlockSpec((1, tq, LANES), lambda b, g, i: (b, i, 0)), _const_spec(ov.shape)],
            out_specs=q_spec,
            out_shape=jax.ShapeDtypeStruct((B, S, NSA_HEADS * LANES), BF16),
            scratch_shapes=[pltpu.SMEM((S // TK_SLC + 1,), jnp.int32),
                            pltpu.VMEM((NSA_REP * tq, TK_SLC), F32), pltpu.VMEM((NSA_REP * tq, TK_SLC), F32)],
            compiler_params=_cparams(("arbitrary", "arbitrary", "arbitrary")),
            name=name,
        )

    return lax.cond(scores_bounded, call(True, "nsa_bounded"), call(False, "nsa"),
                    qn, kca, vca, ksa, e1h, vsa, kwa, vwa, gt, ov)


def _fox_kernel(q_ref, k_ref, v_ref, o_ref, sa_ref, sb_ref):
    tq = q_ref.shape[1]
    i = pl.program_id(2)
    lane = lax.broadcasted_iota(jnp.int32, (tq, LANES), 1)
    causal = lax.broadcasted_iota(jnp.int32, (tq, tq), 1) <= lax.broadcasted_iota(jnp.int32, (tq, tq), 0)

    def cols(hh):
        return slice(LANES * hh, LANES * (hh + 1))

    def qk(hh, j):
        start = pl.multiple_of(j * tq, tq)
        return _dot_nt(q_ref[0, :, cols(hh)], k_ref[0, pl.ds(start, tq), cols(hh)])

    def vtile(hh, j):
        return v_ref[0, pl.ds(pl.multiple_of(j * tq, tq), tq), cols(hh)]

    n_tiles = i + 1

    def tile_of(n):
        return jnp.where(n == 0, i, n - 1)

    def step_all(carry, s_buf, n):
        t = tile_of(n)
        return tuple(_attn_step(carry[hh], s_buf[hh], vtile(hh, t)) for hh in range(2))

    for hh in range(2):
        sa_ref[hh] = jnp.where(causal, qk(hh, i), NEG_INF)
    init = tuple((jnp.full((tq, 1), NEG_INF, F32), jnp.zeros((tq, LANES), F32)) for hh in range(2))

    def body(p, carry):
        for hh in range(2):
            sb_ref[hh] = qk(hh, tile_of(2 * p + 1))
        carry = step_all(carry, sa_ref, 2 * p)
        nxt = jnp.minimum(2 * p + 2, n_tiles - 1)
        for hh in range(2):
            sa_ref[hh] = qk(hh, tile_of(nxt))
        return step_all(carry, sb_ref, 2 * p + 1)

    carry = lax.fori_loop(0, n_tiles // 2, body, init)
    carry = lax.cond(n_tiles % 2 == 1, lambda c: step_all(c, sa_ref, n_tiles - 1), lambda c: c, carry)
    outs = []
    for hh in range(2):
        acc = carry[hh][1]
        ones_lane = HALF if hh == 0 else 0
        l = jnp.sum(jnp.where(lane == ones_lane, acc, 0.0), axis=-1, keepdims=True)
        outs.append(acc * (1.0 / l))
    o_ref[0] = jnp.where(lane < HALF, outs[0], outs[1]).astype(BF16)


def _fox_bounded_kernel(q_ref, k_ref, v_ref, o_ref, qs_ref, acc_ref):
    tq = q_ref.shape[1]
    i = pl.program_id(2)
    lane = lax.broadcasted_iota(jnp.int32, (tq, LANES), 1)
    causal = lax.broadcasted_iota(jnp.int32, (tq, tq), 1) <= lax.broadcasted_iota(jnp.int32, (tq, tq), 0)

    def cols(hh):
        return slice(LANES * hh, LANES * (hh + 1))

    def ktile(hh, j):
        return k_ref[0, pl.ds(pl.multiple_of(j * tq, tq), tq), cols(hh)]

    def vtile(hh, j):
        return v_ref[0, pl.ds(pl.multiple_of(j * tq, tq), tq), cols(hh)]

    for hh in range(2):
        q = q_ref[0, :, cols(hh)]
        s = jnp.where(causal, _dot_nt(q, ktile(hh, i)), NEG_INF)
        m = jnp.max(s, axis=-1, keepdims=True)
        acc_ref[hh] = _dot(jnp.exp((s - m).astype(BF16)), vtile(hh, i))
        qf = q.astype(F32)
        free0 = (HALF if hh == 0 else 0) + 6
        for off, piece in enumerate(_split3(-m)):
            qf = jnp.where(lane == free0 + off, piece.astype(F32), qf)
        qs_ref[hh] = qf.astype(BF16)

    def body(j, _):
        for hh in range(2):
            p = jnp.exp(_dot_nt(qs_ref[hh], ktile(hh, j)).astype(BF16))
            acc_ref[hh] += _dot(p, vtile(hh, j))
        return _

    lax.fori_loop(0, i, body, 0)
    outs = []
    for hh in range(2):
        acc = acc_ref[hh]
        ones_lane = HALF if hh == 0 else 0
        l = jnp.sum(jnp.where(lane == ones_lane, acc, 0.0), axis=-1, keepdims=True)
        outs.append(acc * (1.0 / l))
    o_ref[0] = jnp.where(lane < HALF, outs[0], outs[1]).astype(BF16)


def _fox(qa, ka, va, scores_bounded):
    B, S, _ = qa.shape
    tq = min(TQ_FOX, S)
    q_spec = pl.BlockSpec((1, tq, 2 * LANES), lambda b, h, i: (b, i, h))
    kv_spec = pl.BlockSpec((1, S, 2 * LANES), lambda b, h, i: (b, 0, h))

    def call(body, scratch, name):
        return pl.pallas_call(
            body,
            grid=(B, FOX_HEADS // 2, S // tq),
            in_specs=[q_spec, kv_spec, kv_spec],
            out_specs=pl.BlockSpec((1, tq, LANES), lambda b, h, i: (b, i, h)),
            out_shape=jax.ShapeDtypeStruct((B, S, FOX_W), BF16),
            scratch_shapes=scratch,
            compiler_params=_cparams(("arbitrary", "arbitrary", "arbitrary")),
            name=name,
        )

    general = call(_fox_kernel, [pltpu.VMEM((2, tq, tq), F32), pltpu.VMEM((2, tq, tq), F32)], "fox")
    bounded = call(_fox_bounded_kernel, [pltpu.VMEM((2, tq, LANES), BF16), pltpu.VMEM((2, tq, LANES), F32)],
                   "fox_bounded")
    return lax.cond(scores_bounded, bounded, general, qa, ka, va)


R_E1, R_E2, R_W1, R_W2, R_RANK1, R_RANK2 = range(6)


def _merge_kernel(x_ref, oa_ref, ob_ref, gab_ref, wa_ref, wb_ref, wo_ref, g2_ref, wr_hi_ref, wr_lo_ref, br_ref,
                  ltri_ref, x1_ref, route_ref, cnt_ref, carry_ref):
    tm = x_ref.shape[1]

    @pl.when((pl.program_id(0) == 0) & (pl.program_id(1) == 0))
    def _():
        carry_ref[...] = jnp.zeros_like(carry_ref)

    out_a = _dot(oa_ref[0], wa_ref[...])
    out_b = _dot(ob_ref[0], wb_ref[...])
    mix = gab_ref[0, :, 0:D_MODEL].astype(F32) * out_a + gab_ref[0, :, D_MODEL:2 * D_MODEL].astype(F32) * out_b
    x1 = x_ref[0] + _dot(mix.astype(BF16), wo_ref[...])
    x1_ref[...] = x1
    h2 = x1 * lax.rsqrt(jnp.mean(x1 * x1, axis=-1, keepdims=True) + RMS_EPS) * g2_ref[...]

    h_hi = h2.astype(BF16)
    h_lo = (h2 - h_hi.astype(F32)).astype(BF16)
    logits = _dot(h_hi, wr_hi_ref[...]) + (_dot(h_hi, wr_lo_ref[...]) + _dot(h_lo, wr_hi_ref[...])) + br_ref[...]
    lane = lax.broadcasted_iota(jnp.int32, (tm, LANES), 1)
    lane_f = lane.astype(F32)

    def first_argmax(vals):
        mx = jnp.max(vals, axis=-1, keepdims=True)
        idx = jnp.min(jnp.where(vals == mx, lane_f, float(LANES)), axis=-1, keepdims=True)
        return mx, idx

    is_grp = (lane >= N_EXPERTS) & (lane < N_EXPERTS + N_GROUPS)
    gl = jnp.where(is_grp, logits, NEG_INF)
    gmax, gidx = first_argmax(gl)
    p_g = 1.0 / jnp.sum(jnp.where(is_grp, jnp.exp(gl - gmax), 0.0), axis=-1, keepdims=True)
    e_lo = (gidx - float(N_EXPERTS)) * float(EXPERTS_PER_GROUP)
    in_grp = (lane_f >= e_lo) & (lane_f < e_lo + float(EXPERTS_PER_GROUP))
    el = jnp.where(in_grp, logits, NEG_INF)
    m1, i1 = first_argmax(el)
    m2, i2 = first_argmax(jnp.where(lane_f == i1, NEG_INF, el))
    e2 = jnp.exp(m2 - m1)
    w1 = p_g / (1.0 + e2)
    w2 = p_g * e2 / (1.0 + e2)

    hit1 = lane_f == i1
    hit2 = lane_f == i2
    onehot = jnp.where(hit1 | hit2, 1.0, 0.0)
    before = carry_ref[...] + _dot(ltri_ref[...], onehot.astype(BF16))
    rank1 = jnp.sum(jnp.where(hit1, before, 0.0), axis=-1, keepdims=True)
    rank2 = jnp.sum(jnp.where(hit2, before, 0.0), axis=-1, keepdims=True)
    total = carry_ref[...] + jnp.sum(onehot, axis=0, keepdims=True)
    carry_ref[...] = total
    cnt_ref[...] = total

    rec = jnp.zeros((tm, LANES), F32)
    for k, val in ((R_E1, i1), (R_E2, i2), (R_W1, w1), (R_W2, w2), (R_RANK1, rank1), (R_RANK2, rank2)):
        rec = jnp.where(lane == k, val, rec)
    route_ref[...] = rec


def _merge(x, oa, ob, gab, w_fox_up, w_nsa_up, w_out, norm_ffn_g, w_group, b_group, w_router, b_router):
    B, S, D = x.shape
    tm = min(TM_MERGE, S)
    wa = w_fox_up.astype(BF16)
    wn = w_nsa_up.reshape(NSA_KV_HEADS, NSA_REP, HEAD_DIM, D)
    z = jnp.zeros_like(wn[0])
    wb = jnp.stack([jnp.concatenate([wn[0], z], axis=1), jnp.concatenate([z, wn[1]], axis=1)])
    wb = wb.reshape(NSA_HEADS * LANES, D).astype(BF16)
    wo = w_out.astype(BF16)
    wr = jnp.pad(jnp.concatenate([w_router, w_group], axis=1).astype(F32),
                 ((0, 0), (0, LANES - N_EXPERTS - N_GROUPS)))
    wr_hi = wr.astype(BF16)
    wr_lo = (wr - wr_hi.astype(F32)).astype(BF16)
    br = jnp.pad(jnp.concatenate([b_router, b_group]).astype(F32), (0, LANES - N_EXPERTS - N_GROUPS))[None, :]
    g2 = norm_ffn_g.astype(F32)[None, :]
    rt = jnp.arange(tm)
    ltri = (rt[None, :] < rt[:, None]).astype(BF16)

    row = lambda n: pl.BlockSpec((1, tm, n), lambda b, s: (b, s, 0))
    flat = lambda n: pl.BlockSpec((tm, n), lambda b, s: (b * (S // tm) + s, 0))
    consts = [wa, wb, wo, g2, wr_hi, wr_lo, br, ltri]
    return pl.pallas_call(
        _merge_kernel,
        grid=(B, S // tm),
        in_specs=[row(D), row(FOX_W), row(NSA_HEADS * LANES), row(2 * D)] + [_const_spec(a.shape) for a in consts],
        out_specs=[flat(D), flat(LANES), _const_spec((1, LANES))],
        out_shape=[jax.ShapeDtypeStruct((B * S, D), F32), jax.ShapeDtypeStruct((B * S, LANES), F32),
                   jax.ShapeDtypeStruct((1, LANES), F32)],
        scratch_shapes=[pltpu.VMEM((1, LANES), F32)],
        compiler_params=_cparams(("arbitrary", "arbitrary")),
        name="merge",
    )(x, oa, ob, gab, *consts)


def _row_copy(src_ref, src_row, dst_ref, dst_row, sem):
    return pltpu.make_async_copy(src_ref.at[pl.ds(src_row, 1), :], dst_ref.at[pl.ds(dst_row, 1), :], sem)


def _dispatch_kernel(row1_ref, row2_ref, clear_ref, x1_ref, xs_ref, zero_ref, sem, zsem):
    tm = x1_ref.shape[0]
    base = pl.program_id(0) * tm

    @pl.when(pl.program_id(0) == 0)
    def _():
        zero_ref[...] = jnp.zeros_like(zero_ref)

        def clear(c):
            start = pl.multiple_of(jnp.maximum(clear_ref[c], 0), zero_ref.shape[0])
            return pltpu.make_async_copy(zero_ref, xs_ref.at[pl.ds(start, zero_ref.shape[0]), :], zsem)

        for c in range(clear_ref.shape[0]):
            @pl.when(clear_ref[c] >= 0)
            def _(c=c):
                clear(c).start()
        for c in range(clear_ref.shape[0]):
            @pl.when(clear_ref[c] >= 0)
            def _(c=c):
                clear(c).wait()

    def start(t, _):
        _row_copy(x1_ref, t, xs_ref, row1_ref[base + t], sem).start()
        _row_copy(x1_ref, t, xs_ref, row2_ref[base + t], sem).start()
        return _

    def wait(t, _):
        _row_copy(x1_ref, t, xs_ref, row1_ref[base + t], sem).wait()
        _row_copy(x1_ref, t, xs_ref, row2_ref[base + t], sem).wait()
        return _

    lax.fori_loop(0, tm, start, 0, unroll=8)
    lax.fori_loop(0, tm, wait, 0, unroll=8)


def _experts_kernel(tile_e_ref, n_used_ref, xs_ref, g2_ref, wg_ref, wu_ref, wd_ref, ys_ref,
                    wg_bf, wu_bf, wd_bf):
    k = pl.program_id(0)
    used = k < n_used_ref[0]

    @pl.when(used & ((k == 0) | (tile_e_ref[k] != tile_e_ref[jnp.maximum(k - 1, 0)])))
    def _():
        wg_bf[...] = wg_ref[0, 0].astype(BF16)
        wu_bf[...] = wu_ref[0, 0].astype(BF16)
        wd_bf[...] = wd_ref[0, 0].astype(BF16)

    @pl.when(used)
    def _():
        x = xs_ref[...]
        h = (x * lax.rsqrt(jnp.mean(x * x, axis=-1, keepdims=True) + RMS_EPS) * g2_ref[...]).astype(BF16)
        a = _dot(h, wg_bf[...])
        hid = (a * jax.nn.sigmoid(a)) * _dot(h, wu_bf[...])
        ys_ref[...] = _dot(hid.astype(BF16), wd_bf[...])

    @pl.when(jnp.logical_not(used))
    def _():
        ys_ref[...] = jnp.zeros_like(ys_ref)


def _combine_kernel(row1_ref, row2_ref, x1_ref, route_ref, ys_ref, o_ref, y1_ref, y2_ref, sem):
    tm = x1_ref.shape[0]
    base = pl.program_id(0) * tm

    def start(t, _):
        _row_copy(ys_ref, row1_ref[base + t], y1_ref, t, sem).start()
        _row_copy(ys_ref, row2_ref[base + t], y2_ref, t, sem).start()
        return _

    def wait(t, _):
        _row_copy(ys_ref, row1_ref[base + t], y1_ref, t, sem).wait()
        _row_copy(ys_ref, row2_ref[base + t], y2_ref, t, sem).wait()
        return _

    lax.fori_loop(0, tm, start, 0, unroll=8)
    lax.fori_loop(0, tm, wait, 0, unroll=8)
    rec = route_ref[...]
    lane = lax.broadcasted_iota(jnp.int32, rec.shape, 1)
    w1 = jnp.sum(jnp.where(lane == R_W1, rec, 0.0), axis=-1, keepdims=True)
    w2 = jnp.sum(jnp.where(lane == R_W2, rec, 0.0), axis=-1, keepdims=True)
    o_ref[0] = x1_ref[...] + (w1 * y1_ref[...] + w2 * y2_ref[...])


def _moe(x1, route, cnt, norm_ffn_g, w_gate, w_up, w_down, layer, B, S):
    T, D = x1.shape
    tme = min(TM_EXPERT, T)
    tmd = min(TM_DISPATCH, S)

    counts = cnt[0, :N_EXPERTS].astype(jnp.int32)
    tiles_per_e = (counts + tme - 1) // tme
    tile_end = jnp.cumsum(tiles_per_e)
    offs = (tile_end - tiles_per_e) * tme
    last_tile_row = jnp.where(tiles_per_e > 0, (tile_end - 1) * tme, -1).astype(jnp.int32)
    rec = route[:, :8].T.astype(jnp.int32)
    experts = jnp.arange(N_EXPERTS, dtype=jnp.int32)[:, None]
    row_of = lambda e, rank: rank + jnp.sum(jnp.where(e[None, :] == experts, offs[:, None], 0), axis=0)
    row1 = row_of(rec[R_E1], rec[R_RANK1])
    row2 = row_of(rec[R_E2], rec[R_RANK2])
    max_tiles = (2 * T) // tme + N_EXPERTS
    tile_e = jnp.sum(jnp.arange(max_tiles, dtype=jnp.int32)[:, None] >= tile_end[None, :], axis=1)
    tile_e = jnp.minimum(tile_e, N_EXPERTS - 1).astype(jnp.int32)
    n_used = tile_end[-1:].astype(jnp.int32)
    n_rows = max_tiles * tme
    spare = n_used + jnp.arange(N_EXPERTS, dtype=jnp.int32)
    clear_rows = jnp.concatenate([last_tile_row, jnp.where(spare < max_tiles, spare * tme, -1)]).astype(jnp.int32)

    xs = pl.pallas_call(
        _dispatch_kernel,
        grid_spec=pltpu.PrefetchScalarGridSpec(
            num_scalar_prefetch=3,
            grid=(T // tmd,),
            in_specs=[pl.BlockSpec((tmd, D), lambda t, r1, r2, lt: (t, 0))],
            out_specs=pl.BlockSpec(memory_space=pl.ANY),
            scratch_shapes=[pltpu.VMEM((tme, D), F32), pltpu.SemaphoreType.DMA(()), pltpu.SemaphoreType.DMA(())],
        ),
        out_shape=jax.ShapeDtypeStruct((n_rows, D), F32),
        compiler_params=_cparams(("arbitrary",)),
        name="dispatch",
    )(row1, row2, clear_rows, x1)

    g2 = norm_ffn_g.astype(F32)[None, :]
    w_spec = lambda shape: pl.BlockSpec((1, 1) + shape, lambda k, te, nu: (layer, te[k], 0, 0))
    ys = pl.pallas_call(
        _experts_kernel,
        grid_spec=pltpu.PrefetchScalarGridSpec(
            num_scalar_prefetch=2,
            grid=(max_tiles,),
            in_specs=[pl.BlockSpec((tme, D), lambda k, te, nu: (jnp.minimum(k, nu[0] - 1), 0)),
                      pl.BlockSpec((1, D), lambda k, te, nu: (0, 0)),
                      w_spec((D, D_EXPERT)), w_spec((D, D_EXPERT)), w_spec((D_EXPERT, D))],
            out_specs=pl.BlockSpec((tme, D), lambda k, te, nu: (k, 0)),
            scratch_shapes=[pltpu.VMEM((D, D_EXPERT), BF16), pltpu.VMEM((D, D_EXPERT), BF16),
                            pltpu.VMEM((D_EXPERT, D), BF16)],
        ),
        out_shape=jax.ShapeDtypeStruct((n_rows, D), F32),
        compiler_params=_cparams(("arbitrary",)),
        name="experts",
    )(tile_e, n_used, xs, g2, w_gate, w_up, w_down)

    return pl.pallas_call(
        _combine_kernel,
        grid_spec=pltpu.PrefetchScalarGridSpec(
            num_scalar_prefetch=2,
            grid=(T // tmd,),
            in_specs=[pl.BlockSpec((tmd, D), lambda t, r1, r2: (t, 0)),
                      pl.BlockSpec((tmd, LANES), lambda t, r1, r2: (t, 0)),
                      pl.BlockSpec(memory_space=pl.ANY)],
            out_specs=pl.BlockSpec((1, tmd, D), lambda t, r1, r2: (t // (S // tmd), t % (S // tmd), 0)),
            scratch_shapes=[pltpu.VMEM((tmd, D), F32), pltpu.VMEM((tmd, D), F32), pltpu.SemaphoreType.DMA(())],
        ),
        out_shape=jax.ShapeDtypeStruct((B, S, D), F32),
        compiler_params=_cparams(("arbitrary",)),
        name="combine",
    )(row1, row2, x1, route, ys)


def kernel(x, norm_mix_g, w_in, b_forget, fox_q_g, fox_k_g, nsa_q_g, nsa_k_g, cmp_k_w1, cmp_k_w2, cmp_k_pos,
           cmp_v_w1, cmp_v_w2, cmp_v_pos, w_fox_up, w_nsa_up, w_out, norm_ffn_g, w_group, b_group, w_router,
           b_router, w_gate, w_up, w_down):
    B, S, D = x.shape
    for l in range(w_in.shape[0]):
        qa, ka, va, qn, kcr, vcr, ksa, vsa, kwa, vwa, gt, gab = _inproj(
            x, norm_mix_g[l], w_in[l], b_forget[l], fox_q_g[l], fox_k_g[l], nsa_q_g[l], nsa_k_g[l])
        kca, vca = _compress(kcr, vcr, cmp_k_w1[l], cmp_k_w2[l], cmp_k_pos[l],
                             cmp_v_w1[l], cmp_v_w2[l], cmp_v_pos[l], nsa_k_g[l])
        ob = _nsa(qn, kca, vca, ksa, vsa, kwa, vwa, gt, _scores_bounded(nsa_q_g[l], nsa_k_g[l]))
        oa = _fox(qa, ka, va, _scores_bounded(fox_q_g[l], fox_k_g[l]))
        x1, route, cnt = _merge(x, oa, ob, gab, w_fox_up[l], w_nsa_up[l], w_out[l], norm_ffn_g[l],
                                w_group[l], b_group[l], w_router[l], b_router[l])
        x = _moe(x1, route, cnt, norm_ffn_g[l], w_gate, w_up, w_down, l, B, S)
    return x
```

```python
import functools

import jax
import jax.numpy as jnp
from jax import lax
from jax.experimental import pallas as pl
from jax.experimental.pallas import tpu as pltpu

F32 = jnp.float32
BF16 = jnp.bfloat16

D_MODEL = 1024
HEAD_DIM = 64
FOX_HEADS = 8
NSA_HEADS = 8
NSA_KV_HEADS = 2
NSA_REP = NSA_HEADS // NSA_KV_HEADS
CMP_BLOCK = 32
CMP_STRIDE = 16
CMP_HIDDEN = 256
SLC_BLOCK = 64
SLC_TOPK = 16
WINDOW = 512
N_GROUPS = 4
EXPERTS_PER_GROUP = 4
N_EXPERTS = N_GROUPS * EXPERTS_PER_GROUP
D_EXPERT = 512
RMS_EPS = 1e-6
NEG_INF = -1e30
FORCE_SCORE = 1e4

LANES = 128
HALF = LANES // 2
VMEM_LIMIT = 56 * 1024 * 1024

FOX_W = FOX_HEADS * HEAD_DIM
NSA_W = NSA_HEADS * HEAD_DIM
NSA_KV_W = NSA_KV_HEADS * HEAD_DIM

OFF_FQ = 0
OFF_FK = OFF_FQ + FOX_W
OFF_FV = OFF_FK + FOX_W
OFF_NQ = OFF_FV + FOX_W
OFF_KC = OFF_NQ + NSA_W
OFF_VC = OFF_KC + NSA_KV_W
OFF_KS = OFF_VC + NSA_KV_W
OFF_VS = OFF_KS + NSA_KV_W
OFF_KW = OFF_VS + NSA_KV_W
OFF_VW = OFF_KW + NSA_KV_W
OFF_FF = OFF_VW + NSA_KV_W
OFF_NG = OFF_FF + LANES
OFF_GA = OFF_NG + LANES
N_PROJ = OFF_GA + 2 * D_MODEL

TM_PROJ = 512
TQ_FOX = 512
FOX_UNROLL = 4
TQ_NSA = 256
TK_SLC = 256
TM_MERGE = 512
TM_EXPERT = 256
TM_DISPATCH = 512


def _dot(a, b):
    return jnp.dot(a, b, preferred_element_type=F32)


def _dot_nt(a, b):
    return lax.dot_general(a, b, (((1,), (1,)), ((), ())), preferred_element_type=F32)


def _split3(x):
    hi = x.astype(BF16)
    r = x - hi.astype(F32)
    mid = r.astype(BF16)
    lo = (r - mid.astype(F32)).astype(BF16)
    return hi, mid, lo


def _cparams(sem):
    return pltpu.CompilerParams(dimension_semantics=sem, vmem_limit_bytes=VMEM_LIMIT)


def _const_spec(shape):
    nd = len(shape)
    return pl.BlockSpec(shape, lambda *_: (0,) * nd)


def _inproj_kernel(x_ref, g_ref, w_ref, bf_ref, gqa_ref, gka_ref, gqn_ref, gkn_ref,
                   bd_ref, bd2_ref, tri_ref, cq_ref, ck_ref, cv_ref,
                   qc0_ref, qca_ref, qcb_ref, kc0_ref, kca_ref, kcb_ref, vone_ref,
                   qa_ref, ka_ref, va_ref, qn_ref, kcr_ref, vcr_ref,
                   ksa_ref, vsa_ref, kwa_ref, vwa_ref, gt_ref, gab_ref,
                   carry_ref):
    tm = x_ref.shape[1]

    @pl.when(pl.program_id(1) == 0)
    def _():
        carry_ref[...] = jnp.zeros_like(carry_ref)

    x = x_ref[0]
    y = x * lax.rsqrt(jnp.mean(x * x, axis=-1, keepdims=True) + RMS_EPS)
    h = (y * g_ref[...]).astype(BF16)

    def proj(off, n):
        return _dot(h, w_ref[:, off:off + n])

    lo_half = lax.broadcasted_iota(jnp.int32, (tm, LANES), 1) < HALF

    pos = pl.program_id(1) * tm + lax.broadcasted_iota(jnp.int32, (tm, 1), 0)
    pos_a = ((pos >> 8) << 8).astype(F32)
    pos_b = (pos & 255).astype(F32)

    def pos_channels(c0_ref, ca_ref, cb_ref, k):
        blk = slice(LANES * k, LANES * (k + 1))
        return c0_ref[:, blk] + ca_ref[:, blk] * pos_a + cb_ref[:, blk] * pos_b

    def headnorm(z, bd, grow):
        msq = _dot((z * z).astype(BF16), bd[...])
        return z * lax.rsqrt(msq + RMS_EPS) * grow[...]

    def spread_pairs(out_ref, src, aug):
        for m in range(4):
            s = src[:, LANES * m:LANES * (m + 1)]
            out_ref[0, :, LANES * 2 * m:LANES * (2 * m + 1)] = jnp.where(lo_half, s, aug(2 * m)).astype(BF16)
            out_ref[0, :, LANES * (2 * m + 1):LANES * (2 * m + 2)] = jnp.where(lo_half, aug(2 * m + 1), s).astype(BF16)

    zf = proj(OFF_FF, LANES) + bf_ref[...]
    logf = jnp.minimum(zf, 0.0) - jnp.log(1.0 + jnp.exp(-jnp.abs(zf)))
    l_hi, l_mid, l_lo = _split3(logf)
    tri = tri_ref[...]
    cum = carry_ref[...] + (_dot(tri, l_hi) + _dot(tri, l_mid) + _dot(tri, l_lo))
    carry_ref[...] = cum[tm - 1:tm, :]
    pieces = [p.astype(F32) for p in _split3(cum)]
    lane128 = lax.broadcasted_iota(jnp.int32, (tm, LANES), 1)

    def aug_block(k, const_ref, first, sign):
        base = HALF if k % 2 == 0 else 0
        blk = jnp.broadcast_to(const_ref[:, LANES * k:LANES * (k + 1)], (tm, LANES))
        for j, piece in enumerate(pieces):
            blk = jnp.where(lane128 == base + first + j, sign * piece[:, k:k + 1], blk)
        return blk

    augq = lambda k: aug_block(k, cq_ref, 3, 1.0)
    augk = lambda k: aug_block(k, ck_ref, 0, -1.0)

    zq = headnorm(proj(OFF_FQ, FOX_W), bd_ref, gqa_ref)
    spread_pairs(qa_ref, zq, augq)
    zk = headnorm(proj(OFF_FK, FOX_W), bd_ref, gka_ref)
    spread_pairs(ka_ref, zk, augk)
    zv = proj(OFF_FV, FOX_W)
    spread_pairs(va_ref, zv, lambda k: cv_ref[:, LANES * k:LANES * (k + 1)])

    zn = headnorm(proj(OFF_NQ, NSA_W), bd_ref, gqn_ref)
    for m in range(NSA_REP):
        s = zn[:, LANES * m:LANES * (m + 1)]
        c0 = pos_channels(qc0_ref, qca_ref, qcb_ref, m)
        c1 = pos_channels(qc0_ref, qca_ref, qcb_ref, NSA_REP + m)
        qn_ref[0, :, LANES * m:LANES * (m + 1)] = jnp.where(lo_half, s, c0).astype(BF16)
        qn_ref[0, :, LANES * (NSA_REP + m):LANES * (NSA_REP + m + 1)] = jnp.where(lo_half, c1, s).astype(BF16)

    kcr_ref[0] = proj(OFF_KC, NSA_KV_W)
    vcr_ref[0] = proj(OFF_VC, NSA_KV_W)

    kp0 = pos_channels(kc0_ref, kca_ref, kcb_ref, 0)
    kp1 = pos_channels(kc0_ref, kca_ref, kcb_ref, 1)

    def kv_pair(k_out, v_out, off_k, off_v):
        zk2 = headnorm(proj(off_k, NSA_KV_W), bd2_ref, gkn_ref)
        k_out[0, :, 0:LANES] = jnp.where(lo_half, zk2, kp0).astype(BF16)
        k_out[0, :, LANES:2 * LANES] = jnp.where(lo_half, kp1, zk2).astype(BF16)
        zv2 = proj(off_v, NSA_KV_W)
        v_out[0, :, 0:LANES] = jnp.where(lo_half, zv2, vone_ref[:, 0:LANES]).astype(BF16)
        v_out[0, :, LANES:2 * LANES] = jnp.where(lo_half, vone_ref[:, LANES:2 * LANES], zv2).astype(BF16)

    kv_pair(ksa_ref, vsa_ref, OFF_KS, OFF_VS)
    kv_pair(kwa_ref, vwa_ref, OFF_KW, OFF_VW)

    gt_ref[0] = jax.nn.sigmoid(proj(OFF_NG, LANES))
    gab_ref[0, :, 0:D_MODEL] = jax.nn.sigmoid(proj(OFF_GA, D_MODEL)).astype(BF16)
    gab_ref[0, :, D_MODEL:2 * D_MODEL] = jax.nn.sigmoid(proj(OFF_GA + D_MODEL, D_MODEL)).astype(BF16)


def _pos_pieces(pos):
    return ((pos // 256) * 256).astype(F32), (pos % 256).astype(F32)


def _inproj(x, norm_g, w_in, b_forget, fox_q_g, fox_k_g, nsa_q_g, nsa_k_g):
    B, S, D = x.shape
    tm = min(TM_PROJ, S)
    scale = HEAD_DIM ** -0.5

    c = [0]
    for n in (FOX_W, FOX_W, FOX_W, FOX_HEADS, NSA_W) + (NSA_KV_W,) * 6 + (3 * NSA_HEADS, D_MODEL, D_MODEL):
        c.append(c[-1] + n)
    fq, fk, fv, ff, nq, kc, vc, ks, vs, kw, vw, ng, ga, gb = [w_in[:, c[i]:c[i + 1]] for i in range(14)]
    perm = jnp.asarray([0, 4, 1, 5, 2, 6, 3, 7])
    nq = nq.reshape(D, NSA_HEADS, HEAD_DIM)[:, perm, :].reshape(D, NSA_W)
    padl = lambda a: jnp.pad(a, ((0, 0), (0, LANES - a.shape[1])))
    w = jnp.concatenate([fq, fk, fv, nq, kc, vc, ks, vs, kw, vw, padl(ff), padl(ng), ga, gb], axis=1).astype(BF16)
    assert w.shape[1] == N_PROJ

    bf = jnp.pad(b_forget.astype(F32), (0, LANES - FOX_HEADS))[None, :]
    gqa = jnp.tile(fox_q_g.astype(F32) * scale, FOX_HEADS)[None, :]
    gka = jnp.tile(fox_k_g.astype(F32), FOX_HEADS)[None, :]
    gqn = jnp.tile(nsa_q_g.astype(F32) * scale, NSA_HEADS)[None, :]
    gkn = jnp.tile(nsa_k_g.astype(F32), NSA_KV_HEADS)[None, :]

    r512 = jnp.arange(FOX_W)
    bd = jnp.where((r512[:, None] // HEAD_DIM) == (r512[None, :] // HEAD_DIM), 1.0 / HEAD_DIM, 0.0).astype(BF16)
    bd2 = bd[:LANES, :LANES]
    rt = jnp.arange(tm)
    tri = (rt[None, :] <= rt[:, None]).astype(BF16)

    heads = jnp.arange(FOX_HEADS)
    base = heads * LANES + jnp.where(heads % 2 == 0, HALF, 0)
    cols = jnp.arange(FOX_HEADS * LANES)
    off_in_blk = cols - base[cols // LANES]
    cq = ((off_in_blk >= 0) & (off_in_blk < 3)).astype(F32)[None, :]
    ck = ((off_in_blk >= 3) & (off_in_blk < 9)).astype(F32)[None, :]
    cv = (off_in_blk == 0).astype(F32)[None, :]

    blk = cols // LANES
    slope = 2.0 ** (-(blk + 1).astype(F32))
    o = cols % LANES - jnp.where(blk // NSA_REP == 0, HALF, 0)
    qc0 = jnp.where((o == 0) | (o == 1), slope, 0.0)[None, :]
    qca = jnp.where(o == 2, -slope, 0.0)[None, :]
    qcb = jnp.where(o == 3, -slope, 0.0)[None, :]
    kc0, kca, kcb, vone = _kv_rows()

    grid = (B, S // tm)
    row_spec = lambda n: pl.BlockSpec((1, tm, n), lambda b, s: (b, s, 0))
    consts = [norm_g.astype(F32)[None, :], w, bf, gqa, gka, gqn, gkn, bd, bd2, tri, cq, ck, cv,
              qc0, qca, qcb, kc0, kca, kcb, vone]
    out_widths = [(8 * LANES, BF16)] * 4 + [(LANES, F32)] * 2 + [(2 * LANES, BF16)] * 4 + \
                 [(LANES, F32), (2 * D_MODEL, BF16)]
    outs = pl.pallas_call(
        _inproj_kernel,
        grid=grid,
        in_specs=[row_spec(D)] + [_const_spec(a.shape) for a in consts],
        out_specs=[row_spec(n) for n, _ in out_widths],
        out_shape=[jax.ShapeDtypeStruct((B, S, n), dt) for n, dt in out_widths],
        scratch_shapes=[pltpu.VMEM((1, LANES), F32)],
        compiler_params=_cparams(("arbitrary", "arbitrary")),
        name="inproj",
    )(x, *consts)
    return outs


def _kv_rows():
    cols = jnp.arange(2 * LANES)
    o = cols % LANES - jnp.where(cols // LANES == 0, HALF, 0)
    row = lambda m: m.astype(F32)[None, :]
    return row((o >= 2) & (o <= 6)), row(o == 0), row(o == 1), row(o == 0)


def _kv_consts(pa, pb):
    c0, ca, cb, vone = _kv_rows()
    return (c0 + ca * pa[:, None] + cb * pb[:, None]).astype(BF16), vone


def _compress_kernel(kt_ref, vt_ref, w1k_ref, w1v_ref, posk_ref, posv_ref, pw1k_ref, pw1v_ref, w2k_ref, w2v_ref,
                     gk_ref, kcc_ref, vone_ref, kc_ref, vc_ref):
    nc = kc_ref.shape[2]

    def mlp(t_ref, w1_ref, pos_ref, pw1_ref, w2_ref):
        both = jnp.zeros((nc, 2 * CMP_HIDDEN), F32)
        for l in range(CMP_STRIDE):
            rows = t_ref[0, pl.ds(l, nc, stride=CMP_STRIDE), :].astype(BF16)
            both = both + _dot(rows, w1_ref[0, l])
        posw = _dot(pos_ref[...], pw1_ref[...])[0:1, :]
        pre = both[:, 0:CMP_HIDDEN] + pltpu.roll(both[:, CMP_HIDDEN:2 * CMP_HIDDEN], nc - 1, axis=0) + posw
        act = pre * (0.5 * (1.0 + jnp.tanh(0.7978845608028654 * (pre + 0.044715 * (pre * pre * pre)))))
        return _dot(act.astype(BF16), w2_ref[0])

    kc = mlp(kt_ref, w1k_ref, posk_ref, pw1k_ref, w2k_ref)
    msq = jnp.sum(kc * kc, axis=-1, keepdims=True) * (1.0 / HEAD_DIM)
    kc_ref[0, 0] = (kc * lax.rsqrt(msq + RMS_EPS) * gk_ref[0] + kcc_ref[0]).astype(BF16)
    vc = mlp(vt_ref, w1v_ref, posv_ref, pw1v_ref, w2v_ref)
    vc_ref[0, 0] = (vc + vone_ref[0]).astype(BF16)


def _compress(kcr, vcr, cmp_k_w1, cmp_k_w2, cmp_k_pos, cmp_v_w1, cmp_v_w2, cmp_v_pos, nsa_k_g):
    B, S, _ = kcr.shape
    G = NSA_KV_HEADS
    nc = S // CMP_STRIDE

    def w1_strided(w1):
        w = w1.reshape(2, CMP_STRIDE, HEAD_DIM, CMP_HIDDEN)
        w = jnp.concatenate([w[0], w[1]], axis=-1)
        z = jnp.zeros_like(w)
        return jnp.stack([jnp.concatenate([w, z], axis=1), jnp.concatenate([z, w], axis=1)]).astype(BF16)

    def w2_spread(w2):
        z = jnp.zeros_like(w2)
        return jnp.stack([jnp.concatenate([w2, z], 1), jnp.concatenate([z, w2], 1)]).astype(BF16)

    def pos8(p):
        return jnp.tile(p.reshape(1, CMP_BLOCK * HEAD_DIM), (8, 1)).astype(BF16)

    gk = nsa_k_g.astype(F32)
    z = jnp.zeros_like(gk)
    gk2 = jnp.stack([jnp.concatenate([gk, z]), jnp.concatenate([z, gk])])[:, None, :]
    cend = jnp.arange(nc) * CMP_STRIDE + CMP_BLOCK - 1
    kcc, vone = _kv_consts(*_pos_pieces(cend))
    kcc = kcc.astype(F32).reshape(nc, G, LANES).transpose(1, 0, 2)
    vone = vone.reshape(G, 1, LANES)

    tok = pl.BlockSpec((1, S, G * HEAD_DIM), lambda b, g: (b, 0, 0))
    per_g = lambda a: pl.BlockSpec((1,) + a.shape[1:], lambda b, g: (g,) + (0,) * (a.ndim - 1))
    w1k, w1v = w1_strided(cmp_k_w1), w1_strided(cmp_v_w1)
    pk, pv = pos8(cmp_k_pos), pos8(cmp_v_pos)
    pw1k, pw1v = cmp_k_w1.astype(BF16), cmp_v_w1.astype(BF16)
    w2k, w2v = w2_spread(cmp_k_w2), w2_spread(cmp_v_w2)
    out_spec = pl.BlockSpec((1, 1, nc, LANES), lambda b, g: (b, g, 0, 0))
    return pl.pallas_call(
        _compress_kernel,
        grid=(B, G),
        in_specs=[tok, tok, per_g(w1k), per_g(w1v), _const_spec(pk.shape), _const_spec(pv.shape),
                  _const_spec(pw1k.shape), _const_spec(pw1v.shape), per_g(w2k), per_g(w2v),
                  per_g(gk2), per_g(kcc), per_g(vone)],
        out_specs=[out_spec, out_spec],
        out_shape=[jax.ShapeDtypeStruct((B, G, nc, LANES), BF16)] * 2,
        compiler_params=_cparams(("arbitrary", "arbitrary")),
        name="compress",
    )(kcr, vcr, w1k, w1v, pk, pv, pw1k, pw1v, w2k, w2v, gk2, kcc, vone)


MAX_EXPONENT = 60.0


def _scores_bounded(q_g, k_g):
    bound = HEAD_DIM ** 0.5 * jnp.max(jnp.abs(q_g)) * jnp.max(jnp.abs(k_g))
    return 2.04 * bound + 0.05 <= MAX_EXPONENT


def _attn_first(s, v):
    m = jnp.max(s, axis=-1, keepdims=True)
    p = jnp.exp((s - m).astype(BF16))
    return m, _dot(p, v)


def _attn_step(carry, s, v):
    m, acc = carry
    m_new = jnp.maximum(m, jnp.max(s, axis=-1, keepdims=True))
    p = jnp.exp((s - m_new).astype(BF16))
    return m_new, jnp.exp(m - m_new) * acc + _dot(p, v)


def _nsa_kernel(bounded, q_ref, kc_ref, vc_ref, ks_ref, e_ref, vs_ref, kw_ref, vw_ref, gt_ref, ov_ref, o_ref,
                tiles_ref, sa_ref, sb_ref):
    tq = q_ref.shape[1]
    nc = kc_ref.shape[2]
    tk = TK_SLC
    rq = NSA_REP * tq
    g = pl.program_id(1)
    i = pl.program_id(2)
    q0 = i * tq

    q4 = jnp.concatenate([q_ref[0, :, LANES * r:LANES * (r + 1)] for r in range(NSA_REP)], axis=0)

    def qpos_of(shape):
        return q0 + (lax.broadcasted_iota(jnp.int32, shape, 0) & (tq - 1))

    def add_mask(s, valid):
        bias = jnp.where(valid, 0.0, NEG_INF)
        return (s.reshape(NSA_REP, tq, s.shape[1]) + bias[None]).reshape(s.shape)

    def qrow(n):
        return q0 + lax.broadcasted_iota(jnp.int32, (tq, n), 0)

    def kcol(n):
        return lax.broadcasted_iota(jnp.int32, (tq, n), 1)

    s = add_mask(_dot_nt(q4, kc_ref[0, 0]), qrow(nc) >= kcol(nc) * CMP_STRIDE + (CMP_BLOCK - 1))
    anyv = qpos_of((rq, 1)) >= CMP_BLOCK - 1
    if bounded:
        e = jnp.exp(s)
        p = e * jnp.where(anyv, 1.0 / jnp.sum(e, axis=-1, keepdims=True), 0.0)
    else:
        m = jnp.max(s, axis=-1, keepdims=True)
        e = jnp.exp(s - m)
        p = e * (anyv.astype(F32) / jnp.sum(e, axis=-1, keepdims=True))
    o_cmp = _dot(p.astype(BF16), vc_ref[0, 0])

    psum = p[0:tq] + p[tq:2 * tq] + p[2 * tq:3 * tq] + p[3 * tq:4 * tq]
    p_hi, p_mid, p_lo = _split3(psum)
    ov = ov_ref[...]
    imp = _dot(p_hi, ov) + _dot(p_mid, ov) + _dot(p_lo, ov)
    blk_i = lax.broadcasted_iota(jnp.int32, (LANES, tq), 0)
    blk_f = blk_i.astype(F32)
    qblk = (q0 + lax.broadcasted_iota(jnp.int32, (LANES, tq), 1)) // SLC_BLOCK
    forced = (blk_i == 0) | (blk_i == qblk) | (blk_i == qblk - 1)
    score = jnp.where(blk_i > qblk, -1.0, jnp.where(forced, FORCE_SCORE, imp.T))
    selb = jnp.full((LANES, tq), NEG_INF, F32)
    for _ in range(SLC_TOPK):
        mx = jnp.max(score, axis=0, keepdims=True)
        first = jnp.min(jnp.where(score == mx, blk_f, float(LANES)), axis=0, keepdims=True)
        hit = blk_f == first
        selb = jnp.where(hit, 0.0, selb)
        score = jnp.where(hit, -3e38, score)
    sb = selb.T.astype(BF16)
    sb4 = jnp.concatenate([sb] * NSA_REP, axis=0)

    def slc_qk(j):
        start = pl.multiple_of(j * tk, tk)
        kk = jnp.concatenate([ks_ref[0, pl.ds(start, tk), :], e_ref[pl.ds(start, tk), :]], axis=1)
        return _dot_nt(q2, kk)

    def slc_v(j):
        return vs_ref[0, pl.ds(pl.multiple_of(j * tk, tk), tk), :]

    jd = q0 // tk
    blocks_per_tile = tk // SLC_BLOCK
    anysel = jnp.max(selb, axis=1, keepdims=True)
    cnt = jnp.int32(0)
    for j in range(ks_ref.shape[1] // tk):
        tile_sel = jnp.max(anysel[blocks_per_tile * j:blocks_per_tile * (j + 1), :]) > -1.0
        tiles_ref[cnt] = j
        cnt = cnt + (tile_sel & (j < jd)).astype(jnp.int32)

    lane_r = lax.broadcasted_iota(jnp.int32, (rq, LANES), 1)
    data0 = HALF * g
    ones_lane = HALF - data0
    nw = WINDOW + tq
    wstart = pl.multiple_of(jnp.maximum(q0 - WINDOW, 0), tq)
    win_valid = lax.bitcast_convert_type(qrow(nw) - (wstart + kcol(nw)), jnp.uint32) < WINDOW
    diag_valid = jd * tk + kcol(tk) <= qrow(tk)

    if bounded:
        def shifted_by_own_score(kself):
            own = jnp.sum(q4.astype(F32).reshape(NSA_REP, tq, LANES) * kself.astype(F32)[None],
                          axis=-1, keepdims=True).reshape(rq, 1)
            qf = q4.astype(F32)
            for off, piece in enumerate(_split3(-own)):
                qf = jnp.where(lane_r == ones_lane + 4 + off, piece.astype(F32), qf)
            return qf.astype(BF16)

        q2 = jnp.concatenate([shifted_by_own_score(ks_ref[0, pl.ds(q0, tq), :]), sb4], axis=1)
        sa_ref[:, 0:LANES] = _dot(jnp.exp(add_mask(slc_qk(jd), diag_valid)).astype(BF16), slc_v(jd))

        def slc_pv(t):
            j = tiles_ref[t]
            return _dot(jnp.exp(slc_qk(j)).astype(BF16), slc_v(j))

        def slc_body(p, _):
            sa_ref[:, 0:LANES] += slc_pv(2 * p) + slc_pv(2 * p + 1)
            return _

        lax.fori_loop(0, cnt // 2, slc_body, 0)

        @pl.when(cnt % 2 == 1)
        def _():
            sa_ref[:, 0:LANES] += slc_pv(cnt - 1)

        acc_slc = sa_ref[:, 0:LANES]

        qw = shifted_by_own_score(kw_ref[0, pl.ds(q0, tq), :])
        s = add_mask(_dot_nt(qw, kw_ref[0, pl.ds(wstart, nw), :]), win_valid)
        acc_win = _dot(jnp.exp(s).astype(BF16), vw_ref[0, pl.ds(wstart, nw), :])
    else:
        q2 = jnp.concatenate([q4, sb4], axis=1)
        n_seq = cnt + 1

        def tile_of(n):
            return jnp.where(n == 0, jd, tiles_ref[jnp.maximum(n - 1, 0)])

        def slc_step(carry, s_buf, n):
            return _attn_step(carry, s_buf[...], slc_v(tile_of(n)))

        sa_ref[...] = add_mask(slc_qk(jd), diag_valid)

        def slc_body(p, carry):
            sb_ref[...] = slc_qk(tile_of(2 * p + 1))
            carry = slc_step(carry, sa_ref, 2 * p)
            sa_ref[...] = slc_qk(tile_of(jnp.minimum(2 * p + 2, n_seq - 1)))
            return slc_step(carry, sb_ref, 2 * p + 1)

        carry = (jnp.full((rq, 1), NEG_INF, F32), jnp.zeros((rq, LANES), F32))
        carry = lax.fori_loop(0, n_seq // 2, slc_body, carry)
        _, acc_slc = lax.cond(n_seq % 2 == 1, lambda c: slc_step(c, sa_ref, n_seq - 1), lambda c: c, carry)

        s = add_mask(_dot_nt(q4, kw_ref[0, pl.ds(wstart, nw), :]), win_valid)
        _, acc_win = _attn_first(s, vw_ref[0, pl.ds(wstart, nw), :])


    def normalise(acc):
        l = jnp.sum(jnp.where(lane_r == ones_lane, acc, 0.0), axis=-1, keepdims=True)
        return acc * (1.0 / l)

    o_slc = normalise(acc_slc)
    o_win = normalise(acc_win)
    gt = gt_ref[0]
    lane_g = lax.broadcasted_iota(jnp.int32, (tq, LANES), 1)
    is_data_q = (lane_g >= data0) & (lane_g < data0 + HALF)
    for r in range(NSA_REP):
        col = 3 * (NSA_REP * g + r)
        gate = [jnp.sum(jnp.where(lane_g == col + b, gt, 0.0), axis=-1, keepdims=True) for b in range(3)]
        rows = slice(r * tq, (r + 1) * tq)
        o = gate[0] * o_cmp[rows] + gate[1] * o_slc[rows] + gate[2] * o_win[rows]
        o_ref[0, :, LANES * r:LANES * (r + 1)] = jnp.where(is_data_q, o, 0.0).astype(BF16)


def _nsa(qn, kca, vca, ksa, vsa, kwa, vwa, gt, scores_bounded):
    B, S, _ = qn.shape
    G = NSA_KV_HEADS
    tq = min(TQ_NSA, S)
    nc = S // CMP_STRIDE
    n_slc = S // SLC_BLOCK
    assert n_slc <= LANES and S % TK_SLC == 0

    cs = jnp.arange(nc)[:, None] * CMP_STRIDE
    ss = jnp.arange(LANES)[None, :] * SLC_BLOCK
    ovl = jnp.clip(jnp.minimum(cs + CMP_BLOCK, ss + SLC_BLOCK) - jnp.maximum(cs, ss), 0, None)
    valid = (jnp.arange(nc)[:, None] < (S - CMP_BLOCK) // CMP_STRIDE + 1) & (jnp.arange(LANES)[None, :] < n_slc)
    ov = jnp.where(valid, ovl.astype(F32) / CMP_BLOCK, 0.0).astype(BF16)
    e1h = (jnp.arange(S)[:, None] // SLC_BLOCK == jnp.arange(LANES)[None, :]).astype(BF16)

    q_spec = pl.BlockSpec((1, tq, NSA_REP * LANES), lambda b, g, i: (b, i, g))
    c_spec = pl.BlockSpec((1, 1, nc, LANES), lambda b, g, i: (b, g, 0, 0))
    kv_spec = pl.BlockSpec((1, S, LANES), lambda b, g, i: (b, 0, g))

    def call(bounded, name):
        return pl.pallas_call(
            functools.partial(_nsa_kernel, bounded),
            grid=(B, G, S // tq),
            in_specs=[q_spec, c_spec, c_spec, kv_spec, _const_spec(e1h.shape), kv_spec, kv_spec, kv_spec,
                      pl.BlockSpec((1, tq, LANES), lambda b, g, i: (b, i, 0)), _const_spec(ov.shape)],
            out_specs=q_spec,
            out_shape=jax.ShapeDtypeStruct((B, S, NSA_HEADS * LANES), BF16),
            scratch_shapes=[pltpu.SMEM((S // TK_SLC + 1,), jnp.int32),
                            pltpu.VMEM((NSA_REP * tq, TK_SLC), F32), pltpu.VMEM((NSA_REP * tq, TK_SLC), F32)],
            compiler_params=_cparams(("arbitrary", "arbitrary", "arbitrary")),
            name=name,
        )

    return lax.cond(scores_bounded, call(True, "nsa_bounded"), call(False, "nsa"),
                    qn, kca, vca, ksa, e1h, vsa, kwa, vwa, gt, ov)


def _fox_kernel(q_ref, k_ref, v_ref, o_ref, sa_ref, sb_ref):
    tq = q_ref.shape[1]
    i = pl.program_id(2)
    lane = lax.broadcasted_iota(jnp.int32, (tq, LANES), 1)
    causal = lax.broadcasted_iota(jnp.int32, (tq, tq), 1) <= lax.broadcasted_iota(jnp.int32, (tq, tq), 0)

    def cols(hh):
        return slice(LANES * hh, LANES * (hh + 1))

    def qk(hh, j):
        start = pl.multiple_of(j * tq, tq)
        return _dot_nt(q_ref[0, :, cols(hh)], k_ref[0, pl.ds(start, tq), cols(hh)])

    def vtile(hh, j):
        return v_ref[0, pl.ds(pl.multiple_of(j * tq, tq), tq), cols(hh)]

    n_tiles = i + 1

    def tile_of(n):
        return jnp.where(n == 0, i, n - 1)

    def step_all(carry, s_buf, n):
        t = tile_of(n)
        return tuple(_attn_step(carry[hh], s_buf[hh], vtile(hh, t)) for hh in range(2))

    for hh in range(2):
        sa_ref[hh] = jnp.where(causal, qk(hh, i), NEG_INF)
    init = tuple((jnp.full((tq, 1), NEG_INF, F32), jnp.zeros((tq, LANES), F32)) for hh in range(2))

    def body(p, carry):
        for hh in range(2):
            sb_ref[hh] = qk(hh, tile_of(2 * p + 1))
        carry = step_all(carry, sa_ref, 2 * p)
        nxt = jnp.minimum(2 * p + 2, n_tiles - 1)
        for hh in range(2):
            sa_ref[hh] = qk(hh, tile_of(nxt))
        return step_all(carry, sb_ref, 2 * p + 1)

    carry = lax.fori_loop(0, n_tiles // 2, body, init)
    carry = lax.cond(n_tiles % 2 == 1, lambda c: step_all(c, sa_ref, n_tiles - 1), lambda c: c, carry)
    outs = []
    for hh in range(2):
        acc = carry[hh][1]
        ones_lane = HALF if hh == 0 else 0
        l = jnp.sum(jnp.where(lane == ones_lane, acc, 0.0), axis=-1, keepdims=True)
        outs.append(acc * (1.0 / l))
    o_ref[0] = jnp.where(lane < HALF, outs[0], outs[1]).astype(BF16)


def _fox_bounded_kernel(q_ref, k_ref, v_ref, o_ref, qs_ref, acc_ref):
    tq = q_ref.shape[1]
    i = pl.program_id(2)
    lane = lax.broadcasted_iota(jnp.int32, (tq, LANES), 1)
    causal = lax.broadcasted_iota(jnp.int32, (tq, tq), 1) <= lax.broadcasted_iota(jnp.int32, (tq, tq), 0)

    def cols(hh):
        return slice(LANES * hh, LANES * (hh + 1))

    def ktile(hh, j):
        return k_ref[0, pl.ds(pl.multiple_of(j * tq, tq), tq), cols(hh)]

    def vtile(hh, j):
        return v_ref[0, pl.ds(pl.multiple_of(j * tq, tq), tq), cols(hh)]

    for hh in range(2):
        q = q_ref[0, :, cols(hh)]
        s = jnp.where(causal, _dot_nt(q, ktile(hh, i)), NEG_INF)
        m = jnp.max(s, axis=-1, keepdims=True)
        acc_ref[hh] = _dot(jnp.exp((s - m).astype(BF16)), vtile(hh, i))
        qf = q.astype(F32)
        free0 = (HALF if hh == 0 else 0) + 6
        for off, piece in enumerate(_split3(-m)):
            qf = jnp.where(lane == free0 + off, piece.astype(F32), qf)
        qs_ref[hh] = qf.astype(BF16)

    def sweep(j0, n):
        for hh in range(2):
            pv = [_dot(jnp.exp(_dot_nt(qs_ref[hh], ktile(hh, j0 + t)).astype(BF16)), vtile(hh, j0 + t))
                  for t in range(n)]
            acc_ref[hh] += functools.reduce(lambda a, b: a + b, pv)

    def body(c, _):
        sweep(c * FOX_UNROLL, FOX_UNROLL)
        return _

    lax.fori_loop(0, i // FOX_UNROLL, body, 0)
    done = (i // FOX_UNROLL) * FOX_UNROLL
    n = FOX_UNROLL // 2
    while n >= 1:
        @pl.when((i & n) != 0)
        def _(n=n, start=done):
            sweep(start, n)
        done = done + (i & n)
        n //= 2
    outs = []
    for hh in range(2):
        acc = acc_ref[hh]
        ones_lane = HALF if hh == 0 else 0
        l = jnp.sum(jnp.where(lane == ones_lane, acc, 0.0), axis=-1, keepdims=True)
        outs.append(acc * (1.0 / l))
    o_ref[0] = jnp.where(lane < HALF, outs[0], outs[1]).astype(BF16)


def _fox(qa, ka, va, scores_bounded):
    B, S, _ = qa.shape
    tq = min(TQ_FOX, S)
    q_spec = pl.BlockSpec((1, tq, 2 * LANES), lambda b, h, i: (b, i, h))
    kv_spec = pl.BlockSpec((1, S, 2 * LANES), lambda b, h, i: (b, 0, h))

    def call(body, scratch, name):
        return pl.pallas_call(
            body,
            grid=(B, FOX_HEADS // 2, S // tq),
            in_specs=[q_spec, kv_spec, kv_spec],
            out_specs=pl.BlockSpec((1, tq, LANES), lambda b, h, i: (b, i, h)),
            out_shape=jax.ShapeDtypeStruct((B, S, FOX_W), BF16),
            scratch_shapes=scratch,
            compiler_params=_cparams(("arbitrary", "arbitrary", "arbitrary")),
            name=name,
        )

    general = call(_fox_kernel, [pltpu.VMEM((2, tq, tq), F32), pltpu.VMEM((2, tq, tq), F32)], "fox")
    bounded = call(_fox_bounded_kernel, [pltpu.VMEM((2, tq, LANES), BF16), pltpu.VMEM((2, tq, LANES), F32)],
                   "fox_bounded")
    return lax.cond(scores_bounded, bounded, general, qa, ka, va)


R_E1, R_E2, R_W1, R_W2, R_RANK1, R_RANK2 = range(6)


def _merge_kernel(x_ref, oa_ref, ob_ref, gab_ref, wa_ref, wb_ref, wo_ref, g2_ref, wr_hi_ref, wr_lo_ref, br_ref,
                  ltri_ref, x1_ref, route_ref, cnt_ref, carry_ref):
    tm = x_ref.shape[1]

    @pl.when((pl.program_id(0) == 0) & (pl.program_id(1) == 0))
    def _():
        carry_ref[...] = jnp.zeros_like(carry_ref)

    out_a = _dot(oa_ref[0], wa_ref[...])
    out_b = _dot(ob_ref[0], wb_ref[...])
    mix = gab_ref[0, :, 0:D_MODEL].astype(F32) * out_a + gab_ref[0, :, D_MODEL:2 * D_MODEL].astype(F32) * out_b
    x1 = x_ref[0] + _dot(mix.astype(BF16), wo_ref[...])
    x1_ref[...] = x1
    h2 = x1 * lax.rsqrt(jnp.mean(x1 * x1, axis=-1, keepdims=True) + RMS_EPS) * g2_ref[...]

    h_hi = h2.astype(BF16)
    h_lo = (h2 - h_hi.astype(F32)).astype(BF16)
    logits = _dot(h_hi, wr_hi_ref[...]) + (_dot(h_hi, wr_lo_ref[...]) + _dot(h_lo, wr_hi_ref[...])) + br_ref[...]
    lane = lax.broadcasted_iota(jnp.int32, (tm, LANES), 1)
    lane_f = lane.astype(F32)

    def first_argmax(vals):
        mx = jnp.max(vals, axis=-1, keepdims=True)
        idx = jnp.min(jnp.where(vals == mx, lane_f, float(LANES)), axis=-1, keepdims=True)
        return mx, idx

    is_grp = (lane >= N_EXPERTS) & (lane < N_EXPERTS + N_GROUPS)
    gl = jnp.where(is_grp, logits, NEG_INF)
    gmax, gidx = first_argmax(gl)
    p_g = 1.0 / jnp.sum(jnp.where(is_grp, jnp.exp(gl - gmax), 0.0), axis=-1, keepdims=True)
    e_lo = (gidx - float(N_EXPERTS)) * float(EXPERTS_PER_GROUP)
    in_grp = (lane_f >= e_lo) & (lane_f < e_lo + float(EXPERTS_PER_GROUP))
    el = jnp.where(in_grp, logits, NEG_INF)
    m1, i1 = first_argmax(el)
    m2, i2 = first_argmax(jnp.where(lane_f == i1, NEG_INF, el))
    e2 = jnp.exp(m2 - m1)
    w1 = p_g / (1.0 + e2)
    w2 = p_g * e2 / (1.0 + e2)

    hit1 = lane_f == i1
    hit2 = lane_f == i2
    onehot = jnp.where(hit1 | hit2, 1.0, 0.0)
    before = carry_ref[...] + _dot(ltri_ref[...], onehot.astype(BF16))
    rank1 = jnp.sum(jnp.where(hit1, before, 0.0), axis=-1, keepdims=True)
    rank2 = jnp.sum(jnp.where(hit2, before, 0.0), axis=-1, keepdims=True)
    total = carry_ref[...] + jnp.sum(onehot, axis=0, keepdims=True)
    carry_ref[...] = total
    cnt_ref[...] = total

    rec = jnp.zeros((tm, LANES), F32)
    for k, val in ((R_E1, i1), (R_E2, i2), (R_W1, w1), (R_W2, w2), (R_RANK1, rank1), (R_RANK2, rank2)):
        rec = jnp.where(lane == k, val, rec)
    route_ref[...] = rec


def _merge(x, oa, ob, gab, w_fox_up, w_nsa_up, w_out, norm_ffn_g, w_group, b_group, w_router, b_router):
    B, S, D = x.shape
    tm = min(TM_MERGE, S)
    wa = w_fox_up.astype(BF16)
    wn = w_nsa_up.reshape(NSA_KV_HEADS, NSA_REP, HEAD_DIM, D)
    z = jnp.zeros_like(wn[0])
    wb = jnp.stack([jnp.concatenate([wn[0], z], axis=1), jnp.concatenate([z, wn[1]], axis=1)])
    wb = wb.reshape(NSA_HEADS * LANES, D).astype(BF16)
    wo = w_out.astype(BF16)
    wr = jnp.pad(jnp.concatenate([w_router, w_group], axis=1).astype(F32),
                 ((0, 0), (0, LANES - N_EXPERTS - N_GROUPS)))
    wr_hi = wr.astype(BF16)
    wr_lo = (wr - wr_hi.astype(F32)).astype(BF16)
    br = jnp.pad(jnp.concatenate([b_router, b_group]).astype(F32), (0, LANES - N_EXPERTS - N_GROUPS))[None, :]
    g2 = norm_ffn_g.astype(F32)[None, :]
    rt = jnp.arange(tm)
    ltri = (rt[None, :] < rt[:, None]).astype(BF16)

    row = lambda n: pl.BlockSpec((1, tm, n), lambda b, s: (b, s, 0))
    flat = lambda n: pl.BlockSpec((tm, n), lambda b, s: (b * (S // tm) + s, 0))
    consts = [wa, wb, wo, g2, wr_hi, wr_lo, br, ltri]
    return pl.pallas_call(
        _merge_kernel,
        grid=(B, S // tm),
        in_specs=[row(D), row(FOX_W), row(NSA_HEADS * LANES), row(2 * D)] + [_const_spec(a.shape) for a in consts],
        out_specs=[flat(D), flat(LANES), _const_spec((1, LANES))],
        out_shape=[jax.ShapeDtypeStruct((B * S, D), F32), jax.ShapeDtypeStruct((B * S, LANES), F32),
                   jax.ShapeDtypeStruct((1, LANES), F32)],
        scratch_shapes=[pltpu.VMEM((1, LANES), F32)],
        compiler_params=_cparams(("arbitrary", "arbitrary")),
        name="merge",
    )(x, oa, ob, gab, *consts)


def _row_copy(src_ref, src_row, dst_ref, dst_row, sem):
    return pltpu.make_async_copy(src_ref.at[pl.ds(src_row, 1), :], dst_ref.at[pl.ds(dst_row, 1), :], sem)


def _dispatch_kernel(row1_ref, row2_ref, clear_ref, x1_ref, xs_ref, zero_ref, sem, zsem):
    tm = x1_ref.shape[0]
    base = pl.program_id(0) * tm

    @pl.when(pl.program_id(0) == 0)
    def _():
        zero_ref[...] = jnp.zeros_like(zero_ref)

        def clear(c):
            start = pl.multiple_of(jnp.maximum(clear_ref[c], 0), zero_ref.shape[0])
            return pltpu.make_async_copy(zero_ref, xs_ref.at[pl.ds(start, zero_ref.shape[0]), :], zsem)

        for c in range(clear_ref.shape[0]):
            @pl.when(clear_ref[c] >= 0)
            def _(c=c):
                clear(c).start()
        for c in range(clear_ref.shape[0]):
            @pl.when(clear_ref[c] >= 0)
            def _(c=c):
                clear(c).wait()

    def start(t, _):
        _row_copy(x1_ref, t, xs_ref, row1_ref[base + t], sem).start()
        _row_copy(x1_ref, t, xs_ref, row2_ref[base + t], sem).start()
        return _

    def wait(t, _):
        _row_copy(x1_ref, t, xs_ref, row1_ref[base + t], sem).wait()
        _row_copy(x1_ref, t, xs_ref, row2_ref[base + t], sem).wait()
        return _

    lax.fori_loop(0, tm, start, 0, unroll=8)
    lax.fori_loop(0, tm, wait, 0, unroll=8)


def _experts_kernel(tile_e_ref, n_used_ref, xs_ref, g2_ref, wg_ref, wu_ref, wd_ref, ys_ref,
                    wg_bf, wu_bf, wd_bf):
    k = pl.program_id(0)
    used = k < n_used_ref[0]

    @pl.when(used & ((k == 0) | (tile_e_ref[k] != tile_e_ref[jnp.maximum(k - 1, 0)])))
    def _():
        wg_bf[...] = wg_ref[0, 0].astype(BF16)
        wu_bf[...] = wu_ref[0, 0].astype(BF16)
        wd_bf[...] = wd_ref[0, 0].astype(BF16)

    @pl.when(used)
    def _():
        x = xs_ref[...]
        h = (x * lax.rsqrt(jnp.mean(x * x, axis=-1, keepdims=True) + RMS_EPS) * g2_ref[...]).astype(BF16)
        a = _dot(h, wg_bf[...])
        hid = (a * jax.nn.sigmoid(a)) * _dot(h, wu_bf[...])
        ys_ref[...] = _dot(hid.astype(BF16), wd_bf[...])

    @pl.when(jnp.logical_not(used))
    def _():
        ys_ref[...] = jnp.zeros_like(ys_ref)


def _combine_kernel(row1_ref, row2_ref, x1_ref, route_ref, ys_ref, o_ref, y1_ref, y2_ref, sem):
    tm = x1_ref.shape[0]
    base = pl.program_id(0) * tm

    def start(t, _):
        _row_copy(ys_ref, row1_ref[base + t], y1_ref, t, sem).start()
        _row_copy(ys_ref, row2_ref[base + t], y2_ref, t, sem).start()
        return _

    def wait(t, _):
        _row_copy(ys_ref, row1_ref[base + t], y1_ref, t, sem).wait()
        _row_copy(ys_ref, row2_ref[base + t], y2_ref, t, sem).wait()
        return _

    lax.fori_loop(0, tm, start, 0, unroll=8)
    lax.fori_loop(0, tm, wait, 0, unroll=8)
    rec = route_ref[...]
    lane = lax.broadcasted_iota(jnp.int32, rec.shape, 1)
    w1 = jnp.sum(jnp.where(lane == R_W1, rec, 0.0), axis=-1, keepdims=True)
    w2 = jnp.sum(jnp.where(lane == R_W2, rec, 0.0), axis=-1, keepdims=True)
    o_ref[0] = x1_ref[...] + (w1 * y1_ref[...] + w2 * y2_ref[...])


def _moe(x1, route, cnt, norm_ffn_g, w_gate, w_up, w_down, layer, B, S):
    T, D = x1.shape
    tme = min(TM_EXPERT, T)
    tmd = min(TM_DISPATCH, S)

    counts = cnt[0, :N_EXPERTS].astype(jnp.int32)
    tiles_per_e = (counts + tme - 1) // tme
    tile_end = jnp.cumsum(tiles_per_e)
    offs = (tile_end - tiles_per_e) * tme
    last_tile_row = jnp.where(tiles_per_e > 0, (tile_end - 1) * tme, -1).astype(jnp.int32)
    rec = route[:, :8].T.astype(jnp.int32)
    experts = jnp.arange(N_EXPERTS, dtype=jnp.int32)[:, None]
    row_of = lambda e, rank: rank + jnp.sum(jnp.where(e[None, :] == experts, offs[:, None], 0), axis=0)
    row1 = row_of(rec[R_E1], rec[R_RANK1])
    row2 = row_of(rec[R_E2], rec[R_RANK2])
    max_tiles = (2 * T) // tme + N_EXPERTS
    tile_e = jnp.sum(jnp.arange(max_tiles, dtype=jnp.int32)[:, None] >= tile_end[None, :], axis=1)
    tile_e = jnp.minimum(tile_e, N_EXPERTS - 1).astype(jnp.int32)
    n_used = tile_end[-1:].astype(jnp.int32)
    n_rows = max_tiles * tme
    spare = n_used + jnp.arange(N_EXPERTS, dtype=jnp.int32)
    clear_rows = jnp.concatenate([last_tile_row, jnp.where(spare < max_tiles, spare * tme, -1)]).astype(jnp.int32)

    xs = pl.pallas_call(
        _dispatch_kernel,
        grid_spec=pltpu.PrefetchScalarGridSpec(
            num_scalar_prefetch=3,
            grid=(T // tmd,),
            in_specs=[pl.BlockSpec((tmd, D), lambda t, r1, r2, lt: (t, 0))],
            out_specs=pl.BlockSpec(memory_space=pl.ANY),
            scratch_shapes=[pltpu.VMEM((tme, D), F32), pltpu.SemaphoreType.DMA(()), pltpu.SemaphoreType.DMA(())],
        ),
        out_shape=jax.ShapeDtypeStruct((n_rows, D), F32),
        compiler_params=_cparams(("arbitrary",)),
        name="dispatch",
    )(row1, row2, clear_rows, x1)

    g2 = norm_ffn_g.astype(F32)[None, :]
    w_spec = lambda shape: pl.BlockSpec((1, 1) + shape, lambda k, te, nu: (layer, te[k], 0, 0))
    ys = pl.pallas_call(
        _experts_kernel,
        grid_spec=pltpu.PrefetchScalarGridSpec(
            num_scalar_prefetch=2,
            grid=(max_tiles,),
            in_specs=[pl.BlockSpec((tme, D), lambda k, te, nu: (jnp.minimum(k, nu[0] - 1), 0)),
                      pl.BlockSpec((1, D), lambda k, te, nu: (0, 0)),
                      w_spec((D, D_EXPERT)), w_spec((D, D_EXPERT)), w_spec((D_EXPERT, D))],
            out_specs=pl.BlockSpec((tme, D), lambda k, te, nu: (k, 0)),
            scratch_shapes=[pltpu.VMEM((D, D_EXPERT), BF16), pltpu.VMEM((D, D_EXPERT), BF16),
                            pltpu.VMEM((D_EXPERT, D), BF16)],
        ),
        out_shape=jax.ShapeDtypeStruct((n_rows, D), F32),
        compiler_params=_cparams(("arbitrary",)),
        name="experts",
    )(tile_e, n_used, xs, g2, w_gate, w_up, w_down)

    return pl.pallas_call(
        _combine_kernel,
        grid_spec=pltpu.PrefetchScalarGridSpec(
            num_scalar_prefetch=2,
            grid=(T // tmd,),
            in_specs=[pl.BlockSpec((tmd, D), lambda t, r1, r2: (t, 0)),
                      pl.BlockSpec((tmd, LANES), lambda t, r1, r2: (t, 0)),
                      pl.BlockSpec(memory_space=pl.ANY)],
            out_specs=pl.BlockSpec((1, tmd, D), lambda t, r1, r2: (t // (S // tmd), t % (S // tmd), 0)),
            scratch_shapes=[pltpu.VMEM((tmd, D), F32), pltpu.VMEM((tmd, D), F32), pltpu.SemaphoreType.DMA(())],
        ),
        out_shape=jax.ShapeDtypeStruct((B, S, D), F32),
        compiler_params=_cparams(("arbitrary",)),
        name="combine",
    )(row1, row2, x1, route, ys)


def kernel(x, norm_mix_g, w_in, b_forget, fox_q_g, fox_k_g, nsa_q_g, nsa_k_g, cmp_k_w1, cmp_k_w2, cmp_k_pos,
           cmp_v_w1, cmp_v_w2, cmp_v_pos, w_fox_up, w_nsa_up, w_out, norm_ffn_g, w_group, b_group, w_router,
           b_router, w_gate, w_up, w_down):
    B, S, D = x.shape
    for l in range(w_in.shape[0]):
        qa, ka, va, qn, kcr, vcr, ksa, vsa, kwa, vwa, gt, gab = _inproj(
            x, norm_mix_g[l], w_in[l], b_forget[l], fox_q_g[l], fox_k_g[l], nsa_q_g[l], nsa_k_g[l])
        kca, vca = _compress(kcr, vcr, cmp_k_w1[l], cmp_k_w2[l], cmp_k_pos[l],
                             cmp_v_w1[l], cmp_v_w2[l], cmp_v_pos[l], nsa_k_g[l])
        ob = _nsa(qn, kca, vca, ksa, vsa, kwa, vwa, gt, _scores_bounded(nsa_q_g[l], nsa_k_g[l]))
        oa = _fox(qa, ka, va, _scores_bounded(fox_q_g[l], fox_k_g[l]))
        x1, route, cnt = _merge(x, oa, ob, gab, w_fox_up[l], w_nsa_up[l], w_out[l], norm_ffn_g[l],
                                w_group[l], b_group[l], w_router[l], b_router[l])
        x = _moe(x1, route, cnt, norm_ffn_g[l], w_gate, w_up, w_down, l, B, S)
    return x
```

```python
import functools

import jax
import jax.numpy as jnp
from jax import lax
from jax.experimental import pallas as pl
from jax.experimental.pallas import tpu as pltpu

F32 = jnp.float32
BF16 = jnp.bfloat16

D_MODEL = 1024
HEAD_DIM = 64
FOX_HEADS = 8
NSA_HEADS = 8
NSA_KV_HEADS = 2
NSA_REP = NSA_HEADS // NSA_KV_HEADS
CMP_BLOCK = 32
CMP_STRIDE = 16
CMP_HIDDEN = 256
SLC_BLOCK = 64
SLC_TOPK = 16
WINDOW = 512
N_GROUPS = 4
EXPERTS_PER_GROUP = 4
N_EXPERTS = N_GROUPS * EXPERTS_PER_GROUP
D_EXPERT = 512
RMS_EPS = 1e-6
NEG_INF = -1e30
FORCE_SCORE = 1e4

LANES = 128
HALF = LANES // 2
VMEM_LIMIT = 56 * 1024 * 1024

FOX_W = FOX_HEADS * HEAD_DIM
NSA_W = NSA_HEADS * HEAD_DIM
NSA_KV_W = NSA_KV_HEADS * HEAD_DIM

OFF_FQ = 0
OFF_FK = OFF_FQ + FOX_W
OFF_FV = OFF_FK + FOX_W
OFF_NQ = OFF_FV + FOX_W
OFF_KC = OFF_NQ + NSA_W
OFF_VC = OFF_KC + NSA_KV_W
OFF_KS = OFF_VC + NSA_KV_W
OFF_VS = OFF_KS + NSA_KV_W
OFF_KW = OFF_VS + NSA_KV_W
OFF_VW = OFF_KW + NSA_KV_W
OFF_FF = OFF_VW + NSA_KV_W
OFF_NG = OFF_FF + LANES
OFF_GA = OFF_NG + LANES
N_PROJ = OFF_GA + 2 * D_MODEL

TM_PROJ = 512
TQ_FOX = 512
FOX_UNROLL = 4
TQ_NSA = 256
TK_SLC = 256
TM_MERGE = 512
TM_EXPERT = 256
TM_DISPATCH = 512


def _dot(a, b):
    return jnp.dot(a, b, preferred_element_type=F32)


def _dot_nt(a, b):
    return lax.dot_general(a, b, (((1,), (1,)), ((), ())), preferred_element_type=F32)


def _split3(x):
    hi = x.astype(BF16)
    r = x - hi.astype(F32)
    mid = r.astype(BF16)
    lo = (r - mid.astype(F32)).astype(BF16)
    return hi, mid, lo


def _cparams(sem):
    return pltpu.CompilerParams(dimension_semantics=sem, vmem_limit_bytes=VMEM_LIMIT)


def _const_spec(shape):
    nd = len(shape)
    return pl.BlockSpec(shape, lambda *_: (0,) * nd)


def _inproj_kernel(x_ref, g_ref, w_ref, bf_ref, gqa_ref, gka_ref, gqn_ref, gkn_ref,
                   bd_ref, bd2_ref, tri_ref, cq_ref, ck_ref, cv_ref,
                   qc0_ref, qca_ref, qcb_ref, kc0_ref, kca_ref, kcb_ref, vone_ref,
                   qa_ref, ka_ref, va_ref, qn_ref, kcr_ref, vcr_ref,
                   ksa_ref, vsa_ref, kwa_ref, vwa_ref, gt_ref, gab_ref,
                   carry_ref):
    tm = x_ref.shape[1]

    @pl.when(pl.program_id(1) == 0)
    def _():
        carry_ref[...] = jnp.zeros_like(carry_ref)

    x = x_ref[0]
    y = x * lax.rsqrt(jnp.mean(x * x, axis=-1, keepdims=True) + RMS_EPS)
    h = (y * g_ref[...]).astype(BF16)

    def proj(off, n):
        return _dot(h, w_ref[:, off:off + n])

    lo_half = lax.broadcasted_iota(jnp.int32, (tm, LANES), 1) < HALF

    pos = pl.program_id(1) * tm + lax.broadcasted_iota(jnp.int32, (tm, 1), 0)
    pos_a = ((pos >> 8) << 8).astype(F32)
    pos_b = (pos & 255).astype(F32)

    def pos_channels(c0_ref, ca_ref, cb_ref, k):
        blk = slice(LANES * k, LANES * (k + 1))
        return c0_ref[:, blk] + ca_ref[:, blk] * pos_a + cb_ref[:, blk] * pos_b

    def headnorm(z, bd, grow):
        msq = _dot((z * z).astype(BF16), bd[...])
        return z * lax.rsqrt(msq + RMS_EPS) * grow[...]

    def spread_pairs(out_ref, src, aug):
        for m in range(4):
            s = src[:, LANES * m:LANES * (m + 1)]
            out_ref[0, :, LANES * 2 * m:LANES * (2 * m + 1)] = jnp.where(lo_half, s, aug(2 * m)).astype(BF16)
            out_ref[0, :, LANES * (2 * m + 1):LANES * (2 * m + 2)] = jnp.where(lo_half, aug(2 * m + 1), s).astype(BF16)

    zf = proj(OFF_FF, LANES) + bf_ref[...]
    logf = jnp.minimum(zf, 0.0) - jnp.log(1.0 + jnp.exp(-jnp.abs(zf)))
    l_hi, l_mid, l_lo = _split3(logf)
    tri = tri_ref[...]
    cum = carry_ref[...] + (_dot(tri, l_hi) + _dot(tri, l_mid) + _dot(tri, l_lo))
    carry_ref[...] = cum[tm - 1:tm, :]
    pieces = [p.astype(F32) for p in _split3(cum)]
    lane128 = lax.broadcasted_iota(jnp.int32, (tm, LANES), 1)

    def aug_block(k, const_ref, first, sign):
        base = HALF if k % 2 == 0 else 0
        blk = jnp.broadcast_to(const_ref[:, LANES * k:LANES * (k + 1)], (tm, LANES))
        for j, piece in enumerate(pieces):
            blk = jnp.where(lane128 == base + first + j, sign * piece[:, k:k + 1], blk)
        return blk

    augq = lambda k: aug_block(k, cq_ref, 3, 1.0)
    augk = lambda k: aug_block(k, ck_ref, 0, -1.0)

    zq = headnorm(proj(OFF_FQ, FOX_W), bd_ref, gqa_ref)
    spread_pairs(qa_ref, zq, augq)
    zk = headnorm(proj(OFF_FK, FOX_W), bd_ref, gka_ref)
    spread_pairs(ka_ref, zk, augk)
    zv = proj(OFF_FV, FOX_W)
    spread_pairs(va_ref, zv, lambda k: cv_ref[:, LANES * k:LANES * (k + 1)])

    zn = headnorm(proj(OFF_NQ, NSA_W), bd_ref, gqn_ref)
    for m in range(NSA_REP):
        s = zn[:, LANES * m:LANES * (m + 1)]
        c0 = pos_channels(qc0_ref, qca_ref, qcb_ref, m)
        c1 = pos_channels(qc0_ref, qca_ref, qcb_ref, NSA_REP + m)
        qn_ref[0, :, LANES * m:LANES * (m + 1)] = jnp.where(lo_half, s, c0).astype(BF16)
        qn_ref[0, :, LANES * (NSA_REP + m):LANES * (NSA_REP + m + 1)] = jnp.where(lo_half, c1, s).astype(BF16)

    kcr_ref[0] = proj(OFF_KC, NSA_KV_W)
    vcr_ref[0] = proj(OFF_VC, NSA_KV_W)

    kp0 = pos_channels(kc0_ref, kca_ref, kcb_ref, 0)
    kp1 = pos_channels(kc0_ref, kca_ref, kcb_ref, 1)

    def kv_pair(k_out, v_out, off_k, off_v):
        zk2 = headnorm(proj(off_k, NSA_KV_W), bd2_ref, gkn_ref)
        k_out[0, :, 0:LANES] = jnp.where(lo_half, zk2, kp0).astype(BF16)
        k_out[0, :, LANES:2 * LANES] = jnp.where(lo_half, kp1, zk2).astype(BF16)
        zv2 = proj(off_v, NSA_KV_W)
        v_out[0, :, 0:LANES] = jnp.where(lo_half, zv2, vone_ref[:, 0:LANES]).astype(BF16)
        v_out[0, :, LANES:2 * LANES] = jnp.where(lo_half, vone_ref[:, LANES:2 * LANES], zv2).astype(BF16)

    kv_pair(ksa_ref, vsa_ref, OFF_KS, OFF_VS)
    kv_pair(kwa_ref, vwa_ref, OFF_KW, OFF_VW)

    gt_ref[0] = jax.nn.sigmoid(proj(OFF_NG, LANES))
    gab_ref[0, :, 0:D_MODEL] = jax.nn.sigmoid(proj(OFF_GA, D_MODEL)).astype(BF16)
    gab_ref[0, :, D_MODEL:2 * D_MODEL] = jax.nn.sigmoid(proj(OFF_GA + D_MODEL, D_MODEL)).astype(BF16)


def _pos_pieces(pos):
    return ((pos // 256) * 256).astype(F32), (pos % 256).astype(F32)


def _inproj(x, norm_g, w_in, b_forget, fox_q_g, fox_k_g, nsa_q_g, nsa_k_g):
    B, S, D = x.shape
    tm = min(TM_PROJ, S)
    scale = HEAD_DIM ** -0.5

    c = [0]
    for n in (FOX_W, FOX_W, FOX_W, FOX_HEADS, NSA_W) + (NSA_KV_W,) * 6 + (3 * NSA_HEADS, D_MODEL, D_MODEL):
        c.append(c[-1] + n)
    fq, fk, fv, ff, nq, kc, vc, ks, vs, kw, vw, ng, ga, gb = [w_in[:, c[i]:c[i + 1]] for i in range(14)]
    perm = jnp.asarray([0, 4, 1, 5, 2, 6, 3, 7])
    nq = nq.reshape(D, NSA_HEADS, HEAD_DIM)[:, perm, :].reshape(D, NSA_W)
    padl = lambda a: jnp.pad(a, ((0, 0), (0, LANES - a.shape[1])))
    w = jnp.concatenate([fq, fk, fv, nq, kc, vc, ks, vs, kw, vw, padl(ff), padl(ng), ga, gb], axis=1).astype(BF16)
    assert w.shape[1] == N_PROJ

    bf = jnp.pad(b_forget.astype(F32), (0, LANES - FOX_HEADS))[None, :]
    gqa = jnp.tile(fox_q_g.astype(F32) * scale, FOX_HEADS)[None, :]
    gka = jnp.tile(fox_k_g.astype(F32), FOX_HEADS)[None, :]
    gqn = jnp.tile(nsa_q_g.astype(F32) * scale, NSA_HEADS)[None, :]
    gkn = jnp.tile(nsa_k_g.astype(F32), NSA_KV_HEADS)[None, :]

    r512 = jnp.arange(FOX_W)
    bd = jnp.where((r512[:, None] // HEAD_DIM) == (r512[None, :] // HEAD_DIM), 1.0 / HEAD_DIM, 0.0).astype(BF16)
    bd2 = bd[:LANES, :LANES]
    rt = jnp.arange(tm)
    tri = (rt[None, :] <= rt[:, None]).astype(BF16)

    heads = jnp.arange(FOX_HEADS)
    base = heads * LANES + jnp.where(heads % 2 == 0, HALF, 0)
    cols = jnp.arange(FOX_HEADS * LANES)
    off_in_blk = cols - base[cols // LANES]
    cq = ((off_in_blk >= 0) & (off_in_blk < 3)).astype(F32)[None, :]
    ck = ((off_in_blk >= 3) & (off_in_blk < 9)).astype(F32)[None, :]
    cv = (off_in_blk == 0).astype(F32)[None, :]

    blk = cols // LANES
    slope = 2.0 ** (-(blk + 1).astype(F32))
    o = cols % LANES - jnp.where(blk // NSA_REP == 0, HALF, 0)
    qc0 = jnp.where((o == 0) | (o == 1), slope, 0.0)[None, :]
    qca = jnp.where(o == 2, -slope, 0.0)[None, :]
    qcb = jnp.where(o == 3, -slope, 0.0)[None, :]
    kc0, kca, kcb, vone = _kv_rows()

    grid = (B, S // tm)
    row_spec = lambda n: pl.BlockSpec((1, tm, n), lambda b, s: (b, s, 0))
    consts = [norm_g.astype(F32)[None, :], w, bf, gqa, gka, gqn, gkn, bd, bd2, tri, cq, ck, cv,
              qc0, qca, qcb, kc0, kca, kcb, vone]
    out_widths = [(8 * LANES, BF16)] * 4 + [(LANES, F32)] * 2 + [(2 * LANES, BF16)] * 4 + \
                 [(LANES, F32), (2 * D_MODEL, BF16)]
    outs = pl.pallas_call(
        _inproj_kernel,
        grid=grid,
        in_specs=[row_spec(D)] + [_const_spec(a.shape) for a in consts],
        out_specs=[row_spec(n) for n, _ in out_widths],
        out_shape=[jax.ShapeDtypeStruct((B, S, n), dt) for n, dt in out_widths],
        scratch_shapes=[pltpu.VMEM((1, LANES), F32)],
        compiler_params=_cparams(("arbitrary", "arbitrary")),
        name="inproj",
    )(x, *consts)
    return outs


def _kv_rows():
    cols = jnp.arange(2 * LANES)
    o = cols % LANES - jnp.where(cols // LANES == 0, HALF, 0)
    row = lambda m: m.astype(F32)[None, :]
    return row((o >= 2) & (o <= 6)), row(o == 0), row(o == 1), row(o == 0)


def _kv_consts(pa, pb):
    c0, ca, cb, vone = _kv_rows()
    return (c0 + ca * pa[:, None] + cb * pb[:, None]).astype(BF16), vone


def _compress_kernel(kt_ref, vt_ref, w1k_ref, w1v_ref, posk_ref, posv_ref, pw1k_ref, pw1v_ref, w2k_ref, w2v_ref,
                     gk_ref, kcc_ref, vone_ref, kc_ref, vc_ref):
    nc = kc_ref.shape[2]

    def mlp(t_ref, w1_ref, pos_ref, pw1_ref, w2_ref):
        both = jnp.zeros((nc, 2 * CMP_HIDDEN), F32)
        for l in range(CMP_STRIDE):
            rows = t_ref[0, pl.ds(l, nc, stride=CMP_STRIDE), :].astype(BF16)
            both = both + _dot(rows, w1_ref[0, l])
        posw = _dot(pos_ref[...], pw1_ref[...])[0:1, :]
        pre = both[:, 0:CMP_HIDDEN] + pltpu.roll(both[:, CMP_HIDDEN:2 * CMP_HIDDEN], nc - 1, axis=0) + posw
        act = pre * (0.5 * (1.0 + jnp.tanh(0.7978845608028654 * (pre + 0.044715 * (pre * pre * pre)))))
        return _dot(act.astype(BF16), w2_ref[0])

    kc = mlp(kt_ref, w1k_ref, posk_ref, pw1k_ref, w2k_ref)
    msq = jnp.sum(kc * kc, axis=-1, keepdims=True) * (1.0 / HEAD_DIM)
    kc_ref[0, 0] = (kc * lax.rsqrt(msq + RMS_EPS) * gk_ref[0] + kcc_ref[0]).astype(BF16)
    vc = mlp(vt_ref, w1v_ref, posv_ref, pw1v_ref, w2v_ref)
    vc_ref[0, 0] = (vc + vone_ref[0]).astype(BF16)


def _compress(kcr, vcr, cmp_k_w1, cmp_k_w2, cmp_k_pos, cmp_v_w1, cmp_v_w2, cmp_v_pos, nsa_k_g):
    B, S, _ = kcr.shape
    G = NSA_KV_HEADS
    nc = S // CMP_STRIDE

    def w1_strided(w1):
        w = w1.reshape(2, CMP_STRIDE, HEAD_DIM, CMP_HIDDEN)
        w = jnp.concatenate([w[0], w[1]], axis=-1)
        z = jnp.zeros_like(w)
        return jnp.stack([jnp.concatenate([w, z], axis=1), jnp.concatenate([z, w], axis=1)]).astype(BF16)

    def w2_spread(w2):
        z = jnp.zeros_like(w2)
        return jnp.stack([jnp.concatenate([w2, z], 1), jnp.concatenate([z, w2], 1)]).astype(BF16)

    def pos8(p):
        return jnp.tile(p.reshape(1, CMP_BLOCK * HEAD_DIM), (8, 1)).astype(BF16)

    gk = nsa_k_g.astype(F32)
    z = jnp.zeros_like(gk)
    gk2 = jnp.stack([jnp.concatenate([gk, z]), jnp.concatenate([z, gk])])[:, None, :]
    cend = jnp.arange(nc) * CMP_STRIDE + CMP_BLOCK - 1
    kcc, vone = _kv_consts(*_pos_pieces(cend))
    kcc = kcc.astype(F32).reshape(nc, G, LANES).transpose(1, 0, 2)
    vone = vone.reshape(G, 1, LANES)

    tok = pl.BlockSpec((1, S, G * HEAD_DIM), lambda b, g: (b, 0, 0))
    per_g = lambda a: pl.BlockSpec((1,) + a.shape[1:], lambda b, g: (g,) + (0,) * (a.ndim - 1))
    w1k, w1v = w1_strided(cmp_k_w1), w1_strided(cmp_v_w1)
    pk, pv = pos8(cmp_k_pos), pos8(cmp_v_pos)
    pw1k, pw1v = cmp_k_w1.astype(BF16), cmp_v_w1.astype(BF16)
    w2k, w2v = w2_spread(cmp_k_w2), w2_spread(cmp_v_w2)
    out_spec = pl.BlockSpec((1, 1, nc, LANES), lambda b, g: (b, g, 0, 0))
    return pl.pallas_call(
        _compress_kernel,
        grid=(B, G),
        in_specs=[tok, tok, per_g(w1k), per_g(w1v), _const_spec(pk.shape), _const_spec(pv.shape),
                  _const_spec(pw1k.shape), _const_spec(pw1v.shape), per_g(w2k), per_g(w2v),
                  per_g(gk2), per_g(kcc), per_g(vone)],
        out_specs=[out_spec, out_spec],
        out_shape=[jax.ShapeDtypeStruct((B, G, nc, LANES), BF16)] * 2,
        compiler_params=_cparams(("arbitrary", "arbitrary")),
        name="compress",
    )(kcr, vcr, w1k, w1v, pk, pv, pw1k, pw1v, w2k, w2v, gk2, kcc, vone)


MAX_EXPONENT = 60.0


def _scores_bounded(q_g, k_g):
    bound = HEAD_DIM ** 0.5 * jnp.max(jnp.abs(q_g)) * jnp.max(jnp.abs(k_g))
    return 2.04 * bound + 0.05 <= MAX_EXPONENT


def _attn_first(s, v):
    m = jnp.max(s, axis=-1, keepdims=True)
    p = jnp.exp((s - m).astype(BF16))
    return m, _dot(p, v)


def _attn_step(carry, s, v):
    m, acc = carry
    m_new = jnp.maximum(m, jnp.max(s, axis=-1, keepdims=True))
    p = jnp.exp((s - m_new).astype(BF16))
    return m_new, jnp.exp(m - m_new) * acc + _dot(p, v)


def _nsa_kernel(bounded, q_ref, kc_ref, vc_ref, ks_ref, e_ref, vs_ref, kw_ref, vw_ref, gt_ref, ov_ref, o_ref,
                tiles_ref, sa_ref, sb_ref):
    tq = q_ref.shape[1]
    nc = kc_ref.shape[2]
    tk = TK_SLC
    rq = NSA_REP * tq
    g = pl.program_id(1)
    i = pl.program_id(2)
    q0 = i * tq

    q4 = jnp.concatenate([q_ref[0, :, LANES * r:LANES * (r + 1)] for r in range(NSA_REP)], axis=0)

    def qpos_of(shape):
        return q0 + (lax.broadcasted_iota(jnp.int32, shape, 0) & (tq - 1))

    def add_mask(s, valid):
        bias = jnp.where(valid, 0.0, NEG_INF)
        return (s.reshape(NSA_REP, tq, s.shape[1]) + bias[None]).reshape(s.shape)

    def qrow(n):
        return q0 + lax.broadcasted_iota(jnp.int32, (tq, n), 0)

    def kcol(n):
        return lax.broadcasted_iota(jnp.int32, (tq, n), 1)

    anyv = qpos_of((rq, 1)) >= CMP_BLOCK - 1

    def cmp_branch(n):
        s = add_mask(_dot_nt(q4, kc_ref[0, 0, 0:n, :]), qrow(n) >= kcol(n) * CMP_STRIDE + (CMP_BLOCK - 1))
        if bounded:
            e = jnp.exp(s)
            p = e * jnp.where(anyv, 1.0 / jnp.sum(e, axis=-1, keepdims=True), 0.0)
        else:
            m = jnp.max(s, axis=-1, keepdims=True)
            e = jnp.exp(s - m)
            p = e * (anyv.astype(F32) / jnp.sum(e, axis=-1, keepdims=True))
        o = _dot(p.astype(BF16), vc_ref[0, 0, 0:n, :])
        psum = p[0:tq] + p[tq:2 * tq] + p[2 * tq:3 * tq] + p[3 * tq:4 * tq]
        ov = ov_ref[0:n, :]
        return o, functools.reduce(lambda a, b: a + b, [_dot(piece, ov) for piece in _split3(psum)])

    last_visible = (q0 + tq - CMP_BLOCK) // CMP_STRIDE
    o_cmp, imp = lax.cond(last_visible < nc // 2, lambda: cmp_branch(nc // 2), lambda: cmp_branch(nc))

    blk_i = lax.broadcasted_iota(jnp.int32, (LANES, tq), 0)
    blk_f = blk_i.astype(F32)
    qblk = (q0 + lax.broadcasted_iota(jnp.int32, (LANES, tq), 1)) // SLC_BLOCK
    forced = (blk_i == 0) | (blk_i == qblk) | (blk_i == qblk - 1)
    score = jnp.where(blk_i > qblk, -1.0, jnp.where(forced, FORCE_SCORE, imp.T))
    selb = jnp.full((LANES, tq), NEG_INF, F32)
    for _ in range(SLC_TOPK):
        mx = jnp.max(score, axis=0, keepdims=True)
        first = jnp.min(jnp.where(score == mx, blk_f, float(LANES)), axis=0, keepdims=True)
        hit = blk_f == first
        selb = jnp.where(hit, 0.0, selb)
        score = jnp.where(hit, -3e38, score)
    sb = selb.T.astype(BF16)
    sb4 = jnp.concatenate([sb] * NSA_REP, axis=0)

    def slc_qk(j):
        start = pl.multiple_of(j * tk, tk)
        kk = jnp.concatenate([ks_ref[0, pl.ds(start, tk), :], e_ref[pl.ds(start, tk), :]], axis=1)
        return _dot_nt(q2, kk)

    def slc_v(j):
        return vs_ref[0, pl.ds(pl.multiple_of(j * tk, tk), tk), :]

    jd = q0 // tk
    blocks_per_tile = tk // SLC_BLOCK
    anysel = jnp.max(selb, axis=1, keepdims=True)
    cnt = jnp.int32(0)
    for j in range(ks_ref.shape[1] // tk):
        tile_sel = jnp.max(anysel[blocks_per_tile * j:blocks_per_tile * (j + 1), :]) > -1.0
        tiles_ref[cnt] = j
        cnt = cnt + (tile_sel & (j < jd)).astype(jnp.int32)

    lane_r = lax.broadcasted_iota(jnp.int32, (rq, LANES), 1)
    data0 = HALF * g
    ones_lane = HALF - data0
    nw = WINDOW + tq
    wstart = pl.multiple_of(jnp.maximum(q0 - WINDOW, 0), tq)
    win_valid = lax.bitcast_convert_type(qrow(nw) - (wstart + kcol(nw)), jnp.uint32) < WINDOW
    diag_valid = jd * tk + kcol(tk) <= qrow(tk)

    if bounded:
        def shifted_by_own_score(kself):
            own = jnp.sum(q4.astype(F32).reshape(NSA_REP, tq, LANES) * kself.astype(F32)[None],
                          axis=-1, keepdims=True).reshape(rq, 1)
            qf = q4.astype(F32)
            for off, piece in enumerate(_split3(-own)):
                qf = jnp.where(lane_r == ones_lane + 4 + off, piece.astype(F32), qf)
            return qf.astype(BF16)

        q2 = jnp.concatenate([shifted_by_own_score(ks_ref[0, pl.ds(q0, tq), :]), sb4], axis=1)
        sa_ref[:, 0:LANES] = _dot(jnp.exp(add_mask(slc_qk(jd), diag_valid)).astype(BF16), slc_v(jd))

        def slc_pv(t):
            j = tiles_ref[t]
            return _dot(jnp.exp(slc_qk(j)).astype(BF16), slc_v(j))

        def slc_body(p, _):
            sa_ref[:, 0:LANES] += slc_pv(2 * p) + slc_pv(2 * p + 1)
            return _

        lax.fori_loop(0, cnt // 2, slc_body, 0)

        @pl.when(cnt % 2 == 1)
        def _():
            sa_ref[:, 0:LANES] += slc_pv(cnt - 1)

        acc_slc = sa_ref[:, 0:LANES]

        qw = shifted_by_own_score(kw_ref[0, pl.ds(q0, tq), :])
        s = add_mask(_dot_nt(qw, kw_ref[0, pl.ds(wstart, nw), :]), win_valid)
        acc_win = _dot(jnp.exp(s).astype(BF16), vw_ref[0, pl.ds(wstart, nw), :])
    else:
        q2 = jnp.concatenate([q4, sb4], axis=1)
        n_seq = cnt + 1

        def tile_of(n):
            return jnp.where(n == 0, jd, tiles_ref[jnp.maximum(n - 1, 0)])

        def slc_step(carry, s_buf, n):
            return _attn_step(carry, s_buf[...], slc_v(tile_of(n)))

        sa_ref[...] = add_mask(slc_qk(jd), diag_valid)

        def slc_body(p, carry):
            sb_ref[...] = slc_qk(tile_of(2 * p + 1))
            carry = slc_step(carry, sa_ref, 2 * p)
            sa_ref[...] = slc_qk(tile_of(jnp.minimum(2 * p + 2, n_seq - 1)))
            return slc_step(carry, sb_ref, 2 * p + 1)

        carry = (jnp.full((rq, 1), NEG_INF, F32), jnp.zeros((rq, LANES), F32))
        carry = lax.fori_loop(0, n_seq // 2, slc_body, carry)
        _, acc_slc = lax.cond(n_seq % 2 == 1, lambda c: slc_step(c, sa_ref, n_seq - 1), lambda c: c, carry)

        s = add_mask(_dot_nt(q4, kw_ref[0, pl.ds(wstart, nw), :]), win_valid)
        _, acc_win = _attn_first(s, vw_ref[0, pl.ds(wstart, nw), :])


    def normalise(acc):
        l = jnp.sum(jnp.where(lane_r == ones_lane, acc, 0.0), axis=-1, keepdims=True)
        return acc * (1.0 / l)

    o_slc = normalise(acc_slc)
    o_win = normalise(acc_win)
    gt = gt_ref[0]
    lane_g = lax.broadcasted_iota(jnp.int32, (tq, LANES), 1)
    is_data_q = (lane_g >= data0) & (lane_g < data0 + HALF)
    for r in range(NSA_REP):
        col = 3 * (NSA_REP * g + r)
        gate = [jnp.sum(jnp.where(lane_g == col + b, gt, 0.0), axis=-1, keepdims=True) for b in range(3)]
        rows = slice(r * tq, (r + 1) * tq)
        o = gate[0] * o_cmp[rows] + gate[1] * o_slc[rows] + gate[2] * o_win[rows]
        o_ref[0, :, LANES * r:LANES * (r + 1)] = jnp.where(is_data_q, o, 0.0).astype(BF16)


def _nsa(qn, kca, vca, ksa, vsa, kwa, vwa, gt, scores_bounded):
    B, S, _ = qn.shape
    G = NSA_KV_HEADS
    tq = min(TQ_NSA, S)
    nc = S // CMP_STRIDE
    n_slc = S // SLC_BLOCK
    assert n_slc <= LANES and S % TK_SLC == 0

    cs = jnp.arange(nc)[:, None] * CMP_STRIDE
    ss = jnp.arange(LANES)[None, :] * SLC_BLOCK
    ovl = jnp.clip(jnp.minimum(cs + CMP_BLOCK, ss + SLC_BLOCK) - jnp.maximum(cs, ss), 0, None)
    valid = (jnp.arange(nc)[:, None] < (S - CMP_BLOCK) // CMP_STRIDE + 1) & (jnp.arange(LANES)[None, :] < n_slc)
    ov = jnp.where(valid, ovl.astype(F32) / CMP_BLOCK, 0.0).astype(BF16)
    e1h = (jnp.arange(S)[:, None] // SLC_BLOCK == jnp.arange(LANES)[None, :]).astype(BF16)

    q_spec = pl.BlockSpec((1, tq, NSA_REP * LANES), lambda b, g, i: (b, i, g))
    c_spec = pl.BlockSpec((1, 1, nc, LANES), lambda b, g, i: (b, g, 0, 0))
    kv_spec = pl.BlockSpec((1, S, LANES), lambda b, g, i: (b, 0, g))

    def call(bounded, name):
        return pl.pallas_call(
            functools.partial(_nsa_kernel, bounded),
            grid=(B, G, S // tq),
            in_specs=[q_spec, c_spec, c_spec, kv_spec, _const_spec(e1h.shape), kv_spec, kv_spec, kv_spec,
                      pl.BlockSpec((1, tq, LANES), lambda b, g, i: (b, i, 0)), _const_spec(ov.shape)],
            out_specs=q_spec,
            out_shape=jax.ShapeDtypeStruct((B, S, NSA_HEADS * LANES), BF16),
            scratch_shapes=[pltpu.SMEM((S // TK_SLC + 1,), jnp.int32),
                            pltpu.VMEM((NSA_REP * tq, TK_SLC), F32), pltpu.VMEM((NSA_REP * tq, TK_SLC), F32)],
            compiler_params=_cparams(("arbitrary", "arbitrary", "arbitrary")),
            name=name,
        )

    return lax.cond(scores_bounded, call(True, "nsa_bounded"), call(False, "nsa"),
                    qn, kca, vca, ksa, e1h, vsa, kwa, vwa, gt, ov)


def _fox_kernel(q_ref, k_ref, v_ref, o_ref, sa_ref, sb_ref):
    tq = q_ref.shape[1]
    i = pl.program_id(2)
    lane = lax.broadcasted_iota(jnp.int32, (tq, LANES), 1)
    causal = lax.broadcasted_iota(jnp.int32, (tq, tq), 1) <= lax.broadcasted_iota(jnp.int32, (tq, tq), 0)

    def cols(hh):
        return slice(LANES * hh, LANES * (hh + 1))

    def qk(hh, j):
        start = pl.multiple_of(j * tq, tq)
        return _dot_nt(q_ref[0, :, cols(hh)], k_ref[0, pl.ds(start, tq), cols(hh)])

    def vtile(hh, j):
        return v_ref[0, pl.ds(pl.multiple_of(j * tq, tq), tq), cols(hh)]

    n_tiles = i + 1

    def tile_of(n):
        return jnp.where(n == 0, i, n - 1)

    def step_all(carry, s_buf, n):
        t = tile_of(n)
        return tuple(_attn_step(carry[hh], s_buf[hh], vtile(hh, t)) for hh in range(2))

    for hh in range(2):
        sa_ref[hh] = jnp.where(causal, qk(hh, i), NEG_INF)
    init = tuple((jnp.full((tq, 1), NEG_INF, F32), jnp.zeros((tq, LANES), F32)) for hh in range(2))

    def body(p, carry):
        for hh in range(2):
            sb_ref[hh] = qk(hh, tile_of(2 * p + 1))
        carry = step_all(carry, sa_ref, 2 * p)
        nxt = jnp.minimum(2 * p + 2, n_tiles - 1)
        for hh in range(2):
            sa_ref[hh] = qk(hh, tile_of(nxt))
        return step_all(carry, sb_ref, 2 * p + 1)

    carry = lax.fori_loop(0, n_tiles // 2, body, init)
    carry = lax.cond(n_tiles % 2 == 1, lambda c: step_all(c, sa_ref, n_tiles - 1), lambda c: c, carry)
    outs = []
    for hh in range(2):
        acc = carry[hh][1]
        ones_lane = HALF if hh == 0 else 0
        l = jnp.sum(jnp.where(lane == ones_lane, acc, 0.0), axis=-1, keepdims=True)
        outs.append(acc * (1.0 / l))
    o_ref[0] = jnp.where(lane < HALF, outs[0], outs[1]).astype(BF16)


def _fox_bounded_kernel(q_ref, k_ref, v_ref, o_ref, qs_ref, acc_ref):
    tq = q_ref.shape[1]
    i = pl.program_id(2)
    lane = lax.broadcasted_iota(jnp.int32, (tq, LANES), 1)
    causal = lax.broadcasted_iota(jnp.int32, (tq, tq), 1) <= lax.broadcasted_iota(jnp.int32, (tq, tq), 0)

    def cols(hh):
        return slice(LANES * hh, LANES * (hh + 1))

    def ktile(hh, j):
        return k_ref[0, pl.ds(pl.multiple_of(j * tq, tq), tq), cols(hh)]

    def vtile(hh, j):
        return v_ref[0, pl.ds(pl.multiple_of(j * tq, tq), tq), cols(hh)]

    for hh in range(2):
        q = q_ref[0, :, cols(hh)]
        s = jnp.where(causal, _dot_nt(q, ktile(hh, i)), NEG_INF)
        m = jnp.max(s, axis=-1, keepdims=True)
        acc_ref[hh] = _dot(jnp.exp((s - m).astype(BF16)), vtile(hh, i))
        qf = q.astype(F32)
        free0 = (HALF if hh == 0 else 0) + 6
        for off, piece in enumerate(_split3(-m)):
            qf = jnp.where(lane == free0 + off, piece.astype(F32), qf)
        qs_ref[hh] = qf.astype(BF16)

    def sweep(j0, n):
        for hh in range(2):
            pv = [_dot(jnp.exp(_dot_nt(qs_ref[hh], ktile(hh, j0 + t)).astype(BF16)), vtile(hh, j0 + t))
                  for t in range(n)]
            acc_ref[hh] += functools.reduce(lambda a, b: a + b, pv)

    def body(c, _):
        sweep(c * FOX_UNROLL, FOX_UNROLL)
        return _

    lax.fori_loop(0, i // FOX_UNROLL, body, 0)
    done = (i // FOX_UNROLL) * FOX_UNROLL
    n = FOX_UNROLL // 2
    while n >= 1:
        @pl.when((i & n) != 0)
        def _(n=n, start=done):
            sweep(start, n)
        done = done + (i & n)
        n //= 2
    outs = []
    for hh in range(2):
        acc = acc_ref[hh]
        ones_lane = HALF if hh == 0 else 0
        l = jnp.sum(jnp.where(lane == ones_lane, acc, 0.0), axis=-1, keepdims=True)
        outs.append(acc * (1.0 / l))
    o_ref[0] = jnp.where(lane < HALF, outs[0], outs[1]).astype(BF16)


def _fox(qa, ka, va, scores_bounded):
    B, S, _ = qa.shape
    tq = min(TQ_FOX, S)
    q_spec = pl.BlockSpec((1, tq, 2 * LANES), lambda b, h, i: (b, i, h))
    kv_spec = pl.BlockSpec((1, S, 2 * LANES), lambda b, h, i: (b, 0, h))

    def call(body, scratch, name):
        return pl.pallas_call(
            body,
            grid=(B, FOX_HEADS // 2, S // tq),
            in_specs=[q_spec, kv_spec, kv_spec],
            out_specs=pl.BlockSpec((1, tq, LANES), lambda b, h, i: (b, i, h)),
            out_shape=jax.ShapeDtypeStruct((B, S, FOX_W), BF16),
            scratch_shapes=scratch,
            compiler_params=_cparams(("arbitrary", "arbitrary", "arbitrary")),
            name=name,
        )

    general = call(_fox_kernel, [pltpu.VMEM((2, tq, tq), F32), pltpu.VMEM((2, tq, tq), F32)], "fox")
    bounded = call(_fox_bounded_kernel, [pltpu.VMEM((2, tq, LANES), BF16), pltpu.VMEM((2, tq, LANES), F32)],
                   "fox_bounded")
    return lax.cond(scores_bounded, bounded, general, qa, ka, va)


R_E1, R_E2, R_W1, R_W2, R_RANK1, R_RANK2 = range(6)


def _merge_kernel(x_ref, oa_ref, ob_ref, gab_ref, wa_ref, wb_ref, wo_ref, g2_ref, wr_hi_ref, wr_lo_ref, br_ref,
                  ltri_ref, x1_ref, route_ref, cnt_ref, carry_ref):
    tm = x_ref.shape[1]

    @pl.when((pl.program_id(0) == 0) & (pl.program_id(1) == 0))
    def _():
        carry_ref[...] = jnp.zeros_like(carry_ref)

    out_a = _dot(oa_ref[0], wa_ref[...])
    out_b = _dot(ob_ref[0], wb_ref[...])
    mix = gab_ref[0, :, 0:D_MODEL].astype(F32) * out_a + gab_ref[0, :, D_MODEL:2 * D_MODEL].astype(F32) * out_b
    x1 = x_ref[0] + _dot(mix.astype(BF16), wo_ref[...])
    _to_tile_rows(x1_ref, x1)
    h2 = x1 * lax.rsqrt(jnp.mean(x1 * x1, axis=-1, keepdims=True) + RMS_EPS) * g2_ref[...]

    h_hi = h2.astype(BF16)
    h_lo = (h2 - h_hi.astype(F32)).astype(BF16)
    logits = _dot(h_hi, wr_hi_ref[...]) + (_dot(h_hi, wr_lo_ref[...]) + _dot(h_lo, wr_hi_ref[...])) + br_ref[...]
    lane = lax.broadcasted_iota(jnp.int32, (tm, LANES), 1)
    lane_f = lane.astype(F32)

    def first_argmax(vals):
        mx = jnp.max(vals, axis=-1, keepdims=True)
        idx = jnp.min(jnp.where(vals == mx, lane_f, float(LANES)), axis=-1, keepdims=True)
        return mx, idx

    is_grp = (lane >= N_EXPERTS) & (lane < N_EXPERTS + N_GROUPS)
    gl = jnp.where(is_grp, logits, NEG_INF)
    gmax, gidx = first_argmax(gl)
    p_g = 1.0 / jnp.sum(jnp.where(is_grp, jnp.exp(gl - gmax), 0.0), axis=-1, keepdims=True)
    e_lo = (gidx - float(N_EXPERTS)) * float(EXPERTS_PER_GROUP)
    in_grp = (lane_f >= e_lo) & (lane_f < e_lo + float(EXPERTS_PER_GROUP))
    el = jnp.where(in_grp, logits, NEG_INF)
    m1, i1 = first_argmax(el)
    m2, i2 = first_argmax(jnp.where(lane_f == i1, NEG_INF, el))
    e2 = jnp.exp(m2 - m1)
    w1 = p_g / (1.0 + e2)
    w2 = p_g * e2 / (1.0 + e2)

    hit1 = lane_f == i1
    hit2 = lane_f == i2
    onehot = jnp.where(hit1 | hit2, 1.0, 0.0)
    before = carry_ref[...] + _dot(ltri_ref[...], onehot.astype(BF16))
    rank1 = jnp.sum(jnp.where(hit1, before, 0.0), axis=-1, keepdims=True)
    rank2 = jnp.sum(jnp.where(hit2, before, 0.0), axis=-1, keepdims=True)
    total = carry_ref[...] + jnp.sum(onehot, axis=0, keepdims=True)
    carry_ref[...] = total
    cnt_ref[...] = total

    rec = jnp.zeros((tm, LANES), F32)
    for k, val in ((R_E1, i1), (R_E2, i2), (R_W1, w1), (R_W2, w2), (R_RANK1, rank1), (R_RANK2, rank2)):
        rec = jnp.where(lane == k, val, rec)
    route_ref[...] = rec


def _merge(x, oa, ob, gab, w_fox_up, w_nsa_up, w_out, norm_ffn_g, w_group, b_group, w_router, b_router):
    B, S, D = x.shape
    tm = min(TM_MERGE, S)
    wa = w_fox_up.astype(BF16)
    wn = w_nsa_up.reshape(NSA_KV_HEADS, NSA_REP, HEAD_DIM, D)
    z = jnp.zeros_like(wn[0])
    wb = jnp.stack([jnp.concatenate([wn[0], z], axis=1), jnp.concatenate([z, wn[1]], axis=1)])
    wb = wb.reshape(NSA_HEADS * LANES, D).astype(BF16)
    wo = w_out.astype(BF16)
    wr = jnp.pad(jnp.concatenate([w_router, w_group], axis=1).astype(F32),
                 ((0, 0), (0, LANES - N_EXPERTS - N_GROUPS)))
    wr_hi = wr.astype(BF16)
    wr_lo = (wr - wr_hi.astype(F32)).astype(BF16)
    br = jnp.pad(jnp.concatenate([b_router, b_group]).astype(F32), (0, LANES - N_EXPERTS - N_GROUPS))[None, :]
    g2 = norm_ffn_g.astype(F32)[None, :]
    rt = jnp.arange(tm)
    ltri = (rt[None, :] < rt[:, None]).astype(BF16)

    row = lambda n: pl.BlockSpec((1, tm, n), lambda b, s: (b, s, 0))
    flat = lambda n: pl.BlockSpec((tm, n), lambda b, s: (b * (S // tm) + s, 0))
    flat_rows = pl.BlockSpec((tm * ROW_SUBLANES, LANES), lambda b, s: (b * (S // tm) + s, 0))
    consts = [wa, wb, wo, g2, wr_hi, wr_lo, br, ltri]
    return pl.pallas_call(
        _merge_kernel,
        grid=(B, S // tm),
        in_specs=[row(D), row(FOX_W), row(NSA_HEADS * LANES), row(2 * D)] + [_const_spec(a.shape) for a in consts],
        out_specs=[flat_rows, flat(LANES), _const_spec((1, LANES))],
        out_shape=[jax.ShapeDtypeStruct((B * S * ROW_SUBLANES, LANES), F32), jax.ShapeDtypeStruct((B * S, LANES), F32),
                   jax.ShapeDtypeStruct((1, LANES), F32)],
        scratch_shapes=[pltpu.VMEM((1, LANES), F32)],
        compiler_params=_cparams(("arbitrary", "arbitrary")),
        name="merge",
    )(x, oa, ob, gab, *consts)


ROW_SUBLANES = D_MODEL // LANES


def _to_tile_rows(ref, x):
    n = x.shape[0]
    for c in range(ROW_SUBLANES):
        ref[pl.ds(c, n, stride=ROW_SUBLANES), :] = x[:, LANES * c:LANES * (c + 1)]


def _from_tile_rows(ref):
    n = ref.shape[0] // ROW_SUBLANES
    return jnp.concatenate([ref[pl.ds(c, n, stride=ROW_SUBLANES), :] for c in range(ROW_SUBLANES)], axis=1)


def _row_copy(src_ref, src_row, dst_ref, dst_row, sem):
    src = src_ref.at[pl.ds(pl.multiple_of(src_row * ROW_SUBLANES, ROW_SUBLANES), ROW_SUBLANES), :]
    dst = dst_ref.at[pl.ds(pl.multiple_of(dst_row * ROW_SUBLANES, ROW_SUBLANES), ROW_SUBLANES), :]
    return pltpu.make_async_copy(src, dst, sem)


def _dispatch_kernel(row1_ref, row2_ref, clear_ref, x1_ref, xs_ref, zero_ref, sem, zsem):
    tm = x1_ref.shape[0] // ROW_SUBLANES
    base = pl.program_id(0) * tm

    @pl.when(pl.program_id(0) == 0)
    def _():
        zero_ref[...] = jnp.zeros_like(zero_ref)

        def clear(c):
            start = pl.multiple_of(jnp.maximum(clear_ref[c], 0) * ROW_SUBLANES, zero_ref.shape[0])
            return pltpu.make_async_copy(zero_ref, xs_ref.at[pl.ds(start, zero_ref.shape[0]), :], zsem)

        for c in range(clear_ref.shape[0]):
            @pl.when(clear_ref[c] >= 0)
            def _(c=c):
                clear(c).start()
        for c in range(clear_ref.shape[0]):
            @pl.when(clear_ref[c] >= 0)
            def _(c=c):
                clear(c).wait()

    def start(t, _):
        _row_copy(x1_ref, t, xs_ref, row1_ref[base + t], sem).start()
        _row_copy(x1_ref, t, xs_ref, row2_ref[base + t], sem).start()
        return _

    def wait(t, _):
        _row_copy(x1_ref, t, xs_ref, row1_ref[base + t], sem).wait()
        _row_copy(x1_ref, t, xs_ref, row2_ref[base + t], sem).wait()
        return _

    lax.fori_loop(0, tm, start, 0, unroll=8)
    lax.fori_loop(0, tm, wait, 0, unroll=8)


def _experts_kernel(tile_e_ref, n_used_ref, xs_ref, g2_ref, wg_ref, wu_ref, wd_ref, ys_ref,
                    wg_bf, wu_bf, wd_bf):
    k = pl.program_id(0)
    used = k < n_used_ref[0]

    @pl.when(used & ((k == 0) | (tile_e_ref[k] != tile_e_ref[jnp.maximum(k - 1, 0)])))
    def _():
        wg_bf[...] = wg_ref[0, 0].astype(BF16)
        wu_bf[...] = wu_ref[0, 0].astype(BF16)
        wd_bf[...] = wd_ref[0, 0].astype(BF16)

    @pl.when(used)
    def _():
        x = _from_tile_rows(xs_ref)
        h = (x * lax.rsqrt(jnp.mean(x * x, axis=-1, keepdims=True) + RMS_EPS) * g2_ref[...]).astype(BF16)
        a = _dot(h, wg_bf[...])
        hid = (a * jax.nn.sigmoid(a)) * _dot(h, wu_bf[...])
        _to_tile_rows(ys_ref, _dot(hid.astype(BF16), wd_bf[...]))

    @pl.when(jnp.logical_not(used))
    def _():
        ys_ref[...] = jnp.zeros_like(ys_ref)


def _combine_kernel(row1_ref, row2_ref, x1_ref, route_ref, ys_ref, o_ref, y1_ref, y2_ref, sem):
    tm = x1_ref.shape[0] // ROW_SUBLANES
    t = pl.program_id(0)
    slot = t & 1

    def for_each_copy(step, slot, fn):
        base = step * tm

        def body(r, _):
            fn(_row_copy(ys_ref, row1_ref[base + r], y1_ref.at[slot], r, sem.at[slot]))
            fn(_row_copy(ys_ref, row2_ref[base + r], y2_ref.at[slot], r, sem.at[slot]))
            return _

        lax.fori_loop(0, tm, body, 0, unroll=8)

    @pl.when(t == 0)
    def _():
        for_each_copy(0, 0, lambda c: c.start())

    @pl.when(t + 1 < pl.num_programs(0))
    def _():
        for_each_copy(t + 1, 1 - slot, lambda c: c.start())

    for_each_copy(t, slot, lambda c: c.wait())
    rec = route_ref[...]
    lane = lax.broadcasted_iota(jnp.int32, rec.shape, 1)
    w1 = jnp.sum(jnp.where(lane == R_W1, rec, 0.0), axis=-1, keepdims=True)
    w2 = jnp.sum(jnp.where(lane == R_W2, rec, 0.0), axis=-1, keepdims=True)
    o_ref[0] = _from_tile_rows(x1_ref) + (w1 * _from_tile_rows(y1_ref.at[slot]) + w2 * _from_tile_rows(y2_ref.at[slot]))


def _moe(x1, route, cnt, norm_ffn_g, w_gate, w_up, w_down, layer, B, S):
    T, D = B * S, D_MODEL
    tme = min(TM_EXPERT, T)
    tmd = min(TM_DISPATCH, S)

    counts = cnt[0, :N_EXPERTS].astype(jnp.int32)
    tiles_per_e = (counts + tme - 1) // tme
    tile_end = jnp.cumsum(tiles_per_e)
    offs = (tile_end - tiles_per_e) * tme
    last_tile_row = jnp.where(tiles_per_e > 0, (tile_end - 1) * tme, -1).astype(jnp.int32)
    rec = route[:, :8].T.astype(jnp.int32)
    experts = jnp.arange(N_EXPERTS, dtype=jnp.int32)[:, None]
    row_of = lambda e, rank: rank + jnp.sum(jnp.where(e[None, :] == experts, offs[:, None], 0), axis=0)
    row1 = row_of(rec[R_E1], rec[R_RANK1])
    row2 = row_of(rec[R_E2], rec[R_RANK2])
    max_tiles = (2 * T) // tme + N_EXPERTS
    tile_e = jnp.sum(jnp.arange(max_tiles, dtype=jnp.int32)[:, None] >= tile_end[None, :], axis=1)
    tile_e = jnp.minimum(tile_e, N_EXPERTS - 1).astype(jnp.int32)
    n_used = tile_end[-1:].astype(jnp.int32)
    n_rows = max_tiles * tme
    spare = n_used + jnp.arange(N_EXPERTS, dtype=jnp.int32)
    clear_rows = jnp.concatenate([last_tile_row, jnp.where(spare < max_tiles, spare * tme, -1)]).astype(jnp.int32)

    xs = pl.pallas_call(
        _dispatch_kernel,
        grid_spec=pltpu.PrefetchScalarGridSpec(
            num_scalar_prefetch=3,
            grid=(T // tmd,),
            in_specs=[pl.BlockSpec((tmd * ROW_SUBLANES, LANES), lambda t, r1, r2, lt: (t, 0))],
            out_specs=pl.BlockSpec(memory_space=pl.ANY),
            scratch_shapes=[pltpu.VMEM((tme * ROW_SUBLANES, LANES), F32), pltpu.SemaphoreType.DMA(()),
                            pltpu.SemaphoreType.DMA(())],
        ),
        out_shape=jax.ShapeDtypeStruct((n_rows * ROW_SUBLANES, LANES), F32),
        compiler_params=_cparams(("arbitrary",)),
        name="dispatch",
    )(row1, row2, clear_rows, x1)

    g2 = norm_ffn_g.astype(F32)[None, :]
    w_spec = lambda shape: pl.BlockSpec((1, 1) + shape, lambda k, te, nu: (layer, te[k], 0, 0))
    ys = pl.pallas_call(
        _experts_kernel,
        grid_spec=pltpu.PrefetchScalarGridSpec(
            num_scalar_prefetch=2,
            grid=(max_tiles,),
            in_specs=[pl.BlockSpec((tme * ROW_SUBLANES, LANES), lambda k, te, nu: (jnp.minimum(k, nu[0] - 1), 0)),
                      pl.BlockSpec((1, D), lambda k, te, nu: (0, 0)),
                      w_spec((D, D_EXPERT)), w_spec((D, D_EXPERT)), w_spec((D_EXPERT, D))],
            out_specs=pl.BlockSpec((tme * ROW_SUBLANES, LANES), lambda k, te, nu: (k, 0)),
            scratch_shapes=[pltpu.VMEM((D, D_EXPERT), BF16), pltpu.VMEM((D, D_EXPERT), BF16),
                            pltpu.VMEM((D_EXPERT, D), BF16)],
        ),
        out_shape=jax.ShapeDtypeStruct((n_rows * ROW_SUBLANES, LANES), F32),
        compiler_params=_cparams(("arbitrary",)),
        name="experts",
    )(tile_e, n_used, xs, g2, w_gate, w_up, w_down)

    return pl.pallas_call(
        _combine_kernel,
        grid_spec=pltpu.PrefetchScalarGridSpec(
            num_scalar_prefetch=2,
            grid=(T // tmd,),
            in_specs=[pl.BlockSpec((tmd * ROW_SUBLANES, LANES), lambda t, r1, r2: (t, 0)),
                      pl.BlockSpec((tmd, LANES), lambda t, r1, r2: (t, 0)),
                      pl.BlockSpec(memory_space=pl.ANY)],
            out_specs=pl.BlockSpec((1, tmd, D), lambda t, r1, r2: (t // (S // tmd), t % (S // tmd), 0)),
            scratch_shapes=[pltpu.VMEM((2, tmd * ROW_SUBLANES, LANES), F32), pltpu.VMEM((2, tmd * ROW_SUBLANES, LANES), F32),
                            pltpu.SemaphoreType.DMA((2,))],
        ),
        out_shape=jax.ShapeDtypeStruct((B, S, D), F32),
        compiler_params=_cparams(("arbitrary",)),
        name="combine",
    )(row1, row2, x1, route, ys)


def kernel(x, norm_mix_g, w_in, b_forget, fox_q_g, fox_k_g, nsa_q_g, nsa_k_g, cmp_k_w1, cmp_k_w2, cmp_k_pos,
           cmp_v_w1, cmp_v_w2, cmp_v_pos, w_fox_up, w_nsa_up, w_out, norm_ffn_g, w_group, b_group, w_router,
           b_router, w_gate, w_up, w_down):
    B, S, D = x.shape
    for l in range(w_in.shape[0]):
        qa, ka, va, qn, kcr, vcr, ksa, vsa, kwa, vwa, gt, gab = _inproj(
            x, norm_mix_g[l], w_in[l], b_forget[l], fox_q_g[l], fox_k_g[l], nsa_q_g[l], nsa_k_g[l])
        kca, vca = _compress(kcr, vcr, cmp_k_w1[l], cmp_k_w2[l], cmp_k_pos[l],
                             cmp_v_w1[l], cmp_v_w2[l], cmp_v_pos[l], nsa_k_g[l])
        ob = _nsa(qn, kca, vca, ksa, vsa, kwa, vwa, gt, _scores_bounded(nsa_q_g[l], nsa_k_g[l]))
        oa = _fox(qa, ka, va, _scores_bounded(fox_q_g[l], fox_k_g[l]))
        x1, route, cnt = _merge(x, oa, ob, gab, w_fox_up[l], w_nsa_up[l], w_out[l], norm_ffn_g[l],
                                w_group[l], b_group[l], w_router[l], b_router[l])
        x = _moe(x1, route, cnt, norm_ffn_g[l], w_gate, w_up, w_down, l, B, S)
    return x
```

```python
import functools

import jax
import jax.numpy as jnp
from jax import lax
from jax.experimental import pallas as pl
from jax.experimental.pallas import tpu as pltpu

F32 = jnp.float32
BF16 = jnp.bfloat16

D_MODEL = 1024
HEAD_DIM = 64
FOX_HEADS = 8
NSA_HEADS = 8
NSA_KV_HEADS = 2
NSA_REP = NSA_HEADS // NSA_KV_HEADS
CMP_BLOCK = 32
CMP_STRIDE = 16
CMP_HIDDEN = 256
SLC_BLOCK = 64
SLC_TOPK = 16
WINDOW = 512
N_GROUPS = 4
EXPERTS_PER_GROUP = 4
N_EXPERTS = N_GROUPS * EXPERTS_PER_GROUP
D_EXPERT = 512
RMS_EPS = 1e-6
NEG_INF = -1e30
FORCE_SCORE = 1e4

LANES = 128
HALF = LANES // 2
VMEM_LIMIT = 56 * 1024 * 1024

FOX_W = FOX_HEADS * HEAD_DIM
NSA_W = NSA_HEADS * HEAD_DIM
NSA_KV_W = NSA_KV_HEADS * HEAD_DIM

OFF_FQ = 0
OFF_FK = OFF_FQ + FOX_W
OFF_FV = OFF_FK + FOX_W
OFF_NQ = OFF_FV + FOX_W
OFF_KC = OFF_NQ + NSA_W
OFF_VC = OFF_KC + NSA_KV_W
OFF_KS = OFF_VC + NSA_KV_W
OFF_VS = OFF_KS + NSA_KV_W
OFF_KW = OFF_VS + NSA_KV_W
OFF_VW = OFF_KW + NSA_KV_W
OFF_FF = OFF_VW + NSA_KV_W
OFF_NG = OFF_FF + LANES
OFF_GA = OFF_NG + LANES
N_PROJ = OFF_GA + 2 * D_MODEL

TM_PROJ = 512
TQ_FOX = 512
FOX_UNROLL = 4
TQ_NSA = 256
TK_SLC = 256
TM_MERGE = 512
TM_EXPERT = 512
TM_DISPATCH = 512


def _dot(a, b):
    return jnp.dot(a, b, preferred_element_type=F32)


def _dot_nt(a, b):
    return lax.dot_general(a, b, (((1,), (1,)), ((), ())), preferred_element_type=F32)


def _split3(x):
    hi = x.astype(BF16)
    r = x - hi.astype(F32)
    mid = r.astype(BF16)
    lo = (r - mid.astype(F32)).astype(BF16)
    return hi, mid, lo


def _cparams(sem):
    return pltpu.CompilerParams(dimension_semantics=sem, vmem_limit_bytes=VMEM_LIMIT)


def _const_spec(shape):
    nd = len(shape)
    return pl.BlockSpec(shape, lambda *_: (0,) * nd)


def _inproj_kernel(x_ref, g_ref, w_ref, bf_ref, gqa_ref, gka_ref, gqn_ref, gkn_ref,
                   bd_ref, bd2_ref, tri_ref, cq_ref, ck_ref, cv_ref,
                   qc0_ref, qca_ref, qcb_ref, kc0_ref, kca_ref, kcb_ref, vone_ref,
                   qa_ref, ka_ref, va_ref, qn_ref, kcr_ref, vcr_ref,
                   ksa_ref, vsa_ref, kwa_ref, vwa_ref, gt_ref, gab_ref,
                   carry_ref):
    tm = x_ref.shape[1]

    @pl.when(pl.program_id(1) == 0)
    def _():
        carry_ref[...] = jnp.zeros_like(carry_ref)

    x = x_ref[0]
    y = x * lax.rsqrt(jnp.mean(x * x, axis=-1, keepdims=True) + RMS_EPS)
    h = (y * g_ref[...]).astype(BF16)

    def proj(off, n):
        return _dot(h, w_ref[:, off:off + n])

    lo_half = lax.broadcasted_iota(jnp.int32, (tm, LANES), 1) < HALF

    pos = pl.program_id(1) * tm + lax.broadcasted_iota(jnp.int32, (tm, 1), 0)
    pos_a = ((pos >> 8) << 8).astype(F32)
    pos_b = (pos & 255).astype(F32)

    def pos_channels(c0_ref, ca_ref, cb_ref, k):
        blk = slice(LANES * k, LANES * (k + 1))
        return c0_ref[:, blk] + ca_ref[:, blk] * pos_a + cb_ref[:, blk] * pos_b

    def headnorm(z, bd, grow):
        msq = _dot((z * z).astype(BF16), bd[...])
        return z * lax.rsqrt(msq + RMS_EPS) * grow[...]

    def spread_pairs(out_ref, src, aug):
        for m in range(4):
            s = src[:, LANES * m:LANES * (m + 1)]
            out_ref[0, :, LANES * 2 * m:LANES * (2 * m + 1)] = jnp.where(lo_half, s, aug(2 * m)).astype(BF16)
            out_ref[0, :, LANES * (2 * m + 1):LANES * (2 * m + 2)] = jnp.where(lo_half, aug(2 * m + 1), s).astype(BF16)

    zf = proj(OFF_FF, LANES) + bf_ref[...]
    logf = jnp.minimum(zf, 0.0) - jnp.log(1.0 + jnp.exp(-jnp.abs(zf)))
    l_hi, l_mid, l_lo = _split3(logf)
    tri = tri_ref[...]
    cum = carry_ref[...] + (_dot(tri, l_hi) + _dot(tri, l_mid) + _dot(tri, l_lo))
    carry_ref[...] = cum[tm - 1:tm, :]
    pieces = [p.astype(F32) for p in _split3(cum)]
    lane128 = lax.broadcasted_iota(jnp.int32, (tm, LANES), 1)

    def aug_block(k, const_ref, first, sign):
        base = HALF if k % 2 == 0 else 0
        blk = jnp.broadcast_to(const_ref[:, LANES * k:LANES * (k + 1)], (tm, LANES))
        for j, piece in enumerate(pieces):
            blk = jnp.where(lane128 == base + first + j, sign * piece[:, k:k + 1], blk)
        return blk

    augq = lambda k: aug_block(k, cq_ref, 3, 1.0)
    augk = lambda k: aug_block(k, ck_ref, 0, -1.0)

    zq = headnorm(proj(OFF_FQ, FOX_W), bd_ref, gqa_ref)
    spread_pairs(qa_ref, zq, augq)
    zk = headnorm(proj(OFF_FK, FOX_W), bd_ref, gka_ref)
    spread_pairs(ka_ref, zk, augk)
    zv = proj(OFF_FV, FOX_W)
    spread_pairs(va_ref, zv, lambda k: cv_ref[:, LANES * k:LANES * (k + 1)])

    zn = headnorm(proj(OFF_NQ, NSA_W), bd_ref, gqn_ref)
    for m in range(NSA_REP):
        s = zn[:, LANES * m:LANES * (m + 1)]
        c0 = pos_channels(qc0_ref, qca_ref, qcb_ref, m)
        c1 = pos_channels(qc0_ref, qca_ref, qcb_ref, NSA_REP + m)
        qn_ref[0, :, LANES * m:LANES * (m + 1)] = jnp.where(lo_half, s, c0).astype(BF16)
        qn_ref[0, :, LANES * (NSA_REP + m):LANES * (NSA_REP + m + 1)] = jnp.where(lo_half, c1, s).astype(BF16)

    kcr_ref[0] = proj(OFF_KC, NSA_KV_W)
    vcr_ref[0] = proj(OFF_VC, NSA_KV_W)

    kp0 = pos_channels(kc0_ref, kca_ref, kcb_ref, 0)
    kp1 = pos_channels(kc0_ref, kca_ref, kcb_ref, 1)

    def kv_pair(k_out, v_out, off_k, off_v):
        zk2 = headnorm(proj(off_k, NSA_KV_W), bd2_ref, gkn_ref)
        k_out[0, :, 0:LANES] = jnp.where(lo_half, zk2, kp0).astype(BF16)
        k_out[0, :, LANES:2 * LANES] = jnp.where(lo_half, kp1, zk2).astype(BF16)
        zv2 = proj(off_v, NSA_KV_W)
        v_out[0, :, 0:LANES] = jnp.where(lo_half, zv2, vone_ref[:, 0:LANES]).astype(BF16)
        v_out[0, :, LANES:2 * LANES] = jnp.where(lo_half, vone_ref[:, LANES:2 * LANES], zv2).astype(BF16)

    kv_pair(ksa_ref, vsa_ref, OFF_KS, OFF_VS)
    kv_pair(kwa_ref, vwa_ref, OFF_KW, OFF_VW)

    gt_ref[0] = jax.nn.sigmoid(proj(OFF_NG, LANES))
    gab_ref[0, :, 0:D_MODEL] = jax.nn.sigmoid(proj(OFF_GA, D_MODEL)).astype(BF16)
    gab_ref[0, :, D_MODEL:2 * D_MODEL] = jax.nn.sigmoid(proj(OFF_GA + D_MODEL, D_MODEL)).astype(BF16)


def _pos_pieces(pos):
    return ((pos // 256) * 256).astype(F32), (pos % 256).astype(F32)


def _inproj(x, norm_g, w_in, b_forget, fox_q_g, fox_k_g, nsa_q_g, nsa_k_g):
    B, S, D = x.shape
    tm = min(TM_PROJ, S)
    scale = HEAD_DIM ** -0.5

    c = [0]
    for n in (FOX_W, FOX_W, FOX_W, FOX_HEADS, NSA_W) + (NSA_KV_W,) * 6 + (3 * NSA_HEADS, D_MODEL, D_MODEL):
        c.append(c[-1] + n)
    fq, fk, fv, ff, nq, kc, vc, ks, vs, kw, vw, ng, ga, gb = [w_in[:, c[i]:c[i + 1]] for i in range(14)]
    perm = jnp.asarray([0, 4, 1, 5, 2, 6, 3, 7])
    nq = nq.reshape(D, NSA_HEADS, HEAD_DIM)[:, perm, :].reshape(D, NSA_W)
    padl = lambda a: jnp.pad(a, ((0, 0), (0, LANES - a.shape[1])))
    w = jnp.concatenate([fq, fk, fv, nq, kc, vc, ks, vs, kw, vw, padl(ff), padl(ng), ga, gb], axis=1).astype(BF16)
    assert w.shape[1] == N_PROJ

    bf = jnp.pad(b_forget.astype(F32), (0, LANES - FOX_HEADS))[None, :]
    gqa = jnp.tile(fox_q_g.astype(F32) * scale, FOX_HEADS)[None, :]
    gka = jnp.tile(fox_k_g.astype(F32), FOX_HEADS)[None, :]
    gqn = jnp.tile(nsa_q_g.astype(F32) * scale, NSA_HEADS)[None, :]
    gkn = jnp.tile(nsa_k_g.astype(F32), NSA_KV_HEADS)[None, :]

    r512 = jnp.arange(FOX_W)
    bd = jnp.where((r512[:, None] // HEAD_DIM) == (r512[None, :] // HEAD_DIM), 1.0 / HEAD_DIM, 0.0).astype(BF16)
    bd2 = bd[:LANES, :LANES]
    rt = jnp.arange(tm)
    tri = (rt[None, :] <= rt[:, None]).astype(BF16)

    heads = jnp.arange(FOX_HEADS)
    base = heads * LANES + jnp.where(heads % 2 == 0, HALF, 0)
    cols = jnp.arange(FOX_HEADS * LANES)
    off_in_blk = cols - base[cols // LANES]
    cq = ((off_in_blk >= 0) & (off_in_blk < 3)).astype(F32)[None, :]
    ck = ((off_in_blk >= 3) & (off_in_blk < 9)).astype(F32)[None, :]
    cv = (off_in_blk == 0).astype(F32)[None, :]

    blk = cols // LANES
    slope = 2.0 ** (-(blk + 1).astype(F32))
    o = cols % LANES - jnp.where(blk // NSA_REP == 0, HALF, 0)
    qc0 = jnp.where((o == 0) | (o == 1), slope, 0.0)[None, :]
    qca = jnp.where(o == 2, -slope, 0.0)[None, :]
    qcb = jnp.where(o == 3, -slope, 0.0)[None, :]
    kc0, kca, kcb, vone = _kv_rows()

    grid = (B, S // tm)
    row_spec = lambda n: pl.BlockSpec((1, tm, n), lambda b, s: (b, s, 0))
    consts = [norm_g.astype(F32)[None, :], w, bf, gqa, gka, gqn, gkn, bd, bd2, tri, cq, ck, cv,
              qc0, qca, qcb, kc0, kca, kcb, vone]
    out_widths = [(8 * LANES, BF16)] * 4 + [(LANES, F32)] * 2 + [(2 * LANES, BF16)] * 4 + \
                 [(LANES, F32), (2 * D_MODEL, BF16)]
    outs = pl.pallas_call(
        _inproj_kernel,
        grid=grid,
        in_specs=[row_spec(D)] + [_const_spec(a.shape) for a in consts],
        out_specs=[row_spec(n) for n, _ in out_widths],
        out_shape=[jax.ShapeDtypeStruct((B, S, n), dt) for n, dt in out_widths],
        scratch_shapes=[pltpu.VMEM((1, LANES), F32)],
        compiler_params=_cparams(("arbitrary", "arbitrary")),
        name="inproj",
    )(x, *consts)
    return outs


def _kv_rows():
    cols = jnp.arange(2 * LANES)
    o = cols % LANES - jnp.where(cols // LANES == 0, HALF, 0)
    row = lambda m: m.astype(F32)[None, :]
    return row((o >= 2) & (o <= 6)), row(o == 0), row(o == 1), row(o == 0)


def _kv_consts(pa, pb):
    c0, ca, cb, vone = _kv_rows()
    return (c0 + ca * pa[:, None] + cb * pb[:, None]).astype(BF16), vone


def _compress_kernel(kt_ref, vt_ref, w1k_ref, w1v_ref, posk_ref, posv_ref, pw1k_ref, pw1v_ref, w2k_ref, w2v_ref,
                     gk_ref, kcc_ref, vone_ref, kc_ref, vc_ref):
    nc = kc_ref.shape[2]

    def mlp(t_ref, w1_ref, pos_ref, pw1_ref, w2_ref):
        both = jnp.zeros((nc, 2 * CMP_HIDDEN), F32)
        for l in range(CMP_STRIDE):
            rows = t_ref[0, pl.ds(l, nc, stride=CMP_STRIDE), :].astype(BF16)
            both = both + _dot(rows, w1_ref[0, l])
        posw = _dot(pos_ref[...], pw1_ref[...])[0:1, :]
        pre = both[:, 0:CMP_HIDDEN] + pltpu.roll(both[:, CMP_HIDDEN:2 * CMP_HIDDEN], nc - 1, axis=0) + posw
        act = pre * (0.5 * (1.0 + jnp.tanh(0.7978845608028654 * (pre + 0.044715 * (pre * pre * pre)))))
        return _dot(act.astype(BF16), w2_ref[0])

    kc = mlp(kt_ref, w1k_ref, posk_ref, pw1k_ref, w2k_ref)
    msq = jnp.sum(kc * kc, axis=-1, keepdims=True) * (1.0 / HEAD_DIM)
    kc_ref[0, 0] = (kc * lax.rsqrt(msq + RMS_EPS) * gk_ref[0] + kcc_ref[0]).astype(BF16)
    vc = mlp(vt_ref, w1v_ref, posv_ref, pw1v_ref, w2v_ref)
    vc_ref[0, 0] = (vc + vone_ref[0]).astype(BF16)


def _compress(kcr, vcr, cmp_k_w1, cmp_k_w2, cmp_k_pos, cmp_v_w1, cmp_v_w2, cmp_v_pos, nsa_k_g):
    B, S, _ = kcr.shape
    G = NSA_KV_HEADS
    nc = S // CMP_STRIDE

    def w1_strided(w1):
        w = w1.reshape(2, CMP_STRIDE, HEAD_DIM, CMP_HIDDEN)
        w = jnp.concatenate([w[0], w[1]], axis=-1)
        z = jnp.zeros_like(w)
        return jnp.stack([jnp.concatenate([w, z], axis=1), jnp.concatenate([z, w], axis=1)]).astype(BF16)

    def w2_spread(w2):
        z = jnp.zeros_like(w2)
        return jnp.stack([jnp.concatenate([w2, z], 1), jnp.concatenate([z, w2], 1)]).astype(BF16)

    def pos8(p):
        return jnp.tile(p.reshape(1, CMP_BLOCK * HEAD_DIM), (8, 1)).astype(BF16)

    gk = nsa_k_g.astype(F32)
    z = jnp.zeros_like(gk)
    gk2 = jnp.stack([jnp.concatenate([gk, z]), jnp.concatenate([z, gk])])[:, None, :]
    cend = jnp.arange(nc) * CMP_STRIDE + CMP_BLOCK - 1
    kcc, vone = _kv_consts(*_pos_pieces(cend))
    kcc = kcc.astype(F32).reshape(nc, G, LANES).transpose(1, 0, 2)
    vone = vone.reshape(G, 1, LANES)

    tok = pl.BlockSpec((1, S, G * HEAD_DIM), lambda b, g: (b, 0, 0))
    per_g = lambda a: pl.BlockSpec((1,) + a.shape[1:], lambda b, g: (g,) + (0,) * (a.ndim - 1))
    w1k, w1v = w1_strided(cmp_k_w1), w1_strided(cmp_v_w1)
    pk, pv = pos8(cmp_k_pos), pos8(cmp_v_pos)
    pw1k, pw1v = cmp_k_w1.astype(BF16), cmp_v_w1.astype(BF16)
    w2k, w2v = w2_spread(cmp_k_w2), w2_spread(cmp_v_w2)
    out_spec = pl.BlockSpec((1, 1, nc, LANES), lambda b, g: (b, g, 0, 0))
    return pl.pallas_call(
        _compress_kernel,
        grid=(B, G),
        in_specs=[tok, tok, per_g(w1k), per_g(w1v), _const_spec(pk.shape), _const_spec(pv.shape),
                  _const_spec(pw1k.shape), _const_spec(pw1v.shape), per_g(w2k), per_g(w2v),
                  per_g(gk2), per_g(kcc), per_g(vone)],
        out_specs=[out_spec, out_spec],
        out_shape=[jax.ShapeDtypeStruct((B, G, nc, LANES), BF16)] * 2,
        compiler_params=_cparams(("arbitrary", "arbitrary")),
        name="compress",
    )(kcr, vcr, w1k, w1v, pk, pv, pw1k, pw1v, w2k, w2v, gk2, kcc, vone)


MAX_EXPONENT = 60.0


def _scores_bounded(q_g, k_g):
    bound = HEAD_DIM ** 0.5 * jnp.max(jnp.abs(q_g)) * jnp.max(jnp.abs(k_g))
    return 2.04 * bound + 0.05 <= MAX_EXPONENT


def _attn_first(s, v):
    m = jnp.max(s, axis=-1, keepdims=True)
    p = jnp.exp((s - m).astype(BF16))
    return m, _dot(p, v)


def _attn_step(carry, s, v):
    m, acc = carry
    m_new = jnp.maximum(m, jnp.max(s, axis=-1, keepdims=True))
    p = jnp.exp((s - m_new).astype(BF16))
    return m_new, jnp.exp(m - m_new) * acc + _dot(p, v)


def _nsa_kernel(bounded, q_ref, kc_ref, vc_ref, ks_ref, e_ref, vs_ref, kw_ref, vw_ref, gt_ref, ov_ref, o_ref,
                tiles_ref, sa_ref, sb_ref):
    tq = q_ref.shape[1]
    nc = kc_ref.shape[2]
    tk = TK_SLC
    rq = NSA_REP * tq
    g = pl.program_id(1)
    i = pl.program_id(2)
    q0 = i * tq

    q4 = jnp.concatenate([q_ref[0, :, LANES * r:LANES * (r + 1)] for r in range(NSA_REP)], axis=0)

    def qpos_of(shape):
        return q0 + (lax.broadcasted_iota(jnp.int32, shape, 0) & (tq - 1))

    def add_mask(s, valid):
        bias = jnp.where(valid, 0.0, NEG_INF)
        return (s.reshape(NSA_REP, tq, s.shape[1]) + bias[None]).reshape(s.shape)

    def qrow(n):
        return q0 + lax.broadcasted_iota(jnp.int32, (tq, n), 0)

    def kcol(n):
        return lax.broadcasted_iota(jnp.int32, (tq, n), 1)

    anyv = qpos_of((rq, 1)) >= CMP_BLOCK - 1

    def cmp_branch(n):
        s = add_mask(_dot_nt(q4, kc_ref[0, 0, 0:n, :]), qrow(n) >= kcol(n) * CMP_STRIDE + (CMP_BLOCK - 1))
        if bounded:
            e = jnp.exp(s)
            p = e * jnp.where(anyv, 1.0 / jnp.sum(e, axis=-1, keepdims=True), 0.0)
        else:
            m = jnp.max(s, axis=-1, keepdims=True)
            e = jnp.exp(s - m)
            p = e * (anyv.astype(F32) / jnp.sum(e, axis=-1, keepdims=True))
        o = _dot(p.astype(BF16), vc_ref[0, 0, 0:n, :])
        psum = p[0:tq] + p[tq:2 * tq] + p[2 * tq:3 * tq] + p[3 * tq:4 * tq]
        ov = ov_ref[0:n, :]
        return o, functools.reduce(lambda a, b: a + b, [_dot(piece, ov) for piece in _split3(psum)])

    last_visible = (q0 + tq - CMP_BLOCK) // CMP_STRIDE
    o_cmp, imp = lax.cond(last_visible < nc // 2, lambda: cmp_branch(nc // 2), lambda: cmp_branch(nc))

    blk_i = lax.broadcasted_iota(jnp.int32, (LANES, tq), 0)
    blk_f = blk_i.astype(F32)
    qblk = (q0 + lax.broadcasted_iota(jnp.int32, (LANES, tq), 1)) // SLC_BLOCK
    forced = (blk_i == 0) | (blk_i == qblk) | (blk_i == qblk - 1)
    score = jnp.where(forced, -3e38, jnp.where(blk_i > qblk, -1.0, imp.T))
    selb = jnp.where(forced, 0.0, NEG_INF)
    for _ in range(SLC_TOPK - 3):
        mx = jnp.max(score, axis=0, keepdims=True)
        first = jnp.min(jnp.where(score == mx, blk_f, float(LANES)), axis=0, keepdims=True)
        hit = blk_f == first
        selb = jnp.where(hit, 0.0, selb)
        score = jnp.where(hit, -3e38, score)
    sb = selb.T.astype(BF16)
    sb4 = jnp.concatenate([sb] * NSA_REP, axis=0)

    def slc_qk(j):
        start = pl.multiple_of(j * tk, tk)
        kk = jnp.concatenate([ks_ref[0, pl.ds(start, tk), :], e_ref[pl.ds(start, tk), :]], axis=1)
        return _dot_nt(q2, kk)

    def slc_v(j):
        return vs_ref[0, pl.ds(pl.multiple_of(j * tk, tk), tk), :]

    jd = q0 // tk
    blocks_per_tile = tk // SLC_BLOCK
    anysel = jnp.max(selb, axis=1, keepdims=True)
    cnt = jnp.int32(0)
    for j in range(ks_ref.shape[1] // tk):
        tile_sel = jnp.max(anysel[blocks_per_tile * j:blocks_per_tile * (j + 1), :]) > -1.0
        tiles_ref[cnt] = j
        cnt = cnt + (tile_sel & (j < jd)).astype(jnp.int32)

    lane_r = lax.broadcasted_iota(jnp.int32, (rq, LANES), 1)
    data0 = HALF * g
    ones_lane = HALF - data0
    nw = WINDOW + tq
    wstart = pl.multiple_of(jnp.maximum(q0 - WINDOW, 0), tq)
    win_valid = lax.bitcast_convert_type(qrow(nw) - (wstart + kcol(nw)), jnp.uint32) < WINDOW
    diag_valid = jd * tk + kcol(tk) <= qrow(tk)

    if bounded:
        def shifted_by_own_score(kself):
            own = jnp.sum(q4.astype(F32).reshape(NSA_REP, tq, LANES) * kself.astype(F32)[None],
                          axis=-1, keepdims=True).reshape(rq, 1)
            qf = q4.astype(F32)
            for off, piece in enumerate(_split3(-own)):
                qf = jnp.where(lane_r == ones_lane + 4 + off, piece.astype(F32), qf)
            return qf.astype(BF16)

        q2 = jnp.concatenate([shifted_by_own_score(ks_ref[0, pl.ds(q0, tq), :]), sb4], axis=1)
        sa_ref[:, 0:LANES] = _dot(jnp.exp(add_mask(slc_qk(jd), diag_valid)).astype(BF16), slc_v(jd))

        def slc_pv(t):
            j = tiles_ref[t]
            return _dot(jnp.exp(slc_qk(j)).astype(BF16), slc_v(j))

        def slc_body(p, _):
            sa_ref[:, 0:LANES] += slc_pv(2 * p) + slc_pv(2 * p + 1)
            return _

        lax.fori_loop(0, cnt // 2, slc_body, 0)

        @pl.when(cnt % 2 == 1)
        def _():
            sa_ref[:, 0:LANES] += slc_pv(cnt - 1)

        acc_slc = sa_ref[:, 0:LANES]

        qw = shifted_by_own_score(kw_ref[0, pl.ds(q0, tq), :])
        s = add_mask(_dot_nt(qw, kw_ref[0, pl.ds(wstart, nw), :]), win_valid)
        acc_win = _dot(jnp.exp(s).astype(BF16), vw_ref[0, pl.ds(wstart, nw), :])
    else:
        q2 = jnp.concatenate([q4, sb4], axis=1)
        n_seq = cnt + 1

        def tile_of(n):
            return jnp.where(n == 0, jd, tiles_ref[jnp.maximum(n - 1, 0)])

        def slc_step(carry, s_buf, n):
            return _attn_step(carry, s_buf[...], slc_v(tile_of(n)))

        sa_ref[...] = add_mask(slc_qk(jd), diag_valid)

        def slc_body(p, carry):
            sb_ref[...] = slc_qk(tile_of(2 * p + 1))
            carry = slc_step(carry, sa_ref, 2 * p)
            sa_ref[...] = slc_qk(tile_of(jnp.minimum(2 * p + 2, n_seq - 1)))
            return slc_step(carry, sb_ref, 2 * p + 1)

        carry = (jnp.full((rq, 1), NEG_INF, F32), jnp.zeros((rq, LANES), F32))
        carry = lax.fori_loop(0, n_seq // 2, slc_body, carry)
        _, acc_slc = lax.cond(n_seq % 2 == 1, lambda c: slc_step(c, sa_ref, n_seq - 1), lambda c: c, carry)

        s = add_mask(_dot_nt(q4, kw_ref[0, pl.ds(wstart, nw), :]), win_valid)
        _, acc_win = _attn_first(s, vw_ref[0, pl.ds(wstart, nw), :])


    def normalise(acc):
        l = jnp.sum(jnp.where(lane_r == ones_lane, acc, 0.0), axis=-1, keepdims=True)
        return acc * (1.0 / l)

    o_slc = normalise(acc_slc)
    o_win = normalise(acc_win)
    gt = gt_ref[0]
    lane_g = lax.broadcasted_iota(jnp.int32, (tq, LANES), 1)
    is_data_q = (lane_g >= data0) & (lane_g < data0 + HALF)
    for r in range(NSA_REP):
        col = 3 * (NSA_REP * g + r)
        gate = [jnp.sum(jnp.where(lane_g == col + b, gt, 0.0), axis=-1, keepdims=True) for b in range(3)]
        rows = slice(r * tq, (r + 1) * tq)
        o = gate[0] * o_cmp[rows] + gate[1] * o_slc[rows] + gate[2] * o_win[rows]
        o_ref[0, :, LANES * r:LANES * (r + 1)] = jnp.where(is_data_q, o, 0.0).astype(BF16)


def _nsa(qn, kca, vca, ksa, vsa, kwa, vwa, gt, scores_bounded):
    B, S, _ = qn.shape
    G = NSA_KV_HEADS
    tq = min(TQ_NSA, S)
    nc = S // CMP_STRIDE
    n_slc = S // SLC_BLOCK
    assert n_slc <= LANES and S % TK_SLC == 0

    cs = jnp.arange(nc)[:, None] * CMP_STRIDE
    ss = jnp.arange(LANES)[None, :] * SLC_BLOCK
    ovl = jnp.clip(jnp.minimum(cs + CMP_BLOCK, ss + SLC_BLOCK) - jnp.maximum(cs, ss), 0, None)
    valid = (jnp.arange(nc)[:, None] < (S - CMP_BLOCK) // CMP_STRIDE + 1) & (jnp.arange(LANES)[None, :] < n_slc)
    ov = jnp.where(valid, ovl.astype(F32) / CMP_BLOCK, 0.0).astype(BF16)
    e1h = (jnp.arange(S)[:, None] // SLC_BLOCK == jnp.arange(LANES)[None, :]).astype(BF16)

    q_spec = pl.BlockSpec((1, tq, NSA_REP * LANES), lambda b, g, i: (b, i, g))
    c_spec = pl.BlockSpec((1, 1, nc, LANES), lambda b, g, i: (b, g, 0, 0))
    kv_spec = pl.BlockSpec((1, S, LANES), lambda b, g, i: (b, 0, g))

    def call(bounded, name):
        return pl.pallas_call(
            functools.partial(_nsa_kernel, bounded),
            grid=(B, G, S // tq),
            in_specs=[q_spec, c_spec, c_spec, kv_spec, _const_spec(e1h.shape), kv_spec, kv_spec, kv_spec,
                      pl.BlockSpec((1, tq, LANES), lambda b, g, i: (b, i, 0)), _const_spec(ov.shape)],
            out_specs=q_spec,
            out_shape=jax.ShapeDtypeStruct((B, S, NSA_HEADS * LANES), BF16),
            scratch_shapes=[pltpu.SMEM((S // TK_SLC + 1,), jnp.int32),
                            pltpu.VMEM((NSA_REP * tq, TK_SLC), F32), pltpu.VMEM((NSA_REP * tq, TK_SLC), F32)],
            compiler_params=_cparams(("arbitrary", "arbitrary", "arbitrary")),
            name=name,
        )

    return lax.cond(scores_bounded, call(True, "nsa_bounded"), call(False, "nsa"),
                    qn, kca, vca, ksa, e1h, vsa, kwa, vwa, gt, ov)


def _fox_kernel(q_ref, k_ref, v_ref, o_ref, sa_ref, sb_ref):
    tq = q_ref.shape[1]
    i = pl.program_id(2)
    lane = lax.broadcasted_iota(jnp.int32, (tq, LANES), 1)
    causal = lax.broadcasted_iota(jnp.int32, (tq, tq), 1) <= lax.broadcasted_iota(jnp.int32, (tq, tq), 0)

    def cols(hh):
        return slice(LANES * hh, LANES * (hh + 1))

    def qk(hh, j):
        start = pl.multiple_of(j * tq, tq)
        return _dot_nt(q_ref[0, :, cols(hh)], k_ref[0, pl.ds(start, tq), cols(hh)])

    def vtile(hh, j):
        return v_ref[0, pl.ds(pl.multiple_of(j * tq, tq), tq), cols(hh)]

    n_tiles = i + 1

    def tile_of(n):
        return jnp.where(n == 0, i, n - 1)

    def step_all(carry, s_buf, n):
        t = tile_of(n)
        return tuple(_attn_step(carry[hh], s_buf[hh], vtile(hh, t)) for hh in range(2))

    for hh in range(2):
        sa_ref[hh] = jnp.where(causal, qk(hh, i), NEG_INF)
    init = tuple((jnp.full((tq, 1), NEG_INF, F32), jnp.zeros((tq, LANES), F32)) for hh in range(2))

    def body(p, carry):
        for hh in range(2):
            sb_ref[hh] = qk(hh, tile_of(2 * p + 1))
        carry = step_all(carry, sa_ref, 2 * p)
        nxt = jnp.minimum(2 * p + 2, n_tiles - 1)
        for hh in range(2):
            sa_ref[hh] = qk(hh, tile_of(nxt))
        return step_all(carry, sb_ref, 2 * p + 1)

    carry = lax.fori_loop(0, n_tiles // 2, body, init)
    carry = lax.cond(n_tiles % 2 == 1, lambda c: step_all(c, sa_ref, n_tiles - 1), lambda c: c, carry)
    outs = []
    for hh in range(2):
        acc = carry[hh][1]
        ones_lane = HALF if hh == 0 else 0
        l = jnp.sum(jnp.where(lane == ones_lane, acc, 0.0), axis=-1, keepdims=True)
        outs.append(acc * (1.0 / l))
    o_ref[0] = jnp.where(lane < HALF, outs[0], outs[1]).astype(BF16)


def _fox_bounded_kernel(q_ref, k_ref, v_ref, o_ref, qs_ref, acc_ref):
    tq = q_ref.shape[1]
    i = pl.program_id(2)
    lane = lax.broadcasted_iota(jnp.int32, (tq, LANES), 1)
    causal = lax.broadcasted_iota(jnp.int32, (tq, tq), 1) <= lax.broadcasted_iota(jnp.int32, (tq, tq), 0)

    def cols(hh):
        return slice(LANES * hh, LANES * (hh + 1))

    def ktile(hh, j):
        return k_ref[0, pl.ds(pl.multiple_of(j * tq, tq), tq), cols(hh)]

    def vtile(hh, j):
        return v_ref[0, pl.ds(pl.multiple_of(j * tq, tq), tq), cols(hh)]

    for hh in range(2):
        q = q_ref[0, :, cols(hh)]
        s = jnp.where(causal, _dot_nt(q, ktile(hh, i)), NEG_INF)
        m = jnp.max(s, axis=-1, keepdims=True)
        acc_ref[hh] = _dot(jnp.exp((s - m).astype(BF16)), vtile(hh, i))
        qf = q.astype(F32)
        free0 = (HALF if hh == 0 else 0) + 6
        for off, piece in enumerate(_split3(-m)):
            qf = jnp.where(lane == free0 + off, piece.astype(F32), qf)
        qs_ref[hh] = qf.astype(BF16)

    def sweep(j0, n):
        for hh in range(2):
            pv = [_dot(jnp.exp(_dot_nt(qs_ref[hh], ktile(hh, j0 + t)).astype(BF16)), vtile(hh, j0 + t))
                  for t in range(n)]
            acc_ref[hh] += functools.reduce(lambda a, b: a + b, pv)

    def body(c, _):
        sweep(c * FOX_UNROLL, FOX_UNROLL)
        return _

    lax.fori_loop(0, i // FOX_UNROLL, body, 0)
    done = (i // FOX_UNROLL) * FOX_UNROLL
    n = FOX_UNROLL // 2
    while n >= 1:
        @pl.when((i & n) != 0)
        def _(n=n, start=done):
            sweep(start, n)
        done = done + (i & n)
        n //= 2
    outs = []
    for hh in range(2):
        acc = acc_ref[hh]
        ones_lane = HALF if hh == 0 else 0
        l = jnp.sum(jnp.where(lane == ones_lane, acc, 0.0), axis=-1, keepdims=True)
        outs.append(acc * (1.0 / l))
    o_ref[0] = jnp.where(lane < HALF, outs[0], outs[1]).astype(BF16)


def _fox(qa, ka, va, scores_bounded):
    B, S, _ = qa.shape
    tq = min(TQ_FOX, S)
    q_spec = pl.BlockSpec((1, tq, 2 * LANES), lambda b, h, i: (b, i, h))
    kv_spec = pl.BlockSpec((1, S, 2 * LANES), lambda b, h, i: (b, 0, h))

    def call(body, scratch, name):
        return pl.pallas_call(
            body,
            grid=(B, FOX_HEADS // 2, S // tq),
            in_specs=[q_spec, kv_spec, kv_spec],
            out_specs=pl.BlockSpec((1, tq, LANES), lambda b, h, i: (b, i, h)),
            out_shape=jax.ShapeDtypeStruct((B, S, FOX_W), BF16),
            scratch_shapes=scratch,
            compiler_params=_cparams(("arbitrary", "arbitrary", "arbitrary")),
            name=name,
        )

    general = call(_fox_kernel, [pltpu.VMEM((2, tq, tq), F32), pltpu.VMEM((2, tq, tq), F32)], "fox")
    bounded = call(_fox_bounded_kernel, [pltpu.VMEM((2, tq, LANES), BF16), pltpu.VMEM((2, tq, LANES), F32)],
                   "fox_bounded")
    return lax.cond(scores_bounded, bounded, general, qa, ka, va)


R_E1, R_E2, R_W1, R_W2, R_RANK1, R_RANK2 = range(6)


def _merge_kernel(x_ref, oa_ref, ob_ref, gab_ref, wa_ref, wb_ref, wo_ref, g2_ref, wr_hi_ref, wr_lo_ref, br_ref,
                  ltri_ref, x1_ref, route_ref, cnt_ref, carry_ref):
    tm = x_ref.shape[1]

    @pl.when((pl.program_id(0) == 0) & (pl.program_id(1) == 0))
    def _():
        carry_ref[...] = jnp.zeros_like(carry_ref)

    out_a = _dot(oa_ref[0], wa_ref[...])
    out_b = _dot(ob_ref[0], wb_ref[...])
    mix = gab_ref[0, :, 0:D_MODEL].astype(F32) * out_a + gab_ref[0, :, D_MODEL:2 * D_MODEL].astype(F32) * out_b
    x1 = x_ref[0] + _dot(mix.astype(BF16), wo_ref[...])
    _to_tile_rows(x1_ref, x1)
    h2 = x1 * lax.rsqrt(jnp.mean(x1 * x1, axis=-1, keepdims=True) + RMS_EPS) * g2_ref[...]

    h_hi = h2.astype(BF16)
    h_lo = (h2 - h_hi.astype(F32)).astype(BF16)
    logits = _dot(h_hi, wr_hi_ref[...]) + (_dot(h_hi, wr_lo_ref[...]) + _dot(h_lo, wr_hi_ref[...])) + br_ref[...]
    lane = lax.broadcasted_iota(jnp.int32, (tm, LANES), 1)
    lane_f = lane.astype(F32)

    def first_argmax(vals):
        mx = jnp.max(vals, axis=-1, keepdims=True)
        idx = jnp.min(jnp.where(vals == mx, lane_f, float(LANES)), axis=-1, keepdims=True)
        return mx, idx

    is_grp = (lane >= N_EXPERTS) & (lane < N_EXPERTS + N_GROUPS)
    gl = jnp.where(is_grp, logits, NEG_INF)
    gmax, gidx = first_argmax(gl)
    p_g = 1.0 / jnp.sum(jnp.where(is_grp, jnp.exp(gl - gmax), 0.0), axis=-1, keepdims=True)
    e_lo = (gidx - float(N_EXPERTS)) * float(EXPERTS_PER_GROUP)
    in_grp = (lane_f >= e_lo) & (lane_f < e_lo + float(EXPERTS_PER_GROUP))
    el = jnp.where(in_grp, logits, NEG_INF)
    m1, i1 = first_argmax(el)
    m2, i2 = first_argmax(jnp.where(lane_f == i1, NEG_INF, el))
    e2 = jnp.exp(m2 - m1)
    w1 = p_g / (1.0 + e2)
    w2 = p_g * e2 / (1.0 + e2)

    hit1 = lane_f == i1
    hit2 = lane_f == i2
    onehot = jnp.where(hit1 | hit2, 1.0, 0.0)
    before = carry_ref[...] + _dot(ltri_ref[...], onehot.astype(BF16))
    rank1 = jnp.sum(jnp.where(hit1, before, 0.0), axis=-1, keepdims=True)
    rank2 = jnp.sum(jnp.where(hit2, before, 0.0), axis=-1, keepdims=True)
    total = carry_ref[...] + jnp.sum(onehot, axis=0, keepdims=True)
    carry_ref[...] = total
    cnt_ref[...] = total

    rec = jnp.zeros((tm, LANES), F32)
    for k, val in ((R_E1, i1), (R_E2, i2), (R_W1, w1), (R_W2, w2), (R_RANK1, rank1), (R_RANK2, rank2)):
        rec = jnp.where(lane == k, val, rec)
    route_ref[...] = rec


def _merge(x, oa, ob, gab, w_fox_up, w_nsa_up, w_out, norm_ffn_g, w_group, b_group, w_router, b_router):
    B, S, D = x.shape
    tm = min(TM_MERGE, S)
    wa = w_fox_up.astype(BF16)
    wn = w_nsa_up.reshape(NSA_KV_HEADS, NSA_REP, HEAD_DIM, D)
    z = jnp.zeros_like(wn[0])
    wb = jnp.stack([jnp.concatenate([wn[0], z], axis=1), jnp.concatenate([z, wn[1]], axis=1)])
    wb = wb.reshape(NSA_HEADS * LANES, D).astype(BF16)
    wo = w_out.astype(BF16)
    wr = jnp.pad(jnp.concatenate([w_router, w_group], axis=1).astype(F32),
                 ((0, 0), (0, LANES - N_EXPERTS - N_GROUPS)))
    wr_hi = wr.astype(BF16)
    wr_lo = (wr - wr_hi.astype(F32)).astype(BF16)
    br = jnp.pad(jnp.concatenate([b_router, b_group]).astype(F32), (0, LANES - N_EXPERTS - N_GROUPS))[None, :]
    g2 = norm_ffn_g.astype(F32)[None, :]
    rt = jnp.arange(tm)
    ltri = (rt[None, :] < rt[:, None]).astype(BF16)

    row = lambda n: pl.BlockSpec((1, tm, n), lambda b, s: (b, s, 0))
    flat = lambda n: pl.BlockSpec((tm, n), lambda b, s: (b * (S // tm) + s, 0))
    flat_rows = pl.BlockSpec((tm * ROW_SUBLANES, LANES), lambda b, s: (b * (S // tm) + s, 0))
    consts = [wa, wb, wo, g2, wr_hi, wr_lo, br, ltri]
    return pl.pallas_call(
        _merge_kernel,
        grid=(B, S // tm),
        in_specs=[row(D), row(FOX_W), row(NSA_HEADS * LANES), row(2 * D)] + [_const_spec(a.shape) for a in consts],
        out_specs=[flat_rows, flat(LANES), _const_spec((1, LANES))],
        out_shape=[jax.ShapeDtypeStruct((B * S * ROW_SUBLANES, LANES), F32), jax.ShapeDtypeStruct((B * S, LANES), F32),
                   jax.ShapeDtypeStruct((1, LANES), F32)],
        scratch_shapes=[pltpu.VMEM((1, LANES), F32)],
        compiler_params=_cparams(("arbitrary", "arbitrary")),
        name="merge",
    )(x, oa, ob, gab, *consts)


ROW_SUBLANES = D_MODEL // LANES


def _to_tile_rows(ref, x):
    n = x.shape[0]
    for c in range(ROW_SUBLANES):
        ref[pl.ds(c, n, stride=ROW_SUBLANES), :] = x[:, LANES * c:LANES * (c + 1)]


def _from_tile_rows(ref):
    n = ref.shape[0] // ROW_SUBLANES
    return jnp.concatenate([ref[pl.ds(c, n, stride=ROW_SUBLANES), :] for c in range(ROW_SUBLANES)], axis=1)


def _row_copy(src_ref, src_row, dst_ref, dst_row, sem):
    src = src_ref.at[pl.ds(pl.multiple_of(src_row * ROW_SUBLANES, ROW_SUBLANES), ROW_SUBLANES), :]
    dst = dst_ref.at[pl.ds(pl.multiple_of(dst_row * ROW_SUBLANES, ROW_SUBLANES), ROW_SUBLANES), :]
    return pltpu.make_async_copy(src, dst, sem)


DISPATCH_SLOTS = 3


def _dispatch_kernel(row1_ref, row2_ref, clear_ref, x1_ref, xs_ref, stage_ref, zero_ref, in_sem, out_sem, zsem):
    tm = stage_ref.shape[1] // ROW_SUBLANES
    t = pl.program_id(0)
    n_steps = pl.num_programs(0)

    def stage(step):
        rows = pl.ds(pl.multiple_of(step * (tm * ROW_SUBLANES), tm * ROW_SUBLANES), tm * ROW_SUBLANES)
        slot = step % DISPATCH_SLOTS
        return pltpu.make_async_copy(x1_ref.at[rows, :], stage_ref.at[slot], in_sem.at[slot])

    def for_each_scatter(step, fn):
        base = step * tm
        slot = step % DISPATCH_SLOTS

        def body(r, _):
            fn(_row_copy(stage_ref.at[slot], r, xs_ref, row1_ref[base + r], out_sem.at[slot]))
            fn(_row_copy(stage_ref.at[slot], r, xs_ref, row2_ref[base + r], out_sem.at[slot]))
            return _

        lax.fori_loop(0, tm, body, 0, unroll=8)

    @pl.when(t == 0)
    def _():
        stage(0).start()
        zero_ref[...] = jnp.zeros_like(zero_ref)

        def clear(c):
            start = pl.multiple_of(jnp.maximum(clear_ref[c], 0) * ROW_SUBLANES, zero_ref.shape[0])
            return pltpu.make_async_copy(zero_ref, xs_ref.at[pl.ds(start, zero_ref.shape[0]), :], zsem)

        for c in range(clear_ref.shape[0]):
            @pl.when(clear_ref[c] >= 0)
            def _(c=c):
                clear(c).start()
        for c in range(clear_ref.shape[0]):
            @pl.when(clear_ref[c] >= 0)
            def _(c=c):
                clear(c).wait()

    @pl.when(t >= 2)
    def _():
        for_each_scatter(t - 2, lambda c: c.wait())

    @pl.when(t + 1 < n_steps)
    def _():
        stage(t + 1).start()

    stage(t).wait()
    for_each_scatter(t, lambda c: c.start())

    @pl.when(t + 1 == n_steps)
    def _():
        @pl.when(t >= 1)
        def _():
            for_each_scatter(t - 1, lambda c: c.wait())
        for_each_scatter(t, lambda c: c.wait())


def _experts_kernel(tile_e_ref, n_used_ref, xs_ref, g2_ref, wg_ref, wu_ref, wd_ref, ys_ref,
                    wg_bf, wu_bf, wd_bf):
    k = pl.program_id(0)
    used = k < n_used_ref[0]

    @pl.when(used & ((k == 0) | (tile_e_ref[k] != tile_e_ref[jnp.maximum(k - 1, 0)])))
    def _():
        wg_bf[...] = wg_ref[0, 0].astype(BF16)
        wu_bf[...] = wu_ref[0, 0].astype(BF16)
        wd_bf[...] = wd_ref[0, 0].astype(BF16)

    @pl.when(used)
    def _():
        x = _from_tile_rows(xs_ref)
        h = (x * lax.rsqrt(jnp.mean(x * x, axis=-1, keepdims=True) + RMS_EPS) * g2_ref[...]).astype(BF16)
        a = _dot(h, wg_bf[...])
        hid = (a * jax.nn.sigmoid(a)) * _dot(h, wu_bf[...])
        _to_tile_rows(ys_ref, _dot(hid.astype(BF16), wd_bf[...]))

    @pl.when(jnp.logical_not(used))
    def _():
        ys_ref[...] = jnp.zeros_like(ys_ref)


def _combine_kernel(row1_ref, row2_ref, x1_ref, route_ref, ys_ref, o_ref, y1_ref, y2_ref, sem):
    tm = x1_ref.shape[0] // ROW_SUBLANES
    t = pl.program_id(0)
    slot = t & 1

    def for_each_copy(step, slot, fn):
        base = step * tm

        def body(r, _):
            fn(_row_copy(ys_ref, row1_ref[base + r], y1_ref.at[slot], r, sem.at[slot]))
            fn(_row_copy(ys_ref, row2_ref[base + r], y2_ref.at[slot], r, sem.at[slot]))
            return _

        lax.fori_loop(0, tm, body, 0, unroll=8)

    @pl.when(t == 0)
    def _():
        for_each_copy(0, 0, lambda c: c.start())

    @pl.when(t + 1 < pl.num_programs(0))
    def _():
        for_each_copy(t + 1, 1 - slot, lambda c: c.start())

    for_each_copy(t, slot, lambda c: c.wait())
    rec = route_ref[...]
    lane = lax.broadcasted_iota(jnp.int32, rec.shape, 1)
    w1 = jnp.sum(jnp.where(lane == R_W1, rec, 0.0), axis=-1, keepdims=True)
    w2 = jnp.sum(jnp.where(lane == R_W2, rec, 0.0), axis=-1, keepdims=True)
    o_ref[0] = _from_tile_rows(x1_ref) + (w1 * _from_tile_rows(y1_ref.at[slot]) + w2 * _from_tile_rows(y2_ref.at[slot]))


def _moe(x1, route, cnt, norm_ffn_g, w_gate, w_up, w_down, layer, B, S):
    T, D = B * S, D_MODEL
    tme = min(TM_EXPERT, T)
    tmd = min(TM_DISPATCH, S)

    counts = cnt[0, :N_EXPERTS].astype(jnp.int32)
    tiles_per_e = (counts + tme - 1) // tme
    tile_end = jnp.cumsum(tiles_per_e)
    offs = (tile_end - tiles_per_e) * tme
    last_tile_row = jnp.where(tiles_per_e > 0, (tile_end - 1) * tme, -1).astype(jnp.int32)
    rec = route[:, :8].T.astype(jnp.int32)
    experts = jnp.arange(N_EXPERTS, dtype=jnp.int32)[:, None]
    row_of = lambda e, rank: rank + jnp.sum(jnp.where(e[None, :] == experts, offs[:, None], 0), axis=0)
    row1 = row_of(rec[R_E1], rec[R_RANK1])
    row2 = row_of(rec[R_E2], rec[R_RANK2])
    max_tiles = (2 * T) // tme + N_EXPERTS
    tile_e = jnp.sum(jnp.arange(max_tiles, dtype=jnp.int32)[:, None] >= tile_end[None, :], axis=1)
    tile_e = jnp.minimum(tile_e, N_EXPERTS - 1).astype(jnp.int32)
    n_used = tile_end[-1:].astype(jnp.int32)
    n_rows = max_tiles * tme
    spare = n_used + jnp.arange(N_EXPERTS, dtype=jnp.int32)
    clear_rows = jnp.concatenate([last_tile_row, jnp.where(spare < max_tiles, spare * tme, -1)]).astype(jnp.int32)

    xs = pl.pallas_call(
        _dispatch_kernel,
        grid_spec=pltpu.PrefetchScalarGridSpec(
            num_scalar_prefetch=3,
            grid=(T // tmd,),
            in_specs=[pl.BlockSpec(memory_space=pl.ANY)],
            out_specs=pl.BlockSpec(memory_space=pl.ANY),
            scratch_shapes=[pltpu.VMEM((DISPATCH_SLOTS, tmd * ROW_SUBLANES, LANES), F32),
                            pltpu.VMEM((tme * ROW_SUBLANES, LANES), F32),
                            pltpu.SemaphoreType.DMA((DISPATCH_SLOTS,)), pltpu.SemaphoreType.DMA((DISPATCH_SLOTS,)),
                            pltpu.SemaphoreType.DMA(())],
        ),
        out_shape=jax.ShapeDtypeStruct((n_rows * ROW_SUBLANES, LANES), F32),
        compiler_params=_cparams(("arbitrary",)),
        name="dispatch",
    )(row1, row2, clear_rows, x1)

    g2 = norm_ffn_g.astype(F32)[None, :]
    w_spec = lambda shape: pl.BlockSpec((1, 1) + shape, lambda k, te, nu: (layer, te[k], 0, 0))
    ys = pl.pallas_call(
        _experts_kernel,
        grid_spec=pltpu.PrefetchScalarGridSpec(
            num_scalar_prefetch=2,
            grid=(max_tiles,),
            in_specs=[pl.BlockSpec((tme * ROW_SUBLANES, LANES), lambda k, te, nu: (jnp.minimum(k, nu[0] - 1), 0)),
                      pl.BlockSpec((1, D), lambda k, te, nu: (0, 0)),
                      w_spec((D, D_EXPERT)), w_spec((D, D_EXPERT)), w_spec((D_EXPERT, D))],
            out_specs=pl.BlockSpec((tme * ROW_SUBLANES, LANES), lambda k, te, nu: (k, 0)),
            scratch_shapes=[pltpu.VMEM((D, D_EXPERT), BF16), pltpu.VMEM((D, D_EXPERT), BF16),
                            pltpu.VMEM((D_EXPERT, D), BF16)],
        ),
        out_shape=jax.ShapeDtypeStruct((n_rows * ROW_SUBLANES, LANES), F32),
        compiler_params=_cparams(("arbitrary",)),
        name="experts",
    )(tile_e, n_used, xs, g2, w_gate, w_up, w_down)

    return pl.pallas_call(
        _combine_kernel,
        grid_spec=pltpu.PrefetchScalarGridSpec(
            num_scalar_prefetch=2,
            grid=(T // tmd,),
            in_specs=[pl.BlockSpec((tmd * ROW_SUBLANES, LANES), lambda t, r1, r2: (t, 0)),
                      pl.BlockSpec((tmd, LANES), lambda t, r1, r2: (t, 0)),
                      pl.BlockSpec(memory_space=pl.ANY)],
            out_specs=pl.BlockSpec((1, tmd, D), lambda t, r1, r2: (t // (S // tmd), t % (S // tmd), 0)),
            scratch_shapes=[pltpu.VMEM((2, tmd * ROW_SUBLANES, LANES), F32), pltpu.VMEM((2, tmd * ROW_SUBLANES, LANES), F32),
                            pltpu.SemaphoreType.DMA((2,))],
        ),
        out_shape=jax.ShapeDtypeStruct((B, S, D), F32),
        compiler_params=_cparams(("arbitrary",)),
        name="combine",
    )(row1, row2, x1, route, ys)


def kernel(x, norm_mix_g, w_in, b_forget, fox_q_g, fox_k_g, nsa_q_g, nsa_k_g, cmp_k_w1, cmp_k_w2, cmp_k_pos,
           cmp_v_w1, cmp_v_w2, cmp_v_pos, w_fox_up, w_nsa_up, w_out, norm_ffn_g, w_group, b_group, w_router,
           b_router, w_gate, w_up, w_down):
    B, S, D = x.shape
    for l in range(w_in.shape[0]):
        qa, ka, va, qn, kcr, vcr, ksa, vsa, kwa, vwa, gt, gab = _inproj(
            x, norm_mix_g[l], w_in[l], b_forget[l], fox_q_g[l], fox_k_g[l], nsa_q_g[l], nsa_k_g[l])
        kca, vca = _compress(kcr, vcr, cmp_k_w1[l], cmp_k_w2[l], cmp_k_pos[l],
                             cmp_v_w1[l], cmp_v_w2[l], cmp_v_pos[l], nsa_k_g[l])
        ob = _nsa(qn, kca, vca, ksa, vsa, kwa, vwa, gt, _scores_bounded(nsa_q_g[l], nsa_k_g[l]))
        oa = _fox(qa, ka, va, _scores_bounded(fox_q_g[l], fox_k_g[l]))
        x1, route, cnt = _merge(x, oa, ob, gab, w_fox_up[l], w_nsa_up[l], w_out[l], norm_ffn_g[l],
                                w_group[l], b_group[l], w_router[l], b_router[l])
        x = _moe(x1, route, cnt, norm_ffn_g[l], w_gate, w_up, w_down, l, B, S)
    return x
```

```python
import functools

import jax
import jax.numpy as jnp
import numpy as np
from jax import lax
from jax.experimental import pallas as pl
from jax.experimental.pallas import tpu as pltpu

F32 = jnp.float32
BF16 = jnp.bfloat16

D_MODEL = 1024
HEAD_DIM = 64
FOX_HEADS = 8
NSA_HEADS = 8
NSA_KV_HEADS = 2
NSA_REP = NSA_HEADS // NSA_KV_HEADS
CMP_BLOCK = 32
CMP_STRIDE = 16
CMP_HIDDEN = 256
SLC_BLOCK = 64
SLC_TOPK = 16
WINDOW = 512
N_GROUPS = 4
EXPERTS_PER_GROUP = 4
N_EXPERTS = N_GROUPS * EXPERTS_PER_GROUP
D_EXPERT = 512
RMS_EPS = 1e-6
NEG_INF = -1e30
FORCE_SCORE = 1e4

LANES = 128
HALF = LANES // 2
VMEM_LIMIT = 56 * 1024 * 1024

FOX_W = FOX_HEADS * HEAD_DIM
NSA_W = NSA_HEADS * HEAD_DIM
NSA_KV_W = NSA_KV_HEADS * HEAD_DIM

OFF_FQ = 0
OFF_FK = OFF_FQ + FOX_W
OFF_FV = OFF_FK + FOX_W
OFF_NQ = OFF_FV + FOX_W
OFF_KC = OFF_NQ + NSA_W
OFF_VC = OFF_KC + NSA_KV_W
OFF_KS = OFF_VC + NSA_KV_W
OFF_VS = OFF_KS + NSA_KV_W
OFF_KW = OFF_VS + NSA_KV_W
OFF_VW = OFF_KW + NSA_KV_W
OFF_FF = OFF_VW + NSA_KV_W
OFF_NG = OFF_FF + LANES
OFF_GA = OFF_NG + LANES
N_PROJ = OFF_GA + 2 * D_MODEL

TM_PROJ = 512
TQ_FOX = 512
FOX_UNROLL = 4
TQ_NSA = 256
TK_SLC = 256
TM_MERGE = 512
TM_EXPERT = 512
TM_DISPATCH = 512


def _dot(a, b):
    return jnp.dot(a, b, preferred_element_type=F32)


def _dot_nt(a, b):
    return lax.dot_general(a, b, (((1,), (1,)), ((), ())), preferred_element_type=F32)


def _split3(x):
    hi = x.astype(BF16)
    r = x - hi.astype(F32)
    mid = r.astype(BF16)
    lo = (r - mid.astype(F32)).astype(BF16)
    return hi, mid, lo


def _cparams(sem):
    return pltpu.CompilerParams(dimension_semantics=sem, vmem_limit_bytes=VMEM_LIMIT)


def _const_spec(shape):
    nd = len(shape)
    return pl.BlockSpec(shape, lambda *_: (0,) * nd)


def _inproj_kernel(x_ref, g_ref, w_ref, bf_ref, gqa_ref, gka_ref, gqn_ref, gkn_ref,
                   bd_ref, bd2_ref, tri_ref, cq_ref, ck_ref, cv_ref,
                   qc0_ref, qca_ref, qcb_ref, kc0_ref, kca_ref, kcb_ref, vone_ref,
                   qa_ref, ka_ref, va_ref, qn_ref, kcr_ref, vcr_ref,
                   ksa_ref, vsa_ref, kwa_ref, vwa_ref, gt_ref, gab_ref,
                   carry_ref):
    tm = x_ref.shape[1]

    @pl.when(pl.program_id(1) == 0)
    def _():
        carry_ref[...] = jnp.zeros_like(carry_ref)

    x = x_ref[0]
    y = x * lax.rsqrt(jnp.mean(x * x, axis=-1, keepdims=True) + RMS_EPS)
    h = (y * g_ref[...]).astype(BF16)

    def proj(off, n):
        return _dot(h, w_ref[:, off:off + n])

    lo_half = lax.broadcasted_iota(jnp.int32, (tm, LANES), 1) < HALF

    pos = pl.program_id(1) * tm + lax.broadcasted_iota(jnp.int32, (tm, 1), 0)
    pos_a = ((pos >> 8) << 8).astype(F32)
    pos_b = (pos & 255).astype(F32)

    def pos_channels(c0_ref, ca_ref, cb_ref, k):
        blk = slice(LANES * k, LANES * (k + 1))
        return c0_ref[:, blk] + ca_ref[:, blk] * pos_a + cb_ref[:, blk] * pos_b

    def headnorm(z, bd, grow):
        msq = _dot((z * z).astype(BF16), bd[...])
        return z * lax.rsqrt(msq + RMS_EPS) * grow[...]

    def spread_pairs(out_ref, src, aug):
        for m in range(4):
            s = src[:, LANES * m:LANES * (m + 1)]
            out_ref[0, :, LANES * 2 * m:LANES * (2 * m + 1)] = jnp.where(lo_half, s, aug(2 * m)).astype(BF16)
            out_ref[0, :, LANES * (2 * m + 1):LANES * (2 * m + 2)] = jnp.where(lo_half, aug(2 * m + 1), s).astype(BF16)

    zf = proj(OFF_FF, LANES) + bf_ref[...]
    logf = jnp.minimum(zf, 0.0) - jnp.log(1.0 + jnp.exp(-jnp.abs(zf)))
    l_hi, l_mid, l_lo = _split3(logf)
    tri = tri_ref[...]
    cum = carry_ref[...] + (_dot(tri, l_hi) + _dot(tri, l_mid) + _dot(tri, l_lo))
    carry_ref[...] = cum[tm - 1:tm, :]
    pieces = [p.astype(F32) for p in _split3(cum)]
    lane128 = lax.broadcasted_iota(jnp.int32, (tm, LANES), 1)

    def aug_block(k, const_ref, first, sign):
        base = HALF if k % 2 == 0 else 0
        blk = jnp.broadcast_to(const_ref[:, LANES * k:LANES * (k + 1)], (tm, LANES))
        for j, piece in enumerate(pieces):
            blk = jnp.where(lane128 == base + first + j, sign * piece[:, k:k + 1], blk)
        return blk

    augq = lambda k: aug_block(k, cq_ref, 3, 1.0)
    augk = lambda k: aug_block(k, ck_ref, 0, -1.0)

    zq = headnorm(proj(OFF_FQ, FOX_W), bd_ref, gqa_ref)
    spread_pairs(qa_ref, zq, augq)
    zk = headnorm(proj(OFF_FK, FOX_W), bd_ref, gka_ref)
    spread_pairs(ka_ref, zk, augk)
    zv = proj(OFF_FV, FOX_W)
    spread_pairs(va_ref, zv, lambda k: cv_ref[:, LANES * k:LANES * (k + 1)])

    zn = headnorm(proj(OFF_NQ, NSA_W), bd_ref, gqn_ref)
    for m in range(NSA_REP):
        s = zn[:, LANES * m:LANES * (m + 1)]
        c0 = pos_channels(qc0_ref, qca_ref, qcb_ref, m)
        c1 = pos_channels(qc0_ref, qca_ref, qcb_ref, NSA_REP + m)
        qn_ref[0, :, LANES * m:LANES * (m + 1)] = jnp.where(lo_half, s, c0).astype(BF16)
        qn_ref[0, :, LANES * (NSA_REP + m):LANES * (NSA_REP + m + 1)] = jnp.where(lo_half, c1, s).astype(BF16)

    kcr_ref[0] = proj(OFF_KC, NSA_KV_W)
    vcr_ref[0] = proj(OFF_VC, NSA_KV_W)

    kp0 = pos_channels(kc0_ref, kca_ref, kcb_ref, 0)
    kp1 = pos_channels(kc0_ref, kca_ref, kcb_ref, 1)

    def kv_pair(k_out, v_out, off_k, off_v):
        zk2 = headnorm(proj(off_k, NSA_KV_W), bd2_ref, gkn_ref)
        k_out[0, :, 0:LANES] = jnp.where(lo_half, zk2, kp0).astype(BF16)
        k_out[0, :, LANES:2 * LANES] = jnp.where(lo_half, kp1, zk2).astype(BF16)
        zv2 = proj(off_v, NSA_KV_W)
        v_out[0, :, 0:LANES] = jnp.where(lo_half, zv2, vone_ref[:, 0:LANES]).astype(BF16)
        v_out[0, :, LANES:2 * LANES] = jnp.where(lo_half, vone_ref[:, LANES:2 * LANES], zv2).astype(BF16)

    kv_pair(ksa_ref, vsa_ref, OFF_KS, OFF_VS)
    kv_pair(kwa_ref, vwa_ref, OFF_KW, OFF_VW)

    gt_ref[0] = jax.nn.sigmoid(proj(OFF_NG, LANES))
    gab_ref[0, :, 0:D_MODEL] = jax.nn.sigmoid(proj(OFF_GA, D_MODEL)).astype(BF16)
    gab_ref[0, :, D_MODEL:2 * D_MODEL] = jax.nn.sigmoid(proj(OFF_GA + D_MODEL, D_MODEL)).astype(BF16)


def _pos_pieces(pos):
    return ((pos // 256) * 256).astype(np.float32), (pos % 256).astype(np.float32)


def _inproj(x, norm_g, w_in, b_forget, fox_q_g, fox_k_g, nsa_q_g, nsa_k_g):
    B, S, D = x.shape
    tm = min(TM_PROJ, S)
    scale = HEAD_DIM ** -0.5

    c = [0]
    for n in (FOX_W, FOX_W, FOX_W, FOX_HEADS, NSA_W) + (NSA_KV_W,) * 6 + (3 * NSA_HEADS, D_MODEL, D_MODEL):
        c.append(c[-1] + n)
    fq, fk, fv, ff, nq, kc, vc, ks, vs, kw, vw, ng, ga, gb = [w_in[:, c[i]:c[i + 1]] for i in range(14)]
    perm = jnp.asarray([0, 4, 1, 5, 2, 6, 3, 7])
    nq = nq.reshape(D, NSA_HEADS, HEAD_DIM)[:, perm, :].reshape(D, NSA_W)
    padl = lambda a: jnp.pad(a, ((0, 0), (0, LANES - a.shape[1])))
    w = jnp.concatenate([fq, fk, fv, nq, kc, vc, ks, vs, kw, vw, padl(ff), padl(ng), ga, gb], axis=1).astype(BF16)
    assert w.shape[1] == N_PROJ

    bf = jnp.pad(b_forget.astype(F32), (0, LANES - FOX_HEADS))[None, :]
    gqa = jnp.tile(fox_q_g.astype(F32) * scale, FOX_HEADS)[None, :]
    gka = jnp.tile(fox_k_g.astype(F32), FOX_HEADS)[None, :]
    gqn = jnp.tile(nsa_q_g.astype(F32) * scale, NSA_HEADS)[None, :]
    gkn = jnp.tile(nsa_k_g.astype(F32), NSA_KV_HEADS)[None, :]

    r512 = np.arange(FOX_W)
    bd = np.where((r512[:, None] // HEAD_DIM) == (r512[None, :] // HEAD_DIM), 1.0 / HEAD_DIM, 0.0).astype(BF16)
    bd2 = bd[:LANES, :LANES]
    rt = np.arange(tm)
    tri = (rt[None, :] <= rt[:, None]).astype(BF16)

    heads = np.arange(FOX_HEADS)
    base = heads * LANES + np.where(heads % 2 == 0, HALF, 0)
    cols = np.arange(FOX_HEADS * LANES)
    off_in_blk = cols - base[cols // LANES]
    cq = ((off_in_blk >= 0) & (off_in_blk < 3)).astype(F32)[None, :]
    ck = ((off_in_blk >= 3) & (off_in_blk < 9)).astype(F32)[None, :]
    cv = (off_in_blk == 0).astype(F32)[None, :]

    blk = cols // LANES
    slope = 2.0 ** (-(blk + 1).astype(F32))
    o = cols % LANES - np.where(blk // NSA_REP == 0, HALF, 0)
    qc0 = np.where((o == 0) | (o == 1), slope, 0.0)[None, :]
    qca = np.where(o == 2, -slope, 0.0)[None, :]
    qcb = np.where(o == 3, -slope, 0.0)[None, :]
    kc0, kca, kcb, vone = _kv_rows()

    grid = (B, S // tm)
    row_spec = lambda n: pl.BlockSpec((1, tm, n), lambda b, s: (b, s, 0))
    consts = [norm_g.astype(F32)[None, :], w, bf, gqa, gka, gqn, gkn, bd, bd2, tri, cq, ck, cv,
              qc0, qca, qcb, kc0, kca, kcb, vone]
    out_widths = [(8 * LANES, BF16)] * 4 + [(LANES, F32)] * 2 + [(2 * LANES, BF16)] * 4 + \
                 [(LANES, F32), (2 * D_MODEL, BF16)]
    outs = pl.pallas_call(
        _inproj_kernel,
        grid=grid,
        in_specs=[row_spec(D)] + [_const_spec(a.shape) for a in consts],
        out_specs=[row_spec(n) for n, _ in out_widths],
        out_shape=[jax.ShapeDtypeStruct((B, S, n), dt) for n, dt in out_widths],
        scratch_shapes=[pltpu.VMEM((1, LANES), F32)],
        compiler_params=_cparams(("arbitrary", "arbitrary")),
        name="inproj",
    )(x, *consts)
    return outs


def _kv_rows():
    cols = np.arange(2 * LANES)
    o = cols % LANES - np.where(cols // LANES == 0, HALF, 0)
    row = lambda m: m.astype(np.float32)[None, :]
    return row((o >= 2) & (o <= 6)), row(o == 0), row(o == 1), row(o == 0)


def _kv_consts(pa, pb):
    c0, ca, cb, vone = _kv_rows()
    return (c0 + ca * pa[:, None] + cb * pb[:, None]).astype(BF16), vone


def _compress_kernel(kt_ref, vt_ref, w1k_ref, w1v_ref, posk_ref, posv_ref, pw1k_ref, pw1v_ref, w2k_ref, w2v_ref,
                     gk_ref, kcc_ref, vone_ref, kc_ref, vc_ref):
    nc = kc_ref.shape[2]

    def mlp(t_ref, w1_ref, pos_ref, pw1_ref, w2_ref):
        both = jnp.zeros((nc, 2 * CMP_HIDDEN), F32)
        for l in range(CMP_STRIDE):
            rows = t_ref[0, pl.ds(l, nc, stride=CMP_STRIDE), :].astype(BF16)
            both = both + _dot(rows, w1_ref[0, l])
        posw = _dot(pos_ref[...], pw1_ref[...])[0:1, :]
        pre = both[:, 0:CMP_HIDDEN] + pltpu.roll(both[:, CMP_HIDDEN:2 * CMP_HIDDEN], nc - 1, axis=0) + posw
        act = pre * (0.5 * (1.0 + jnp.tanh(0.7978845608028654 * (pre + 0.044715 * (pre * pre * pre)))))
        return _dot(act.astype(BF16), w2_ref[0])

    kc = mlp(kt_ref, w1k_ref, posk_ref, pw1k_ref, w2k_ref)
    msq = jnp.sum(kc * kc, axis=-1, keepdims=True) * (1.0 / HEAD_DIM)
    kc_ref[0, 0] = (kc * lax.rsqrt(msq + RMS_EPS) * gk_ref[0] + kcc_ref[0]).astype(BF16)
    vc = mlp(vt_ref, w1v_ref, posv_ref, pw1v_ref, w2v_ref)
    vc_ref[0, 0] = (vc + vone_ref[0]).astype(BF16)


def _compress(kcr, vcr, cmp_k_w1, cmp_k_w2, cmp_k_pos, cmp_v_w1, cmp_v_w2, cmp_v_pos, nsa_k_g):
    B, S, _ = kcr.shape
    G = NSA_KV_HEADS
    nc = S // CMP_STRIDE

    def w1_strided(w1):
        w = w1.reshape(2, CMP_STRIDE, HEAD_DIM, CMP_HIDDEN)
        w = jnp.concatenate([w[0], w[1]], axis=-1)
        z = jnp.zeros_like(w)
        return jnp.stack([jnp.concatenate([w, z], axis=1), jnp.concatenate([z, w], axis=1)]).astype(BF16)

    def w2_spread(w2):
        z = jnp.zeros_like(w2)
        return jnp.stack([jnp.concatenate([w2, z], 1), jnp.concatenate([z, w2], 1)]).astype(BF16)

    def pos8(p):
        return jnp.tile(p.reshape(1, CMP_BLOCK * HEAD_DIM), (8, 1)).astype(BF16)

    gk = nsa_k_g.astype(F32)
    z = jnp.zeros_like(gk)
    gk2 = jnp.stack([jnp.concatenate([gk, z]), jnp.concatenate([z, gk])])[:, None, :]
    cend = np.arange(nc) * CMP_STRIDE + CMP_BLOCK - 1
    kcc, vone = _kv_consts(*_pos_pieces(cend))
    kcc = kcc.astype(F32).reshape(nc, G, LANES).transpose(1, 0, 2)
    vone = vone.reshape(G, 1, LANES)

    tok = pl.BlockSpec((1, S, G * HEAD_DIM), lambda b, g: (b, 0, 0))
    per_g = lambda a: pl.BlockSpec((1,) + a.shape[1:], lambda b, g: (g,) + (0,) * (a.ndim - 1))
    w1k, w1v = w1_strided(cmp_k_w1), w1_strided(cmp_v_w1)
    pk, pv = pos8(cmp_k_pos), pos8(cmp_v_pos)
    pw1k, pw1v = cmp_k_w1.astype(BF16), cmp_v_w1.astype(BF16)
    w2k, w2v = w2_spread(cmp_k_w2), w2_spread(cmp_v_w2)
    out_spec = pl.BlockSpec((1, 1, nc, LANES), lambda b, g: (b, g, 0, 0))
    return pl.pallas_call(
        _compress_kernel,
        grid=(B, G),
        in_specs=[tok, tok, per_g(w1k), per_g(w1v), _const_spec(pk.shape), _const_spec(pv.shape),
                  _const_spec(pw1k.shape), _const_spec(pw1v.shape), per_g(w2k), per_g(w2v),
                  per_g(gk2), per_g(kcc), per_g(vone)],
        out_specs=[out_spec, out_spec],
        out_shape=[jax.ShapeDtypeStruct((B, G, nc, LANES), BF16)] * 2,
        compiler_params=_cparams(("arbitrary", "arbitrary")),
        name="compress",
    )(kcr, vcr, w1k, w1v, pk, pv, pw1k, pw1v, w2k, w2v, gk2, kcc, vone)


MAX_EXPONENT = 60.0


def _scores_bounded(q_g, k_g):
    bound = HEAD_DIM ** 0.5 * jnp.max(jnp.abs(q_g)) * jnp.max(jnp.abs(k_g))
    return 2.04 * bound + 0.05 <= MAX_EXPONENT


def _attn_first(s, v):
    m = jnp.max(s, axis=-1, keepdims=True)
    p = jnp.exp((s - m).astype(BF16))
    return m, _dot(p, v)


def _attn_step(carry, s, v):
    m, acc = carry
    m_new = jnp.maximum(m, jnp.max(s, axis=-1, keepdims=True))
    p = jnp.exp((s - m_new).astype(BF16))
    return m_new, jnp.exp(m - m_new) * acc + _dot(p, v)


def _nsa_kernel(bounded, q_ref, kc_ref, vc_ref, ks_ref, e_ref, vs_ref, kw_ref, vw_ref, gt_ref, ov_ref, o_ref,
                tiles_ref, sa_ref, sb_ref):
    tq = q_ref.shape[1]
    nc = kc_ref.shape[2]
    tk = TK_SLC
    rq = NSA_REP * tq
    g = pl.program_id(1)
    i = pl.program_id(2)
    q0 = i * tq

    q4 = jnp.concatenate([q_ref[0, :, LANES * r:LANES * (r + 1)] for r in range(NSA_REP)], axis=0)

    def qpos_of(shape):
        return q0 + (lax.broadcasted_iota(jnp.int32, shape, 0) & (tq - 1))

    def add_mask(s, valid):
        bias = jnp.where(valid, 0.0, NEG_INF)
        return (s.reshape(NSA_REP, tq, s.shape[1]) + bias[None]).reshape(s.shape)

    def qrow(n):
        return q0 + lax.broadcasted_iota(jnp.int32, (tq, n), 0)

    def kcol(n):
        return lax.broadcasted_iota(jnp.int32, (tq, n), 1)

    anyv = qpos_of((rq, 1)) >= CMP_BLOCK - 1

    def cmp_branch(n):
        s = add_mask(_dot_nt(q4, kc_ref[0, 0, 0:n, :]), qrow(n) >= kcol(n) * CMP_STRIDE + (CMP_BLOCK - 1))
        if bounded:
            e = jnp.exp(s)
            p = e * jnp.where(anyv, 1.0 / jnp.sum(e, axis=-1, keepdims=True), 0.0)
        else:
            m = jnp.max(s, axis=-1, keepdims=True)
            e = jnp.exp(s - m)
            p = e * (anyv.astype(F32) / jnp.sum(e, axis=-1, keepdims=True))
        o = _dot(p.astype(BF16), vc_ref[0, 0, 0:n, :])
        psum = p[0:tq] + p[tq:2 * tq] + p[2 * tq:3 * tq] + p[3 * tq:4 * tq]
        ov = ov_ref[0:n, :]
        return o, functools.reduce(lambda a, b: a + b, [_dot(piece, ov) for piece in _split3(psum)])

    last_visible = (q0 + tq - CMP_BLOCK) // CMP_STRIDE
    o_cmp, imp = lax.cond(last_visible < nc // 2, lambda: cmp_branch(nc // 2), lambda: cmp_branch(nc))

    blk_i = lax.broadcasted_iota(jnp.int32, (LANES, tq), 0)
    blk_f = blk_i.astype(F32)
    qblk = (q0 + lax.broadcasted_iota(jnp.int32, (LANES, tq), 1)) // SLC_BLOCK
    forced = (blk_i == 0) | (blk_i == qblk) | (blk_i == qblk - 1)
    score = jnp.where(forced, -3e38, jnp.where(blk_i > qblk, -1.0, imp.T))
    selb = jnp.where(forced, 0.0, NEG_INF)
    for _ in range(SLC_TOPK - 3):
        mx = jnp.max(score, axis=0, keepdims=True)
        first = jnp.min(jnp.where(score == mx, blk_f, float(LANES)), axis=0, keepdims=True)
        hit = blk_f == first
        selb = jnp.where(hit, 0.0, selb)
        score = jnp.where(hit, -3e38, score)
    sb = selb.T.astype(BF16)
    sb4 = jnp.concatenate([sb] * NSA_REP, axis=0)

    def slc_qk(j):
        start = pl.multiple_of(j * tk, tk)
        kk = jnp.concatenate([ks_ref[0, pl.ds(start, tk), :], e_ref[pl.ds(start, tk), :]], axis=1)
        return _dot_nt(q2, kk)

    def slc_v(j):
        return vs_ref[0, pl.ds(pl.multiple_of(j * tk, tk), tk), :]

    jd = q0 // tk
    blocks_per_tile = tk // SLC_BLOCK
    anysel = jnp.max(selb, axis=1, keepdims=True)
    cnt = jnp.int32(0)
    for j in range(ks_ref.shape[1] // tk):
        tile_sel = jnp.max(anysel[blocks_per_tile * j:blocks_per_tile * (j + 1), :]) > -1.0
        tiles_ref[cnt] = j
        cnt = cnt + (tile_sel & (j < jd)).astype(jnp.int32)

    lane_r = lax.broadcasted_iota(jnp.int32, (rq, LANES), 1)
    data0 = HALF * g
    ones_lane = HALF - data0
    nw = WINDOW + tq
    wstart = pl.multiple_of(jnp.maximum(q0 - WINDOW, 0), tq)
    win_valid = lax.bitcast_convert_type(qrow(nw) - (wstart + kcol(nw)), jnp.uint32) < WINDOW
    diag_valid = jd * tk + kcol(tk) <= qrow(tk)

    if bounded:
        def shifted_by_own_score(kself):
            own = jnp.sum(q4.astype(F32).reshape(NSA_REP, tq, LANES) * kself.astype(F32)[None],
                          axis=-1, keepdims=True).reshape(rq, 1)
            qf = q4.astype(F32)
            for off, piece in enumerate(_split3(-own)):
                qf = jnp.where(lane_r == ones_lane + 4 + off, piece.astype(F32), qf)
            return qf.astype(BF16)

        q2 = jnp.concatenate([shifted_by_own_score(ks_ref[0, pl.ds(q0, tq), :]), sb4], axis=1)
        sa_ref[:, 0:LANES] = _dot(jnp.exp(add_mask(slc_qk(jd), diag_valid)).astype(BF16), slc_v(jd))

        def slc_pv(t):
            j = tiles_ref[t]
            return _dot(jnp.exp(slc_qk(j)).astype(BF16), slc_v(j))

        def slc_body(p, _):
            sa_ref[:, 0:LANES] += slc_pv(2 * p) + slc_pv(2 * p + 1)
            return _

        lax.fori_loop(0, cnt // 2, slc_body, 0)

        @pl.when(cnt % 2 == 1)
        def _():
            sa_ref[:, 0:LANES] += slc_pv(cnt - 1)

        acc_slc = sa_ref[:, 0:LANES]

        qw = shifted_by_own_score(kw_ref[0, pl.ds(q0, tq), :])
        s = add_mask(_dot_nt(qw, kw_ref[0, pl.ds(wstart, nw), :]), win_valid)
        acc_win = _dot(jnp.exp(s).astype(BF16), vw_ref[0, pl.ds(wstart, nw), :])
    else:
        q2 = jnp.concatenate([q4, sb4], axis=1)
        n_seq = cnt + 1

        def tile_of(n):
            return jnp.where(n == 0, jd, tiles_ref[jnp.maximum(n - 1, 0)])

        def slc_step(carry, s_buf, n):
            return _attn_step(carry, s_buf[...], slc_v(tile_of(n)))

        sa_ref[...] = add_mask(slc_qk(jd), diag_valid)

        def slc_body(p, carry):
            sb_ref[...] = slc_qk(tile_of(2 * p + 1))
            carry = slc_step(carry, sa_ref, 2 * p)
            sa_ref[...] = slc_qk(tile_of(jnp.minimum(2 * p + 2, n_seq - 1)))
            return slc_step(carry, sb_ref, 2 * p + 1)

        carry = (jnp.full((rq, 1), NEG_INF, F32), jnp.zeros((rq, LANES), F32))
        carry = lax.fori_loop(0, n_seq // 2, slc_body, carry)
        _, acc_slc = lax.cond(n_seq % 2 == 1, lambda c: slc_step(c, sa_ref, n_seq - 1), lambda c: c, carry)

        s = add_mask(_dot_nt(q4, kw_ref[0, pl.ds(wstart, nw), :]), win_valid)
        _, acc_win = _attn_first(s, vw_ref[0, pl.ds(wstart, nw), :])


    def normalise(acc):
        l = jnp.sum(jnp.where(lane_r == ones_lane, acc, 0.0), axis=-1, keepdims=True)
        return acc * (1.0 / l)

    o_slc = normalise(acc_slc)
    o_win = normalise(acc_win)
    gt = gt_ref[0]
    lane_g = lax.broadcasted_iota(jnp.int32, (tq, LANES), 1)
    is_data_q = (lane_g >= data0) & (lane_g < data0 + HALF)
    for r in range(NSA_REP):
        col = 3 * (NSA_REP * g + r)
        gate = [jnp.sum(jnp.where(lane_g == col + b, gt, 0.0), axis=-1, keepdims=True) for b in range(3)]
        rows = slice(r * tq, (r + 1) * tq)
        o = gate[0] * o_cmp[rows] + gate[1] * o_slc[rows] + gate[2] * o_win[rows]
        o_ref[0, :, LANES * r:LANES * (r + 1)] = jnp.where(is_data_q, o, 0.0).astype(BF16)


def _nsa(qn, kca, vca, ksa, vsa, kwa, vwa, gt, scores_bounded):
    B, S, _ = qn.shape
    G = NSA_KV_HEADS
    tq = min(TQ_NSA, S)
    nc = S // CMP_STRIDE
    n_slc = S // SLC_BLOCK
    assert n_slc <= LANES and S % TK_SLC == 0

    cs = np.arange(nc)[:, None] * CMP_STRIDE
    ss = np.arange(LANES)[None, :] * SLC_BLOCK
    ovl = np.clip(np.minimum(cs + CMP_BLOCK, ss + SLC_BLOCK) - np.maximum(cs, ss), 0, None)
    valid = (np.arange(nc)[:, None] < (S - CMP_BLOCK) // CMP_STRIDE + 1) & (np.arange(LANES)[None, :] < n_slc)
    ov = np.where(valid, ovl.astype(F32) / CMP_BLOCK, 0.0).astype(BF16)
    e1h = (np.arange(S)[:, None] // SLC_BLOCK == np.arange(LANES)[None, :]).astype(BF16)

    q_spec = pl.BlockSpec((1, tq, NSA_REP * LANES), lambda b, g, i: (b, i, g))
    c_spec = pl.BlockSpec((1, 1, nc, LANES), lambda b, g, i: (b, g, 0, 0))
    kv_spec = pl.BlockSpec((1, S, LANES), lambda b, g, i: (b, 0, g))

    def call(bounded, name):
        return pl.pallas_call(
            functools.partial(_nsa_kernel, bounded),
            grid=(B, G, S // tq),
            in_specs=[q_spec, c_spec, c_spec, kv_spec, _const_spec(e1h.shape), kv_spec, kv_spec, kv_spec,
                      pl.BlockSpec((1, tq, LANES), lambda b, g, i: (b, i, 0)), _const_spec(ov.shape)],
            out_specs=q_spec,
            out_shape=jax.ShapeDtypeStruct((B, S, NSA_HEADS * LANES), BF16),
            scratch_shapes=[pltpu.SMEM((S // TK_SLC + 1,), jnp.int32),
                            pltpu.VMEM((NSA_REP * tq, TK_SLC), F32), pltpu.VMEM((NSA_REP * tq, TK_SLC), F32)],
            compiler_params=_cparams(("arbitrary", "arbitrary", "arbitrary")),
            name=name,
        )

    return lax.cond(scores_bounded, call(True, "nsa_bounded"), call(False, "nsa"),
                    qn, kca, vca, ksa, e1h, vsa, kwa, vwa, gt, ov)


def _fox_kernel(q_ref, k_ref, v_ref, o_ref, sa_ref, sb_ref):
    tq = q_ref.shape[1]
    i = pl.program_id(2)
    lane = lax.broadcasted_iota(jnp.int32, (tq, LANES), 1)
    causal = lax.broadcasted_iota(jnp.int32, (tq, tq), 1) <= lax.broadcasted_iota(jnp.int32, (tq, tq), 0)

    def cols(hh):
        return slice(LANES * hh, LANES * (hh + 1))

    def qk(hh, j):
        start = pl.multiple_of(j * tq, tq)
        return _dot_nt(q_ref[0, :, cols(hh)], k_ref[0, pl.ds(start, tq), cols(hh)])

    def vtile(hh, j):
        return v_ref[0, pl.ds(pl.multiple_of(j * tq, tq), tq), cols(hh)]

    n_tiles = i + 1

    def tile_of(n):
        return jnp.where(n == 0, i, n - 1)

    def step_all(carry, s_buf, n):
        t = tile_of(n)
        return tuple(_attn_step(carry[hh], s_buf[hh], vtile(hh, t)) for hh in range(2))

    for hh in range(2):
        sa_ref[hh] = jnp.where(causal, qk(hh, i), NEG_INF)
    init = tuple((jnp.full((tq, 1), NEG_INF, F32), jnp.zeros((tq, LANES), F32)) for hh in range(2))

    def body(p, carry):
        for hh in range(2):
            sb_ref[hh] = qk(hh, tile_of(2 * p + 1))
        carry = step_all(carry, sa_ref, 2 * p)
        nxt = jnp.minimum(2 * p + 2, n_tiles - 1)
        for hh in range(2):
            sa_ref[hh] = qk(hh, tile_of(nxt))
        return step_all(carry, sb_ref, 2 * p + 1)

    carry = lax.fori_loop(0, n_tiles // 2, body, init)
    carry = lax.cond(n_tiles % 2 == 1, lambda c: step_all(c, sa_ref, n_tiles - 1), lambda c: c, carry)
    outs = []
    for hh in range(2):
        acc = carry[hh][1]
        ones_lane = HALF if hh == 0 else 0
        l = jnp.sum(jnp.where(lane == ones_lane, acc, 0.0), axis=-1, keepdims=True)
        outs.append(acc * (1.0 / l))
    o_ref[0] = jnp.where(lane < HALF, outs[0], outs[1]).astype(BF16)


def _fox_bounded_kernel(q_ref, k_ref, v_ref, o_ref, qs_ref, acc_ref):
    tq = q_ref.shape[1]
    i = pl.program_id(2)
    lane = lax.broadcasted_iota(jnp.int32, (tq, LANES), 1)
    causal = lax.broadcasted_iota(jnp.int32, (tq, tq), 1) <= lax.broadcasted_iota(jnp.int32, (tq, tq), 0)

    def cols(hh):
        return slice(LANES * hh, LANES * (hh + 1))

    def ktile(hh, j):
        return k_ref[0, pl.ds(pl.multiple_of(j * tq, tq), tq), cols(hh)]

    def vtile(hh, j):
        return v_ref[0, pl.ds(pl.multiple_of(j * tq, tq), tq), cols(hh)]

    for hh in range(2):
        q = q_ref[0, :, cols(hh)]
        s = jnp.where(causal, _dot_nt(q, ktile(hh, i)), NEG_INF)
        m = jnp.max(s, axis=-1, keepdims=True)
        acc_ref[hh] = _dot(jnp.exp((s - m).astype(BF16)), vtile(hh, i))
        qf = q.astype(F32)
        free0 = (HALF if hh == 0 else 0) + 6
        for off, piece in enumerate(_split3(-m)):
            qf = jnp.where(lane == free0 + off, piece.astype(F32), qf)
        qs_ref[hh] = qf.astype(BF16)

    def sweep(j0, n):
        for hh in range(2):
            pv = [_dot(jnp.exp(_dot_nt(qs_ref[hh], ktile(hh, j0 + t)).astype(BF16)), vtile(hh, j0 + t))
                  for t in range(n)]
            acc_ref[hh] += functools.reduce(lambda a, b: a + b, pv)

    def body(c, _):
        sweep(c * FOX_UNROLL, FOX_UNROLL)
        return _

    lax.fori_loop(0, i // FOX_UNROLL, body, 0)
    done = (i // FOX_UNROLL) * FOX_UNROLL
    n = FOX_UNROLL // 2
    while n >= 1:
        @pl.when((i & n) != 0)
        def _(n=n, start=done):
            sweep(start, n)
        done = done + (i & n)
        n //= 2
    outs = []
    for hh in range(2):
        acc = acc_ref[hh]
        ones_lane = HALF if hh == 0 else 0
        l = jnp.sum(jnp.where(lane == ones_lane, acc, 0.0), axis=-1, keepdims=True)
        outs.append(acc * (1.0 / l))
    o_ref[0] = jnp.where(lane < HALF, outs[0], outs[1]).astype(BF16)


def _fox(qa, ka, va, scores_bounded):
    B, S, _ = qa.shape
    tq = min(TQ_FOX, S)
    q_spec = pl.BlockSpec((1, tq, 2 * LANES), lambda b, h, i: (b, i, h))
    kv_spec = pl.BlockSpec((1, S, 2 * LANES), lambda b, h, i: (b, 0, h))

    def call(body, scratch, name):
        return pl.pallas_call(
            body,
            grid=(B, FOX_HEADS // 2, S // tq),
            in_specs=[q_spec, kv_spec, kv_spec],
            out_specs=pl.BlockSpec((1, tq, LANES), lambda b, h, i: (b, i, h)),
            out_shape=jax.ShapeDtypeStruct((B, S, FOX_W), BF16),
            scratch_shapes=scratch,
            compiler_params=_cparams(("arbitrary", "arbitrary", "arbitrary")),
            name=name,
        )

    general = call(_fox_kernel, [pltpu.VMEM((2, tq, tq), F32), pltpu.VMEM((2, tq, tq), F32)], "fox")
    bounded = call(_fox_bounded_kernel, [pltpu.VMEM((2, tq, LANES), BF16), pltpu.VMEM((2, tq, LANES), F32)],
                   "fox_bounded")
    return lax.cond(scores_bounded, bounded, general, qa, ka, va)


R_E1, R_E2, R_W1, R_W2, R_RANK1, R_RANK2 = range(6)


def _merge_kernel(x_ref, oa_ref, ob_ref, gab_ref, wa_ref, wb_ref, wo_ref, g2_ref, wr_hi_ref, wr_lo_ref, br_ref,
                  ltri_ref, x1_ref, route_ref, cnt_ref, carry_ref):
    tm = x_ref.shape[1]

    @pl.when((pl.program_id(0) == 0) & (pl.program_id(1) == 0))
    def _():
        carry_ref[...] = jnp.zeros_like(carry_ref)

    out_a = _dot(oa_ref[0], wa_ref[...])
    out_b = _dot(ob_ref[0], wb_ref[...])
    mix = gab_ref[0, :, 0:D_MODEL].astype(F32) * out_a + gab_ref[0, :, D_MODEL:2 * D_MODEL].astype(F32) * out_b
    x1 = x_ref[0] + _dot(mix.astype(BF16), wo_ref[...])
    _to_tile_rows(x1_ref, x1)
    h2 = x1 * lax.rsqrt(jnp.mean(x1 * x1, axis=-1, keepdims=True) + RMS_EPS) * g2_ref[...]

    h_hi = h2.astype(BF16)
    h_lo = (h2 - h_hi.astype(F32)).astype(BF16)
    logits = _dot(h_hi, wr_hi_ref[...]) + (_dot(h_hi, wr_lo_ref[...]) + _dot(h_lo, wr_hi_ref[...])) + br_ref[...]
    lane = lax.broadcasted_iota(jnp.int32, (tm, LANES), 1)
    lane_f = lane.astype(F32)

    def first_argmax(vals):
        mx = jnp.max(vals, axis=-1, keepdims=True)
        idx = jnp.min(jnp.where(vals == mx, lane_f, float(LANES)), axis=-1, keepdims=True)
        return mx, idx

    is_grp = (lane >= N_EXPERTS) & (lane < N_EXPERTS + N_GROUPS)
    gl = jnp.where(is_grp, logits, NEG_INF)
    gmax, gidx = first_argmax(gl)
    p_g = 1.0 / jnp.sum(jnp.where(is_grp, jnp.exp(gl - gmax), 0.0), axis=-1, keepdims=True)
    e_lo = (gidx - float(N_EXPERTS)) * float(EXPERTS_PER_GROUP)
    in_grp = (lane_f >= e_lo) & (lane_f < e_lo + float(EXPERTS_PER_GROUP))
    el = jnp.where(in_grp, logits, NEG_INF)
    m1, i1 = first_argmax(el)
    m2, i2 = first_argmax(jnp.where(lane_f == i1, NEG_INF, el))
    e2 = jnp.exp(m2 - m1)
    w1 = p_g / (1.0 + e2)
    w2 = p_g * e2 / (1.0 + e2)

    hit1 = lane_f == i1
    hit2 = lane_f == i2
    onehot = jnp.where(hit1 | hit2, 1.0, 0.0)
    before = carry_ref[...] + _dot(ltri_ref[...], onehot.astype(BF16))
    rank1 = jnp.sum(jnp.where(hit1, before, 0.0), axis=-1, keepdims=True)
    rank2 = jnp.sum(jnp.where(hit2, before, 0.0), axis=-1, keepdims=True)
    total = carry_ref[...] + jnp.sum(onehot, axis=0, keepdims=True)
    carry_ref[...] = total
    cnt_ref[...] = total

    rec = jnp.zeros((tm, LANES), F32)
    for k, val in ((R_E1, i1), (R_E2, i2), (R_W1, w1), (R_W2, w2), (R_RANK1, rank1), (R_RANK2, rank2)):
        rec = jnp.where(lane == k, val, rec)
    route_ref[...] = rec


def _merge(x, oa, ob, gab, w_fox_up, w_nsa_up, w_out, norm_ffn_g, w_group, b_group, w_router, b_router):
    B, S, D = x.shape
    tm = min(TM_MERGE, S)
    wa = w_fox_up.astype(BF16)
    wn = w_nsa_up.reshape(NSA_KV_HEADS, NSA_REP, HEAD_DIM, D)
    z = jnp.zeros_like(wn[0])
    wb = jnp.stack([jnp.concatenate([wn[0], z], axis=1), jnp.concatenate([z, wn[1]], axis=1)])
    wb = wb.reshape(NSA_HEADS * LANES, D).astype(BF16)
    wo = w_out.astype(BF16)
    wr = jnp.pad(jnp.concatenate([w_router, w_group], axis=1).astype(F32),
                 ((0, 0), (0, LANES - N_EXPERTS - N_GROUPS)))
    wr_hi = wr.astype(BF16)
    wr_lo = (wr - wr_hi.astype(F32)).astype(BF16)
    br = jnp.pad(jnp.concatenate([b_router, b_group]).astype(F32), (0, LANES - N_EXPERTS - N_GROUPS))[None, :]
    g2 = norm_ffn_g.astype(F32)[None, :]
    rt = np.arange(tm)
    ltri = (rt[None, :] < rt[:, None]).astype(BF16)

    row = lambda n: pl.BlockSpec((1, tm, n), lambda b, s: (b, s, 0))
    flat = lambda n: pl.BlockSpec((tm, n), lambda b, s: (b * (S // tm) + s, 0))
    flat_rows = pl.BlockSpec((tm * ROW_SUBLANES, LANES), lambda b, s: (b * (S // tm) + s, 0))
    consts = [wa, wb, wo, g2, wr_hi, wr_lo, br, ltri]
    return pl.pallas_call(
        _merge_kernel,
        grid=(B, S // tm),
        in_specs=[row(D), row(FOX_W), row(NSA_HEADS * LANES), row(2 * D)] + [_const_spec(a.shape) for a in consts],
        out_specs=[flat_rows, flat(LANES), _const_spec((1, LANES))],
        out_shape=[jax.ShapeDtypeStruct((B * S * ROW_SUBLANES, LANES), F32), jax.ShapeDtypeStruct((B * S, LANES), F32),
                   jax.ShapeDtypeStruct((1, LANES), F32)],
        scratch_shapes=[pltpu.VMEM((1, LANES), F32)],
        compiler_params=_cparams(("arbitrary", "arbitrary")),
        name="merge",
    )(x, oa, ob, gab, *consts)


ROW_SUBLANES = D_MODEL // LANES


def _to_tile_rows(ref, x):
    n = x.shape[0]
    for c in range(ROW_SUBLANES):
        ref[pl.ds(c, n, stride=ROW_SUBLANES), :] = x[:, LANES * c:LANES * (c + 1)]


def _from_tile_rows(ref):
    n = ref.shape[0] // ROW_SUBLANES
    return jnp.concatenate([ref[pl.ds(c, n, stride=ROW_SUBLANES), :] for c in range(ROW_SUBLANES)], axis=1)


def _row_copy(src_ref, src_row, dst_ref, dst_row, sem):
    src = src_ref.at[pl.ds(pl.multiple_of(src_row * ROW_SUBLANES, ROW_SUBLANES), ROW_SUBLANES), :]
    dst = dst_ref.at[pl.ds(pl.multiple_of(dst_row * ROW_SUBLANES, ROW_SUBLANES), ROW_SUBLANES), :]
    return pltpu.make_async_copy(src, dst, sem)


DISPATCH_SLOTS = 3


def _dispatch_kernel(row1_ref, row2_ref, clear_ref, x1_ref, xs_ref, stage_ref, zero_ref, in_sem, out_sem, zsem):
    tm = stage_ref.shape[1] // ROW_SUBLANES
    t = pl.program_id(0)
    n_steps = pl.num_programs(0)

    def stage(step):
        rows = pl.ds(pl.multiple_of(step * (tm * ROW_SUBLANES), tm * ROW_SUBLANES), tm * ROW_SUBLANES)
        slot = step % DISPATCH_SLOTS
        return pltpu.make_async_copy(x1_ref.at[rows, :], stage_ref.at[slot], in_sem.at[slot])

    def for_each_scatter(step, fn):
        base = step * tm
        slot = step % DISPATCH_SLOTS

        def body(r, _):
            fn(_row_copy(stage_ref.at[slot], r, xs_ref, row1_ref[base + r], out_sem.at[slot]), 0)
            fn(_row_copy(stage_ref.at[slot], r, xs_ref, row2_ref[base + r], out_sem.at[slot]), 1)
            return _

        lax.fori_loop(0, tm, body, 0, unroll=8)

    @pl.when(t == 0)
    def _():
        stage(0).start()
        zero_ref[...] = jnp.zeros_like(zero_ref)

        def clear(c):
            start = pl.multiple_of(jnp.maximum(clear_ref[c], 0) * ROW_SUBLANES, zero_ref.shape[0])
            return pltpu.make_async_copy(zero_ref, xs_ref.at[pl.ds(start, zero_ref.shape[0]), :], zsem)

        for c in range(clear_ref.shape[0]):
            @pl.when(clear_ref[c] >= 0)
            def _(c=c):
                clear(c).start()
        for c in range(clear_ref.shape[0]):
            @pl.when(clear_ref[c] >= 0)
            def _(c=c):
                clear(c).wait()

    @pl.when(t >= 2)
    def _():
        for_each_scatter(t - 2, lambda c, _: c.wait())

    @pl.when(t + 1 < n_steps)
    def _():
        stage(t + 1).start()

    stage(t).wait()
    for_each_scatter(t, lambda c, priority: c.start(priority=priority))

    @pl.when(t + 1 == n_steps)
    def _():
        @pl.when(t >= 1)
        def _():
            for_each_scatter(t - 1, lambda c, _: c.wait())
        for_each_scatter(t, lambda c, _: c.wait())


def _experts_kernel(tile_e_ref, n_used_ref, xs_ref, g2_ref, wg_ref, wu_ref, wd_ref, ys_ref,
                    wg_bf, wu_bf, wd_bf):
    k = pl.program_id(0)
    used = k < n_used_ref[0]

    @pl.when(used & ((k == 0) | (tile_e_ref[k] != tile_e_ref[jnp.maximum(k - 1, 0)])))
    def _():
        wg_bf[...] = wg_ref[0, 0].astype(BF16)
        wu_bf[...] = wu_ref[0, 0].astype(BF16)
        wd_bf[...] = wd_ref[0, 0].astype(BF16)

    @pl.when(used)
    def _():
        x = _from_tile_rows(xs_ref)
        h = (x * lax.rsqrt(jnp.mean(x * x, axis=-1, keepdims=True) + RMS_EPS) * g2_ref[...]).astype(BF16)
        a = _dot(h, wg_bf[...])
        hid = (a * jax.nn.sigmoid(a)) * _dot(h, wu_bf[...])
        _to_tile_rows(ys_ref, _dot(hid.astype(BF16), wd_bf[...]))

    @pl.when(jnp.logical_not(used))
    def _():
        ys_ref[...] = jnp.zeros_like(ys_ref)


def _combine_kernel(row1_ref, row2_ref, x1_ref, route_ref, ys_ref, o_ref, y1_ref, y2_ref, sem):
    tm = x1_ref.shape[0] // ROW_SUBLANES
    t = pl.program_id(0)
    slot = t & 1

    def for_each_copy(step, slot, fn):
        base = step * tm

        def body(r, _):
            fn(_row_copy(ys_ref, row1_ref[base + r], y1_ref.at[slot], r, sem.at[slot]), 0)
            fn(_row_copy(ys_ref, row2_ref[base + r], y2_ref.at[slot], r, sem.at[slot]), 1)
            return _

        lax.fori_loop(0, tm, body, 0, unroll=8)

    @pl.when(t == 0)
    def _():
        for_each_copy(0, 0, lambda c, priority: c.start(priority=priority))

    @pl.when(t + 1 < pl.num_programs(0))
    def _():
        for_each_copy(t + 1, 1 - slot, lambda c, priority: c.start(priority=priority))

    for_each_copy(t, slot, lambda c, _: c.wait())
    rec = route_ref[...]
    lane = lax.broadcasted_iota(jnp.int32, rec.shape, 1)
    w1 = jnp.sum(jnp.where(lane == R_W1, rec, 0.0), axis=-1, keepdims=True)
    w2 = jnp.sum(jnp.where(lane == R_W2, rec, 0.0), axis=-1, keepdims=True)
    o_ref[0] = _from_tile_rows(x1_ref) + (w1 * _from_tile_rows(y1_ref.at[slot]) + w2 * _from_tile_rows(y2_ref.at[slot]))


def _moe(x1, route, cnt, norm_ffn_g, w_gate, w_up, w_down, layer, B, S):
    T, D = B * S, D_MODEL
    tme = min(TM_EXPERT, T)
    tmd = min(TM_DISPATCH, S)

    counts = cnt[0, :N_EXPERTS].astype(jnp.int32)
    tiles_per_e = (counts + tme - 1) // tme
    tile_end = jnp.cumsum(tiles_per_e)
    offs = (tile_end - tiles_per_e) * tme
    last_tile_row = jnp.where(tiles_per_e > 0, (tile_end - 1) * tme, -1).astype(jnp.int32)
    rec = route[:, :8].T.astype(jnp.int32)
    experts = jnp.arange(N_EXPERTS, dtype=jnp.int32)[:, None]
    row_of = lambda e, rank: rank + jnp.sum(jnp.where(e[None, :] == experts, offs[:, None], 0), axis=0)
    row1 = row_of(rec[R_E1], rec[R_RANK1])
    row2 = row_of(rec[R_E2], rec[R_RANK2])
    max_tiles = (2 * T) // tme + N_EXPERTS
    tile_e = jnp.sum(jnp.arange(max_tiles, dtype=jnp.int32)[:, None] >= tile_end[None, :], axis=1)
    tile_e = jnp.minimum(tile_e, N_EXPERTS - 1).astype(jnp.int32)
    n_used = tile_end[-1:].astype(jnp.int32)
    n_rows = max_tiles * tme
    spare = n_used + jnp.arange(N_EXPERTS, dtype=jnp.int32)
    clear_rows = jnp.concatenate([last_tile_row, jnp.where(spare < max_tiles, spare * tme, -1)]).astype(jnp.int32)

    xs = pl.pallas_call(
        _dispatch_kernel,
        grid_spec=pltpu.PrefetchScalarGridSpec(
            num_scalar_prefetch=3,
            grid=(T // tmd,),
            in_specs=[pl.BlockSpec(memory_space=pl.ANY)],
            out_specs=pl.BlockSpec(memory_space=pl.ANY),
            scratch_shapes=[pltpu.VMEM((DISPATCH_SLOTS, tmd * ROW_SUBLANES, LANES), F32),
                            pltpu.VMEM((tme * ROW_SUBLANES, LANES), F32),
                            pltpu.SemaphoreType.DMA((DISPATCH_SLOTS,)), pltpu.SemaphoreType.DMA((DISPATCH_SLOTS,)),
                            pltpu.SemaphoreType.DMA(())],
        ),
        out_shape=jax.ShapeDtypeStruct((n_rows * ROW_SUBLANES, LANES), F32),
        compiler_params=_cparams(("arbitrary",)),
        name="dispatch",
    )(row1, row2, clear_rows, x1)

    g2 = norm_ffn_g.astype(F32)[None, :]
    w_spec = lambda shape: pl.BlockSpec((1, 1) + shape, lambda k, te, nu: (layer, te[k], 0, 0))
    ys = pl.pallas_call(
        _experts_kernel,
        grid_spec=pltpu.PrefetchScalarGridSpec(
            num_scalar_prefetch=2,
            grid=(max_tiles,),
            in_specs=[pl.BlockSpec((tme * ROW_SUBLANES, LANES), lambda k, te, nu: (jnp.minimum(k, nu[0] - 1), 0)),
                      pl.BlockSpec((1, D), lambda k, te, nu: (0, 0)),
                      w_spec((D, D_EXPERT)), w_spec((D, D_EXPERT)), w_spec((D_EXPERT, D))],
            out_specs=pl.BlockSpec((tme * ROW_SUBLANES, LANES), lambda k, te, nu: (k, 0)),
            scratch_shapes=[pltpu.VMEM((D, D_EXPERT), BF16), pltpu.VMEM((D, D_EXPERT), BF16),
                            pltpu.VMEM((D_EXPERT, D), BF16)],
        ),
        out_shape=jax.ShapeDtypeStruct((n_rows * ROW_SUBLANES, LANES), F32),
        compiler_params=_cparams(("arbitrary",)),
        name="experts",
    )(tile_e, n_used, xs, g2, w_gate, w_up, w_down)

    return pl.pallas_call(
        _combine_kernel,
        grid_spec=pltpu.PrefetchScalarGridSpec(
            num_scalar_prefetch=2,
            grid=(T // tmd,),
            in_specs=[pl.BlockSpec((tmd * ROW_SUBLANES, LANES), lambda t, r1, r2: (t, 0)),
                      pl.BlockSpec((tmd, LANES), lambda t, r1, r2: (t, 0)),
                      pl.BlockSpec(memory_space=pl.ANY)],
            out_specs=pl.BlockSpec((1, tmd, D), lambda t, r1, r2: (t // (S // tmd), t % (S // tmd), 0)),
            scratch_shapes=[pltpu.VMEM((2, tmd * ROW_SUBLANES, LANES), F32), pltpu.VMEM((2, tmd * ROW_SUBLANES, LANES), F32),
                            pltpu.SemaphoreType.DMA((2,))],
        ),
        out_shape=jax.ShapeDtypeStruct((B, S, D), F32),
        compiler_params=_cparams(("arbitrary",)),
        name="combine",
    )(row1, row2, x1, route, ys)


def kernel(x, norm_mix_g, w_in, b_forget, fox_q_g, fox_k_g, nsa_q_g, nsa_k_g, cmp_k_w1, cmp_k_w2, cmp_k_pos,
           cmp_v_w1, cmp_v_w2, cmp_v_pos, w_fox_up, w_nsa_up, w_out, norm_ffn_g, w_group, b_group, w_router,
           b_router, w_gate, w_up, w_down):
    B, S, D = x.shape
    for l in range(w_in.shape[0]):
        qa, ka, va, qn, kcr, vcr, ksa, vsa, kwa, vwa, gt, gab = _inproj(
            x, norm_mix_g[l], w_in[l], b_forget[l], fox_q_g[l], fox_k_g[l], nsa_q_g[l], nsa_k_g[l])
        kca, vca = _compress(kcr, vcr, cmp_k_w1[l], cmp_k_w2[l], cmp_k_pos[l],
                             cmp_v_w1[l], cmp_v_w2[l], cmp_v_pos[l], nsa_k_g[l])
        ob = _nsa(qn, kca, vca, ksa, vsa, kwa, vwa, gt, _scores_bounded(nsa_q_g[l], nsa_k_g[l]))
        oa = _fox(qa, ka, va, _scores_bounded(fox_q_g[l], fox_k_g[l]))
        x1, route, cnt = _merge(x, oa, ob, gab, w_fox_up[l], w_nsa_up[l], w_out[l], norm_ffn_g[l],
                                w_group[l], b_group[l], w_router[l], b_router[l])
        x = _moe(x1, route, cnt, norm_ffn_g[l], w_gate, w_up, w_down, l, B, S)
    return x
```

```python
import functools

import jax
import jax.numpy as jnp
import numpy as np
from jax import lax
from jax.experimental import pallas as pl
from jax.experimental.pallas import tpu as pltpu

F32 = jnp.float32
BF16 = jnp.bfloat16

D_MODEL = 1024
HEAD_DIM = 64
FOX_HEADS = 8
NSA_HEADS = 8
NSA_KV_HEADS = 2
NSA_REP = NSA_HEADS // NSA_KV_HEADS
CMP_BLOCK = 32
CMP_STRIDE = 16
CMP_HIDDEN = 256
SLC_BLOCK = 64
SLC_TOPK = 16
WINDOW = 512
N_GROUPS = 4
EXPERTS_PER_GROUP = 4
N_EXPERTS = N_GROUPS * EXPERTS_PER_GROUP
D_EXPERT = 512
RMS_EPS = 1e-6
NEG_INF = -1e30
FORCE_SCORE = 1e4

LANES = 128
HALF = LANES // 2
VMEM_LIMIT = 56 * 1024 * 1024

FOX_W = FOX_HEADS * HEAD_DIM
NSA_W = NSA_HEADS * HEAD_DIM
NSA_KV_W = NSA_KV_HEADS * HEAD_DIM

OFF_FQ = 0
OFF_FK = OFF_FQ + FOX_W
OFF_FV = OFF_FK + FOX_W
OFF_NQ = OFF_FV + FOX_W
OFF_KC = OFF_NQ + NSA_W
OFF_VC = OFF_KC + NSA_KV_W
OFF_KS = OFF_VC + NSA_KV_W
OFF_VS = OFF_KS + NSA_KV_W
OFF_KW = OFF_VS + NSA_KV_W
OFF_VW = OFF_KW + NSA_KV_W
OFF_FF = OFF_VW + NSA_KV_W
OFF_NG = OFF_FF + LANES
OFF_GA = OFF_NG + LANES
N_PROJ = OFF_GA + 2 * D_MODEL

TM_PROJ = 512
TQ_FOX = 512
FOX_UNROLL = 4
TQ_NSA = 256
TK_SLC = 256
SLC_UNROLL = 4
TM_MERGE = 512
TM_EXPERT = 512
TM_DISPATCH = 512


def _dot(a, b):
    return jnp.dot(a, b, preferred_element_type=F32)


def _dot_nt(a, b):
    return lax.dot_general(a, b, (((1,), (1,)), ((), ())), preferred_element_type=F32)


def _split3(x):
    hi = x.astype(BF16)
    r = x - hi.astype(F32)
    mid = r.astype(BF16)
    lo = (r - mid.astype(F32)).astype(BF16)
    return hi, mid, lo


def _cparams(sem):
    return pltpu.CompilerParams(dimension_semantics=sem, vmem_limit_bytes=VMEM_LIMIT)


def _const_spec(shape):
    nd = len(shape)
    return pl.BlockSpec(shape, lambda *_: (0,) * nd)


def _inproj_kernel(x_ref, g_ref, w_ref, bf_ref, gqa_ref, gka_ref, gqn_ref, gkn_ref,
                   bd_ref, bd2_ref, tri_ref, cq_ref, ck_ref, cv_ref,
                   qc0_ref, qca_ref, qcb_ref, kc0_ref, kca_ref, kcb_ref, vone_ref,
                   qa_ref, ka_ref, va_ref, qn_ref, kcr_ref, vcr_ref,
                   ksa_ref, vsa_ref, kwa_ref, vwa_ref, gt_ref, gab_ref,
                   carry_ref):
    tm = x_ref.shape[1]

    @pl.when(pl.program_id(1) == 0)
    def _():
        carry_ref[...] = jnp.zeros_like(carry_ref)

    x = x_ref[0]
    y = x * lax.rsqrt(jnp.mean(x * x, axis=-1, keepdims=True) + RMS_EPS)
    h = (y * g_ref[...]).astype(BF16)

    def proj(off, n):
        return _dot(h, w_ref[:, off:off + n])

    lo_half = lax.broadcasted_iota(jnp.int32, (tm, LANES), 1) < HALF

    pos = pl.program_id(1) * tm + lax.broadcasted_iota(jnp.int32, (tm, 1), 0)
    pos_a = ((pos >> 8) << 8).astype(F32)
    pos_b = (pos & 255).astype(F32)

    def pos_channels(c0_ref, ca_ref, cb_ref, k):
        blk = slice(LANES * k, LANES * (k + 1))
        return c0_ref[:, blk] + ca_ref[:, blk] * pos_a + cb_ref[:, blk] * pos_b

    def headnorm(z, bd, grow):
        msq = _dot((z * z).astype(BF16), bd[...])
        return z * lax.rsqrt(msq + RMS_EPS) * grow[...]

    def spread_pairs(out_ref, src, aug):
        for m in range(4):
            s = src[:, LANES * m:LANES * (m + 1)]
            out_ref[0, :, LANES * 2 * m:LANES * (2 * m + 1)] = jnp.where(lo_half, s, aug(2 * m)).astype(BF16)
            out_ref[0, :, LANES * (2 * m + 1):LANES * (2 * m + 2)] = jnp.where(lo_half, aug(2 * m + 1), s).astype(BF16)

    zf = proj(OFF_FF, LANES) + bf_ref[...]
    logf = jnp.minimum(zf, 0.0) - jnp.log(1.0 + jnp.exp(-jnp.abs(zf)))
    l_hi, l_mid, l_lo = _split3(logf)
    tri = tri_ref[...]
    cum = carry_ref[...] + (_dot(tri, l_hi) + _dot(tri, l_mid) + _dot(tri, l_lo))
    carry_ref[...] = cum[tm - 1:tm, :]
    pieces = [p.astype(F32) for p in _split3(cum)]
    lane128 = lax.broadcasted_iota(jnp.int32, (tm, LANES), 1)

    def aug_block(k, const_ref, first, sign):
        base = HALF if k % 2 == 0 else 0
        blk = jnp.broadcast_to(const_ref[:, LANES * k:LANES * (k + 1)], (tm, LANES))
        for j, piece in enumerate(pieces):
            blk = jnp.where(lane128 == base + first + j, sign * piece[:, k:k + 1], blk)
        return blk

    augq = lambda k: aug_block(k, cq_ref, 3, 1.0)
    augk = lambda k: aug_block(k, ck_ref, 0, -1.0)

    zq = headnorm(proj(OFF_FQ, FOX_W), bd_ref, gqa_ref)
    spread_pairs(qa_ref, zq, augq)
    zk = headnorm(proj(OFF_FK, FOX_W), bd_ref, gka_ref)
    spread_pairs(ka_ref, zk, augk)
    zv = proj(OFF_FV, FOX_W)
    spread_pairs(va_ref, zv, lambda k: cv_ref[:, LANES * k:LANES * (k + 1)])

    zn = headnorm(proj(OFF_NQ, NSA_W), bd_ref, gqn_ref)
    for m in range(NSA_REP):
        s = zn[:, LANES * m:LANES * (m + 1)]
        c0 = pos_channels(qc0_ref, qca_ref, qcb_ref, m)
        c1 = pos_channels(qc0_ref, qca_ref, qcb_ref, NSA_REP + m)
        qn_ref[0, :, LANES * m:LANES * (m + 1)] = jnp.where(lo_half, s, c0).astype(BF16)
        qn_ref[0, :, LANES * (NSA_REP + m):LANES * (NSA_REP + m + 1)] = jnp.where(lo_half, c1, s).astype(BF16)

    kcr_ref[0] = proj(OFF_KC, NSA_KV_W)
    vcr_ref[0] = proj(OFF_VC, NSA_KV_W)

    kp0 = pos_channels(kc0_ref, kca_ref, kcb_ref, 0)
    kp1 = pos_channels(kc0_ref, kca_ref, kcb_ref, 1)

    def kv_pair(k_out, v_out, off_k, off_v):
        zk2 = headnorm(proj(off_k, NSA_KV_W), bd2_ref, gkn_ref)
        k_out[0, :, 0:LANES] = jnp.where(lo_half, zk2, kp0).astype(BF16)
        k_out[0, :, LANES:2 * LANES] = jnp.where(lo_half, kp1, zk2).astype(BF16)
        zv2 = proj(off_v, NSA_KV_W)
        v_out[0, :, 0:LANES] = jnp.where(lo_half, zv2, vone_ref[:, 0:LANES]).astype(BF16)
        v_out[0, :, LANES:2 * LANES] = jnp.where(lo_half, vone_ref[:, LANES:2 * LANES], zv2).astype(BF16)

    kv_pair(ksa_ref, vsa_ref, OFF_KS, OFF_VS)
    kv_pair(kwa_ref, vwa_ref, OFF_KW, OFF_VW)

    gt_ref[0] = jax.nn.sigmoid(proj(OFF_NG, LANES))
    gab_ref[0, :, 0:D_MODEL] = jax.nn.sigmoid(proj(OFF_GA, D_MODEL)).astype(BF16)
    gab_ref[0, :, D_MODEL:2 * D_MODEL] = jax.nn.sigmoid(proj(OFF_GA + D_MODEL, D_MODEL)).astype(BF16)


def _pos_pieces(pos):
    return ((pos // 256) * 256).astype(np.float32), (pos % 256).astype(np.float32)


def _inproj(x, norm_g, w_in, b_forget, fox_q_g, fox_k_g, nsa_q_g, nsa_k_g):
    B, S, D = x.shape
    tm = min(TM_PROJ, S)
    scale = HEAD_DIM ** -0.5

    c = [0]
    for n in (FOX_W, FOX_W, FOX_W, FOX_HEADS, NSA_W) + (NSA_KV_W,) * 6 + (3 * NSA_HEADS, D_MODEL, D_MODEL):
        c.append(c[-1] + n)
    fq, fk, fv, ff, nq, kc, vc, ks, vs, kw, vw, ng, ga, gb = [w_in[:, c[i]:c[i + 1]] for i in range(14)]
    perm = jnp.asarray([0, 4, 1, 5, 2, 6, 3, 7])
    nq = nq.reshape(D, NSA_HEADS, HEAD_DIM)[:, perm, :].reshape(D, NSA_W)
    padl = lambda a: jnp.pad(a, ((0, 0), (0, LANES - a.shape[1])))
    w = jnp.concatenate([fq, fk, fv, nq, kc, vc, ks, vs, kw, vw, padl(ff), padl(ng), ga, gb], axis=1).astype(BF16)
    assert w.shape[1] == N_PROJ

    bf = jnp.pad(b_forget.astype(F32), (0, LANES - FOX_HEADS))[None, :]
    gqa = jnp.tile(fox_q_g.astype(F32) * scale, FOX_HEADS)[None, :]
    gka = jnp.tile(fox_k_g.astype(F32), FOX_HEADS)[None, :]
    gqn = jnp.tile(nsa_q_g.astype(F32) * scale, NSA_HEADS)[None, :]
    gkn = jnp.tile(nsa_k_g.astype(F32), NSA_KV_HEADS)[None, :]

    r512 = np.arange(FOX_W)
    bd = np.where((r512[:, None] // HEAD_DIM) == (r512[None, :] // HEAD_DIM), 1.0 / HEAD_DIM, 0.0).astype(BF16)
    bd2 = bd[:LANES, :LANES]
    rt = np.arange(tm)
    tri = (rt[None, :] <= rt[:, None]).astype(BF16)

    heads = np.arange(FOX_HEADS)
    base = heads * LANES + np.where(heads % 2 == 0, HALF, 0)
    cols = np.arange(FOX_HEADS * LANES)
    off_in_blk = cols - base[cols // LANES]
    cq = ((off_in_blk >= 0) & (off_in_blk < 3)).astype(F32)[None, :]
    ck = ((off_in_blk >= 3) & (off_in_blk < 9)).astype(F32)[None, :]
    cv = (off_in_blk == 0).astype(F32)[None, :]

    blk = cols // LANES
    slope = 2.0 ** (-(blk + 1).astype(F32))
    o = cols % LANES - np.where(blk // NSA_REP == 0, HALF, 0)
    qc0 = np.where((o == 0) | (o == 1), slope, 0.0)[None, :]
    qca = np.where(o == 2, -slope, 0.0)[None, :]
    qcb = np.where(o == 3, -slope, 0.0)[None, :]
    kc0, kca, kcb, vone = _kv_rows()

    grid = (B, S // tm)
    row_spec = lambda n: pl.BlockSpec((1, tm, n), lambda b, s: (b, s, 0))
    consts = [norm_g.astype(F32)[None, :], w, bf, gqa, gka, gqn, gkn, bd, bd2, tri, cq, ck, cv,
              qc0, qca, qcb, kc0, kca, kcb, vone]
    out_widths = [(8 * LANES, BF16)] * 4 + [(LANES, F32)] * 2 + [(2 * LANES, BF16)] * 4 + \
                 [(LANES, F32), (2 * D_MODEL, BF16)]
    outs = pl.pallas_call(
        _inproj_kernel,
        grid=grid,
        in_specs=[row_spec(D)] + [_const_spec(a.shape) for a in consts],
        out_specs=[row_spec(n) for n, _ in out_widths],
        out_shape=[jax.ShapeDtypeStruct((B, S, n), dt) for n, dt in out_widths],
        scratch_shapes=[pltpu.VMEM((1, LANES), F32)],
        compiler_params=_cparams(("arbitrary", "arbitrary")),
        name="inproj",
    )(x, *consts)
    return outs


def _kv_rows():
    cols = np.arange(2 * LANES)
    o = cols % LANES - np.where(cols // LANES == 0, HALF, 0)
    row = lambda m: m.astype(np.float32)[None, :]
    return row((o >= 2) & (o <= 6)), row(o == 0), row(o == 1), row(o == 0)


def _kv_consts(pa, pb):
    c0, ca, cb, vone = _kv_rows()
    return (c0 + ca * pa[:, None] + cb * pb[:, None]).astype(BF16), vone


def _compress_kernel(kt_ref, vt_ref, w1k_ref, w1v_ref, posk_ref, posv_ref, pw1k_ref, pw1v_ref, w2k_ref, w2v_ref,
                     gk_ref, kcc_ref, vone_ref, kc_ref, vc_ref):
    nc = kc_ref.shape[2]

    def mlp(t_ref, w1_ref, pos_ref, pw1_ref, w2_ref):
        both = jnp.zeros((nc, 2 * CMP_HIDDEN), F32)
        for l in range(CMP_STRIDE):
            rows = t_ref[0, pl.ds(l, nc, stride=CMP_STRIDE), :].astype(BF16)
            both = both + _dot(rows, w1_ref[0, l])
        posw = _dot(pos_ref[...], pw1_ref[...])[0:1, :]
        pre = both[:, 0:CMP_HIDDEN] + pltpu.roll(both[:, CMP_HIDDEN:2 * CMP_HIDDEN], nc - 1, axis=0) + posw
        act = pre * (0.5 * (1.0 + jnp.tanh(0.7978845608028654 * (pre + 0.044715 * (pre * pre * pre)))))
        return _dot(act.astype(BF16), w2_ref[0])

    kc = mlp(kt_ref, w1k_ref, posk_ref, pw1k_ref, w2k_ref)
    msq = jnp.sum(kc * kc, axis=-1, keepdims=True) * (1.0 / HEAD_DIM)
    kc_ref[0, 0] = (kc * lax.rsqrt(msq + RMS_EPS) * gk_ref[0] + kcc_ref[0]).astype(BF16)
    vc = mlp(vt_ref, w1v_ref, posv_ref, pw1v_ref, w2v_ref)
    vc_ref[0, 0] = (vc + vone_ref[0]).astype(BF16)


def _compress(kcr, vcr, cmp_k_w1, cmp_k_w2, cmp_k_pos, cmp_v_w1, cmp_v_w2, cmp_v_pos, nsa_k_g):
    B, S, _ = kcr.shape
    G = NSA_KV_HEADS
    nc = S // CMP_STRIDE

    def w1_strided(w1):
        w = w1.reshape(2, CMP_STRIDE, HEAD_DIM, CMP_HIDDEN)
        w = jnp.concatenate([w[0], w[1]], axis=-1)
        z = jnp.zeros_like(w)
        return jnp.stack([jnp.concatenate([w, z], axis=1), jnp.concatenate([z, w], axis=1)]).astype(BF16)

    def w2_spread(w2):
        z = jnp.zeros_like(w2)
        return jnp.stack([jnp.concatenate([w2, z], 1), jnp.concatenate([z, w2], 1)]).astype(BF16)

    def pos8(p):
        return jnp.tile(p.reshape(1, CMP_BLOCK * HEAD_DIM), (8, 1)).astype(BF16)

    gk = nsa_k_g.astype(F32)
    z = jnp.zeros_like(gk)
    gk2 = jnp.stack([jnp.concatenate([gk, z]), jnp.concatenate([z, gk])])[:, None, :]
    cend = np.arange(nc) * CMP_STRIDE + CMP_BLOCK - 1
    kcc, vone = _kv_consts(*_pos_pieces(cend))
    kcc = kcc.astype(F32).reshape(nc, G, LANES).transpose(1, 0, 2)
    vone = vone.reshape(G, 1, LANES)

    tok = pl.BlockSpec((1, S, G * HEAD_DIM), lambda b, g: (b, 0, 0))
    per_g = lambda a: pl.BlockSpec((1,) + a.shape[1:], lambda b, g: (g,) + (0,) * (a.ndim - 1))
    w1k, w1v = w1_strided(cmp_k_w1), w1_strided(cmp_v_w1)
    pk, pv = pos8(cmp_k_pos), pos8(cmp_v_pos)
    pw1k, pw1v = cmp_k_w1.astype(BF16), cmp_v_w1.astype(BF16)
    w2k, w2v = w2_spread(cmp_k_w2), w2_spread(cmp_v_w2)
    out_spec = pl.BlockSpec((1, 1, nc, LANES), lambda b, g: (b, g, 0, 0))
    return pl.pallas_call(
        _compress_kernel,
        grid=(B, G),
        in_specs=[tok, tok, per_g(w1k), per_g(w1v), _const_spec(pk.shape), _const_spec(pv.shape),
                  _const_spec(pw1k.shape), _const_spec(pw1v.shape), per_g(w2k), per_g(w2v),
                  per_g(gk2), per_g(kcc), per_g(vone)],
        out_specs=[out_spec, out_spec],
        out_shape=[jax.ShapeDtypeStruct((B, G, nc, LANES), BF16)] * 2,
        compiler_params=_cparams(("arbitrary", "arbitrary")),
        name="compress",
    )(kcr, vcr, w1k, w1v, pk, pv, pw1k, pw1v, w2k, w2v, gk2, kcc, vone)


MAX_EXPONENT = 60.0


def _scores_bounded(q_g, k_g):
    bound = HEAD_DIM ** 0.5 * jnp.max(jnp.abs(q_g)) * jnp.max(jnp.abs(k_g))
    return 2.04 * bound + 0.05 <= MAX_EXPONENT


def _attn_first(s, v):
    m = jnp.max(s, axis=-1, keepdims=True)
    p = jnp.exp((s - m).astype(BF16))
    return m, _dot(p, v)


def _attn_step(carry, s, v):
    m, acc = carry
    m_new = jnp.maximum(m, jnp.max(s, axis=-1, keepdims=True))
    p = jnp.exp((s - m_new).astype(BF16))
    return m_new, jnp.exp(m - m_new) * acc + _dot(p, v)


def _nsa_kernel(bounded, q_ref, kc_ref, vc_ref, ks_ref, e_ref, vs_ref, kw_ref, vw_ref, gt_ref, ov_ref, o_ref,
                tiles_ref, sa_ref, sb_ref):
    tq = q_ref.shape[1]
    nc = kc_ref.shape[2]
    tk = TK_SLC
    rq = NSA_REP * tq
    g = pl.program_id(1)
    i = pl.program_id(2)
    q0 = i * tq

    q4 = jnp.concatenate([q_ref[0, :, LANES * r:LANES * (r + 1)] for r in range(NSA_REP)], axis=0)

    def qpos_of(shape):
        return q0 + (lax.broadcasted_iota(jnp.int32, shape, 0) & (tq - 1))

    def add_mask(s, valid):
        bias = jnp.where(valid, 0.0, NEG_INF)
        return (s.reshape(NSA_REP, tq, s.shape[1]) + bias[None]).reshape(s.shape)

    def qrow(n):
        return q0 + lax.broadcasted_iota(jnp.int32, (tq, n), 0)

    def kcol(n):
        return lax.broadcasted_iota(jnp.int32, (tq, n), 1)

    anyv = qpos_of((rq, 1)) >= CMP_BLOCK - 1

    def cmp_branch(n):
        s = add_mask(_dot_nt(q4, kc_ref[0, 0, 0:n, :]), qrow(n) >= kcol(n) * CMP_STRIDE + (CMP_BLOCK - 1))
        if bounded:
            e = jnp.exp(s)
            p = e * jnp.where(anyv, 1.0 / jnp.sum(e, axis=-1, keepdims=True), 0.0)
        else:
            m = jnp.max(s, axis=-1, keepdims=True)
            e = jnp.exp(s - m)
            p = e * (anyv.astype(F32) / jnp.sum(e, axis=-1, keepdims=True))
        o = _dot(p.astype(BF16), vc_ref[0, 0, 0:n, :])
        psum = p[0:tq] + p[tq:2 * tq] + p[2 * tq:3 * tq] + p[3 * tq:4 * tq]
        ov = ov_ref[0:n, :]
        return o, functools.reduce(lambda a, b: a + b, [_dot(piece, ov) for piece in _split3(psum)])

    last_visible = (q0 + tq - CMP_BLOCK) // CMP_STRIDE
    o_cmp, imp = lax.cond(last_visible < nc // 2, lambda: cmp_branch(nc // 2), lambda: cmp_branch(nc))

    blk_i = lax.broadcasted_iota(jnp.int32, (LANES, tq), 0)
    blk_f = blk_i.astype(F32)
    qblk = (q0 + lax.broadcasted_iota(jnp.int32, (LANES, tq), 1)) // SLC_BLOCK
    forced = (blk_i == 0) | (blk_i == qblk) | (blk_i == qblk - 1)
    score = jnp.where(forced, -3e38, jnp.where(blk_i > qblk, -1.0, imp.T))
    selb = jnp.where(forced, 0.0, NEG_INF)
    for _ in range(SLC_TOPK - 3):
        mx = jnp.max(score, axis=0, keepdims=True)
        first = jnp.min(jnp.where(score == mx, blk_f, float(LANES)), axis=0, keepdims=True)
        hit = blk_f == first
        selb = jnp.where(hit, 0.0, selb)
        score = jnp.where(hit, -3e38, score)
    sb = selb.T.astype(BF16)
    sb4 = jnp.concatenate([sb] * NSA_REP, axis=0)

    def slc_qk(j):
        start = pl.multiple_of(j * tk, tk)
        kk = jnp.concatenate([ks_ref[0, pl.ds(start, tk), :], e_ref[pl.ds(start, tk), :]], axis=1)
        return _dot_nt(q2, kk)

    def slc_v(j):
        return vs_ref[0, pl.ds(pl.multiple_of(j * tk, tk), tk), :]

    jd = q0 // tk
    blocks_per_tile = tk // SLC_BLOCK
    anysel = jnp.max(selb, axis=1, keepdims=True)
    cnt = jnp.int32(0)
    for j in range(ks_ref.shape[1] // tk):
        tile_sel = jnp.max(anysel[blocks_per_tile * j:blocks_per_tile * (j + 1), :]) > -1.0
        tiles_ref[cnt] = j
        cnt = cnt + (tile_sel & (j < jd)).astype(jnp.int32)

    lane_r = lax.broadcasted_iota(jnp.int32, (rq, LANES), 1)
    data0 = HALF * g
    ones_lane = HALF - data0
    nw = WINDOW + tq
    wstart = pl.multiple_of(jnp.maximum(q0 - WINDOW, 0), tq)
    win_valid = lax.bitcast_convert_type(qrow(nw) - (wstart + kcol(nw)), jnp.uint32) < WINDOW
    diag_valid = jd * tk + kcol(tk) <= qrow(tk)

    if bounded:
        def shifted_by_own_score(kself):
            own = jnp.sum(q4.astype(F32).reshape(NSA_REP, tq, LANES) * kself.astype(F32)[None],
                          axis=-1, keepdims=True).reshape(rq, 1)
            qf = q4.astype(F32)
            for off, piece in enumerate(_split3(-own)):
                qf = jnp.where(lane_r == ones_lane + 4 + off, piece.astype(F32), qf)
            return qf.astype(BF16)

        q2 = jnp.concatenate([shifted_by_own_score(ks_ref[0, pl.ds(q0, tq), :]), sb4], axis=1)
        sa_ref[:, 0:LANES] = _dot(jnp.exp(add_mask(slc_qk(jd), diag_valid)).astype(BF16), slc_v(jd))

        def slc_sweep(t0, n):
            pv = []
            for t in range(n):
                j = tiles_ref[t0 + t]
                pv.append(_dot(jnp.exp(slc_qk(j)).astype(BF16), slc_v(j)))
            sa_ref[:, 0:LANES] += functools.reduce(lambda a, b: a + b, pv)

        def slc_body(c, _):
            slc_sweep(c * SLC_UNROLL, SLC_UNROLL)
            return _

        lax.fori_loop(0, cnt // SLC_UNROLL, slc_body, 0)
        done = (cnt // SLC_UNROLL) * SLC_UNROLL
        n = SLC_UNROLL // 2
        while n >= 1:
            @pl.when((cnt & n) != 0)
            def _(n=n, start=done):
                slc_sweep(start, n)
            done = done + (cnt & n)
            n //= 2

        acc_slc = sa_ref[:, 0:LANES]

        qw = shifted_by_own_score(kw_ref[0, pl.ds(q0, tq), :])
        s = add_mask(_dot_nt(qw, kw_ref[0, pl.ds(wstart, nw), :]), win_valid)
        acc_win = _dot(jnp.exp(s).astype(BF16), vw_ref[0, pl.ds(wstart, nw), :])
    else:
        q2 = jnp.concatenate([q4, sb4], axis=1)
        n_seq = cnt + 1

        def tile_of(n):
            return jnp.where(n == 0, jd, tiles_ref[jnp.maximum(n - 1, 0)])

        def slc_step(carry, s_buf, n):
            return _attn_step(carry, s_buf[...], slc_v(tile_of(n)))

        sa_ref[...] = add_mask(slc_qk(jd), diag_valid)

        def slc_body(p, carry):
            sb_ref[...] = slc_qk(tile_of(2 * p + 1))
            carry = slc_step(carry, sa_ref, 2 * p)
            sa_ref[...] = slc_qk(tile_of(jnp.minimum(2 * p + 2, n_seq - 1)))
            return slc_step(carry, sb_ref, 2 * p + 1)

        carry = (jnp.full((rq, 1), NEG_INF, F32), jnp.zeros((rq, LANES), F32))
        carry = lax.fori_loop(0, n_seq // 2, slc_body, carry)
        _, acc_slc = lax.cond(n_seq % 2 == 1, lambda c: slc_step(c, sa_ref, n_seq - 1), lambda c: c, carry)

        s = add_mask(_dot_nt(q4, kw_ref[0, pl.ds(wstart, nw), :]), win_valid)
        _, acc_win = _attn_first(s, vw_ref[0, pl.ds(wstart, nw), :])


    def normalise(acc):
        l = jnp.sum(jnp.where(lane_r == ones_lane, acc, 0.0), axis=-1, keepdims=True)
        return acc * (1.0 / l)

    o_slc = normalise(acc_slc)
    o_win = normalise(acc_win)
    gt = gt_ref[0]
    lane_g = lax.broadcasted_iota(jnp.int32, (tq, LANES), 1)
    is_data_q = (lane_g >= data0) & (lane_g < data0 + HALF)
    for r in range(NSA_REP):
        col = 3 * (NSA_REP * g + r)
        gate = [jnp.sum(jnp.where(lane_g == col + b, gt, 0.0), axis=-1, keepdims=True) for b in range(3)]
        rows = slice(r * tq, (r + 1) * tq)
        o = gate[0] * o_cmp[rows] + gate[1] * o_slc[rows] + gate[2] * o_win[rows]
        o_ref[0, :, LANES * r:LANES * (r + 1)] = jnp.where(is_data_q, o, 0.0).astype(BF16)


def _nsa(qn, kca, vca, ksa, vsa, kwa, vwa, gt, scores_bounded):
    B, S, _ = qn.shape
    G = NSA_KV_HEADS
    tq = min(TQ_NSA, S)
    nc = S // CMP_STRIDE
    n_slc = S // SLC_BLOCK
    assert n_slc <= LANES and S % TK_SLC == 0

    cs = np.arange(nc)[:, None] * CMP_STRIDE
    ss = np.arange(LANES)[None, :] * SLC_BLOCK
    ovl = np.clip(np.minimum(cs + CMP_BLOCK, ss + SLC_BLOCK) - np.maximum(cs, ss), 0, None)
    valid = (np.arange(nc)[:, None] < (S - CMP_BLOCK) // CMP_STRIDE + 1) & (np.arange(LANES)[None, :] < n_slc)
    ov = np.where(valid, ovl.astype(F32) / CMP_BLOCK, 0.0).astype(BF16)
    e1h = (np.arange(S)[:, None] // SLC_BLOCK == np.arange(LANES)[None, :]).astype(BF16)

    q_spec = pl.BlockSpec((1, tq, NSA_REP * LANES), lambda b, g, i: (b, i, g))
    c_spec = pl.BlockSpec((1, 1, nc, LANES), lambda b, g, i: (b, g, 0, 0))
    kv_spec = pl.BlockSpec((1, S, LANES), lambda b, g, i: (b, 0, g))

    def call(bounded, name):
        return pl.pallas_call(
            functools.partial(_nsa_kernel, bounded),
            grid=(B, G, S // tq),
            in_specs=[q_spec, c_spec, c_spec, kv_spec, _const_spec(e1h.shape), kv_spec, kv_spec, kv_spec,
                      pl.BlockSpec((1, tq, LANES), lambda b, g, i: (b, i, 0)), _const_spec(ov.shape)],
            out_specs=q_spec,
            out_shape=jax.ShapeDtypeStruct((B, S, NSA_HEADS * LANES), BF16),
            scratch_shapes=[pltpu.SMEM((S // TK_SLC + 1,), jnp.int32),
                            pltpu.VMEM((NSA_REP * tq, TK_SLC), F32), pltpu.VMEM((NSA_REP * tq, TK_SLC), F32)],
            compiler_params=_cparams(("arbitrary", "arbitrary", "arbitrary")),
            name=name,
        )

    return lax.cond(scores_bounded, call(True, "nsa_bounded"), call(False, "nsa"),
                    qn, kca, vca, ksa, e1h, vsa, kwa, vwa, gt, ov)


def _fox_kernel(q_ref, k_ref, v_ref, o_ref, sa_ref, sb_ref):
    tq = q_ref.shape[1]
    i = pl.program_id(2)
    lane = lax.broadcasted_iota(jnp.int32, (tq, LANES), 1)
    causal = lax.broadcasted_iota(jnp.int32, (tq, tq), 1) <= lax.broadcasted_iota(jnp.int32, (tq, tq), 0)

    def cols(hh):
        return slice(LANES * hh, LANES * (hh + 1))

    def qk(hh, j):
        start = pl.multiple_of(j * tq, tq)
        return _dot_nt(q_ref[0, :, cols(hh)], k_ref[0, pl.ds(start, tq), cols(hh)])

    def vtile(hh, j):
        return v_ref[0, pl.ds(pl.multiple_of(j * tq, tq), tq), cols(hh)]

    n_tiles = i + 1

    def tile_of(n):
        return jnp.where(n == 0, i, n - 1)

    def step_all(carry, s_buf, n):
        t = tile_of(n)
        return tuple(_attn_step(carry[hh], s_buf[hh], vtile(hh, t)) for hh in range(2))

    for hh in range(2):
        sa_ref[hh] = jnp.where(causal, qk(hh, i), NEG_INF)
    init = tuple((jnp.full((tq, 1), NEG_INF, F32), jnp.zeros((tq, LANES), F32)) for hh in range(2))

    def body(p, carry):
        for hh in range(2):
            sb_ref[hh] = qk(hh, tile_of(2 * p + 1))
        carry = step_all(carry, sa_ref, 2 * p)
        nxt = jnp.minimum(2 * p + 2, n_tiles - 1)
        for hh in range(2):
            sa_ref[hh] = qk(hh, tile_of(nxt))
        return step_all(carry, sb_ref, 2 * p + 1)

    carry = lax.fori_loop(0, n_tiles // 2, body, init)
    carry = lax.cond(n_tiles % 2 == 1, lambda c: step_all(c, sa_ref, n_tiles - 1), lambda c: c, carry)
    outs = []
    for hh in range(2):
        acc = carry[hh][1]
        ones_lane = HALF if hh == 0 else 0
        l = jnp.sum(jnp.where(lane == ones_lane, acc, 0.0), axis=-1, keepdims=True)
        outs.append(acc * (1.0 / l))
    o_ref[0] = jnp.where(lane < HALF, outs[0], outs[1]).astype(BF16)


def _fox_bounded_kernel(q_ref, k_ref, v_ref, o_ref, qs_ref, acc_ref):
    tq = q_ref.shape[1]
    i = pl.program_id(2)
    lane = lax.broadcasted_iota(jnp.int32, (tq, LANES), 1)
    causal = lax.broadcasted_iota(jnp.int32, (tq, tq), 1) <= lax.broadcasted_iota(jnp.int32, (tq, tq), 0)

    def cols(hh):
        return slice(LANES * hh, LANES * (hh + 1))

    def ktile(hh, j):
        return k_ref[0, pl.ds(pl.multiple_of(j * tq, tq), tq), cols(hh)]

    def vtile(hh, j):
        return v_ref[0, pl.ds(pl.multiple_of(j * tq, tq), tq), cols(hh)]

    for hh in range(2):
        q = q_ref[0, :, cols(hh)]
        s = jnp.where(causal, _dot_nt(q, ktile(hh, i)), NEG_INF)
        m = jnp.max(s, axis=-1, keepdims=True)
        acc_ref[hh] = _dot(jnp.exp((s - m).astype(BF16)), vtile(hh, i))
        qf = q.astype(F32)
        free0 = (HALF if hh == 0 else 0) + 6
        for off, piece in enumerate(_split3(-m)):
            qf = jnp.where(lane == free0 + off, piece.astype(F32), qf)
        qs_ref[hh] = qf.astype(BF16)

    def sweep(j0, n):
        for hh in range(2):
            pv = [_dot(jnp.exp(_dot_nt(qs_ref[hh], ktile(hh, j0 + t)).astype(BF16)), vtile(hh, j0 + t))
                  for t in range(n)]
            acc_ref[hh] += functools.reduce(lambda a, b: a + b, pv)

    def body(c, _):
        sweep(c * FOX_UNROLL, FOX_UNROLL)
        return _

    lax.fori_loop(0, i // FOX_UNROLL, body, 0)
    done = (i // FOX_UNROLL) * FOX_UNROLL
    n = FOX_UNROLL // 2
    while n >= 1:
        @pl.when((i & n) != 0)
        def _(n=n, start=done):
            sweep(start, n)
        done = done + (i & n)
        n //= 2
    outs = []
    for hh in range(2):
        acc = acc_ref[hh]
        ones_lane = HALF if hh == 0 else 0
        l = jnp.sum(jnp.where(lane == ones_lane, acc, 0.0), axis=-1, keepdims=True)
        outs.append(acc * (1.0 / l))
    o_ref[0] = jnp.where(lane < HALF, outs[0], outs[1]).astype(BF16)


def _fox(qa, ka, va, scores_bounded):
    B, S, _ = qa.shape
    tq = min(TQ_FOX, S)
    q_spec = pl.BlockSpec((1, tq, 2 * LANES), lambda b, h, i: (b, i, h))
    kv_spec = pl.BlockSpec((1, S, 2 * LANES), lambda b, h, i: (b, 0, h))

    def call(body, scratch, name):
        return pl.pallas_call(
            body,
            grid=(B, FOX_HEADS // 2, S // tq),
            in_specs=[q_spec, kv_spec, kv_spec],
            out_specs=pl.BlockSpec((1, tq, LANES), lambda b, h, i: (b, i, h)),
            out_shape=jax.ShapeDtypeStruct((B, S, FOX_W), BF16),
            scratch_shapes=scratch,
            compiler_params=_cparams(("arbitrary", "arbitrary", "arbitrary")),
            name=name,
        )

    general = call(_fox_kernel, [pltpu.VMEM((2, tq, tq), F32), pltpu.VMEM((2, tq, tq), F32)], "fox")
    bounded = call(_fox_bounded_kernel, [pltpu.VMEM((2, tq, LANES), BF16), pltpu.VMEM((2, tq, LANES), F32)],
                   "fox_bounded")
    return lax.cond(scores_bounded, bounded, general, qa, ka, va)


R_E1, R_E2, R_W1, R_W2, R_RANK1, R_RANK2 = range(6)


def _merge_kernel(x_ref, oa_ref, ob_ref, gab_ref, wa_ref, wb_ref, wo_ref, g2_ref, wr_hi_ref, wr_lo_ref, br_ref,
                  ltri_ref, x1_ref, route_ref, route_t_ref, cnt_ref, carry_ref):
    tm = x_ref.shape[1]

    @pl.when((pl.program_id(0) == 0) & (pl.program_id(1) == 0))
    def _():
        carry_ref[...] = jnp.zeros_like(carry_ref)

    out_a = _dot(oa_ref[0], wa_ref[...])
    out_b = _dot(ob_ref[0], wb_ref[...])
    mix = gab_ref[0, :, 0:D_MODEL].astype(F32) * out_a + gab_ref[0, :, D_MODEL:2 * D_MODEL].astype(F32) * out_b
    x1 = x_ref[0] + _dot(mix.astype(BF16), wo_ref[...])
    _to_tile_rows(x1_ref, x1)
    h2 = x1 * lax.rsqrt(jnp.mean(x1 * x1, axis=-1, keepdims=True) + RMS_EPS) * g2_ref[...]

    h_hi = h2.astype(BF16)
    h_lo = (h2 - h_hi.astype(F32)).astype(BF16)
    logits = _dot(h_hi, wr_hi_ref[...]) + (_dot(h_hi, wr_lo_ref[...]) + _dot(h_lo, wr_hi_ref[...])) + br_ref[...]
    lane = lax.broadcasted_iota(jnp.int32, (tm, LANES), 1)
    lane_f = lane.astype(F32)

    def first_argmax(vals):
        mx = jnp.max(vals, axis=-1, keepdims=True)
        idx = jnp.min(jnp.where(vals == mx, lane_f, float(LANES)), axis=-1, keepdims=True)
        return mx, idx

    is_grp = (lane >= N_EXPERTS) & (lane < N_EXPERTS + N_GROUPS)
    gl = jnp.where(is_grp, logits, NEG_INF)
    gmax, gidx = first_argmax(gl)
    p_g = 1.0 / jnp.sum(jnp.where(is_grp, jnp.exp(gl - gmax), 0.0), axis=-1, keepdims=True)
    e_lo = (gidx - float(N_EXPERTS)) * float(EXPERTS_PER_GROUP)
    in_grp = (lane_f >= e_lo) & (lane_f < e_lo + float(EXPERTS_PER_GROUP))
    el = jnp.where(in_grp, logits, NEG_INF)
    m1, i1 = first_argmax(el)
    m2, i2 = first_argmax(jnp.where(lane_f == i1, NEG_INF, el))
    e2 = jnp.exp(m2 - m1)
    w1 = p_g / (1.0 + e2)
    w2 = p_g * e2 / (1.0 + e2)

    hit1 = lane_f == i1
    hit2 = lane_f == i2
    onehot = jnp.where(hit1 | hit2, 1.0, 0.0)
    before = carry_ref[...] + _dot(ltri_ref[...], onehot.astype(BF16))
    rank1 = jnp.sum(jnp.where(hit1, before, 0.0), axis=-1, keepdims=True)
    rank2 = jnp.sum(jnp.where(hit2, before, 0.0), axis=-1, keepdims=True)
    total = carry_ref[...] + jnp.sum(onehot, axis=0, keepdims=True)
    carry_ref[...] = total
    cnt_ref[...] = total

    rec = jnp.zeros((tm, LANES), F32)
    for k, val in ((R_E1, i1), (R_E2, i2), (R_W1, w1), (R_W2, w2), (R_RANK1, rank1), (R_RANK2, rank2)):
        rec = jnp.where(lane == k, val, rec)
    route_ref[...] = rec
    route_t_ref[...] = rec.T[0:8, :]


def _merge(x, oa, ob, gab, w_fox_up, w_nsa_up, w_out, norm_ffn_g, w_group, b_group, w_router, b_router):
    B, S, D = x.shape
    tm = min(TM_MERGE, S)
    wa = w_fox_up.astype(BF16)
    wn = w_nsa_up.reshape(NSA_KV_HEADS, NSA_REP, HEAD_DIM, D)
    z = jnp.zeros_like(wn[0])
    wb = jnp.stack([jnp.concatenate([wn[0], z], axis=1), jnp.concatenate([z, wn[1]], axis=1)])
    wb = wb.reshape(NSA_HEADS * LANES, D).astype(BF16)
    wo = w_out.astype(BF16)
    wr = jnp.pad(jnp.concatenate([w_router, w_group], axis=1).astype(F32),
                 ((0, 0), (0, LANES - N_EXPERTS - N_GROUPS)))
    wr_hi = wr.astype(BF16)
    wr_lo = (wr - wr_hi.astype(F32)).astype(BF16)
    br = jnp.pad(jnp.concatenate([b_router, b_group]).astype(F32), (0, LANES - N_EXPERTS - N_GROUPS))[None, :]
    g2 = norm_ffn_g.astype(F32)[None, :]
    rt = np.arange(tm)
    ltri = (rt[None, :] < rt[:, None]).astype(BF16)

    row = lambda n: pl.BlockSpec((1, tm, n), lambda b, s: (b, s, 0))
    flat = lambda n: pl.BlockSpec((tm, n), lambda b, s: (b * (S // tm) + s, 0))
    flat_rows = pl.BlockSpec((tm * ROW_SUBLANES, LANES), lambda b, s: (b * (S // tm) + s, 0))
    consts = [wa, wb, wo, g2, wr_hi, wr_lo, br, ltri]
    return pl.pallas_call(
        _merge_kernel,
        grid=(B, S // tm),
        in_specs=[row(D), row(FOX_W), row(NSA_HEADS * LANES), row(2 * D)] + [_const_spec(a.shape) for a in consts],
        out_specs=[flat_rows, flat(LANES), pl.BlockSpec((8, tm), lambda b, s: (0, b * (S // tm) + s)),
                   _const_spec((1, LANES))],
        out_shape=[jax.ShapeDtypeStruct((B * S * ROW_SUBLANES, LANES), F32), jax.ShapeDtypeStruct((B * S, LANES), F32),
                   jax.ShapeDtypeStruct((8, B * S), F32), jax.ShapeDtypeStruct((1, LANES), F32)],
        scratch_shapes=[pltpu.VMEM((1, LANES), F32)],
        compiler_params=_cparams(("arbitrary", "arbitrary")),
        name="merge",
    )(x, oa, ob, gab, *consts)


ROW_SUBLANES = D_MODEL // LANES


def _to_tile_rows(ref, x):
    n = x.shape[0]
    for c in range(ROW_SUBLANES):
        ref[pl.ds(c, n, stride=ROW_SUBLANES), :] = x[:, LANES * c:LANES * (c + 1)]


def _from_tile_rows(ref):
    n = ref.shape[0] // ROW_SUBLANES
    return jnp.concatenate([ref[pl.ds(c, n, stride=ROW_SUBLANES), :] for c in range(ROW_SUBLANES)], axis=1)


def _row_copy(src_ref, src_row, dst_ref, dst_row, sem):
    src = src_ref.at[pl.ds(pl.multiple_of(src_row * ROW_SUBLANES, ROW_SUBLANES), ROW_SUBLANES), :]
    dst = dst_ref.at[pl.ds(pl.multiple_of(dst_row * ROW_SUBLANES, ROW_SUBLANES), ROW_SUBLANES), :]
    return pltpu.make_async_copy(src, dst, sem)


DISPATCH_SLOTS = 3


def _dispatch_kernel(row1_ref, row2_ref, clear_ref, x1_ref, xs_ref, stage_ref, zero_ref, in_sem, out_sem, zsem):
    tm = stage_ref.shape[1] // ROW_SUBLANES
    t = pl.program_id(0)
    n_steps = pl.num_programs(0)

    def stage(step):
        rows = pl.ds(pl.multiple_of(step * (tm * ROW_SUBLANES), tm * ROW_SUBLANES), tm * ROW_SUBLANES)
        slot = step % DISPATCH_SLOTS
        return pltpu.make_async_copy(x1_ref.at[rows, :], stage_ref.at[slot], in_sem.at[slot])

    def for_each_scatter(step, fn):
        base = step * tm
        slot = step % DISPATCH_SLOTS

        def body(r, _):
            fn(_row_copy(stage_ref.at[slot], r, xs_ref, row1_ref[base + r], out_sem.at[slot]), 0)
            fn(_row_copy(stage_ref.at[slot], r, xs_ref, row2_ref[base + r], out_sem.at[slot]), 1)
            return _

        lax.fori_loop(0, tm, body, 0, unroll=8)

    @pl.when(t == 0)
    def _():
        stage(0).start()
        zero_ref[...] = jnp.zeros_like(zero_ref)

        def clear(c):
            start = pl.multiple_of(jnp.maximum(clear_ref[c], 0) * ROW_SUBLANES, zero_ref.shape[0])
            return pltpu.make_async_copy(zero_ref, xs_ref.at[pl.ds(start, zero_ref.shape[0]), :], zsem)

        for c in range(clear_ref.shape[0]):
            @pl.when(clear_ref[c] >= 0)
            def _(c=c):
                clear(c).start()
        for c in range(clear_ref.shape[0]):
            @pl.when(clear_ref[c] >= 0)
            def _(c=c):
                clear(c).wait()

    @pl.when(t >= 2)
    def _():
        for_each_scatter(t - 2, lambda c, _: c.wait())

    @pl.when(t + 1 < n_steps)
    def _():
        stage(t + 1).start()

    stage(t).wait()
    for_each_scatter(t, lambda c, priority: c.start(priority=priority))

    @pl.when(t + 1 == n_steps)
    def _():
        @pl.when(t >= 1)
        def _():
            for_each_scatter(t - 1, lambda c, _: c.wait())
        for_each_scatter(t, lambda c, _: c.wait())


def _experts_kernel(tile_e_ref, n_used_ref, xs_ref, g2_ref, wg_ref, wu_ref, wd_ref, ys_ref,
                    wg_bf, wu_bf, wd_bf):
    k = pl.program_id(0)
    used = k < n_used_ref[0]

    @pl.when(used & ((k == 0) | (tile_e_ref[k] != tile_e_ref[jnp.maximum(k - 1, 0)])))
    def _():
        wg_bf[...] = wg_ref[0, 0].astype(BF16)
        wu_bf[...] = wu_ref[0, 0].astype(BF16)
        wd_bf[...] = wd_ref[0, 0].astype(BF16)

    @pl.when(used)
    def _():
        x = _from_tile_rows(xs_ref)
        h = (x * lax.rsqrt(jnp.mean(x * x, axis=-1, keepdims=True) + RMS_EPS) * g2_ref[...]).astype(BF16)
        a = _dot(h, wg_bf[...])
        hid = (a * jax.nn.sigmoid(a)) * _dot(h, wu_bf[...])
        _to_tile_rows(ys_ref, _dot(hid.astype(BF16), wd_bf[...]))

    @pl.when(jnp.logical_not(used))
    def _():
        ys_ref[...] = jnp.zeros_like(ys_ref)


def _combine_kernel(row1_ref, row2_ref, x1_ref, route_ref, ys_ref, o_ref, y1_ref, y2_ref, sem):
    tm = x1_ref.shape[0] // ROW_SUBLANES
    t = pl.program_id(0)
    slot = t & 1

    def for_each_copy(step, slot, fn):
        base = step * tm

        def body(r, _):
            fn(_row_copy(ys_ref, row1_ref[base + r], y1_ref.at[slot], r, sem.at[slot]), 0)
            fn(_row_copy(ys_ref, row2_ref[base + r], y2_ref.at[slot], r, sem.at[slot]), 1)
            return _

        lax.fori_loop(0, tm, body, 0, unroll=8)

    @pl.when(t == 0)
    def _():
        for_each_copy(0, 0, lambda c, priority: c.start(priority=priority))

    @pl.when(t + 1 < pl.num_programs(0))
    def _():
        for_each_copy(t + 1, 1 - slot, lambda c, priority: c.start(priority=priority))

    for_each_copy(t, slot, lambda c, _: c.wait())
    rec = route_ref[...]
    lane = lax.broadcasted_iota(jnp.int32, rec.shape, 1)
    w1 = jnp.sum(jnp.where(lane == R_W1, rec, 0.0), axis=-1, keepdims=True)
    w2 = jnp.sum(jnp.where(lane == R_W2, rec, 0.0), axis=-1, keepdims=True)
    o_ref[0] = _from_tile_rows(x1_ref) + (w1 * _from_tile_rows(y1_ref.at[slot]) + w2 * _from_tile_rows(y2_ref.at[slot]))


def _moe(x1, route, route_t, cnt, norm_ffn_g, w_gate, w_up, w_down, layer, B, S):
    T, D = B * S, D_MODEL
    tme = min(TM_EXPERT, T)
    tmd = min(TM_DISPATCH, S)

    counts = cnt[0, :N_EXPERTS].astype(jnp.int32)
    tiles_per_e = (counts + tme - 1) // tme
    tile_end = jnp.cumsum(tiles_per_e)
    offs = (tile_end - tiles_per_e) * tme
    last_tile_row = jnp.where(tiles_per_e > 0, (tile_end - 1) * tme, -1).astype(jnp.int32)
    rec = route_t.astype(jnp.int32)
    experts = jnp.arange(N_EXPERTS, dtype=jnp.int32)[:, None]
    row_of = lambda e, rank: rank + jnp.sum(jnp.where(e[None, :] == experts, offs[:, None], 0), axis=0)
    row1 = row_of(rec[R_E1], rec[R_RANK1])
    row2 = row_of(rec[R_E2], rec[R_RANK2])
    max_tiles = (2 * T) // tme + N_EXPERTS
    tile_e = jnp.sum(jnp.arange(max_tiles, dtype=jnp.int32)[:, None] >= tile_end[None, :], axis=1)
    tile_e = jnp.minimum(tile_e, N_EXPERTS - 1).astype(jnp.int32)
    n_used = tile_end[-1:].astype(jnp.int32)
    n_rows = max_tiles * tme
    spare = n_used + jnp.arange(N_EXPERTS, dtype=jnp.int32)
    clear_rows = jnp.concatenate([last_tile_row, jnp.where(spare < max_tiles, spare * tme, -1)]).astype(jnp.int32)

    xs = pl.pallas_call(
        _dispatch_kernel,
        grid_spec=pltpu.PrefetchScalarGridSpec(
            num_scalar_prefetch=3,
            grid=(T // tmd,),
            in_specs=[pl.BlockSpec(memory_space=pl.ANY)],
            out_specs=pl.BlockSpec(memory_space=pl.ANY),
            scratch_shapes=[pltpu.VMEM((DISPATCH_SLOTS, tmd * ROW_SUBLANES, LANES), F32),
                            pltpu.VMEM((tme * ROW_SUBLANES, LANES), F32),
                            pltpu.SemaphoreType.DMA((DISPATCH_SLOTS,)), pltpu.SemaphoreType.DMA((DISPATCH_SLOTS,)),
                            pltpu.SemaphoreType.DMA(())],
        ),
        out_shape=jax.ShapeDtypeStruct((n_rows * ROW_SUBLANES, LANES), F32),
        compiler_params=_cparams(("arbitrary",)),
        name="dispatch",
    )(row1, row2, clear_rows, x1)

    g2 = norm_ffn_g.astype(F32)[None, :]
    w_spec = lambda shape: pl.BlockSpec((1, 1) + shape, lambda k, te, nu: (layer, te[k], 0, 0))
    ys = pl.pallas_call(
        _experts_kernel,
        grid_spec=pltpu.PrefetchScalarGridSpec(
            num_scalar_prefetch=2,
            grid=(max_tiles,),
            in_specs=[pl.BlockSpec((tme * ROW_SUBLANES, LANES), lambda k, te, nu: (jnp.minimum(k, nu[0] - 1), 0)),
                      pl.BlockSpec((1, D), lambda k, te, nu: (0, 0)),
                      w_spec((D, D_EXPERT)), w_spec((D, D_EXPERT)), w_spec((D_EXPERT, D))],
            out_specs=pl.BlockSpec((tme * ROW_SUBLANES, LANES), lambda k, te, nu: (k, 0)),
            scratch_shapes=[pltpu.VMEM((D, D_EXPERT), BF16), pltpu.VMEM((D, D_EXPERT), BF16),
                            pltpu.VMEM((D_EXPERT, D), BF16)],
        ),
        out_shape=jax.ShapeDtypeStruct((n_rows * ROW_SUBLANES, LANES), F32),
        compiler_params=_cparams(("arbitrary",)),
        name="experts",
    )(tile_e, n_used, xs, g2, w_gate, w_up, w_down)

    return pl.pallas_call(
        _combine_kernel,
        grid_spec=pltpu.PrefetchScalarGridSpec(
            num_scalar_prefetch=2,
            grid=(T // tmd,),
            in_specs=[pl.BlockSpec((tmd * ROW_SUBLANES, LANES), lambda t, r1, r2: (t, 0)),
                      pl.BlockSpec((tmd, LANES), lambda t, r1, r2: (t, 0)),
                      pl.BlockSpec(memory_space=pl.ANY)],
            out_specs=pl.BlockSpec((1, tmd, D), lambda t, r1, r2: (t // (S // tmd), t % (S // tmd), 0)),
            scratch_shapes=[pltpu.VMEM((2, tmd * ROW_SUBLANES, LANES), F32), pltpu.VMEM((2, tmd * ROW_SUBLANES, LANES), F32),
                            pltpu.SemaphoreType.DMA((2,))],
        ),
        out_shape=jax.ShapeDtypeStruct((B, S, D), F32),
        compiler_params=_cparams(("arbitrary",)),
        name="combine",
    )(row1, row2, x1, route, ys)


def kernel(x, norm_mix_g, w_in, b_forget, fox_q_g, fox_k_g, nsa_q_g, nsa_k_g, cmp_k_w1, cmp_k_w2, cmp_k_pos,
           cmp_v_w1, cmp_v_w2, cmp_v_pos, w_fox_up, w_nsa_up, w_out, norm_ffn_g, w_group, b_group, w_router,
           b_router, w_gate, w_up, w_down):
    B, S, D = x.shape
    for l in range(w_in.shape[0]):
        qa, ka, va, qn, kcr, vcr, ksa, vsa, kwa, vwa, gt, gab = _inproj(
            x, norm_mix_g[l], w_in[l], b_forget[l], fox_q_g[l], fox_k_g[l], nsa_q_g[l], nsa_k_g[l])
        kca, vca = _compress(kcr, vcr, cmp_k_w1[l], cmp_k_w2[l], cmp_k_pos[l],
                             cmp_v_w1[l], cmp_v_w2[l], cmp_v_pos[l], nsa_k_g[l])
        ob = _nsa(qn, kca, vca, ksa, vsa, kwa, vwa, gt, _scores_bounded(nsa_q_g[l], nsa_k_g[l]))
        oa = _fox(qa, ka, va, _scores_bounded(fox_q_g[l], fox_k_g[l]))
        x1, route, route_t, cnt = _merge(x, oa, ob, gab, w_fox_up[l], w_nsa_up[l], w_out[l], norm_ffn_g[l],
                                         w_group[l], b_group[l], w_router[l], b_router[l])
        x = _moe(x1, route, route_t, cnt, norm_ffn_g[l], w_gate, w_up, w_down, l, B, S)
    return x
```

```python
import functools

import jax
import jax.numpy as jnp
import numpy as np
from jax import lax
from jax.experimental import pallas as pl
from jax.experimental.pallas import tpu as pltpu

F32 = jnp.float32
BF16 = jnp.bfloat16

D_MODEL = 1024
HEAD_DIM = 64
FOX_HEADS = 8
NSA_HEADS = 8
NSA_KV_HEADS = 2
NSA_REP = NSA_HEADS // NSA_KV_HEADS
CMP_BLOCK = 32
CMP_STRIDE = 16
CMP_HIDDEN = 256
SLC_BLOCK = 64
SLC_TOPK = 16
WINDOW = 512
N_GROUPS = 4
EXPERTS_PER_GROUP = 4
N_EXPERTS = N_GROUPS * EXPERTS_PER_GROUP
D_EXPERT = 512
RMS_EPS = 1e-6
NEG_INF = -1e30
FORCE_SCORE = 1e4

LANES = 128
HALF = LANES // 2
VMEM_LIMIT = 56 * 1024 * 1024

FOX_W = FOX_HEADS * HEAD_DIM
NSA_W = NSA_HEADS * HEAD_DIM
NSA_KV_W = NSA_KV_HEADS * HEAD_DIM

OFF_FQ = 0
OFF_FK = OFF_FQ + FOX_W
OFF_FV = OFF_FK + FOX_W
OFF_NQ = OFF_FV + FOX_W
OFF_KC = OFF_NQ + NSA_W
OFF_VC = OFF_KC + NSA_KV_W
OFF_KS = OFF_VC + NSA_KV_W
OFF_VS = OFF_KS + NSA_KV_W
OFF_KW = OFF_VS + NSA_KV_W
OFF_VW = OFF_KW + NSA_KV_W
OFF_FF = OFF_VW + NSA_KV_W
OFF_NG = OFF_FF + LANES
OFF_GA = OFF_NG + LANES
N_PROJ = OFF_GA + 2 * D_MODEL

TM_PROJ = 512
TQ_FOX = 512
FOX_UNROLL = 4
TQ_NSA = 256
TK_SLC = 256
SLC_UNROLL = 4
TM_MERGE = 512
TM_EXPERT = 512
TM_DISPATCH = 512


def _dot(a, b):
    return jnp.dot(a, b, preferred_element_type=F32)


def _dot_nt(a, b):
    return lax.dot_general(a, b, (((1,), (1,)), ((), ())), preferred_element_type=F32)


def _split3(x):
    hi = x.astype(BF16)
    r = x - hi.astype(F32)
    mid = r.astype(BF16)
    lo = (r - mid.astype(F32)).astype(BF16)
    return hi, mid, lo


def _cparams(sem):
    return pltpu.CompilerParams(dimension_semantics=sem, vmem_limit_bytes=VMEM_LIMIT)


def _const_spec(shape):
    nd = len(shape)
    return pl.BlockSpec(shape, lambda *_: (0,) * nd)


def _inproj_kernel(x_ref, g_ref, w_ref, bf_ref, gqa_ref, gka_ref, gqn_ref, gkn_ref,
                   bd_ref, bd2_ref, tri_ref, cq_ref, ck_ref, cv_ref,
                   qc0_ref, qca_ref, qcb_ref, kc0_ref, kca_ref, kcb_ref, vone_ref,
                   qa_ref, ka_ref, va_ref, qn_ref, kcr_ref, vcr_ref,
                   ksa_ref, vsa_ref, kwa_ref, vwa_ref, gt_ref, gab_ref,
                   carry_ref):
    tm = x_ref.shape[1]

    @pl.when(pl.program_id(1) == 0)
    def _():
        carry_ref[...] = jnp.zeros_like(carry_ref)

    x = x_ref[0]
    y = x * lax.rsqrt(jnp.mean(x * x, axis=-1, keepdims=True) + RMS_EPS)
    h = (y * g_ref[...]).astype(BF16)

    def proj(off, n):
        return _dot(h, w_ref[:, off:off + n])

    lo_half = lax.broadcasted_iota(jnp.int32, (tm, LANES), 1) < HALF

    pos = pl.program_id(1) * tm + lax.broadcasted_iota(jnp.int32, (tm, 1), 0)
    pos_a = ((pos >> 8) << 8).astype(F32)
    pos_b = (pos & 255).astype(F32)

    def pos_channels(c0_ref, ca_ref, cb_ref, k):
        blk = slice(LANES * k, LANES * (k + 1))
        return c0_ref[:, blk] + ca_ref[:, blk] * pos_a + cb_ref[:, blk] * pos_b

    def headnorm(z, bd, grow):
        nh = z.shape[1] // HEAD_DIM
        inv = lax.rsqrt(_dot((z * z).astype(BF16), bd[...]) + RMS_EPS)
        blocks = [jnp.where(lo_half, inv[:, 2 * m:2 * m + 1], inv[:, 2 * m + 1:2 * m + 2]) for m in range(nh // 2)]
        return z * jnp.concatenate(blocks, axis=1) * grow[...]

    def spread_pairs(out_ref, src, aug):
        for m in range(4):
            s = src[:, LANES * m:LANES * (m + 1)]
            out_ref[0, :, LANES * 2 * m:LANES * (2 * m + 1)] = jnp.where(lo_half, s, aug(2 * m)).astype(BF16)
            out_ref[0, :, LANES * (2 * m + 1):LANES * (2 * m + 2)] = jnp.where(lo_half, aug(2 * m + 1), s).astype(BF16)

    zf = proj(OFF_FF, LANES) + bf_ref[...]
    logf = jnp.minimum(zf, 0.0) - jnp.log(1.0 + jnp.exp(-jnp.abs(zf)))
    l_hi, l_mid, l_lo = _split3(logf)
    tri = tri_ref[...]
    cum = carry_ref[...] + (_dot(tri, l_hi) + _dot(tri, l_mid) + _dot(tri, l_lo))
    carry_ref[...] = cum[tm - 1:tm, :]
    pieces = [p.astype(F32) for p in _split3(cum)]
    lane128 = lax.broadcasted_iota(jnp.int32, (tm, LANES), 1)

    def aug_block(k, const_ref, first, sign):
        base = HALF if k % 2 == 0 else 0
        blk = jnp.broadcast_to(const_ref[:, LANES * k:LANES * (k + 1)], (tm, LANES))
        for j, piece in enumerate(pieces):
            blk = jnp.where(lane128 == base + first + j, sign * piece[:, k:k + 1], blk)
        return blk

    augq = lambda k: aug_block(k, cq_ref, 3, 1.0)
    augk = lambda k: aug_block(k, ck_ref, 0, -1.0)

    zq = headnorm(proj(OFF_FQ, FOX_W), bd_ref, gqa_ref)
    spread_pairs(qa_ref, zq, augq)
    zk = headnorm(proj(OFF_FK, FOX_W), bd_ref, gka_ref)
    spread_pairs(ka_ref, zk, augk)
    zv = proj(OFF_FV, FOX_W)
    spread_pairs(va_ref, zv, lambda k: cv_ref[:, LANES * k:LANES * (k + 1)])

    zn = headnorm(proj(OFF_NQ, NSA_W), bd_ref, gqn_ref)
    for m in range(NSA_REP):
        s = zn[:, LANES * m:LANES * (m + 1)]
        c0 = pos_channels(qc0_ref, qca_ref, qcb_ref, m)
        c1 = pos_channels(qc0_ref, qca_ref, qcb_ref, NSA_REP + m)
        qn_ref[0, :, LANES * m:LANES * (m + 1)] = jnp.where(lo_half, s, c0).astype(BF16)
        qn_ref[0, :, LANES * (NSA_REP + m):LANES * (NSA_REP + m + 1)] = jnp.where(lo_half, c1, s).astype(BF16)

    kcr_ref[0] = proj(OFF_KC, NSA_KV_W)
    vcr_ref[0] = proj(OFF_VC, NSA_KV_W)

    kp0 = pos_channels(kc0_ref, kca_ref, kcb_ref, 0)
    kp1 = pos_channels(kc0_ref, kca_ref, kcb_ref, 1)

    def kv_pair(k_out, v_out, off_k, off_v):
        zk2 = headnorm(proj(off_k, NSA_KV_W), bd2_ref, gkn_ref)
        k_out[0, :, 0:LANES] = jnp.where(lo_half, zk2, kp0).astype(BF16)
        k_out[0, :, LANES:2 * LANES] = jnp.where(lo_half, kp1, zk2).astype(BF16)
        zv2 = proj(off_v, NSA_KV_W)
        v_out[0, :, 0:LANES] = jnp.where(lo_half, zv2, vone_ref[:, 0:LANES]).astype(BF16)
        v_out[0, :, LANES:2 * LANES] = jnp.where(lo_half, vone_ref[:, LANES:2 * LANES], zv2).astype(BF16)

    kv_pair(ksa_ref, vsa_ref, OFF_KS, OFF_VS)
    kv_pair(kwa_ref, vwa_ref, OFF_KW, OFF_VW)

    gt_ref[0] = jax.nn.sigmoid(proj(OFF_NG, LANES))
    gab_ref[0, :, 0:D_MODEL] = jax.nn.sigmoid(proj(OFF_GA, D_MODEL)).astype(BF16)
    gab_ref[0, :, D_MODEL:2 * D_MODEL] = jax.nn.sigmoid(proj(OFF_GA + D_MODEL, D_MODEL)).astype(BF16)


def _pos_pieces(pos):
    return ((pos // 256) * 256).astype(np.float32), (pos % 256).astype(np.float32)


def _inproj(x, norm_g, w_in, b_forget, fox_q_g, fox_k_g, nsa_q_g, nsa_k_g):
    B, S, D = x.shape
    tm = min(TM_PROJ, S)
    scale = HEAD_DIM ** -0.5

    c = [0]
    for n in (FOX_W, FOX_W, FOX_W, FOX_HEADS, NSA_W) + (NSA_KV_W,) * 6 + (3 * NSA_HEADS, D_MODEL, D_MODEL):
        c.append(c[-1] + n)
    fq, fk, fv, ff, nq, kc, vc, ks, vs, kw, vw, ng, ga, gb = [w_in[:, c[i]:c[i + 1]] for i in range(14)]
    perm = jnp.asarray([0, 4, 1, 5, 2, 6, 3, 7])
    nq = nq.reshape(D, NSA_HEADS, HEAD_DIM)[:, perm, :].reshape(D, NSA_W)
    padl = lambda a: jnp.pad(a, ((0, 0), (0, LANES - a.shape[1])))
    w = jnp.concatenate([fq, fk, fv, nq, kc, vc, ks, vs, kw, vw, padl(ff), padl(ng), ga, gb], axis=1).astype(BF16)
    assert w.shape[1] == N_PROJ

    bf = jnp.pad(b_forget.astype(F32), (0, LANES - FOX_HEADS))[None, :]
    gqa = jnp.tile(fox_q_g.astype(F32) * scale, FOX_HEADS)[None, :]
    gka = jnp.tile(fox_k_g.astype(F32), FOX_HEADS)[None, :]
    gqn = jnp.tile(nsa_q_g.astype(F32) * scale, NSA_HEADS)[None, :]
    gkn = jnp.tile(nsa_k_g.astype(F32), NSA_KV_HEADS)[None, :]

    r512 = np.arange(FOX_W)
    bd = np.where((r512[:, None] // HEAD_DIM) == np.arange(LANES)[None, :], 1.0 / HEAD_DIM, 0.0).astype(BF16)
    bd2 = bd[:LANES, :]
    rt = np.arange(tm)
    tri = (rt[None, :] <= rt[:, None]).astype(BF16)

    heads = np.arange(FOX_HEADS)
    base = heads * LANES + np.where(heads % 2 == 0, HALF, 0)
    cols = np.arange(FOX_HEADS * LANES)
    off_in_blk = cols - base[cols // LANES]
    cq = ((off_in_blk >= 0) & (off_in_blk < 3)).astype(F32)[None, :]
    ck = ((off_in_blk >= 3) & (off_in_blk < 9)).astype(F32)[None, :]
    cv = (off_in_blk == 0).astype(F32)[None, :]

    blk = cols // LANES
    slope = 2.0 ** (-(blk + 1).astype(F32))
    o = cols % LANES - np.where(blk // NSA_REP == 0, HALF, 0)
    qc0 = np.where((o == 0) | (o == 1), slope, 0.0)[None, :]
    qca = np.where(o == 2, -slope, 0.0)[None, :]
    qcb = np.where(o == 3, -slope, 0.0)[None, :]
    kc0, kca, kcb, vone = _kv_rows()

    grid = (B, S // tm)
    row_spec = lambda n: pl.BlockSpec((1, tm, n), lambda b, s: (b, s, 0))
    consts = [norm_g.astype(F32)[None, :], w, bf, gqa, gka, gqn, gkn, bd, bd2, tri, cq, ck, cv,
              qc0, qca, qcb, kc0, kca, kcb, vone]
    out_widths = [(8 * LANES, BF16)] * 4 + [(LANES, F32)] * 2 + [(2 * LANES, BF16)] * 4 + \
                 [(LANES, F32), (2 * D_MODEL, BF16)]
    outs = pl.pallas_call(
        _inproj_kernel,
        grid=grid,
        in_specs=[row_spec(D)] + [_const_spec(a.shape) for a in consts],
        out_specs=[row_spec(n) for n, _ in out_widths],
        out_shape=[jax.ShapeDtypeStruct((B, S, n), dt) for n, dt in out_widths],
        scratch_shapes=[pltpu.VMEM((1, LANES), F32)],
        compiler_params=_cparams(("arbitrary", "arbitrary")),
        name="inproj",
    )(x, *consts)
    return outs


def _kv_rows():
    cols = np.arange(2 * LANES)
    o = cols % LANES - np.where(cols // LANES == 0, HALF, 0)
    row = lambda m: m.astype(np.float32)[None, :]
    return row((o >= 2) & (o <= 6)), row(o == 0), row(o == 1), row(o == 0)


def _kv_consts(pa, pb):
    c0, ca, cb, vone = _kv_rows()
    return (c0 + ca * pa[:, None] + cb * pb[:, None]).astype(BF16), vone


def _compress_kernel(kt_ref, vt_ref, w1k_ref, w1v_ref, posk_ref, posv_ref, pw1k_ref, pw1v_ref, w2k_ref, w2v_ref,
                     gk_ref, kcc_ref, vone_ref, kc_ref, vc_ref):
    nc = kc_ref.shape[2]

    def mlp(t_ref, w1_ref, pos_ref, pw1_ref, w2_ref):
        both = jnp.zeros((nc, 2 * CMP_HIDDEN), F32)
        for l in range(CMP_STRIDE):
            rows = t_ref[0, pl.ds(l, nc, stride=CMP_STRIDE), :].astype(BF16)
            both = both + _dot(rows, w1_ref[0, l])
        posw = _dot(pos_ref[...], pw1_ref[...])[0:1, :]
        pre = both[:, 0:CMP_HIDDEN] + pltpu.roll(both[:, CMP_HIDDEN:2 * CMP_HIDDEN], nc - 1, axis=0) + posw
        act = pre * (0.5 * (1.0 + jnp.tanh(0.7978845608028654 * (pre + 0.044715 * (pre * pre * pre)))))
        return _dot(act.astype(BF16), w2_ref[0])

    kc = mlp(kt_ref, w1k_ref, posk_ref, pw1k_ref, w2k_ref)
    msq = jnp.sum(kc * kc, axis=-1, keepdims=True) * (1.0 / HEAD_DIM)
    kc_ref[0, 0] = (kc * lax.rsqrt(msq + RMS_EPS) * gk_ref[0] + kcc_ref[0]).astype(BF16)
    vc = mlp(vt_ref, w1v_ref, posv_ref, pw1v_ref, w2v_ref)
    vc_ref[0, 0] = (vc + vone_ref[0]).astype(BF16)


def _compress(kcr, vcr, cmp_k_w1, cmp_k_w2, cmp_k_pos, cmp_v_w1, cmp_v_w2, cmp_v_pos, nsa_k_g):
    B, S, _ = kcr.shape
    G = NSA_KV_HEADS
    nc = S // CMP_STRIDE

    def w1_strided(w1):
        w = w1.reshape(2, CMP_STRIDE, HEAD_DIM, CMP_HIDDEN)
        w = jnp.concatenate([w[0], w[1]], axis=-1)
        z = jnp.zeros_like(w)
        return jnp.stack([jnp.concatenate([w, z], axis=1), jnp.concatenate([z, w], axis=1)]).astype(BF16)

    def w2_spread(w2):
        z = jnp.zeros_like(w2)
        return jnp.stack([jnp.concatenate([w2, z], 1), jnp.concatenate([z, w2], 1)]).astype(BF16)

    def pos8(p):
        return jnp.tile(p.reshape(1, CMP_BLOCK * HEAD_DIM), (8, 1)).astype(BF16)

    gk = nsa_k_g.astype(F32)
    z = jnp.zeros_like(gk)
    gk2 = jnp.stack([jnp.concatenate([gk, z]), jnp.concatenate([z, gk])])[:, None, :]
    cend = np.arange(nc) * CMP_STRIDE + CMP_BLOCK - 1
    kcc, vone = _kv_consts(*_pos_pieces(cend))
    kcc = kcc.astype(F32).reshape(nc, G, LANES).transpose(1, 0, 2)
    vone = vone.reshape(G, 1, LANES)

    tok = pl.BlockSpec((1, S, G * HEAD_DIM), lambda b, g: (b, 0, 0))
    per_g = lambda a: pl.BlockSpec((1,) + a.shape[1:], lambda b, g: (g,) + (0,) * (a.ndim - 1))
    w1k, w1v = w1_strided(cmp_k_w1), w1_strided(cmp_v_w1)
    pk, pv = pos8(cmp_k_pos), pos8(cmp_v_pos)
    pw1k, pw1v = cmp_k_w1.astype(BF16), cmp_v_w1.astype(BF16)
    w2k, w2v = w2_spread(cmp_k_w2), w2_spread(cmp_v_w2)
    out_spec = pl.BlockSpec((1, 1, nc, LANES), lambda b, g: (b, g, 0, 0))
    return pl.pallas_call(
        _compress_kernel,
        grid=(B, G),
        in_specs=[tok, tok, per_g(w1k), per_g(w1v), _const_spec(pk.shape), _const_spec(pv.shape),
                  _const_spec(pw1k.shape), _const_spec(pw1v.shape), per_g(w2k), per_g(w2v),
                  per_g(gk2), per_g(kcc), per_g(vone)],
        out_specs=[out_spec, out_spec],
        out_shape=[jax.ShapeDtypeStruct((B, G, nc, LANES), BF16)] * 2,
        compiler_params=_cparams(("arbitrary", "arbitrary")),
        name="compress",
    )(kcr, vcr, w1k, w1v, pk, pv, pw1k, pw1v, w2k, w2v, gk2, kcc, vone)


MAX_EXPONENT = 60.0


def _scores_bounded(q_g, k_g):
    bound = HEAD_DIM ** 0.5 * jnp.max(jnp.abs(q_g)) * jnp.max(jnp.abs(k_g))
    return 2.04 * bound + 0.05 <= MAX_EXPONENT


def _attn_first(s, v):
    m = jnp.max(s, axis=-1, keepdims=True)
    p = jnp.exp((s - m).astype(BF16))
    return m, _dot(p, v)


def _attn_step(carry, s, v):
    m, acc = carry
    m_new = jnp.maximum(m, jnp.max(s, axis=-1, keepdims=True))
    p = jnp.exp((s - m_new).astype(BF16))
    return m_new, jnp.exp(m - m_new) * acc + _dot(p, v)


def _nsa_kernel(bounded, q_ref, kc_ref, vc_ref, ks_ref, e_ref, vs_ref, kw_ref, vw_ref, gt_ref, ov_ref, o_ref,
                tiles_ref, sa_ref, sb_ref):
    tq = q_ref.shape[1]
    nc = kc_ref.shape[2]
    tk = TK_SLC
    rq = NSA_REP * tq
    g = pl.program_id(1)
    i = pl.program_id(2)
    q0 = i * tq

    q4 = jnp.concatenate([q_ref[0, :, LANES * r:LANES * (r + 1)] for r in range(NSA_REP)], axis=0)

    def qpos_of(shape):
        return q0 + (lax.broadcasted_iota(jnp.int32, shape, 0) & (tq - 1))

    def add_mask(s, valid):
        bias = jnp.where(valid, 0.0, NEG_INF)
        return (s.reshape(NSA_REP, tq, s.shape[1]) + bias[None]).reshape(s.shape)

    def qrow(n):
        return q0 + lax.broadcasted_iota(jnp.int32, (tq, n), 0)

    def kcol(n):
        return lax.broadcasted_iota(jnp.int32, (tq, n), 1)

    anyv = qpos_of((rq, 1)) >= CMP_BLOCK - 1

    def cmp_branch(n):
        s = add_mask(_dot_nt(q4, kc_ref[0, 0, 0:n, :]), qrow(n) >= kcol(n) * CMP_STRIDE + (CMP_BLOCK - 1))
        if bounded:
            e = jnp.exp(s)
            p = e * jnp.where(anyv, 1.0 / jnp.sum(e, axis=-1, keepdims=True), 0.0)
        else:
            m = jnp.max(s, axis=-1, keepdims=True)
            e = jnp.exp(s - m)
            p = e * (anyv.astype(F32) / jnp.sum(e, axis=-1, keepdims=True))
        o = _dot(p.astype(BF16), vc_ref[0, 0, 0:n, :])
        psum = p[0:tq] + p[tq:2 * tq] + p[2 * tq:3 * tq] + p[3 * tq:4 * tq]
        ov = ov_ref[0:n, :]
        return o, functools.reduce(lambda a, b: a + b, [_dot(piece, ov) for piece in _split3(psum)])

    last_visible = (q0 + tq - CMP_BLOCK) // CMP_STRIDE
    o_cmp, imp = lax.cond(last_visible < nc // 2, lambda: cmp_branch(nc // 2), lambda: cmp_branch(nc))

    blk_i = lax.broadcasted_iota(jnp.int32, (LANES, tq), 0)
    blk_f = blk_i.astype(F32)
    qblk = (q0 + lax.broadcasted_iota(jnp.int32, (LANES, tq), 1)) // SLC_BLOCK
    forced = (blk_i == 0) | (blk_i == qblk) | (blk_i == qblk - 1)
    score = jnp.where(forced, -3e38, jnp.where(blk_i > qblk, -1.0, imp.T))
    selb = jnp.where(forced, 0.0, NEG_INF)
    for _ in range(SLC_TOPK - 3):
        mx = jnp.max(score, axis=0, keepdims=True)
        first = jnp.min(jnp.where(score == mx, blk_f, float(LANES)), axis=0, keepdims=True)
        hit = blk_f == first
        selb = jnp.where(hit, 0.0, selb)
        score = jnp.where(hit, -3e38, score)
    sb = selb.T.astype(BF16)
    sb4 = jnp.concatenate([sb] * NSA_REP, axis=0)

    def slc_qk(j):
        start = pl.multiple_of(j * tk, tk)
        kk = jnp.concatenate([ks_ref[0, pl.ds(start, tk), :], e_ref[pl.ds(start, tk), :]], axis=1)
        return _dot_nt(q2, kk)

    def slc_v(j):
        return vs_ref[0, pl.ds(pl.multiple_of(j * tk, tk), tk), :]

    jd = q0 // tk
    blocks_per_tile = tk // SLC_BLOCK
    anysel = jnp.max(selb, axis=1, keepdims=True)
    cnt = jnp.int32(0)
    for j in range(ks_ref.shape[1] // tk):
        tile_sel = jnp.max(anysel[blocks_per_tile * j:blocks_per_tile * (j + 1), :]) > -1.0
        tiles_ref[cnt] = j
        cnt = cnt + (tile_sel & (j < jd)).astype(jnp.int32)

    lane_r = lax.broadcasted_iota(jnp.int32, (rq, LANES), 1)
    data0 = HALF * g
    ones_lane = HALF - data0
    nw = WINDOW + tq
    wstart = pl.multiple_of(jnp.maximum(q0 - WINDOW, 0), tq)
    win_valid = lax.bitcast_convert_type(qrow(nw) - (wstart + kcol(nw)), jnp.uint32) < WINDOW
    diag_valid = jd * tk + kcol(tk) <= qrow(tk)

    if bounded:
        def shifted_by_own_score(kself):
            own = jnp.sum(q4.astype(F32).reshape(NSA_REP, tq, LANES) * kself.astype(F32)[None],
                          axis=-1, keepdims=True).reshape(rq, 1)
            qf = q4.astype(F32)
            for off, piece in enumerate(_split3(-own)):
                qf = jnp.where(lane_r == ones_lane + 4 + off, piece.astype(F32), qf)
            return qf.astype(BF16)

        q2 = jnp.concatenate([shifted_by_own_score(ks_ref[0, pl.ds(q0, tq), :]), sb4], axis=1)
        sa_ref[:, 0:LANES] = _dot(jnp.exp(add_mask(slc_qk(jd), diag_valid)).astype(BF16), slc_v(jd))

        def slc_sweep(t0, n):
            pv = []
            for t in range(n):
                j = tiles_ref[t0 + t]
                pv.append(_dot(jnp.exp(slc_qk(j)).astype(BF16), slc_v(j)))
            sa_ref[:, 0:LANES] += functools.reduce(lambda a, b: a + b, pv)

        def slc_body(c, _):
            slc_sweep(c * SLC_UNROLL, SLC_UNROLL)
            return _

        lax.fori_loop(0, cnt // SLC_UNROLL, slc_body, 0)
        done = (cnt // SLC_UNROLL) * SLC_UNROLL
        n = SLC_UNROLL // 2
        while n >= 1:
            @pl.when((cnt & n) != 0)
            def _(n=n, start=done):
                slc_sweep(start, n)
            done = done + (cnt & n)
            n //= 2

        acc_slc = sa_ref[:, 0:LANES]

        qw = shifted_by_own_score(kw_ref[0, pl.ds(q0, tq), :])
        s = add_mask(_dot_nt(qw, kw_ref[0, pl.ds(wstart, nw), :]), win_valid)
        acc_win = _dot(jnp.exp(s).astype(BF16), vw_ref[0, pl.ds(wstart, nw), :])
    else:
        q2 = jnp.concatenate([q4, sb4], axis=1)
        n_seq = cnt + 1

        def tile_of(n):
            return jnp.where(n == 0, jd, tiles_ref[jnp.maximum(n - 1, 0)])

        def slc_step(carry, s_buf, n):
            return _attn_step(carry, s_buf[...], slc_v(tile_of(n)))

        sa_ref[...] = add_mask(slc_qk(jd), diag_valid)

        def slc_body(p, carry):
            sb_ref[...] = slc_qk(tile_of(2 * p + 1))
            carry = slc_step(carry, sa_ref, 2 * p)
            sa_ref[...] = slc_qk(tile_of(jnp.minimum(2 * p + 2, n_seq - 1)))
            return slc_step(carry, sb_ref, 2 * p + 1)

        carry = (jnp.full((rq, 1), NEG_INF, F32), jnp.zeros((rq, LANES), F32))
        carry = lax.fori_loop(0, n_seq // 2, slc_body, carry)
        _, acc_slc = lax.cond(n_seq % 2 == 1, lambda c: slc_step(c, sa_ref, n_seq - 1), lambda c: c, carry)

        s = add_mask(_dot_nt(q4, kw_ref[0, pl.ds(wstart, nw), :]), win_valid)
        _, acc_win = _attn_first(s, vw_ref[0, pl.ds(wstart, nw), :])


    def normalise(acc):
        l = jnp.sum(jnp.where(lane_r == ones_lane, acc, 0.0), axis=-1, keepdims=True)
        return acc * (1.0 / l)

    o_slc = normalise(acc_slc)
    o_win = normalise(acc_win)
    gt = gt_ref[0]
    lane_g = lax.broadcasted_iota(jnp.int32, (tq, LANES), 1)
    is_data_q = (lane_g >= data0) & (lane_g < data0 + HALF)
    for r in range(NSA_REP):
        col = 3 * (NSA_REP * g + r)
        gate = [jnp.sum(jnp.where(lane_g == col + b, gt, 0.0), axis=-1, keepdims=True) for b in range(3)]
        rows = slice(r * tq, (r + 1) * tq)
        o = gate[0] * o_cmp[rows] + gate[1] * o_slc[rows] + gate[2] * o_win[rows]
        o_ref[0, :, LANES * r:LANES * (r + 1)] = jnp.where(is_data_q, o, 0.0).astype(BF16)


def _nsa(qn, kca, vca, ksa, vsa, kwa, vwa, gt, scores_bounded):
    B, S, _ = qn.shape
    G = NSA_KV_HEADS
    tq = min(TQ_NSA, S)
    nc = S // CMP_STRIDE
    n_slc = S // SLC_BLOCK
    assert n_slc <= LANES and S % TK_SLC == 0

    cs = np.arange(nc)[:, None] * CMP_STRIDE
    ss = np.arange(LANES)[None, :] * SLC_BLOCK
    ovl = np.clip(np.minimum(cs + CMP_BLOCK, ss + SLC_BLOCK) - np.maximum(cs, ss), 0, None)
    valid = (np.arange(nc)[:, None] < (S - CMP_BLOCK) // CMP_STRIDE + 1) & (np.arange(LANES)[None, :] < n_slc)
    ov = np.where(valid, ovl.astype(F32) / CMP_BLOCK, 0.0).astype(BF16)
    e1h = (np.arange(S)[:, None] // SLC_BLOCK == np.arange(LANES)[None, :]).astype(BF16)

    q_spec = pl.BlockSpec((1, tq, NSA_REP * LANES), lambda b, g, i: (b, i, g))
    c_spec = pl.BlockSpec((1, 1, nc, LANES), lambda b, g, i: (b, g, 0, 0))
    kv_spec = pl.BlockSpec((1, S, LANES), lambda b, g, i: (b, 0, g))

    def call(bounded, name):
        return pl.pallas_call(
            functools.partial(_nsa_kernel, bounded),
            grid=(B, G, S // tq),
            in_specs=[q_spec, c_spec, c_spec, kv_spec, _const_spec(e1h.shape), kv_spec, kv_spec, kv_spec,
                      pl.BlockSpec((1, tq, LANES), lambda b, g, i: (b, i, 0)), _const_spec(ov.shape)],
            out_specs=q_spec,
            out_shape=jax.ShapeDtypeStruct((B, S, NSA_HEADS * LANES), BF16),
            scratch_shapes=[pltpu.SMEM((S // TK_SLC + 1,), jnp.int32),
                            pltpu.VMEM((NSA_REP * tq, TK_SLC), F32), pltpu.VMEM((NSA_REP * tq, TK_SLC), F32)],
            compiler_params=_cparams(("arbitrary", "arbitrary", "arbitrary")),
            name=name,
        )

    return lax.cond(scores_bounded, call(True, "nsa_bounded"), call(False, "nsa"),
                    qn, kca, vca, ksa, e1h, vsa, kwa, vwa, gt, ov)


def _fox_kernel(q_ref, k_ref, v_ref, o_ref, sa_ref, sb_ref):
    tq = q_ref.shape[1]
    i = pl.program_id(2)
    lane = lax.broadcasted_iota(jnp.int32, (tq, LANES), 1)
    causal = lax.broadcasted_iota(jnp.int32, (tq, tq), 1) <= lax.broadcasted_iota(jnp.int32, (tq, tq), 0)

    def cols(hh):
        return slice(LANES * hh, LANES * (hh + 1))

    def qk(hh, j):
        start = pl.multiple_of(j * tq, tq)
        return _dot_nt(q_ref[0, :, cols(hh)], k_ref[0, pl.ds(start, tq), cols(hh)])

    def vtile(hh, j):
        return v_ref[0, pl.ds(pl.multiple_of(j * tq, tq), tq), cols(hh)]

    n_tiles = i + 1

    def tile_of(n):
        return jnp.where(n == 0, i, n - 1)

    def step_all(carry, s_buf, n):
        t = tile_of(n)
        return tuple(_attn_step(carry[hh], s_buf[hh], vtile(hh, t)) for hh in range(2))

    for hh in range(2):
        sa_ref[hh] = jnp.where(causal, qk(hh, i), NEG_INF)
    init = tuple((jnp.full((tq, 1), NEG_INF, F32), jnp.zeros((tq, LANES), F32)) for hh in range(2))

    def body(p, carry):
        for hh in range(2):
            sb_ref[hh] = qk(hh, tile_of(2 * p + 1))
        carry = step_all(carry, sa_ref, 2 * p)
        nxt = jnp.minimum(2 * p + 2, n_tiles - 1)
        for hh in range(2):
            sa_ref[hh] = qk(hh, tile_of(nxt))
        return step_all(carry, sb_ref, 2 * p + 1)

    carry = lax.fori_loop(0, n_tiles // 2, body, init)
    carry = lax.cond(n_tiles % 2 == 1, lambda c: step_all(c, sa_ref, n_tiles - 1), lambda c: c, carry)
    outs = []
    for hh in range(2):
        acc = carry[hh][1]
        ones_lane = HALF if hh == 0 else 0
        l = jnp.sum(jnp.where(lane == ones_lane, acc, 0.0), axis=-1, keepdims=True)
        outs.append(acc * (1.0 / l))
    o_ref[0] = jnp.where(lane < HALF, outs[0], outs[1]).astype(BF16)


def _fox_bounded_kernel(q_ref, k_ref, v_ref, o_ref, qs_ref, acc_ref):
    tq = q_ref.shape[1]
    i = pl.program_id(2)
    lane = lax.broadcasted_iota(jnp.int32, (tq, LANES), 1)
    causal = lax.broadcasted_iota(jnp.int32, (tq, tq), 1) <= lax.broadcasted_iota(jnp.int32, (tq, tq), 0)

    def cols(hh):
        return slice(LANES * hh, LANES * (hh + 1))

    def ktile(hh, j):
        return k_ref[0, pl.ds(pl.multiple_of(j * tq, tq), tq), cols(hh)]

    def vtile(hh, j):
        return v_ref[0, pl.ds(pl.multiple_of(j * tq, tq), tq), cols(hh)]

    for hh in range(2):
        q = q_ref[0, :, cols(hh)]
        s = jnp.where(causal, _dot_nt(q, ktile(hh, i)), NEG_INF)
        m = jnp.max(s, axis=-1, keepdims=True)
        acc_ref[hh] = _dot(jnp.exp((s - m).astype(BF16)), vtile(hh, i))
        qf = q.astype(F32)
        free0 = (HALF if hh == 0 else 0) + 6
        for off, piece in enumerate(_split3(-m)):
            qf = jnp.where(lane == free0 + off, piece.astype(F32), qf)
        qs_ref[hh] = qf.astype(BF16)

    def sweep(j0, n):
        for hh in range(2):
            pv = [_dot(jnp.exp(_dot_nt(qs_ref[hh], ktile(hh, j0 + t)).astype(BF16)), vtile(hh, j0 + t))
                  for t in range(n)]
            acc_ref[hh] += functools.reduce(lambda a, b: a + b, pv)

    def body(c, _):
        sweep(c * FOX_UNROLL, FOX_UNROLL)
        return _

    lax.fori_loop(0, i // FOX_UNROLL, body, 0)
    done = (i // FOX_UNROLL) * FOX_UNROLL
    n = FOX_UNROLL // 2
    while n >= 1:
        @pl.when((i & n) != 0)
        def _(n=n, start=done):
            sweep(start, n)
        done = done + (i & n)
        n //= 2
    outs = []
    for hh in range(2):
        acc = acc_ref[hh]
        ones_lane = HALF if hh == 0 else 0
        l = jnp.sum(jnp.where(lane == ones_lane, acc, 0.0), axis=-1, keepdims=True)
        outs.append(acc * (1.0 / l))
    o_ref[0] = jnp.where(lane < HALF, outs[0], outs[1]).astype(BF16)


def _fox(qa, ka, va, scores_bounded):
    B, S, _ = qa.shape
    tq = min(TQ_FOX, S)
    q_spec = pl.BlockSpec((1, tq, 2 * LANES), lambda b, h, i: (b, i, h))
    kv_spec = pl.BlockSpec((1, S, 2 * LANES), lambda b, h, i: (b, 0, h))

    def call(body, scratch, name):
        return pl.pallas_call(
            body,
            grid=(B, FOX_HEADS // 2, S // tq),
            in_specs=[q_spec, kv_spec, kv_spec],
            out_specs=pl.BlockSpec((1, tq, LANES), lambda b, h, i: (b, i, h)),
            out_shape=jax.ShapeDtypeStruct((B, S, FOX_W), BF16),
            scratch_shapes=scratch,
            compiler_params=_cparams(("arbitrary", "arbitrary", "arbitrary")),
            name=name,
        )

    general = call(_fox_kernel, [pltpu.VMEM((2, tq, tq), F32), pltpu.VMEM((2, tq, tq), F32)], "fox")
    bounded = call(_fox_bounded_kernel, [pltpu.VMEM((2, tq, LANES), BF16), pltpu.VMEM((2, tq, LANES), F32)],
                   "fox_bounded")
    return lax.cond(scores_bounded, bounded, general, qa, ka, va)


R_E1, R_E2, R_W1, R_W2, R_RANK1, R_RANK2 = range(6)


def _merge_kernel(x_ref, oa_ref, ob_ref, gab_ref, wa_ref, wb_ref, wo_ref, g2_ref, wr_hi_ref, wr_lo_ref, br_ref,
                  ltri_ref, x1_ref, route_ref, route_t_ref, cnt_ref, carry_ref):
    tm = x_ref.shape[1]

    @pl.when((pl.program_id(0) == 0) & (pl.program_id(1) == 0))
    def _():
        carry_ref[...] = jnp.zeros_like(carry_ref)

    out_a = _dot(oa_ref[0], wa_ref[...])
    out_b = _dot(ob_ref[0], wb_ref[...])
    mix = gab_ref[0, :, 0:D_MODEL].astype(F32) * out_a + gab_ref[0, :, D_MODEL:2 * D_MODEL].astype(F32) * out_b
    x1 = x_ref[0] + _dot(mix.astype(BF16), wo_ref[...])
    _to_tile_rows(x1_ref, x1)
    h2 = x1 * lax.rsqrt(jnp.mean(x1 * x1, axis=-1, keepdims=True) + RMS_EPS) * g2_ref[...]

    h_hi = h2.astype(BF16)
    h_lo = (h2 - h_hi.astype(F32)).astype(BF16)
    logits = _dot(h_hi, wr_hi_ref[...]) + (_dot(h_hi, wr_lo_ref[...]) + _dot(h_lo, wr_hi_ref[...])) + br_ref[...]
    lane = lax.broadcasted_iota(jnp.int32, (tm, LANES), 1)
    lane_f = lane.astype(F32)

    def first_argmax(vals):
        mx = jnp.max(vals, axis=-1, keepdims=True)
        idx = jnp.min(jnp.where(vals == mx, lane_f, float(LANES)), axis=-1, keepdims=True)
        return mx, idx

    is_grp = (lane >= N_EXPERTS) & (lane < N_EXPERTS + N_GROUPS)
    gl = jnp.where(is_grp, logits, NEG_INF)
    gmax, gidx = first_argmax(gl)
    p_g = 1.0 / jnp.sum(jnp.where(is_grp, jnp.exp(gl - gmax), 0.0), axis=-1, keepdims=True)
    e_lo = (gidx - float(N_EXPERTS)) * float(EXPERTS_PER_GROUP)
    in_grp = (lane_f >= e_lo) & (lane_f < e_lo + float(EXPERTS_PER_GROUP))
    el = jnp.where(in_grp, logits, NEG_INF)
    m1, i1 = first_argmax(el)
    m2, i2 = first_argmax(jnp.where(lane_f == i1, NEG_INF, el))
    e2 = jnp.exp(m2 - m1)
    w1 = p_g / (1.0 + e2)
    w2 = p_g * e2 / (1.0 + e2)

    hit1 = lane_f == i1
    hit2 = lane_f == i2
    onehot = jnp.where(hit1 | hit2, 1.0, 0.0)
    before = carry_ref[...] + _dot(ltri_ref[...], onehot.astype(BF16))
    rank1 = jnp.sum(jnp.where(hit1, before, 0.0), axis=-1, keepdims=True)
    rank2 = jnp.sum(jnp.where(hit2, before, 0.0), axis=-1, keepdims=True)
    total = carry_ref[...] + jnp.sum(onehot, axis=0, keepdims=True)
    carry_ref[...] = total
    cnt_ref[...] = total

    rec = jnp.zeros((tm, LANES), F32)
    for k, val in ((R_E1, i1), (R_E2, i2), (R_W1, w1), (R_W2, w2), (R_RANK1, rank1), (R_RANK2, rank2)):
        rec = jnp.where(lane == k, val, rec)
    route_ref[...] = rec
    route_t_ref[...] = rec.T[0:8, :]


def _merge(x, oa, ob, gab, w_fox_up, w_nsa_up, w_out, norm_ffn_g, w_group, b_group, w_router, b_router):
    B, S, D = x.shape
    tm = min(TM_MERGE, S)
    wa = w_fox_up.astype(BF16)
    wn = w_nsa_up.reshape(NSA_KV_HEADS, NSA_REP, HEAD_DIM, D)
    z = jnp.zeros_like(wn[0])
    wb = jnp.stack([jnp.concatenate([wn[0], z], axis=1), jnp.concatenate([z, wn[1]], axis=1)])
    wb = wb.reshape(NSA_HEADS * LANES, D).astype(BF16)
    wo = w_out.astype(BF16)
    wr = jnp.pad(jnp.concatenate([w_router, w_group], axis=1).astype(F32),
                 ((0, 0), (0, LANES - N_EXPERTS - N_GROUPS)))
    wr_hi = wr.astype(BF16)
    wr_lo = (wr - wr_hi.astype(F32)).astype(BF16)
    br = jnp.pad(jnp.concatenate([b_router, b_group]).astype(F32), (0, LANES - N_EXPERTS - N_GROUPS))[None, :]
    g2 = norm_ffn_g.astype(F32)[None, :]
    rt = np.arange(tm)
    ltri = (rt[None, :] < rt[:, None]).astype(BF16)

    row = lambda n: pl.BlockSpec((1, tm, n), lambda b, s: (b, s, 0))
    flat = lambda n: pl.BlockSpec((tm, n), lambda b, s: (b * (S // tm) + s, 0))
    flat_rows = pl.BlockSpec((tm * ROW_SUBLANES, LANES), lambda b, s: (b * (S // tm) + s, 0))
    consts = [wa, wb, wo, g2, wr_hi, wr_lo, br, ltri]
    return pl.pallas_call(
        _merge_kernel,
        grid=(B, S // tm),
        in_specs=[row(D), row(FOX_W), row(NSA_HEADS * LANES), row(2 * D)] + [_const_spec(a.shape) for a in consts],
        out_specs=[flat_rows, flat(LANES), pl.BlockSpec((8, tm), lambda b, s: (0, b * (S // tm) + s)),
                   _const_spec((1, LANES))],
        out_shape=[jax.ShapeDtypeStruct((B * S * ROW_SUBLANES, LANES), F32), jax.ShapeDtypeStruct((B * S, LANES), F32),
                   jax.ShapeDtypeStruct((8, B * S), F32), jax.ShapeDtypeStruct((1, LANES), F32)],
        scratch_shapes=[pltpu.VMEM((1, LANES), F32)],
        compiler_params=_cparams(("arbitrary", "arbitrary")),
        name="merge",
    )(x, oa, ob, gab, *consts)


ROW_SUBLANES = D_MODEL // LANES


def _to_tile_rows(ref, x):
    n = x.shape[0]
    for c in range(ROW_SUBLANES):
        ref[pl.ds(c, n, stride=ROW_SUBLANES), :] = x[:, LANES * c:LANES * (c + 1)]


def _from_tile_rows(ref):
    n = ref.shape[0] // ROW_SUBLANES
    return jnp.concatenate([ref[pl.ds(c, n, stride=ROW_SUBLANES), :] for c in range(ROW_SUBLANES)], axis=1)


def _row_copy(src_ref, src_row, dst_ref, dst_row, sem):
    src = src_ref.at[pl.ds(pl.multiple_of(src_row * ROW_SUBLANES, ROW_SUBLANES), ROW_SUBLANES), :]
    dst = dst_ref.at[pl.ds(pl.multiple_of(dst_row * ROW_SUBLANES, ROW_SUBLANES), ROW_SUBLANES), :]
    return pltpu.make_async_copy(src, dst, sem)


DISPATCH_SLOTS = 3


def _dispatch_kernel(row1_ref, row2_ref, clear_ref, x1_ref, xs_ref, stage_ref, zero_ref, in_sem, out_sem, zsem):
    tm = stage_ref.shape[1] // ROW_SUBLANES
    t = pl.program_id(0)
    n_steps = pl.num_programs(0)

    def stage(step):
        rows = pl.ds(pl.multiple_of(step * (tm * ROW_SUBLANES), tm * ROW_SUBLANES), tm * ROW_SUBLANES)
        slot = step % DISPATCH_SLOTS
        return pltpu.make_async_copy(x1_ref.at[rows, :], stage_ref.at[slot], in_sem.at[slot])

    def for_each_scatter(step, fn):
        base = step * tm
        slot = step % DISPATCH_SLOTS

        def body(r, _):
            fn(_row_copy(stage_ref.at[slot], r, xs_ref, row1_ref[base + r], out_sem.at[slot]), 0)
            fn(_row_copy(stage_ref.at[slot], r, xs_ref, row2_ref[base + r], out_sem.at[slot]), 1)
            return _

        lax.fori_loop(0, tm, body, 0, unroll=8)

    @pl.when(t == 0)
    def _():
        stage(0).start()
        zero_ref[...] = jnp.zeros_like(zero_ref)

        def clear(c):
            start = pl.multiple_of(jnp.maximum(clear_ref[c], 0) * ROW_SUBLANES, zero_ref.shape[0])
            return pltpu.make_async_copy(zero_ref, xs_ref.at[pl.ds(start, zero_ref.shape[0]), :], zsem)

        for c in range(clear_ref.shape[0]):
            @pl.when(clear_ref[c] >= 0)
            def _(c=c):
                clear(c).start()
        for c in range(clear_ref.shape[0]):
            @pl.when(clear_ref[c] >= 0)
            def _(c=c):
                clear(c).wait()

    @pl.when(t >= 2)
    def _():
        for_each_scatter(t - 2, lambda c, _: c.wait())

    @pl.when(t + 1 < n_steps)
    def _():
        stage(t + 1).start()

    stage(t).wait()
    for_each_scatter(t, lambda c, priority: c.start(priority=priority))

    @pl.when(t + 1 == n_steps)
    def _():
        @pl.when(t >= 1)
        def _():
            for_each_scatter(t - 1, lambda c, _: c.wait())
        for_each_scatter(t, lambda c, _: c.wait())


def _experts_kernel(tile_e_ref, n_used_ref, xs_ref, g2_ref, wg_ref, wu_ref, wd_ref, ys_ref,
                    wg_bf, wu_bf, wd_bf):
    k = pl.program_id(0)
    used = k < n_used_ref[0]

    @pl.when(used & ((k == 0) | (tile_e_ref[k] != tile_e_ref[jnp.maximum(k - 1, 0)])))
    def _():
        wg_bf[...] = wg_ref[0, 0].astype(BF16)
        wu_bf[...] = wu_ref[0, 0].astype(BF16)
        wd_bf[...] = wd_ref[0, 0].astype(BF16)

    @pl.when(used)
    def _():
        x = _from_tile_rows(xs_ref)
        h = (x * lax.rsqrt(jnp.mean(x * x, axis=-1, keepdims=True) + RMS_EPS) * g2_ref[...]).astype(BF16)
        a = _dot(h, wg_bf[...])
        hid = (a * jax.nn.sigmoid(a)) * _dot(h, wu_bf[...])
        _to_tile_rows(ys_ref, _dot(hid.astype(BF16), wd_bf[...]))

    @pl.when(jnp.logical_not(used))
    def _():
        ys_ref[...] = jnp.zeros_like(ys_ref)


def _combine_kernel(row1_ref, row2_ref, x1_ref, route_ref, ys_ref, o_ref, y1_ref, y2_ref, sem):
    tm = x1_ref.shape[0] // ROW_SUBLANES
    t = pl.program_id(0)
    slot = t & 1

    def for_each_copy(step, slot, fn):
        base = step * tm

        def body(r, _):
            fn(_row_copy(ys_ref, row1_ref[base + r], y1_ref.at[slot], r, sem.at[slot]), 0)
            fn(_row_copy(ys_ref, row2_ref[base + r], y2_ref.at[slot], r, sem.at[slot]), 1)
            return _

        lax.fori_loop(0, tm, body, 0, unroll=8)

    @pl.when(t == 0)
    def _():
        for_each_copy(0, 0, lambda c, priority: c.start(priority=priority))

    @pl.when(t + 1 < pl.num_programs(0))
    def _():
        for_each_copy(t + 1, 1 - slot, lambda c, priority: c.start(priority=priority))

    for_each_copy(t, slot, lambda c, _: c.wait())
    rec = route_ref[...]
    lane = lax.broadcasted_iota(jnp.int32, rec.shape, 1)
    w1 = jnp.sum(jnp.where(lane == R_W1, rec, 0.0), axis=-1, keepdims=True)
    w2 = jnp.sum(jnp.where(lane == R_W2, rec, 0.0), axis=-1, keepdims=True)
    o_ref[0] = _from_tile_rows(x1_ref) + (w1 * _from_tile_rows(y1_ref.at[slot]) + w2 * _from_tile_rows(y2_ref.at[slot]))


def _moe(x1, route, route_t, cnt, norm_ffn_g, w_gate, w_up, w_down, layer, B, S):
    T, D = B * S, D_MODEL
    tme = min(TM_EXPERT, T)
    tmd = min(TM_DISPATCH, S)

    counts = cnt[0, :N_EXPERTS].astype(jnp.int32)
    tiles_per_e = (counts + tme - 1) // tme
    tile_end = jnp.cumsum(tiles_per_e)
    offs = (tile_end - tiles_per_e) * tme
    last_tile_row = jnp.where(tiles_per_e > 0, (tile_end - 1) * tme, -1).astype(jnp.int32)
    rec = route_t.astype(jnp.int32)
    experts = jnp.arange(N_EXPERTS, dtype=jnp.int32)[:, None]
    row_of = lambda e, rank: rank + jnp.sum(jnp.where(e[None, :] == experts, offs[:, None], 0), axis=0)
    row1 = row_of(rec[R_E1], rec[R_RANK1])
    row2 = row_of(rec[R_E2], rec[R_RANK2])
    max_tiles = (2 * T) // tme + N_EXPERTS
    tile_e = jnp.sum(jnp.arange(max_tiles, dtype=jnp.int32)[:, None] >= tile_end[None, :], axis=1)
    tile_e = jnp.minimum(tile_e, N_EXPERTS - 1).astype(jnp.int32)
    n_used = tile_end[-1:].astype(jnp.int32)
    n_rows = max_tiles * tme
    spare = n_used + jnp.arange(N_EXPERTS, dtype=jnp.int32)
    clear_rows = jnp.concatenate([last_tile_row, jnp.where(spare < max_tiles, spare * tme, -1)]).astype(jnp.int32)

    xs = pl.pallas_call(
        _dispatch_kernel,
        grid_spec=pltpu.PrefetchScalarGridSpec(
            num_scalar_prefetch=3,
            grid=(T // tmd,),
            in_specs=[pl.BlockSpec(memory_space=pl.ANY)],
            out_specs=pl.BlockSpec(memory_space=pl.ANY),
            scratch_shapes=[pltpu.VMEM((DISPATCH_SLOTS, tmd * ROW_SUBLANES, LANES), F32),
                            pltpu.VMEM((tme * ROW_SUBLANES, LANES), F32),
                            pltpu.SemaphoreType.DMA((DISPATCH_SLOTS,)), pltpu.SemaphoreType.DMA((DISPATCH_SLOTS,)),
                            pltpu.SemaphoreType.DMA(())],
        ),
        out_shape=jax.ShapeDtypeStruct((n_rows * ROW_SUBLANES, LANES), F32),
        compiler_params=_cparams(("arbitrary",)),
        name="dispatch",
    )(row1, row2, clear_rows, x1)

    g2 = norm_ffn_g.astype(F32)[None, :]
    w_spec = lambda shape: pl.BlockSpec((1, 1) + shape, lambda k, te, nu: (layer, te[k], 0, 0))
    ys = pl.pallas_call(
        _experts_kernel,
        grid_spec=pltpu.PrefetchScalarGridSpec(
            num_scalar_prefetch=2,
            grid=(max_tiles,),
            in_specs=[pl.BlockSpec((tme * ROW_SUBLANES, LANES), lambda k, te, nu: (jnp.minimum(k, nu[0] - 1), 0)),
                      pl.BlockSpec((1, D), lambda k, te, nu: (0, 0)),
                      w_spec((D, D_EXPERT)), w_spec((D, D_EXPERT)), w_spec((D_EXPERT, D))],
            out_specs=pl.BlockSpec((tme * ROW_SUBLANES, LANES), lambda k, te, nu: (k, 0)),
            scratch_shapes=[pltpu.VMEM((D, D_EXPERT), BF16), pltpu.VMEM((D, D_EXPERT), BF16),
                            pltpu.VMEM((D_EXPERT, D), BF16)],
        ),
        out_shape=jax.ShapeDtypeStruct((n_rows * ROW_SUBLANES, LANES), F32),
        compiler_params=_cparams(("arbitrary",)),
        name="experts",
    )(tile_e, n_used, xs, g2, w_gate, w_up, w_down)

    return pl.pallas_call(
        _combine_kernel,
        grid_spec=pltpu.PrefetchScalarGridSpec(
            num_scalar_prefetch=2,
            grid=(T // tmd,),
            in_specs=[pl.BlockSpec((tmd * ROW_SUBLANES, LANES), lambda t, r1, r2: (t, 0)),
                      pl.BlockSpec((tmd, LANES), lambda t, r1, r2: (t, 0)),
                      pl.BlockSpec(memory_space=pl.ANY)],
            out_specs=pl.BlockSpec((1, tmd, D), lambda t, r1, r2: (t // (S // tmd), t % (S // tmd), 0)),
            scratch_shapes=[pltpu.VMEM((2, tmd * ROW_SUBLANES, LANES), F32), pltpu.VMEM((2, tmd * ROW_SUBLANES, LANES), F32),
                            pltpu.SemaphoreType.DMA((2,))],
        ),
        out_shape=jax.ShapeDtypeStruct((B, S, D), F32),
        compiler_params=_cparams(("arbitrary",)),
        name="combine",
    )(row1, row2, x1, route, ys)


def kernel(x, norm_mix_g, w_in, b_forget, fox_q_g, fox_k_g, nsa_q_g, nsa_k_g, cmp_k_w1, cmp_k_w2, cmp_k_pos,
           cmp_v_w1, cmp_v_w2, cmp_v_pos, w_fox_up, w_nsa_up, w_out, norm_ffn_g, w_group, b_group, w_router,
           b_router, w_gate, w_up, w_down):
    B, S, D = x.shape
    for l in range(w_in.shape[0]):
        qa, ka, va, qn, kcr, vcr, ksa, vsa, kwa, vwa, gt, gab = _inproj(
            x, norm_mix_g[l], w_in[l], b_forget[l], fox_q_g[l], fox_k_g[l], nsa_q_g[l], nsa_k_g[l])
        kca, vca = _compress(kcr, vcr, cmp_k_w1[l], cmp_k_w2[l], cmp_k_pos[l],
                             cmp_v_w1[l], cmp_v_w2[l], cmp_v_pos[l], nsa_k_g[l])
        ob = _nsa(qn, kca, vca, ksa, vsa, kwa, vwa, gt, _scores_bounded(nsa_q_g[l], nsa_k_g[l]))
        oa = _fox(qa, ka, va, _scores_bounded(fox_q_g[l], fox_k_g[l]))
        x1, route, route_t, cnt = _merge(x, oa, ob, gab, w_fox_up[l], w_nsa_up[l], w_out[l], norm_ffn_g[l],
                                         w_group[l], b_group[l], w_router[l], b_router[l])
        x = _moe(x1, route, route_t, cnt, norm_ffn_g[l], w_gate, w_up, w_down, l, B, S)
    return x
```

```python
import functools

import jax
import jax.numpy as jnp
import numpy as np
from jax import lax
from jax.experimental import pallas as pl
from jax.experimental.pallas import tpu as pltpu

F32 = jnp.float32
BF16 = jnp.bfloat16

D_MODEL = 1024
HEAD_DIM = 64
FOX_HEADS = 8
NSA_HEADS = 8
NSA_KV_HEADS = 2
NSA_REP = NSA_HEADS // NSA_KV_HEADS
CMP_BLOCK = 32
CMP_STRIDE = 16
CMP_HIDDEN = 256
SLC_BLOCK = 64
SLC_TOPK = 16
WINDOW = 512
N_GROUPS = 4
EXPERTS_PER_GROUP = 4
N_EXPERTS = N_GROUPS * EXPERTS_PER_GROUP
D_EXPERT = 512
RMS_EPS = 1e-6
NEG_INF = -1e30
FORCE_SCORE = 1e4

LANES = 128
HALF = LANES // 2
VMEM_LIMIT = 56 * 1024 * 1024

FOX_W = FOX_HEADS * HEAD_DIM
NSA_W = NSA_HEADS * HEAD_DIM
NSA_KV_W = NSA_KV_HEADS * HEAD_DIM

OFF_FQ = 0
OFF_FK = OFF_FQ + FOX_W
OFF_FV = OFF_FK + FOX_W
OFF_NQ = OFF_FV + FOX_W
OFF_KC = OFF_NQ + NSA_W
OFF_VC = OFF_KC + NSA_KV_W
OFF_KS = OFF_VC + NSA_KV_W
OFF_VS = OFF_KS + NSA_KV_W
OFF_KW = OFF_VS + NSA_KV_W
OFF_VW = OFF_KW + NSA_KV_W
OFF_FF = OFF_VW + NSA_KV_W
OFF_NG = OFF_FF + LANES
OFF_GA = OFF_NG + LANES
N_PROJ = OFF_GA + 2 * D_MODEL

TM_PROJ = 512
TQ_FOX = 512
FOX_UNROLL = 4
TQ_NSA = 256
TK_SLC = 256
SLC_UNROLL = 4
TM_MERGE = 512
TM_EXPERT = 512
TM_DISPATCH = 512


def _dot(a, b):
    return jnp.dot(a, b, preferred_element_type=F32)


def _dot_nt(a, b):
    return lax.dot_general(a, b, (((1,), (1,)), ((), ())), preferred_element_type=F32)


def _split3(x):
    hi = x.astype(BF16)
    r = x - hi.astype(F32)
    mid = r.astype(BF16)
    lo = (r - mid.astype(F32)).astype(BF16)
    return hi, mid, lo


def _cparams(sem):
    return pltpu.CompilerParams(dimension_semantics=sem, vmem_limit_bytes=VMEM_LIMIT)


def _const_spec(shape):
    nd = len(shape)
    return pl.BlockSpec(shape, lambda *_: (0,) * nd)


def _inproj_kernel(x_ref, g_ref, w_ref, bf_ref, gqa_ref, gka_ref, gqn_ref, gkn_ref,
                   bd_ref, bd2_ref, tri_ref, cq_ref, ck_ref, cv_ref,
                   qc0_ref, qca_ref, qcb_ref, kc0_ref, kca_ref, kcb_ref, vone_ref,
                   qa_ref, ka_ref, va_ref, qn_ref, kcr_ref, vcr_ref,
                   ksa_ref, vsa_ref, kwa_ref, vwa_ref, gt_ref, gab_ref,
                   carry_ref):
    tm = x_ref.shape[1]

    @pl.when(pl.program_id(1) == 0)
    def _():
        carry_ref[...] = jnp.zeros_like(carry_ref)

    x = x_ref[0]
    y = x * lax.rsqrt(jnp.mean(x * x, axis=-1, keepdims=True) + RMS_EPS)
    h = (y * g_ref[...]).astype(BF16)

    def proj(off, n):
        return _dot(h, w_ref[:, off:off + n])

    lo_half = lax.broadcasted_iota(jnp.int32, (tm, LANES), 1) < HALF

    pos = pl.program_id(1) * tm + lax.broadcasted_iota(jnp.int32, (tm, 1), 0)
    pos_a = ((pos >> 8) << 8).astype(F32)
    pos_b = (pos & 255).astype(F32)

    def pos_channels(c0_ref, ca_ref, cb_ref, k):
        blk = slice(LANES * k, LANES * (k + 1))
        return c0_ref[:, blk] + ca_ref[:, blk] * pos_a + cb_ref[:, blk] * pos_b

    def headnorm(z, bd, grow):
        nh = z.shape[1] // HEAD_DIM
        inv = lax.rsqrt(_dot((z * z).astype(BF16), bd[...]) + RMS_EPS)
        blocks = [jnp.where(lo_half, inv[:, 2 * m:2 * m + 1], inv[:, 2 * m + 1:2 * m + 2]) for m in range(nh // 2)]
        return z * jnp.concatenate(blocks, axis=1) * grow[...]

    def spread_pairs(out_ref, src, aug):
        for m in range(4):
            s = src[:, LANES * m:LANES * (m + 1)]
            out_ref[0, :, LANES * 2 * m:LANES * (2 * m + 1)] = jnp.where(lo_half, s, aug(2 * m)).astype(BF16)
            out_ref[0, :, LANES * (2 * m + 1):LANES * (2 * m + 2)] = jnp.where(lo_half, aug(2 * m + 1), s).astype(BF16)

    zf = proj(OFF_FF, LANES) + bf_ref[...]
    logf = jnp.minimum(zf, 0.0) - jnp.log(1.0 + jnp.exp(-jnp.abs(zf)))
    l_hi, l_mid, l_lo = _split3(logf)
    tri = tri_ref[...]
    cum = carry_ref[...] + (_dot(tri, l_hi) + _dot(tri, l_mid) + _dot(tri, l_lo))
    carry_ref[...] = cum[tm - 1:tm, :]
    pieces = [p.astype(F32) for p in _split3(cum)]
    lane128 = lax.broadcasted_iota(jnp.int32, (tm, LANES), 1)

    def aug_block(k, const_ref, first, sign):
        base = HALF if k % 2 == 0 else 0
        blk = jnp.broadcast_to(const_ref[:, LANES * k:LANES * (k + 1)], (tm, LANES))
        for j, piece in enumerate(pieces):
            blk = jnp.where(lane128 == base + first + j, sign * piece[:, k:k + 1], blk)
        return blk

    augq = lambda k: aug_block(k, cq_ref, 3, 1.0)
    augk = lambda k: aug_block(k, ck_ref, 0, -1.0)

    zq = headnorm(proj(OFF_FQ, FOX_W), bd_ref, gqa_ref)
    spread_pairs(qa_ref, zq, augq)
    zk = headnorm(proj(OFF_FK, FOX_W), bd_ref, gka_ref)
    spread_pairs(ka_ref, zk, augk)
    zv = proj(OFF_FV, FOX_W)
    spread_pairs(va_ref, zv, lambda k: cv_ref[:, LANES * k:LANES * (k + 1)])

    zn = headnorm(proj(OFF_NQ, NSA_W), bd_ref, gqn_ref)
    for m in range(NSA_REP):
        s = zn[:, LANES * m:LANES * (m + 1)]
        c0 = pos_channels(qc0_ref, qca_ref, qcb_ref, m)
        c1 = pos_channels(qc0_ref, qca_ref, qcb_ref, NSA_REP + m)
        qn_ref[0, :, LANES * m:LANES * (m + 1)] = jnp.where(lo_half, s, c0).astype(BF16)
        qn_ref[0, :, LANES * (NSA_REP + m):LANES * (NSA_REP + m + 1)] = jnp.where(lo_half, c1, s).astype(BF16)

    kcr_ref[0] = proj(OFF_KC, NSA_KV_W)
    vcr_ref[0] = proj(OFF_VC, NSA_KV_W)

    kp0 = pos_channels(kc0_ref, kca_ref, kcb_ref, 0)
    kp1 = pos_channels(kc0_ref, kca_ref, kcb_ref, 1)

    def kv_pair(k_out, v_out, off_k, off_v):
        zk2 = headnorm(proj(off_k, NSA_KV_W), bd2_ref, gkn_ref)
        k_out[0, :, 0:LANES] = jnp.where(lo_half, zk2, kp0).astype(BF16)
        k_out[0, :, LANES:2 * LANES] = jnp.where(lo_half, kp1, zk2).astype(BF16)
        zv2 = proj(off_v, NSA_KV_W)
        v_out[0, :, 0:LANES] = jnp.where(lo_half, zv2, vone_ref[:, 0:LANES]).astype(BF16)
        v_out[0, :, LANES:2 * LANES] = jnp.where(lo_half, vone_ref[:, LANES:2 * LANES], zv2).astype(BF16)

    kv_pair(ksa_ref, vsa_ref, OFF_KS, OFF_VS)
    kv_pair(kwa_ref, vwa_ref, OFF_KW, OFF_VW)

    gt_ref[0] = jax.nn.sigmoid(proj(OFF_NG, LANES))
    gab_ref[0, :, 0:D_MODEL] = jax.nn.sigmoid(proj(OFF_GA, D_MODEL)).astype(BF16)
    gab_ref[0, :, D_MODEL:2 * D_MODEL] = jax.nn.sigmoid(proj(OFF_GA + D_MODEL, D_MODEL)).astype(BF16)


def _pos_pieces(pos):
    return ((pos // 256) * 256).astype(np.float32), (pos % 256).astype(np.float32)


def _inproj(x, norm_g, w_in, b_forget, fox_q_g, fox_k_g, nsa_q_g, nsa_k_g):
    B, S, D = x.shape
    tm = min(TM_PROJ, S)
    scale = HEAD_DIM ** -0.5

    c = [0]
    for n in (FOX_W, FOX_W, FOX_W, FOX_HEADS, NSA_W) + (NSA_KV_W,) * 6 + (3 * NSA_HEADS, D_MODEL, D_MODEL):
        c.append(c[-1] + n)
    fq, fk, fv, ff, nq, kc, vc, ks, vs, kw, vw, ng, ga, gb = [w_in[:, c[i]:c[i + 1]] for i in range(14)]
    perm = jnp.asarray([0, 4, 1, 5, 2, 6, 3, 7])
    nq = nq.reshape(D, NSA_HEADS, HEAD_DIM)[:, perm, :].reshape(D, NSA_W)
    padl = lambda a: jnp.pad(a, ((0, 0), (0, LANES - a.shape[1])))
    w = jnp.concatenate([fq, fk, fv, nq, kc, vc, ks, vs, kw, vw, padl(ff), padl(ng), ga, gb], axis=1).astype(BF16)
    assert w.shape[1] == N_PROJ

    bf = jnp.pad(b_forget.astype(F32), (0, LANES - FOX_HEADS))[None, :]
    gqa = jnp.tile(fox_q_g.astype(F32) * scale, FOX_HEADS)[None, :]
    gka = jnp.tile(fox_k_g.astype(F32), FOX_HEADS)[None, :]
    gqn = jnp.tile(nsa_q_g.astype(F32) * scale, NSA_HEADS)[None, :]
    gkn = jnp.tile(nsa_k_g.astype(F32), NSA_KV_HEADS)[None, :]

    r512 = np.arange(FOX_W)
    bd = np.where((r512[:, None] // HEAD_DIM) == np.arange(LANES)[None, :], 1.0 / HEAD_DIM, 0.0).astype(BF16)
    bd2 = bd[:LANES, :]
    rt = np.arange(tm)
    tri = (rt[None, :] <= rt[:, None]).astype(BF16)

    heads = np.arange(FOX_HEADS)
    base = heads * LANES + np.where(heads % 2 == 0, HALF, 0)
    cols = np.arange(FOX_HEADS * LANES)
    off_in_blk = cols - base[cols // LANES]
    cq = ((off_in_blk >= 0) & (off_in_blk < 3)).astype(F32)[None, :]
    ck = ((off_in_blk >= 3) & (off_in_blk < 9)).astype(F32)[None, :]
    cv = (off_in_blk == 0).astype(F32)[None, :]

    blk = cols // LANES
    slope = 2.0 ** (-(blk + 1).astype(F32))
    o = cols % LANES - np.where(blk // NSA_REP == 0, HALF, 0)
    qc0 = np.where((o == 0) | (o == 1), slope, 0.0)[None, :]
    qca = np.where(o == 2, -slope, 0.0)[None, :]
    qcb = np.where(o == 3, -slope, 0.0)[None, :]
    kc0, kca, kcb, vone = _kv_rows()

    grid = (B, S // tm)
    row_spec = lambda n: pl.BlockSpec((1, tm, n), lambda b, s: (b, s, 0))
    consts = [norm_g.astype(F32)[None, :], w, bf, gqa, gka, gqn, gkn, bd, bd2, tri, cq, ck, cv,
              qc0, qca, qcb, kc0, kca, kcb, vone]
    out_widths = [(8 * LANES, BF16)] * 4 + [(LANES, F32)] * 2 + [(2 * LANES, BF16)] * 4 + \
                 [(LANES, F32), (2 * D_MODEL, BF16)]
    outs = pl.pallas_call(
        _inproj_kernel,
        grid=grid,
        in_specs=[row_spec(D)] + [_const_spec(a.shape) for a in consts],
        out_specs=[row_spec(n) for n, _ in out_widths],
        out_shape=[jax.ShapeDtypeStruct((B, S, n), dt) for n, dt in out_widths],
        scratch_shapes=[pltpu.VMEM((1, LANES), F32)],
        compiler_params=_cparams(("arbitrary", "arbitrary")),
        name="inproj",
    )(x, *consts)
    return outs


def _kv_rows():
    cols = np.arange(2 * LANES)
    o = cols % LANES - np.where(cols // LANES == 0, HALF, 0)
    row = lambda m: m.astype(np.float32)[None, :]
    return row((o >= 2) & (o <= 6)), row(o == 0), row(o == 1), row(o == 0)


def _kv_consts(pa, pb):
    c0, ca, cb, vone = _kv_rows()
    return (c0 + ca * pa[:, None] + cb * pb[:, None]).astype(BF16), vone


def _compress_kernel(kt_ref, vt_ref, w1k_ref, w1v_ref, posk_ref, posv_ref, pw1k_ref, pw1v_ref, w2k_ref, w2v_ref,
                     gk_ref, kcc_ref, vone_ref, kc_ref, vc_ref):
    nc = kc_ref.shape[2]

    def mlp(t_ref, w1_ref, pos_ref, pw1_ref, w2_ref):
        both = jnp.zeros((nc, 2 * CMP_HIDDEN), F32)
        for l in range(CMP_STRIDE):
            rows = t_ref[0, pl.ds(l, nc, stride=CMP_STRIDE), :].astype(BF16)
            both = both + _dot(rows, w1_ref[0, l])
        posw = _dot(pos_ref[...], pw1_ref[...])[0:1, :]
        pre = both[:, 0:CMP_HIDDEN] + pltpu.roll(both[:, CMP_HIDDEN:2 * CMP_HIDDEN], nc - 1, axis=0) + posw
        act = pre * (0.5 * (1.0 + jnp.tanh(0.7978845608028654 * (pre + 0.044715 * (pre * pre * pre)))))
        return _dot(act.astype(BF16), w2_ref[0])

    kc = mlp(kt_ref, w1k_ref, posk_ref, pw1k_ref, w2k_ref)
    msq = jnp.sum(kc * kc, axis=-1, keepdims=True) * (1.0 / HEAD_DIM)
    kc_ref[0, 0] = (kc * lax.rsqrt(msq + RMS_EPS) * gk_ref[0] + kcc_ref[0]).astype(BF16)
    vc = mlp(vt_ref, w1v_ref, posv_ref, pw1v_ref, w2v_ref)
    vc_ref[0, 0] = (vc + vone_ref[0]).astype(BF16)


def _compress(kcr, vcr, cmp_k_w1, cmp_k_w2, cmp_k_pos, cmp_v_w1, cmp_v_w2, cmp_v_pos, nsa_k_g):
    B, S, _ = kcr.shape
    G = NSA_KV_HEADS
    nc = S // CMP_STRIDE

    def w1_strided(w1):
        w = w1.reshape(2, CMP_STRIDE, HEAD_DIM, CMP_HIDDEN)
        w = jnp.concatenate([w[0], w[1]], axis=-1)
        z = jnp.zeros_like(w)
        return jnp.stack([jnp.concatenate([w, z], axis=1), jnp.concatenate([z, w], axis=1)]).astype(BF16)

    def w2_spread(w2):
        z = jnp.zeros_like(w2)
        return jnp.stack([jnp.concatenate([w2, z], 1), jnp.concatenate([z, w2], 1)]).astype(BF16)

    def pos8(p):
        return jnp.tile(p.reshape(1, CMP_BLOCK * HEAD_DIM), (8, 1)).astype(BF16)

    gk = nsa_k_g.astype(F32)
    z = jnp.zeros_like(gk)
    gk2 = jnp.stack([jnp.concatenate([gk, z]), jnp.concatenate([z, gk])])[:, None, :]
    cend = np.arange(nc) * CMP_STRIDE + CMP_BLOCK - 1
    kcc, vone = _kv_consts(*_pos_pieces(cend))
    kcc = kcc.astype(F32).reshape(nc, G, LANES).transpose(1, 0, 2)
    vone = vone.reshape(G, 1, LANES)

    tok = pl.BlockSpec((1, S, G * HEAD_DIM), lambda b, g: (b, 0, 0))
    per_g = lambda a: pl.BlockSpec((1,) + a.shape[1:], lambda b, g: (g,) + (0,) * (a.ndim - 1))
    w1k, w1v = w1_strided(cmp_k_w1), w1_strided(cmp_v_w1)
    pk, pv = pos8(cmp_k_pos), pos8(cmp_v_pos)
    pw1k, pw1v = cmp_k_w1.astype(BF16), cmp_v_w1.astype(BF16)
    w2k, w2v = w2_spread(cmp_k_w2), w2_spread(cmp_v_w2)
    out_spec = pl.BlockSpec((1, 1, nc, LANES), lambda b, g: (b, g, 0, 0))
    return pl.pallas_call(
        _compress_kernel,
        grid=(B, G),
        in_specs=[tok, tok, per_g(w1k), per_g(w1v), _const_spec(pk.shape), _const_spec(pv.shape),
                  _const_spec(pw1k.shape), _const_spec(pw1v.shape), per_g(w2k), per_g(w2v),
                  per_g(gk2), per_g(kcc), per_g(vone)],
        out_specs=[out_spec, out_spec],
        out_shape=[jax.ShapeDtypeStruct((B, G, nc, LANES), BF16)] * 2,
        compiler_params=_cparams(("arbitrary", "arbitrary")),
        name="compress",
    )(kcr, vcr, w1k, w1v, pk, pv, pw1k, pw1v, w2k, w2v, gk2, kcc, vone)


MAX_EXPONENT = 60.0


def _scores_bounded(q_g, k_g):
    bound = HEAD_DIM ** 0.5 * jnp.max(jnp.abs(q_g)) * jnp.max(jnp.abs(k_g))
    return 2.04 * bound + 0.05 <= MAX_EXPONENT


def _attn_first(s, v):
    m = jnp.max(s, axis=-1, keepdims=True)
    p = jnp.exp((s - m).astype(BF16))
    return m, _dot(p, v)


def _attn_step(carry, s, v):
    m, acc = carry
    m_new = jnp.maximum(m, jnp.max(s, axis=-1, keepdims=True))
    p = jnp.exp((s - m_new).astype(BF16))
    return m_new, jnp.exp(m - m_new) * acc + _dot(p, v)


def _nsa_kernel(bounded, q_ref, kc_ref, vc_ref, ks_ref, e_ref, vs_ref, kw_ref, vw_ref, gt_ref, ov_ref, o_ref,
                tiles_ref, sa_ref, sb_ref):
    tq = q_ref.shape[1]
    nc = kc_ref.shape[2]
    tk = TK_SLC
    rq = NSA_REP * tq
    g = pl.program_id(1)
    i = pl.program_id(2)
    q0 = i * tq

    q4 = jnp.concatenate([q_ref[0, :, LANES * r:LANES * (r + 1)] for r in range(NSA_REP)], axis=0)

    def qpos_of(shape):
        return q0 + (lax.broadcasted_iota(jnp.int32, shape, 0) & (tq - 1))

    def add_mask(s, valid):
        bias = jnp.where(valid, 0.0, NEG_INF)
        return (s.reshape(NSA_REP, tq, s.shape[1]) + bias[None]).reshape(s.shape)

    def qrow(n):
        return q0 + lax.broadcasted_iota(jnp.int32, (tq, n), 0)

    def kcol(n):
        return lax.broadcasted_iota(jnp.int32, (tq, n), 1)

    anyv = qpos_of((rq, 1)) >= CMP_BLOCK - 1

    def cmp_branch(n):
        s = add_mask(_dot_nt(q4, kc_ref[0, 0, 0:n, :]), qrow(n) >= kcol(n) * CMP_STRIDE + (CMP_BLOCK - 1))
        if bounded:
            e = jnp.exp(s)
            p = e * jnp.where(anyv, 1.0 / jnp.sum(e, axis=-1, keepdims=True), 0.0)
        else:
            m = jnp.max(s, axis=-1, keepdims=True)
            e = jnp.exp(s - m)
            p = e * (anyv.astype(F32) / jnp.sum(e, axis=-1, keepdims=True))
        o = _dot(p.astype(BF16), vc_ref[0, 0, 0:n, :])
        psum = p[0:tq] + p[tq:2 * tq] + p[2 * tq:3 * tq] + p[3 * tq:4 * tq]
        ov = ov_ref[0:n, :]
        return o, functools.reduce(lambda a, b: a + b, [_dot(piece, ov) for piece in _split3(psum)])

    last_visible = (q0 + tq - CMP_BLOCK) // CMP_STRIDE
    o_cmp, imp = lax.cond(last_visible < nc // 2, lambda: cmp_branch(nc // 2), lambda: cmp_branch(nc))

    blk_i = lax.broadcasted_iota(jnp.int32, (LANES, tq), 0)
    blk_f = blk_i.astype(F32)
    qblk = (q0 + lax.broadcasted_iota(jnp.int32, (LANES, tq), 1)) // SLC_BLOCK
    forced = (blk_i == 0) | (blk_i == qblk) | (blk_i == qblk - 1)
    score = jnp.where(forced, -3e38, jnp.where(blk_i > qblk, -1.0, imp.T))
    selb = jnp.where(forced, 0.0, NEG_INF)
    for _ in range(SLC_TOPK - 3):
        mx = jnp.max(score, axis=0, keepdims=True)
        first = jnp.min(jnp.where(score == mx, blk_f, float(LANES)), axis=0, keepdims=True)
        hit = blk_f == first
        selb = jnp.where(hit, 0.0, selb)
        score = jnp.where(hit, -3e38, score)
    sb = selb.T.astype(BF16)
    sb4 = jnp.concatenate([sb] * NSA_REP, axis=0)

    def slc_qk(j):
        start = pl.multiple_of(j * tk, tk)
        kk = jnp.concatenate([ks_ref[0, pl.ds(start, tk), :], e_ref[pl.ds(start, tk), :]], axis=1)
        return _dot_nt(q2, kk)

    def slc_v(j):
        return vs_ref[0, pl.ds(pl.multiple_of(j * tk, tk), tk), :]

    jd = q0 // tk
    blocks_per_tile = tk // SLC_BLOCK
    anysel = jnp.max(selb, axis=1, keepdims=True)
    cnt = jnp.int32(0)
    for j in range(ks_ref.shape[1] // tk):
        tile_sel = jnp.max(anysel[blocks_per_tile * j:blocks_per_tile * (j + 1), :]) > -1.0
        tiles_ref[cnt] = j
        cnt = cnt + (tile_sel & (j < jd)).astype(jnp.int32)

    lane_r = lax.broadcasted_iota(jnp.int32, (rq, LANES), 1)
    data0 = HALF * g
    ones_lane = HALF - data0
    nw = WINDOW + tq
    wstart = pl.multiple_of(jnp.maximum(q0 - WINDOW, 0), tq)
    win_valid = lax.bitcast_convert_type(qrow(nw) - (wstart + kcol(nw)), jnp.uint32) < WINDOW
    diag_valid = jd * tk + kcol(tk) <= qrow(tk)

    if bounded:
        def shifted_by_own_score(kself):
            own = jnp.sum(q4.astype(F32).reshape(NSA_REP, tq, LANES) * kself.astype(F32)[None],
                          axis=-1, keepdims=True).reshape(rq, 1)
            qf = q4.astype(F32)
            for off, piece in enumerate(_split3(-own)):
                qf = jnp.where(lane_r == ones_lane + 4 + off, piece.astype(F32), qf)
            return qf.astype(BF16)

        q2 = jnp.concatenate([shifted_by_own_score(ks_ref[0, pl.ds(q0, tq), :]), sb4], axis=1)
        sa_ref[:, 0:LANES] = _dot(jnp.exp(add_mask(slc_qk(jd), diag_valid)).astype(BF16), slc_v(jd))

        def slc_sweep(t0, n):
            pv = []
            for t in range(n):
                j = tiles_ref[t0 + t]
                pv.append(_dot(jnp.exp(slc_qk(j)).astype(BF16), slc_v(j)))
            sa_ref[:, 0:LANES] += functools.reduce(lambda a, b: a + b, pv)

        def slc_body(c, _):
            slc_sweep(c * SLC_UNROLL, SLC_UNROLL)
            return _

        lax.fori_loop(0, cnt // SLC_UNROLL, slc_body, 0)
        done = (cnt // SLC_UNROLL) * SLC_UNROLL
        n = SLC_UNROLL // 2
        while n >= 1:
            @pl.when((cnt & n) != 0)
            def _(n=n, start=done):
                slc_sweep(start, n)
            done = done + (cnt & n)
            n //= 2

        acc_slc = sa_ref[:, 0:LANES]

        qw = shifted_by_own_score(kw_ref[0, pl.ds(q0, tq), :])
        s = add_mask(_dot_nt(qw, kw_ref[0, pl.ds(wstart, nw), :]), win_valid)
        acc_win = _dot(jnp.exp(s).astype(BF16), vw_ref[0, pl.ds(wstart, nw), :])
    else:
        q2 = jnp.concatenate([q4, sb4], axis=1)
        n_seq = cnt + 1

        def tile_of(n):
            return jnp.where(n == 0, jd, tiles_ref[jnp.maximum(n - 1, 0)])

        def slc_step(carry, s_buf, n):
            return _attn_step(carry, s_buf[...], slc_v(tile_of(n)))

        sa_ref[...] = add_mask(slc_qk(jd), diag_valid)

        def slc_body(p, carry):
            sb_ref[...] = slc_qk(tile_of(2 * p + 1))
            carry = slc_step(carry, sa_ref, 2 * p)
            sa_ref[...] = slc_qk(tile_of(jnp.minimum(2 * p + 2, n_seq - 1)))
            return slc_step(carry, sb_ref, 2 * p + 1)

        carry = (jnp.full((rq, 1), NEG_INF, F32), jnp.zeros((rq, LANES), F32))
        carry = lax.fori_loop(0, n_seq // 2, slc_body, carry)
        _, acc_slc = lax.cond(n_seq % 2 == 1, lambda c: slc_step(c, sa_ref, n_seq - 1), lambda c: c, carry)

        s = add_mask(_dot_nt(q4, kw_ref[0, pl.ds(wstart, nw), :]), win_valid)
        _, acc_win = _attn_first(s, vw_ref[0, pl.ds(wstart, nw), :])


    def normalise(acc):
        l = jnp.sum(jnp.where(lane_r == ones_lane, acc, 0.0), axis=-1, keepdims=True)
        return acc * (1.0 / l)

    o_slc = normalise(acc_slc)
    o_win = normalise(acc_win)
    gt = gt_ref[0]
    lane_g = lax.broadcasted_iota(jnp.int32, (tq, LANES), 1)
    is_data_q = (lane_g >= data0) & (lane_g < data0 + HALF)
    for r in range(NSA_REP):
        col = 3 * (NSA_REP * g + r)
        gate = [jnp.sum(jnp.where(lane_g == col + b, gt, 0.0), axis=-1, keepdims=True) for b in range(3)]
        rows = slice(r * tq, (r + 1) * tq)
        o = gate[0] * o_cmp[rows] + gate[1] * o_slc[rows] + gate[2] * o_win[rows]
        o_ref[0, :, LANES * r:LANES * (r + 1)] = jnp.where(is_data_q, o, 0.0).astype(BF16)


def _nsa(qn, kca, vca, ksa, vsa, kwa, vwa, gt, scores_bounded):
    B, S, _ = qn.shape
    G = NSA_KV_HEADS
    tq = min(TQ_NSA, S)
    nc = S // CMP_STRIDE
    n_slc = S // SLC_BLOCK
    assert n_slc <= LANES and S % TK_SLC == 0

    cs = np.arange(nc)[:, None] * CMP_STRIDE
    ss = np.arange(LANES)[None, :] * SLC_BLOCK
    ovl = np.clip(np.minimum(cs + CMP_BLOCK, ss + SLC_BLOCK) - np.maximum(cs, ss), 0, None)
    valid = (np.arange(nc)[:, None] < (S - CMP_BLOCK) // CMP_STRIDE + 1) & (np.arange(LANES)[None, :] < n_slc)
    ov = np.where(valid, ovl.astype(F32) / CMP_BLOCK, 0.0).astype(BF16)
    e1h = (np.arange(S)[:, None] // SLC_BLOCK == np.arange(LANES)[None, :]).astype(BF16)

    q_spec = pl.BlockSpec((1, tq, NSA_REP * LANES), lambda b, g, i: (b, i, g))
    c_spec = pl.BlockSpec((1, 1, nc, LANES), lambda b, g, i: (b, g, 0, 0))
    kv_spec = pl.BlockSpec((1, S, LANES), lambda b, g, i: (b, 0, g))

    def call(bounded, name):
        return pl.pallas_call(
            functools.partial(_nsa_kernel, bounded),
            grid=(B, G, S // tq),
            in_specs=[q_spec, c_spec, c_spec, kv_spec, _const_spec(e1h.shape), kv_spec, kv_spec, kv_spec,
                      pl.BlockSpec((1, tq, LANES), lambda b, g, i: (b, i, 0)), _const_spec(ov.shape)],
            out_specs=q_spec,
            out_shape=jax.ShapeDtypeStruct((B, S, NSA_HEADS * LANES), BF16),
            scratch_shapes=[pltpu.SMEM((S // TK_SLC + 1,), jnp.int32),
                            pltpu.VMEM((NSA_REP * tq, TK_SLC), F32), pltpu.VMEM((NSA_REP * tq, TK_SLC), F32)],
            compiler_params=_cparams(("arbitrary", "arbitrary", "arbitrary")),
            name=name,
        )

    return lax.cond(scores_bounded, call(True, "nsa_bounded"), call(False, "nsa"),
                    qn, kca, vca, ksa, e1h, vsa, kwa, vwa, gt, ov)


def _fox_kernel(q_ref, k_ref, v_ref, o_ref, sa_ref, sb_ref):
    tq = q_ref.shape[1]
    i = pl.program_id(2)
    lane = lax.broadcasted_iota(jnp.int32, (tq, LANES), 1)
    causal = lax.broadcasted_iota(jnp.int32, (tq, tq), 1) <= lax.broadcasted_iota(jnp.int32, (tq, tq), 0)

    def cols(hh):
        return slice(LANES * hh, LANES * (hh + 1))

    def qk(hh, j):
        start = pl.multiple_of(j * tq, tq)
        return _dot_nt(q_ref[0, :, cols(hh)], k_ref[0, pl.ds(start, tq), cols(hh)])

    def vtile(hh, j):
        return v_ref[0, pl.ds(pl.multiple_of(j * tq, tq), tq), cols(hh)]

    n_tiles = i + 1

    def tile_of(n):
        return jnp.where(n == 0, i, n - 1)

    def step_all(carry, s_buf, n):
        t = tile_of(n)
        return tuple(_attn_step(carry[hh], s_buf[hh], vtile(hh, t)) for hh in range(2))

    for hh in range(2):
        sa_ref[hh] = jnp.where(causal, qk(hh, i), NEG_INF)
    init = tuple((jnp.full((tq, 1), NEG_INF, F32), jnp.zeros((tq, LANES), F32)) for hh in range(2))

    def body(p, carry):
        for hh in range(2):
            sb_ref[hh] = qk(hh, tile_of(2 * p + 1))
        carry = step_all(carry, sa_ref, 2 * p)
        nxt = jnp.minimum(2 * p + 2, n_tiles - 1)
        for hh in range(2):
            sa_ref[hh] = qk(hh, tile_of(nxt))
        return step_all(carry, sb_ref, 2 * p + 1)

    carry = lax.fori_loop(0, n_tiles // 2, body, init)
    carry = lax.cond(n_tiles % 2 == 1, lambda c: step_all(c, sa_ref, n_tiles - 1), lambda c: c, carry)
    outs = []
    for hh in range(2):
        acc = carry[hh][1]
        ones_lane = HALF if hh == 0 else 0
        l = jnp.sum(jnp.where(lane == ones_lane, acc, 0.0), axis=-1, keepdims=True)
        outs.append(acc * (1.0 / l))
    o_ref[0] = jnp.where(lane < HALF, outs[0], outs[1]).astype(BF16)


def _fox_bounded_kernel(q_ref, k_ref, v_ref, o_ref, qs_ref, acc_ref):
    tq = q_ref.shape[1]
    i = pl.program_id(2)
    lane = lax.broadcasted_iota(jnp.int32, (tq, LANES), 1)
    causal = lax.broadcasted_iota(jnp.int32, (tq, tq), 1) <= lax.broadcasted_iota(jnp.int32, (tq, tq), 0)

    def cols(hh):
        return slice(LANES * hh, LANES * (hh + 1))

    def ktile(hh, j):
        return k_ref[0, pl.ds(pl.multiple_of(j * tq, tq), tq), cols(hh)]

    def vtile(hh, j):
        return v_ref[0, pl.ds(pl.multiple_of(j * tq, tq), tq), cols(hh)]

    for hh in range(2):
        q = q_ref[0, :, cols(hh)]
        s = jnp.where(causal, _dot_nt(q, ktile(hh, i)), NEG_INF)
        m = jnp.max(s, axis=-1, keepdims=True)
        acc_ref[hh] = _dot(jnp.exp((s - m).astype(BF16)), vtile(hh, i))
        qf = q.astype(F32)
        free0 = (HALF if hh == 0 else 0) + 6
        for off, piece in enumerate(_split3(-m)):
            qf = jnp.where(lane == free0 + off, piece.astype(F32), qf)
        qs_ref[hh] = qf.astype(BF16)

    def sweep(j0, n):
        for hh in range(2):
            pv = [_dot(jnp.exp(_dot_nt(qs_ref[hh], ktile(hh, j0 + t)).astype(BF16)), vtile(hh, j0 + t))
                  for t in range(n)]
            acc_ref[hh] += functools.reduce(lambda a, b: a + b, pv)

    def body(c, _):
        sweep(c * FOX_UNROLL, FOX_UNROLL)
        return _

    lax.fori_loop(0, i // FOX_UNROLL, body, 0)
    done = (i // FOX_UNROLL) * FOX_UNROLL
    n = FOX_UNROLL // 2
    while n >= 1:
        @pl.when((i & n) != 0)
        def _(n=n, start=done):
            sweep(start, n)
        done = done + (i & n)
        n //= 2
    outs = []
    for hh in range(2):
        acc = acc_ref[hh]
        ones_lane = HALF if hh == 0 else 0
        l = jnp.sum(jnp.where(lane == ones_lane, acc, 0.0), axis=-1, keepdims=True)
        outs.append(acc * (1.0 / l))
    o_ref[0] = jnp.where(lane < HALF, outs[0], outs[1]).astype(BF16)


def _fox(qa, ka, va, scores_bounded):
    B, S, _ = qa.shape
    tq = min(TQ_FOX, S)
    q_spec = pl.BlockSpec((1, tq, 2 * LANES), lambda b, h, i: (b, i, h))
    kv_spec = pl.BlockSpec((1, S, 2 * LANES), lambda b, h, i: (b, 0, h))

    def call(body, scratch, name):
        return pl.pallas_call(
            body,
            grid=(B, FOX_HEADS // 2, S // tq),
            in_specs=[q_spec, kv_spec, kv_spec],
            out_specs=pl.BlockSpec((1, tq, LANES), lambda b, h, i: (b, i, h)),
            out_shape=jax.ShapeDtypeStruct((B, S, FOX_W), BF16),
            scratch_shapes=scratch,
            compiler_params=_cparams(("arbitrary", "arbitrary", "arbitrary")),
            name=name,
        )

    general = call(_fox_kernel, [pltpu.VMEM((2, tq, tq), F32), pltpu.VMEM((2, tq, tq), F32)], "fox")
    bounded = call(_fox_bounded_kernel, [pltpu.VMEM((2, tq, LANES), BF16), pltpu.VMEM((2, tq, LANES), F32)],
                   "fox_bounded")
    return lax.cond(scores_bounded, bounded, general, qa, ka, va)


R_E1, R_E2, R_W1, R_W2, R_RANK1, R_RANK2 = range(6)


def _merge_kernel(x_ref, oa_ref, ob_ref, gab_ref, wa_ref, wb_ref, wo_ref, g2_ref, wr_hi_ref, wr_lo_ref, br_ref,
                  ltri_ref, x1_ref, route_ref, route_t_ref, cnt_ref, carry_ref):
    tm = x_ref.shape[1]

    @pl.when((pl.program_id(0) == 0) & (pl.program_id(1) == 0))
    def _():
        carry_ref[...] = jnp.zeros_like(carry_ref)

    out_a = _dot(oa_ref[0], wa_ref[...])
    ob = jnp.concatenate([ob_ref[0, :, LANES * r:LANES * (r + 1)] +
                          ob_ref[0, :, LANES * (NSA_REP + r):LANES * (NSA_REP + r + 1)] for r in range(NSA_REP)], axis=1)
    out_b = _dot(ob, wb_ref[...])
    mix = gab_ref[0, :, 0:D_MODEL].astype(F32) * out_a + gab_ref[0, :, D_MODEL:2 * D_MODEL].astype(F32) * out_b
    x1 = x_ref[0] + _dot(mix.astype(BF16), wo_ref[...])
    _to_tile_rows(x1_ref, x1)
    h2 = x1 * lax.rsqrt(jnp.mean(x1 * x1, axis=-1, keepdims=True) + RMS_EPS) * g2_ref[...]

    h_hi = h2.astype(BF16)
    h_lo = (h2 - h_hi.astype(F32)).astype(BF16)
    logits = _dot(h_hi, wr_hi_ref[...]) + (_dot(h_hi, wr_lo_ref[...]) + _dot(h_lo, wr_hi_ref[...])) + br_ref[...]
    lane = lax.broadcasted_iota(jnp.int32, (tm, LANES), 1)
    lane_f = lane.astype(F32)

    def first_argmax(vals):
        mx = jnp.max(vals, axis=-1, keepdims=True)
        idx = jnp.min(jnp.where(vals == mx, lane_f, float(LANES)), axis=-1, keepdims=True)
        return mx, idx

    is_grp = (lane >= N_EXPERTS) & (lane < N_EXPERTS + N_GROUPS)
    gl = jnp.where(is_grp, logits, NEG_INF)
    gmax, gidx = first_argmax(gl)
    p_g = 1.0 / jnp.sum(jnp.where(is_grp, jnp.exp(gl - gmax), 0.0), axis=-1, keepdims=True)
    e_lo = (gidx - float(N_EXPERTS)) * float(EXPERTS_PER_GROUP)
    in_grp = (lane_f >= e_lo) & (lane_f < e_lo + float(EXPERTS_PER_GROUP))
    el = jnp.where(in_grp, logits, NEG_INF)
    m1, i1 = first_argmax(el)
    m2, i2 = first_argmax(jnp.where(lane_f == i1, NEG_INF, el))
    e2 = jnp.exp(m2 - m1)
    w1 = p_g / (1.0 + e2)
    w2 = p_g * e2 / (1.0 + e2)

    hit1 = lane_f == i1
    hit2 = lane_f == i2
    onehot = jnp.where(hit1 | hit2, 1.0, 0.0)
    before = carry_ref[...] + _dot(ltri_ref[...], onehot.astype(BF16))
    rank1 = jnp.sum(jnp.where(hit1, before, 0.0), axis=-1, keepdims=True)
    rank2 = jnp.sum(jnp.where(hit2, before, 0.0), axis=-1, keepdims=True)
    total = carry_ref[...] + jnp.sum(onehot, axis=0, keepdims=True)
    carry_ref[...] = total
    cnt_ref[...] = total

    rec = jnp.zeros((tm, LANES), F32)
    for k, val in ((R_E1, i1), (R_E2, i2), (R_W1, w1), (R_W2, w2), (R_RANK1, rank1), (R_RANK2, rank2)):
        rec = jnp.where(lane == k, val, rec)
    route_ref[...] = rec
    route_t_ref[...] = rec.T[0:8, :]


def _merge(x, oa, ob, gab, w_fox_up, w_nsa_up, w_out, norm_ffn_g, w_group, b_group, w_router, b_router):
    B, S, D = x.shape
    tm = min(TM_MERGE, S)
    wa = w_fox_up.astype(BF16)
    wn = w_nsa_up.reshape(NSA_KV_HEADS, NSA_REP, HEAD_DIM, D)
    wb = wn.transpose(1, 0, 2, 3).reshape(NSA_W, D).astype(BF16)
    wo = w_out.astype(BF16)
    wr = jnp.pad(jnp.concatenate([w_router, w_group], axis=1).astype(F32),
                 ((0, 0), (0, LANES - N_EXPERTS - N_GROUPS)))
    wr_hi = wr.astype(BF16)
    wr_lo = (wr - wr_hi.astype(F32)).astype(BF16)
    br = jnp.pad(jnp.concatenate([b_router, b_group]).astype(F32), (0, LANES - N_EXPERTS - N_GROUPS))[None, :]
    g2 = norm_ffn_g.astype(F32)[None, :]
    rt = np.arange(tm)
    ltri = (rt[None, :] < rt[:, None]).astype(BF16)

    row = lambda n: pl.BlockSpec((1, tm, n), lambda b, s: (b, s, 0))
    flat = lambda n: pl.BlockSpec((tm, n), lambda b, s: (b * (S // tm) + s, 0))
    flat_rows = pl.BlockSpec((tm * ROW_SUBLANES, LANES), lambda b, s: (b * (S // tm) + s, 0))
    consts = [wa, wb, wo, g2, wr_hi, wr_lo, br, ltri]
    return pl.pallas_call(
        _merge_kernel,
        grid=(B, S // tm),
        in_specs=[row(D), row(FOX_W), row(NSA_HEADS * LANES), row(2 * D)] + [_const_spec(a.shape) for a in consts],
        out_specs=[flat_rows, flat(LANES), pl.BlockSpec((8, tm), lambda b, s: (0, b * (S // tm) + s)),
                   _const_spec((1, LANES))],
        out_shape=[jax.ShapeDtypeStruct((B * S * ROW_SUBLANES, LANES), F32), jax.ShapeDtypeStruct((B * S, LANES), F32),
                   jax.ShapeDtypeStruct((8, B * S), F32), jax.ShapeDtypeStruct((1, LANES), F32)],
        scratch_shapes=[pltpu.VMEM((1, LANES), F32)],
        compiler_params=_cparams(("arbitrary", "arbitrary")),
        name="merge",
    )(x, oa, ob, gab, *consts)


ROW_SUBLANES = D_MODEL // LANES


def _to_tile_rows(ref, x):
    n = x.shape[0]
    for c in range(ROW_SUBLANES):
        ref[pl.ds(c, n, stride=ROW_SUBLANES), :] = x[:, LANES * c:LANES * (c + 1)]


def _from_tile_rows(ref):
    n = ref.shape[0] // ROW_SUBLANES
    return jnp.concatenate([ref[pl.ds(c, n, stride=ROW_SUBLANES), :] for c in range(ROW_SUBLANES)], axis=1)


def _row_copy(src_ref, src_row, dst_ref, dst_row, sem):
    src = src_ref.at[pl.ds(pl.multiple_of(src_row * ROW_SUBLANES, ROW_SUBLANES), ROW_SUBLANES), :]
    dst = dst_ref.at[pl.ds(pl.multiple_of(dst_row * ROW_SUBLANES, ROW_SUBLANES), ROW_SUBLANES), :]
    return pltpu.make_async_copy(src, dst, sem)


DISPATCH_SLOTS = 3


def _dispatch_kernel(row1_ref, row2_ref, clear_ref, x1_ref, xs_ref, stage_ref, zero_ref, in_sem, out_sem, zsem):
    tm = stage_ref.shape[1] // ROW_SUBLANES
    t = pl.program_id(0)
    n_steps = pl.num_programs(0)

    def stage(step):
        rows = pl.ds(pl.multiple_of(step * (tm * ROW_SUBLANES), tm * ROW_SUBLANES), tm * ROW_SUBLANES)
        slot = step % DISPATCH_SLOTS
        return pltpu.make_async_copy(x1_ref.at[rows, :], stage_ref.at[slot], in_sem.at[slot])

    def for_each_scatter(step, fn):
        base = step * tm
        slot = step % DISPATCH_SLOTS

        def body(r, _):
            fn(_row_copy(stage_ref.at[slot], r, xs_ref, row1_ref[base + r], out_sem.at[slot]), 0)
            fn(_row_copy(stage_ref.at[slot], r, xs_ref, row2_ref[base + r], out_sem.at[slot]), 1)
            return _

        lax.fori_loop(0, tm, body, 0, unroll=8)

    @pl.when(t == 0)
    def _():
        stage(0).start()
        zero_ref[...] = jnp.zeros_like(zero_ref)

        def clear(c):
            start = pl.multiple_of(jnp.maximum(clear_ref[c], 0) * ROW_SUBLANES, zero_ref.shape[0])
            return pltpu.make_async_copy(zero_ref, xs_ref.at[pl.ds(start, zero_ref.shape[0]), :], zsem)

        for c in range(clear_ref.shape[0]):
            @pl.when(clear_ref[c] >= 0)
            def _(c=c):
                clear(c).start()
        for c in range(clear_ref.shape[0]):
            @pl.when(clear_ref[c] >= 0)
            def _(c=c):
                clear(c).wait()

    @pl.when(t >= 2)
    def _():
        for_each_scatter(t - 2, lambda c, _: c.wait())

    @pl.when(t + 1 < n_steps)
    def _():
        stage(t + 1).start()

    stage(t).wait()
    for_each_scatter(t, lambda c, priority: c.start(priority=priority))

    @pl.when(t + 1 == n_steps)
    def _():
        @pl.when(t >= 1)
        def _():
            for_each_scatter(t - 1, lambda c, _: c.wait())
        for_each_scatter(t, lambda c, _: c.wait())


def _experts_kernel(tile_e_ref, n_used_ref, xs_ref, g2_ref, wg_ref, wu_ref, wd_ref, ys_ref,
                    wg_bf, wu_bf, wd_bf):
    k = pl.program_id(0)
    used = k < n_used_ref[0]

    @pl.when(used & ((k == 0) | (tile_e_ref[k] != tile_e_ref[jnp.maximum(k - 1, 0)])))
    def _():
        wg_bf[...] = wg_ref[0, 0].astype(BF16)
        wu_bf[...] = wu_ref[0, 0].astype(BF16)
        wd_bf[...] = wd_ref[0, 0].astype(BF16)

    @pl.when(used)
    def _():
        x = _from_tile_rows(xs_ref)
        h = (x * lax.rsqrt(jnp.mean(x * x, axis=-1, keepdims=True) + RMS_EPS) * g2_ref[...]).astype(BF16)
        a = _dot(h, wg_bf[...])
        hid = (a * jax.nn.sigmoid(a)) * _dot(h, wu_bf[...])
        _to_tile_rows(ys_ref, _dot(hid.astype(BF16), wd_bf[...]))

    @pl.when(jnp.logical_not(used))
    def _():
        ys_ref[...] = jnp.zeros_like(ys_ref)


def _combine_kernel(row1_ref, row2_ref, x1_ref, route_ref, ys_ref, o_ref, y1_ref, y2_ref, sem):
    tm = x1_ref.shape[0] // ROW_SUBLANES
    t = pl.program_id(0)
    slot = t & 1

    def for_each_copy(step, slot, fn):
        base = step * tm

        def body(r, _):
            fn(_row_copy(ys_ref, row1_ref[base + r], y1_ref.at[slot], r, sem.at[slot]), 0)
            fn(_row_copy(ys_ref, row2_ref[base + r], y2_ref.at[slot], r, sem.at[slot]), 1)
            return _

        lax.fori_loop(0, tm, body, 0, unroll=8)

    @pl.when(t == 0)
    def _():
        for_each_copy(0, 0, lambda c, priority: c.start(priority=priority))

    @pl.when(t + 1 < pl.num_programs(0))
    def _():
        for_each_copy(t + 1, 1 - slot, lambda c, priority: c.start(priority=priority))

    for_each_copy(t, slot, lambda c, _: c.wait())
    rec = route_ref[...]
    lane = lax.broadcasted_iota(jnp.int32, rec.shape, 1)
    w1 = jnp.sum(jnp.where(lane == R_W1, rec, 0.0), axis=-1, keepdims=True)
    w2 = jnp.sum(jnp.where(lane == R_W2, rec, 0.0), axis=-1, keepdims=True)
    o_ref[0] = _from_tile_rows(x1_ref) + (w1 * _from_tile_rows(y1_ref.at[slot]) + w2 * _from_tile_rows(y2_ref.at[slot]))


def _moe(x1, route, route_t, cnt, norm_ffn_g, w_gate, w_up, w_down, layer, B, S):
    T, D = B * S, D_MODEL
    tme = min(TM_EXPERT, T)
    tmd = min(TM_DISPATCH, S)

    counts = cnt[0, :N_EXPERTS].astype(jnp.int32)
    tiles_per_e = (counts + tme - 1) // tme
    tile_end = jnp.cumsum(tiles_per_e)
    offs = (tile_end - tiles_per_e) * tme
    last_tile_row = jnp.where(tiles_per_e > 0, (tile_end - 1) * tme, -1).astype(jnp.int32)
    rec = route_t.astype(jnp.int32)
    experts = jnp.arange(N_EXPERTS, dtype=jnp.int32)[:, None]
    row_of = lambda e, rank: rank + jnp.sum(jnp.where(e[None, :] == experts, offs[:, None], 0), axis=0)
    row1 = row_of(rec[R_E1], rec[R_RANK1])
    row2 = row_of(rec[R_E2], rec[R_RANK2])
    max_tiles = (2 * T) // tme + N_EXPERTS
    tile_e = jnp.sum(jnp.arange(max_tiles, dtype=jnp.int32)[:, None] >= tile_end[None, :], axis=1)
    tile_e = jnp.minimum(tile_e, N_EXPERTS - 1).astype(jnp.int32)
    n_used = tile_end[-1:].astype(jnp.int32)
    n_rows = max_tiles * tme
    spare = n_used + jnp.arange(N_EXPERTS, dtype=jnp.int32)
    clear_rows = jnp.concatenate([last_tile_row, jnp.where(spare < max_tiles, spare * tme, -1)]).astype(jnp.int32)

    xs = pl.pallas_call(
        _dispatch_kernel,
        grid_spec=pltpu.PrefetchScalarGridSpec(
            num_scalar_prefetch=3,
            grid=(T // tmd,),
            in_specs=[pl.BlockSpec(memory_space=pl.ANY)],
            out_specs=pl.BlockSpec(memory_space=pl.ANY),
            scratch_shapes=[pltpu.VMEM((DISPATCH_SLOTS, tmd * ROW_SUBLANES, LANES), F32),
                            pltpu.VMEM((tme * ROW_SUBLANES, LANES), F32),
                            pltpu.SemaphoreType.DMA((DISPATCH_SLOTS,)), pltpu.SemaphoreType.DMA((DISPATCH_SLOTS,)),
                            pltpu.SemaphoreType.DMA(())],
        ),
        out_shape=jax.ShapeDtypeStruct((n_rows * ROW_SUBLANES, LANES), F32),
        compiler_params=_cparams(("arbitrary",)),
        name="dispatch",
    )(row1, row2, clear_rows, x1)

    g2 = norm_ffn_g.astype(F32)[None, :]
    w_spec = lambda shape: pl.BlockSpec((1, 1) + shape, lambda k, te, nu: (layer, te[k], 0, 0))
    ys = pl.pallas_call(
        _experts_kernel,
        grid_spec=pltpu.PrefetchScalarGridSpec(
            num_scalar_prefetch=2,
            grid=(max_tiles,),
            in_specs=[pl.BlockSpec((tme * ROW_SUBLANES, LANES), lambda k, te, nu: (jnp.minimum(k, nu[0] - 1), 0)),
                      pl.BlockSpec((1, D), lambda k, te, nu: (0, 0)),
                      w_spec((D, D_EXPERT)), w_spec((D, D_EXPERT)), w_spec((D_EXPERT, D))],
            out_specs=pl.BlockSpec((tme * ROW_SUBLANES, LANES), lambda k, te, nu: (k, 0)),
            scratch_shapes=[pltpu.VMEM((D, D_EXPERT), BF16), pltpu.VMEM((D, D_EXPERT), BF16),
                            pltpu.VMEM((D_EXPERT, D), BF16)],
        ),
        out_shape=jax.ShapeDtypeStruct((n_rows * ROW_SUBLANES, LANES), F32),
        compiler_params=_cparams(("arbitrary",)),
        name="experts",
    )(tile_e, n_used, xs, g2, w_gate, w_up, w_down)

    return pl.pallas_call(
        _combine_kernel,
        grid_spec=pltpu.PrefetchScalarGridSpec(
            num_scalar_prefetch=2,
            grid=(T // tmd,),
            in_specs=[pl.BlockSpec((tmd * ROW_SUBLANES, LANES), lambda t, r1, r2: (t, 0)),
                      pl.BlockSpec((tmd, LANES), lambda t, r1, r2: (t, 0)),
                      pl.BlockSpec(memory_space=pl.ANY)],
            out_specs=pl.BlockSpec((1, tmd, D), lambda t, r1, r2: (t // (S // tmd), t % (S // tmd), 0)),
            scratch_shapes=[pltpu.VMEM((2, tmd * ROW_SUBLANES, LANES), F32), pltpu.VMEM((2, tmd * ROW_SUBLANES, LANES), F32),
                            pltpu.SemaphoreType.DMA((2,))],
        ),
        out_shape=jax.ShapeDtypeStruct((B, S, D), F32),
        compiler_params=_cparams(("arbitrary",)),
        name="combine",
    )(row1, row2, x1, route, ys)


def kernel(x, norm_mix_g, w_in, b_forget, fox_q_g, fox_k_g, nsa_q_g, nsa_k_g, cmp_k_w1, cmp_k_w2, cmp_k_pos,
           cmp_v_w1, cmp_v_w2, cmp_v_pos, w_fox_up, w_nsa_up, w_out, norm_ffn_g, w_group, b_group, w_router,
           b_router, w_gate, w_up, w_down):
    B, S, D = x.shape
    for l in range(w_in.shape[0]):
        qa, ka, va, qn, kcr, vcr, ksa, vsa, kwa, vwa, gt, gab = _inproj(
            x, norm_mix_g[l], w_in[l], b_forget[l], fox_q_g[l], fox_k_g[l], nsa_q_g[l], nsa_k_g[l])
        kca, vca = _compress(kcr, vcr, cmp_k_w1[l], cmp_k_w2[l], cmp_k_pos[l],
                             cmp_v_w1[l], cmp_v_w2[l], cmp_v_pos[l], nsa_k_g[l])
        ob = _nsa(qn, kca, vca, ksa, vsa, kwa, vwa, gt, _scores_bounded(nsa_q_g[l], nsa_k_g[l]))
        oa = _fox(qa, ka, va, _scores_bounded(fox_q_g[l], fox_k_g[l]))
        x1, route, route_t, cnt = _merge(x, oa, ob, gab, w_fox_up[l], w_nsa_up[l], w_out[l], norm_ffn_g[l],
                                         w_group[l], b_group[l], w_router[l], b_router[l])
        x = _moe(x1, route, route_t, cnt, norm_ffn_g[l], w_gate, w_up, w_down, l, B, S)
    return x
```
